```python
import math
import jax
import jax.numpy as jnp
from jax import lax
import numpy as np

D_MODEL = 2048
BATCH = 1
SEQ = 8192
DEPTH = 2

CTX_LEN = 256
GRID_W = 64
MIX_WIDTH = D_MODEL
HYENA_WIDTH = MIX_WIDTH // 2
HYENA_GROUPS = 8
HYENA_PROJ = 3 * HYENA_WIDTH
FILTER_BANDS = 16
FILTER_EMB = 1 + 2 * FILTER_BANDS
FILTER_HIDDEN = 64
FILTER_TARGET = 1e-2
FILTER_FAST_PCT = 0.3
FILTER_SLOW_PCT = 1.5
GLA_HEADS = 8
GLA_DV = (MIX_WIDTH - HYENA_WIDTH) // GLA_HEADS
GLA_DK = GLA_DV // 2
GLA_QK = GLA_HEADS * GLA_DK
GLA_V = GLA_HEADS * GLA_DV
GLA_RANK = 16
GLA_GATE_TAU = 16.0
GLA_CHUNK = 64
GLA_STATE_COLS = GLA_QK + GLA_V + 2 * GLA_RANK
GLA_PROJ = GLA_STATE_COLS + GLA_QK + GLA_V
IN_WIDTH = HYENA_PROJ + GLA_PROJ
D_FF = 5632
NORM_EPS = 1e-6

kernel_name = 'hybrid_hyena_gla_dit_block'


def rmsnorm(x, g):
    x32 = x.astype(jnp.float32)
    y = x32 * lax.rsqrt(jnp.mean(x32 * x32, axis=-1, keepdims=True) + NORM_EPS)
    return (y * g.astype(jnp.float32)).astype(x.dtype)


def adaln_params(cvec, w_mod, b_mod):
    m = jax.nn.silu(cvec) @ w_mod + b_mod
    return jnp.split(m, 6, axis=-1)


def modulate(h, shift, scale):
    return h * (1 + scale) + shift


def dwconv1d(x, w):
    return lax.conv_general_dilated(x, w[:, None, :].astype(x.dtype), (1,), 'SAME',
                                    dimension_numbers=('NWC', 'WIO', 'NWC'),
                                    feature_group_count=x.shape[-1])


def dwconv2d(x, w, rows, cols):
    b, l, ch = x.shape
    y = lax.conv_general_dilated(x.reshape(b, rows, cols, ch), w[:, :, None, :].astype(x.dtype), (1, 1), 'SAME',
                                 dimension_numbers=('NHWC', 'HWIO', 'NHWC'), feature_group_count=ch)
    return y.reshape(b, l, ch)


def hyena_filter(seq_len, w1, b1, freq, w2, b2, w3):
    f32 = jnp.float32
    pos = jnp.arange(seq_len, dtype=f32)
    t = pos / max(seq_len - 1, 1)
    bands = jnp.arange(1, FILTER_BANDS + 1, dtype=f32)
    ang = (2.0 * math.pi / seq_len) * pos[:, None] * bands[None, :]
    z = jnp.concatenate([t[:, None], jnp.cos(ang), jnp.sin(ang)], axis=-1)
    fr = freq.astype(f32)
    h = jnp.sin(fr * (z @ w1.astype(f32) + b1.astype(f32)))
    h = jnp.sin(fr * (h @ w2.astype(f32) + b2.astype(f32)))
    h = h @ w3.astype(f32)
    deltas = jnp.abs(jnp.linspace(math.log(FILTER_TARGET) / FILTER_SLOW_PCT,
                                  math.log(FILTER_TARGET) / FILTER_FAST_PCT, HYENA_WIDTH, dtype=f32))
    decay = jnp.exp(-t[:, None] * deltas[None, :])
    h_fwd = h[:, :HYENA_WIDTH] * decay
    h_bwd = h[:, HYENA_WIDTH:] * decay
    return jnp.concatenate([h_fwd, jnp.zeros((1, HYENA_WIDTH), f32), h_bwd[:0:-1]], axis=0)


def fft_long_conv(u, k2):
    l = u.shape[1]
    uf = jnp.fft.rfft(u.astype(jnp.float32), n=2 * l, axis=1)
    kf = jnp.fft.rfft(k2, n=2 * l, axis=0)
    return jnp.fft.irfft(uf * kf[None], n=2 * l, axis=1)[:, :l].astype(u.dtype)


def hyena_mixer(p, lp):
    b, l, _ = p.shape
    x0, x1, v = jnp.split(dwconv1d(p, lp['hy_conv']), 3, axis=-1)
    z = v * x1
    k2 = hyena_filter(l, lp['hy_w1'], lp['hy_b1'], lp['hy_freq'], lp['hy_w2'], lp['hy_b2'], lp['hy_w3'])
    y = (fft_long_conv(z, k2) + z * lp['hy_bias']) * x0
    gw = HYENA_WIDTH // HYENA_GROUPS
    y = rmsnorm(y.reshape(b, l, HYENA_GROUPS, gw), lp['hy_out_g'].reshape(HYENA_GROUPS, gw))
    return y.reshape(b, l, HYENA_WIDTH)


def to_heads(t, d):
    b, l, _ = t.shape
    return t.reshape(b, l, GLA_HEADS, d).transpose(0, 2, 1, 3)


def _flip(t):
    return jnp.flip(t, axis=2)


def gla_log_gate(r, w, bias):
    return jax.nn.log_sigmoid((r @ w + bias).astype(jnp.float32)) / GLA_GATE_TAU


def gla_chunk_stats(k, v, log_a):
    f32 = jnp.float32
    b, h, l, _ = k.shape
    n = l // GLA_CHUNK
    kc = k.astype(f32).reshape(b, h, n, GLA_CHUNK, GLA_DK)
    vc = v.astype(f32).reshape(b, h, n, GLA_CHUNK, GLA_DV)
    cum = jnp.cumsum(log_a.reshape(b, h, n, GLA_CHUNK, GLA_DK), axis=3)
    last = cum[:, :, :, -1:, :]
    u = jnp.einsum('bhncd,bhnce->bhnde', kc * jnp.exp(last - cum), vc)
    decay = jnp.exp(last[:, :, :, 0, :])
    return kc, vc, cum, u, decay


def gla_state_scan(u, decay, s0, keep_prev):
    def step(s, inp):
        dec, du = inp
        return dec[..., None] * s + du, (s if keep_prev else None)
    return lax.scan(step, s0, (jnp.moveaxis(decay, 2, 0), jnp.moveaxis(u, 2, 0)))


def gla_scan(q, k, v, log_a, s0, strict):
    b, h, l, _ = q.shape
    kc, vc, cum, u, decay = gla_chunk_stats(k, v, log_a)
    s_fin, s_prev = gla_state_scan(u, decay, s0, True)
    qc = (q.astype(jnp.float32) * GLA_DK ** -0.5).reshape(b, h, -1, GLA_CHUNK, GLA_DK)
    qe = qc * jnp.exp(cum)
    scores = jnp.einsum('bhnid,bhnjd->bhnij', qe, kc * jnp.exp(-cum))
    mask = jnp.tril(jnp.ones((GLA_CHUNK, GLA_CHUNK), dtype=bool), k=-1 if strict else 0)
    scores = jnp.where(mask, scores, 0.0)
    out = (jnp.einsum('bhnij,bhnje->bhnie', scores, vc)
           + jnp.einsum('bhnid,bhnde->bhnie', qe, jnp.moveaxis(s_prev, 0, 2)))
    return out.reshape(b, h, l, GLA_DV), s_fin


def gla_final_state(k, v, log_a, s0):
    _, _, _, u, decay = gla_chunk_stats(k, v, log_a)
    s_fin, _ = gla_state_scan(u, decay, s0, False)
    return s_fin


def gla_state_inputs(pg, lp):
    k = to_heads(pg[..., :GLA_QK], GLA_DK)
    v = to_heads(pg[..., GLA_QK:GLA_QK + GLA_V], GLA_DV)
    r_f = pg[..., GLA_QK + GLA_V:GLA_QK + GLA_V + GLA_RANK]
    r_b = pg[..., GLA_QK + GLA_V + GLA_RANK:GLA_STATE_COLS]
    la_f = to_heads(gla_log_gate(r_f, lp['gla_wg_f'], lp['gla_bg_f']), GLA_DK)
    la_b = to_heads(gla_log_gate(r_b, lp['gla_wg_b'], lp['gla_bg_b']), GLA_DK)
    return k, v, la_f, la_b


def gla_context_states(pg, lp):
    k, v, la_f, la_b = gla_state_inputs(pg, lp)
    s0 = jnp.zeros((k.shape[0], GLA_HEADS, GLA_DK, GLA_DV), jnp.float32)
    return gla_final_state(k, v, la_f, s0), gla_final_state(_flip(k), _flip(v), _flip(la_b), s0)


def gla_mixer(pg, lp, s0_f, s0_b):
    k, v, la_f, la_b = gla_state_inputs(pg, lp)
    q = to_heads(pg[..., GLA_STATE_COLS:GLA_STATE_COLS + GLA_QK], GLA_DK)
    g = pg[..., GLA_STATE_COLS + GLA_QK:]
    o_f, s_f = gla_scan(q, k, v, la_f, s0_f, False)
    o_b, s_b = gla_scan(_flip(q), _flip(k), _flip(v), _flip(la_b), s0_b, True)
    o = rmsnorm(o_f + _flip(o_b), lp['gla_out_g'])
    b, _, l, _ = o.shape
    o = o.transpose(0, 2, 1, 3).reshape(b, l, GLA_V).astype(pg.dtype)
    return o * jax.nn.silu(g), s_f, s_b


def mixing_block(h, lp, s0_f, s0_b):
    p = h @ lp['w_in']
    hy = hyena_mixer(p[..., :HYENA_PROJ], lp)
    o, s_f, s_b = gla_mixer(p[..., HYENA_PROJ:], lp, s0_f, s0_b)
    return jnp.concatenate([hy, o], axis=-1) @ lp['w_out'], s_f, s_b


def conv_ffn(h, lp, rows, cols):
    a, u = jnp.split(h @ lp['ffn_w_up'], 2, axis=-1)
    a = dwconv2d(a, lp['ffn_conv'], rows, cols)
    return (jax.nn.silu(a) * u) @ lp['ffn_w_down']


def trunk_layer(x, ctx_x, c, c_ctx, lp, rows, last):
    sa, sca, ga, sf, scf, gf = [m[:, None, :] for m in adaln_params(c, lp['w_mod'], lp['b_mod'])]
    csa, csca, cga, csf, cscf, cgf = adaln_params(c_ctx, lp['w_mod'], lp['b_mod'])
    hc = modulate(rmsnorm(ctx_x, lp['g_attn']), csa, csca)
    if last:
        pgc = hc @ lp['w_in'][:, HYENA_PROJ:HYENA_PROJ + GLA_STATE_COLS]
        s_f, s_b = gla_context_states(pgc, lp)
    else:
        s0 = jnp.zeros((ctx_x.shape[0], GLA_HEADS, GLA_DK, GLA_DV), jnp.float32)
        mix_c, s_f, s_b = mixing_block(hc, lp, s0, s0)
        ctx_x = ctx_x + cga * mix_c
        ctx_x = ctx_x + cgf * conv_ffn(modulate(rmsnorm(ctx_x, lp['g_ffn']), csf, cscf), lp, 1, ctx_x.shape[1])
    h = modulate(rmsnorm(x, lp['g_attn']), sa, sca)
    mix, _, _ = mixing_block(h, lp, s_f, s_b)
    x = x + ga * mix
    x = x + gf * conv_ffn(modulate(rmsnorm(x, lp['g_ffn']), sf, scf), lp, rows, GRID_W)
    return x, ctx_x


def setup_inputs(seed: int = 0) -> dict:
    key = jax.random.key(seed)
    ks = jax.random.split(key, 28)

    def nrm(k, shape, scale):
        return jax.random.normal(k, shape, jnp.float32) * scale

    d = D_MODEL
    return {
        'x': nrm(ks[0], (BATCH, SEQ, d), 1.0),
        'c': nrm(ks[1], (BATCH, d), 1.0),
        'ctx': nrm(ks[2], (BATCH, CTX_LEN, d), 1.0),
        'c_ctx': nrm(ks[3], (d,), 1.0),
        'w_mod': nrm(ks[4], (DEPTH, d, 6 * d), 0.5 * d ** -0.5),
        'b_mod': nrm(ks[5], (DEPTH, 6 * d), 0.02),
        'g_attn': 1.0 + nrm(ks[6], (DEPTH, d), 0.02),
        'w_in': nrm(ks[7], (DEPTH, d, IN_WIDTH), d ** -0.5),
        'hy_conv': nrm(ks[8], (DEPTH, 3, HYENA_PROJ), 3 ** -0.5),
        'hy_w1': nrm(ks[9], (DEPTH, FILTER_EMB, FILTER_HIDDEN), FILTER_EMB ** -0.5),
        'hy_b1': nrm(ks[10], (DEPTH, FILTER_HIDDEN), 0.1),
        'hy_freq': 1.0 + nrm(ks[11], (DEPTH, FILTER_HIDDEN), 0.02),
        'hy_w2': nrm(ks[12], (DEPTH, FILTER_HIDDEN, FILTER_HIDDEN), FILTER_HIDDEN ** -0.5),
        'hy_b2': nrm(ks[13], (DEPTH, FILTER_HIDDEN), 0.1),
        'hy_w3': nrm(ks[14], (DEPTH, FILTER_HIDDEN, 2 * HYENA_WIDTH), FILTER_HIDDEN ** -0.5),
        'hy_bias': nrm(ks[15], (DEPTH, HYENA_WIDTH), 0.5),
        'hy_out_g': 1.0 + nrm(ks[16], (DEPTH, HYENA_WIDTH), 0.02),
        'gla_wg_f': nrm(ks[17], (DEPTH, GLA_RANK, GLA_QK), GLA_RANK ** -0.5),
        'gla_bg_f': nrm(ks[18], (DEPTH, GLA_QK), 0.1),
        'gla_wg_b': nrm(ks[19], (DEPTH, GLA_RANK, GLA_QK), GLA_RANK ** -0.5),
        'gla_bg_b': nrm(ks[20], (DEPTH, GLA_QK), 0.1),
        'gla_out_g': 1.0 + nrm(ks[21], (DEPTH, GLA_DV), 0.02),
        'w_out': nrm(ks[22], (DEPTH, MIX_WIDTH, d), MIX_WIDTH ** -0.5),
        'g_ffn': 1.0 + nrm(ks[23], (DEPTH, d), 0.02),
        'ffn_w_up': nrm(ks[24], (DEPTH, d, 2 * D_FF), d ** -0.5),
        'ffn_conv': nrm(ks[25], (DEPTH, 3, 3, D_FF), 9 ** -0.5),
        'ffn_w_down': nrm(ks[26], (DEPTH, D_FF, d), D_FF ** -0.5),
        'g_final': 1.0 + nrm(ks[27], (d,), 0.02),
    }


def reference(x, c, ctx, c_ctx, w_mod, b_mod, g_attn, w_in, hy_conv, hy_w1, hy_b1, hy_freq, hy_w2, hy_b2, hy_w3,
              hy_bias, hy_out_g, gla_wg_f, gla_bg_f, gla_wg_b, gla_bg_b, gla_out_g, w_out, g_ffn, ffn_w_up,
              ffn_conv, ffn_w_down, g_final):
    rows = x.shape[1] // GRID_W
    ctx_x = ctx
    for l in range(DEPTH):
        lp = {
            'w_mod': w_mod[l], 'b_mod': b_mod[l], 'g_attn': g_attn[l], 'w_in': w_in[l],
            'hy_conv': hy_conv[l], 'hy_w1': hy_w1[l], 'hy_b1': hy_b1[l], 'hy_freq': hy_freq[l],
            'hy_w2': hy_w2[l], 'hy_b2': hy_b2[l], 'hy_w3': hy_w3[l], 'hy_bias': hy_bias[l],
            'hy_out_g': hy_out_g[l], 'gla_wg_f': gla_wg_f[l], 'gla_bg_f': gla_bg_f[l],
            'gla_wg_b': gla_wg_b[l], 'gla_bg_b': gla_bg_b[l], 'gla_out_g': gla_out_g[l],
            'w_out': w_out[l], 'g_ffn': g_ffn[l], 'ffn_w_up': ffn_w_up[l], 'ffn_conv': ffn_conv[l],
            'ffn_w_down': ffn_w_down[l],
        }
        x, ctx_x = trunk_layer(x, ctx_x, c, c_ctx, lp, rows, l == DEPTH - 1)
    return rmsnorm(x, g_final)
```

```python
import functools
import math

import numpy as np
import jax
import jax.numpy as jnp
from jax import lax
from jax.experimental import pallas as pl
from jax.experimental.pallas import tpu as pltpu

F32 = jnp.float32
BF16 = jnp.bfloat16
HIGHEST = lax.Precision.HIGHEST

LANES = 128
SUBLANES = 8
VMEM_BYTES_V7X = 64 * 1024 * 1024
VMEM_LIMIT = VMEM_BYTES_V7X - 12 * 1024 * 1024

NORM_EPS = 1e-6
HYENA_GROUP = 128
FILTER_BANDS = 16
FILTER_HIDDEN = 64
FILTER_TARGET = 1e-2
FILTER_FAST_PCT = 0.3
FILTER_SLOW_PCT = 1.5
GLA_HEADS = 8
GLA_DK = 64
GLA_DV = 128
GLA_RANK = 16
GLA_GATE_TAU = 16.0
GLA_CHUNK = 64
GRID_W = 64
DFT_B = 128


def _params(*sem):
    return pltpu.CompilerParams(dimension_semantics=sem, vmem_limit_bytes=VMEM_LIMIT)


def _silu(x):
    return x * jax.nn.sigmoid(x)


def _adaln_kernel(c_ref, w_ref, b_ref, o_ref):
    s = _silu(c_ref[...]).astype(BF16)
    o_ref[...] = jnp.dot(s, w_ref[...].astype(BF16), preferred_element_type=F32) + b_ref[...]


def _adaln(cc, w_mod, b_mod):
    d, n = w_mod.shape
    tn = 1024
    return pl.pallas_call(
        _adaln_kernel,
        grid=(n // tn,),
        in_specs=[pl.BlockSpec((SUBLANES, d), lambda j: (0, 0)),
                  pl.BlockSpec((d, tn), lambda j: (0, j)),
                  pl.BlockSpec((1, tn), lambda j: (0, j))],
        out_specs=pl.BlockSpec((SUBLANES, tn), lambda j: (0, j)),
        out_shape=jax.ShapeDtypeStruct((SUBLANES, n), F32),
        compiler_params=_params("arbitrary"),
        name="adaln",
    )(cc, w_mod, b_mod.reshape(1, n))


def _norm_mod(x, g, shift, scale):
    ms = jnp.mean(x * x, axis=-1, keepdims=True)
    return (x * lax.rsqrt(ms + NORM_EPS) * g) * (1.0 + scale) + shift


def _in_proj_kernel(x_ref, g_ref, sh_ref, sc_ref, w_ref, o_ref, h_ref):
    @pl.when(pl.program_id(1) == 0)
    def _():
        h_ref[...] = _norm_mod(x_ref[...], g_ref[...], sh_ref[...], sc_ref[...]).astype(BF16)

    o_ref[...] = jnp.dot(h_ref[...], w_ref[...], preferred_element_type=F32)


def _in_proj(x, g, shift, scale, w):
    m, d = x.shape
    n = w.shape[1]
    tm = min(m, 512)
    tn = 896
    vec = pl.BlockSpec((1, d), lambda i, j: (0, 0))
    return pl.pallas_call(
        _in_proj_kernel,
        grid=(m // tm, n // tn),
        in_specs=[pl.BlockSpec((tm, d), lambda i, j: (i, 0)), vec, vec, vec,
                  pl.BlockSpec((d, tn), lambda i, j: (0, j))],
        out_specs=pl.BlockSpec((tm, tn), lambda i, j: (i, j)),
        out_shape=jax.ShapeDtypeStruct((m, n), F32),
        scratch_shapes=[pltpu.VMEM((tm, d), BF16)],
        compiler_params=_params("parallel", "arbitrary"),
        name="in_proj",
    )(x, g, shift, scale, w)


def _hy_pre_kernel(x0_ref, x1_ref, v_ref, x0p_ref, x1p_ref, vp_ref, x0n_ref, x1n_ref, vn_ref,
                   w0_ref, w1_ref, wv_ref, z_ref, x0c_ref):
    i = pl.program_id(0)
    first = i == 0
    last = i == pl.num_programs(0) - 1
    tt = x0_ref.shape[0]
    row = lax.broadcasted_iota(jnp.int32, x0_ref.shape, 0)

    def conv(ref, prev_ref, next_ref, w_ref):
        x = ref[...]
        prev_row = jnp.where(first, 0.0, prev_ref[SUBLANES - 1:SUBLANES, :])
        next_row = jnp.where(last, 0.0, next_ref[0:1, :])
        below = jnp.where(row == 0, prev_row, pltpu.roll(x, 1, 0))
        above = jnp.where(row == tt - 1, next_row, pltpu.roll(x, tt - 1, 0))
        return w_ref[0:1, :] * below + w_ref[1:2, :] * x + w_ref[2:3, :] * above

    x0c_ref[...] = conv(x0_ref, x0p_ref, x0n_ref, w0_ref)
    z_ref[...] = conv(v_ref, vp_ref, vn_ref, wv_ref) * conv(x1_ref, x1p_ref, x1n_ref, w1_ref)


def _hy_pre(p, hy_conv, width):
    l = p.shape[0]
    tt = min(l, 1024)
    cb = 512
    ncb = width // cb
    nb8 = l // SUBLANES
    t8 = tt // SUBLANES

    def main(g):
        return pl.BlockSpec((tt, cb), lambda i, j: (i, g * ncb + j))

    def prev(g):
        return pl.BlockSpec((SUBLANES, cb), lambda i, j: (jnp.maximum(i * t8 - 1, 0), g * ncb + j))

    def nxt(g):
        return pl.BlockSpec((SUBLANES, cb), lambda i, j: (jnp.minimum((i + 1) * t8, nb8 - 1), g * ncb + j))

    def wspec(g):
        return pl.BlockSpec((3, cb), lambda i, j: (0, g * ncb + j))

    out = pl.BlockSpec((tt, cb), lambda i, j: (i, j))
    return pl.pallas_call(
        _hy_pre_kernel,
        grid=(l // tt, ncb),
        in_specs=[main(0), main(1), main(2), prev(0), prev(1), prev(2), nxt(0), nxt(1), nxt(2),
                  wspec(0), wspec(1), wspec(2)],
        out_specs=[out, out],
        out_shape=[jax.ShapeDtypeStruct((l, width), F32), jax.ShapeDtypeStruct((l, width), F32)],
        compiler_params=_params("parallel", "parallel"),
        name="hy_pre",
    )(p, p, p, p, p, p, p, p, p, hy_conv, hy_conv, hy_conv)


def _filter_kernel(w1t_ref, w1c_ref, w1s_ref, b1_ref, fr_ref, w2_ref, b2_ref, w3_ref, delta_ref, o_ref, *, seq_len):
    tr = o_ref.shape[0]
    j = pl.program_id(0) * tr + lax.broadcasted_iota(jnp.int32, (tr, LANES), 0)
    pos_i = jnp.where(j < seq_len, j, 2 * seq_len - j)
    pos = pos_i.astype(F32)
    t = pos / float(max(seq_len - 1, 1))
    lane = lax.broadcasted_iota(jnp.int32, (tr, LANES), 1)
    band = jnp.where(lane < FILTER_BANDS, lane + 1, 0).astype(F32)
    ang = (2.0 * math.pi / seq_len) * pos * band
    fr = fr_ref[...]
    pre = (t[:, 0:1] * w1t_ref[...]
           + jnp.dot(jnp.cos(ang), w1c_ref[...], precision=HIGHEST, preferred_element_type=F32)
           + jnp.dot(jnp.sin(ang), w1s_ref[...], precision=HIGHEST, preferred_element_type=F32)
           + b1_ref[...])
    h = jnp.sin(fr * pre)
    h = jnp.sin(fr * (jnp.dot(h, w2_ref[...], precision=HIGHEST, preferred_element_type=F32) + b2_ref[...]))
    hh = jnp.dot(h.astype(BF16), w3_ref[...].astype(BF16), preferred_element_type=F32)
    decay = jnp.exp(-t[:, 0:1] * delta_ref[...])
    valid = pos_i[:, 0:1] != seq_len
    o_ref[...] = jnp.where(valid, hh * decay, 0.0)


def _hyena_filter2(seq_len, w1, b1, freq, w2, b2, w3, width):
    n2 = 2 * seq_len
    tr = min(seq_len, 1024)
    nfwd = seq_len // tr
    hid = FILTER_HIDDEN
    w1c = jnp.zeros((LANES, hid), F32).at[:FILTER_BANDS].set(w1[1:1 + FILTER_BANDS])
    w1s = jnp.zeros((LANES, hid), F32).at[:FILTER_BANDS].set(w1[1 + FILTER_BANDS:1 + 2 * FILTER_BANDS])
    delta = jnp.abs(jnp.linspace(math.log(FILTER_TARGET) / FILTER_SLOW_PCT,
                                 math.log(FILTER_TARGET) / FILTER_FAST_PCT, width, dtype=F32)).reshape(1, width)

    def full(shape):
        return pl.BlockSpec(shape, lambda i: (0, 0))

    return pl.pallas_call(
        functools.partial(_filter_kernel, seq_len=seq_len),
        grid=(n2 // tr,),
        in_specs=[full((1, hid)), full((LANES, hid)), full((LANES, hid)), full((1, hid)), full((1, hid)),
                  full((hid, hid)), full((1, hid)),
                  pl.BlockSpec((hid, width), lambda i: (0, jnp.where(i < nfwd, 0, 1))),
                  full((1, width))],
        out_specs=pl.BlockSpec((tr, width), lambda i: (i, 0)),
        out_shape=jax.ShapeDtypeStruct((n2, width), F32),
        compiler_params=_params("parallel"),
        name="hyena_filter",
    )(w1[0:1], w1c, w1s, b1.reshape(1, hid), freq.reshape(1, hid), w2, b2.reshape(1, hid), w3, delta)


def _dft_tables(n_a, a_in):
    n = n_a * DFT_B
    b = np.arange(DFT_B)[:, None, None]
    k1 = np.arange(n_a)[None, :, None]
    a = np.arange(a_in)[None, None, :]
    e = np.exp(-2j * np.pi * ((k1 * (DFT_B * a + b)) % n) / n)
    g = np.concatenate([e.real, e.imag], axis=1)
    return jnp.asarray(g, F32)


def _idft_tables(n_a, a_out):
    n = n_a * DFT_B
    b = np.arange(DFT_B)[:, None, None]
    a = np.arange(a_out)[None, :, None]
    k1 = np.arange(n_a)[None, None, :]
    e = np.exp(2j * np.pi * ((k1 * (DFT_B * a + b)) % n) / n) / n
    h = np.concatenate([e.real, -e.imag], axis=2)
    return jnp.asarray(h, F32)


def _dft128_embed():
    i = np.arange(DFT_B)
    f = np.exp(-2j * np.pi * ((i[:, None] * i[None, :]) % DFT_B) / DFT_B)
    fwd = np.block([[f.real, -f.imag], [f.imag, f.real]])
    inv = np.block([[f.real, f.imag], [-f.imag, f.real]])
    return jnp.asarray(fwd, F32), jnp.asarray(inv, F32)


def _fa_kernel(x_ref, g_ref, o_ref):
    n_a = o_ref.shape[1]
    for j in range(SUBLANES):
        xs = x_ref[:, j, :].astype(BF16)
        r = jnp.dot(g_ref[j], xs, preferred_element_type=F32)
        o_ref[0, :, j, :] = r[:n_a]
        o_ref[1, :, j, :] = r[n_a:]


def _dft_stage_a(x3, g_tab):
    a_in, _, c = x3.shape
    n_a = g_tab.shape[1] // 2
    return pl.pallas_call(
        _fa_kernel,
        grid=(c // LANES, DFT_B // SUBLANES),
        in_specs=[pl.BlockSpec((a_in, SUBLANES, LANES), lambda ci, bi: (0, bi, ci)),
                  pl.BlockSpec((SUBLANES, 2 * n_a, a_in), lambda ci, bi: (bi, 0, 0))],
        out_specs=pl.BlockSpec((2, n_a, SUBLANES, LANES), lambda ci, bi: (0, 0, bi, ci)),
        out_shape=jax.ShapeDtypeStruct((2, n_a, DFT_B, c), F32),
        compiler_params=_params("parallel", "parallel"),
        name="dft_stage_a",
    )(x3, g_tab)


def _fbk_kernel(b_ref, wf_ref, o_ref):
    for i in range(b_ref.shape[1]):
        x = jnp.concatenate([b_ref[0, i], b_ref[1, i]], axis=0).astype(BF16)
        z = jnp.dot(wf_ref[...], x, preferred_element_type=F32)
        o_ref[0, i] = z[:DFT_B]
        o_ref[1, i] = z[DFT_B:]


def _fb_kernel(b_ref, kf_ref, wf_ref, wi_ref, o_ref):
    for i in range(b_ref.shape[1]):
        x = jnp.concatenate([b_ref[0, i], b_ref[1, i]], axis=0).astype(BF16)
        z = jnp.dot(wf_ref[...], x, preferred_element_type=F32)
        zr, zi = z[:DFT_B], z[DFT_B:]
        kr, ki = kf_ref[0, i], kf_ref[1, i]
        y = jnp.concatenate([zr * kr - zi * ki, zr * ki + zi * kr], axis=0).astype(BF16)
        cc = jnp.dot(wi_ref[...], y, preferred_element_type=F32)
        o_ref[0, i] = cc[:DFT_B]
        o_ref[1, i] = cc[DFT_B:]


def _spec_block(kb, cb):
    return pl.BlockSpec((2, kb, DFT_B, cb), lambda ki, ci: (0, ki, 0, ci))


def _dft_stage_b_filter(bk, wf):
    _, n_a, _, c = bk.shape
    kb, cb = min(n_a, 4), 512
    return pl.pallas_call(
        _fbk_kernel,
        grid=(n_a // kb, c // cb),
        in_specs=[_spec_block(kb, cb), pl.BlockSpec((2 * DFT_B, 2 * DFT_B), lambda ki, ci: (0, 0))],
        out_specs=_spec_block(kb, cb),
        out_shape=jax.ShapeDtypeStruct(bk.shape, F32),
        compiler_params=_params("parallel", "parallel"),
        name="dft_stage_b_filter",
    )(bk, wf)


def _dft_stage_b(bz, kf, wf, wi):
    _, n_a, _, c = bz.shape
    kb, cb = min(n_a, 4), 512
    mat = pl.BlockSpec((2 * DFT_B, 2 * DFT_B), lambda ki, ci: (0, 0))
    return pl.pallas_call(
        _fb_kernel,
        grid=(n_a // kb, c // cb),
        in_specs=[_spec_block(kb, cb), _spec_block(kb, cb), mat, mat],
        out_specs=_spec_block(kb, cb),
        out_shape=jax.ShapeDtypeStruct(bz.shape, F32),
        compiler_params=_params("parallel", "parallel"),
        name="dft_stage_b",
    )(bz, kf, wf, wi)


def _hyena_epilogue(y, z, x0, bias, gain):
    yv = (y + z * bias) * x0
    ms = jnp.mean(yv * yv, axis=-1, keepdims=True)
    return yv * lax.rsqrt(ms + NORM_EPS) * gain


def _fc_kernel(d_ref, h_ref, z_ref, x0_ref, bias_ref, gain_ref, o_ref, y_ref):
    for j in range(SUBLANES):
        x = jnp.concatenate([d_ref[0, :, j, :], d_ref[1, :, j, :]], axis=0).astype(BF16)
        y_ref[:, j, :] = jnp.dot(h_ref[j], x, preferred_element_type=F32)
    o_ref[...] = _hyena_epilogue(y_ref[...], z_ref[...], x0_ref[...], bias_ref[...], gain_ref[...]).astype(o_ref.dtype)


def _idft_stage_c(dz, h_tab, z3, x03, bias, gain):
    _, n_a, _, c = dz.shape
    a_out = h_tab.shape[1]
    blk = pl.BlockSpec((a_out, SUBLANES, LANES), lambda ci, bi: (0, bi, ci))
    vec = pl.BlockSpec((1, 1, LANES), lambda ci, bi: (0, 0, ci))
    return pl.pallas_call(
        _fc_kernel,
        grid=(c // LANES, DFT_B // SUBLANES),
        in_specs=[pl.BlockSpec((2, n_a, SUBLANES, LANES), lambda ci, bi: (0, 0, bi, ci)),
                  pl.BlockSpec((SUBLANES, a_out, 2 * n_a), lambda ci, bi: (bi, 0, 0)),
                  blk, blk, vec, vec],
        out_specs=blk,
        out_shape=jax.ShapeDtypeStruct((a_out, DFT_B, c), F32),
        scratch_shapes=[pltpu.VMEM((a_out, SUBLANES, LANES), F32)],
        compiler_params=_params("parallel", "parallel"),
        name="idft_stage_c",
    )(dz, h_tab, z3, x03, bias.reshape(1, 1, c), gain.reshape(1, 1, c))


def _hyena_long(z, x0c, k2, bias, gain):
    l, c = z.shape
    n_a = 2 * l // DFT_B
    wf, wi = _dft128_embed()
    wf, wi = wf.astype(BF16), wi.astype(BF16)
    g_z = _dft_tables(n_a, n_a // 2).astype(BF16)
    g_k = _dft_tables(n_a, n_a).astype(BF16)
    h_tab = _idft_tables(n_a, n_a // 2).astype(BF16)
    kf = _dft_stage_b_filter(_dft_stage_a(k2.reshape(n_a, DFT_B, c), g_k), wf)
    bz = _dft_stage_a(z.reshape(n_a // 2, DFT_B, c), g_z)
    dz = _dft_stage_b(bz, kf, wf, wi)
    hy = _idft_stage_c(dz, h_tab, z.reshape(n_a // 2, DFT_B, c), x0c.reshape(n_a // 2, DFT_B, c), bias, gain)
    return hy.reshape(l, c)


def _ctx_conv_kernel(z_ref, k_ref, x0_ref, fz_ref, fk_ref, hi_ref, bias_ref, gain_ref, o_ref):
    n = k_ref.shape[0]
    zs = jnp.dot(fz_ref[...], z_ref[...].astype(BF16), preferred_element_type=F32)
    ks = jnp.dot(fk_ref[...], k_ref[...].astype(BF16), preferred_element_type=F32)
    zr, zi, kr, ki = zs[:n], zs[n:], ks[:n], ks[n:]
    y = jnp.concatenate([zr * kr - zi * ki, zr * ki + zi * kr], axis=0).astype(BF16)
    yt = jnp.dot(hi_ref[...], y, preferred_element_type=F32)
    o_ref[...] = _hyena_epilogue(yt, z_ref[...], x0_ref[...], bias_ref[...], gain_ref[...]).astype(o_ref.dtype)


def _hyena_short(z, x0c, k2, bias, gain):
    l, c = z.shape
    n = 2 * l
    k = np.arange(n)[:, None]
    t = np.arange(n)[None, :]
    e = np.exp(-2j * np.pi * ((k * t) % n) / n)
    fk = jnp.asarray(np.concatenate([e.real, e.imag], axis=0), F32).astype(BF16)
    fz = fk[:, :l]
    ei = np.conj(e[:, :l]).T / n
    hi = jnp.asarray(np.concatenate([ei.real, -ei.imag], axis=1), F32).astype(BF16)
    col = lambda rows: pl.BlockSpec((rows, LANES), lambda ci: (0, ci))
    full = lambda a: pl.BlockSpec(a.shape, lambda ci: (0, 0))
    return pl.pallas_call(
        _ctx_conv_kernel,
        grid=(c // LANES,),
        in_specs=[col(l), col(n), col(l), full(fz), full(fk), full(hi), col(1), col(1)],
        out_specs=col(l),
        out_shape=jax.ShapeDtypeStruct((l, c), F32),
        compiler_params=_params("parallel"),
        name="hyena_short",
    )(z, k2, x0c, fz, fk, hi, bias.reshape(1, c), gain.reshape(1, c))


def _gla_kernel(*refs, reverse, combine):
    if combine:
        (k_ref, q_ref, v_ref, r_ref, wg_ref, bg_ref, s0_ref, of_ref, gate_ref, gout_ref,
         o_ref, sfin_ref, st_ref, ob_ref) = refs
    else:
        k_ref, q_ref, v_ref, r_ref, wg_ref, bg_ref, s0_ref, o_ref, sfin_ref, st_ref = refs
        ob_ref = o_ref
    rows_blk = k_ref.shape[0]
    nc = rows_blk // GLA_CHUNK

    @pl.when(pl.program_id(0) == 0)
    def _():
        st_ref[...] = s0_ref[...]

    ri = lax.broadcasted_iota(jnp.int32, (GLA_CHUNK, GLA_CHUNK), 0)
    ci = lax.broadcasted_iota(jnp.int32, (GLA_CHUNK, GLA_CHUNK), 1)
    if reverse:
        tri = (ci >= ri).astype(F32)
        keep = ci > ri
    else:
        tri = (ci <= ri).astype(F32)
        keep = ci <= ri
    lane = lax.broadcasted_iota(jnp.int32, (1, LANES), 1)
    head_mask = (lane < GLA_DK, lane >= GLA_DK)
    nt = (((1,), (1,)), ((), ()))
    tn = (((0,), (0,)), ((), ()))

    def chunk(idx, carry):
        cidx = (nc - 1 - idx) if reverse else idx
        rows = pl.ds(pl.multiple_of(cidx * GLA_CHUNK, GLA_CHUNK), GLA_CHUNK)
        pre = jnp.dot(r_ref[rows, :], wg_ref[...], precision=HIGHEST, preferred_element_type=F32) + bg_ref[...]
        la = jax.nn.log_sigmoid(pre) / GLA_GATE_TAU
        cum = jnp.dot(tri, la, precision=HIGHEST, preferred_element_type=F32)
        tot = cum[0:1] if reverse else cum[GLA_CHUNK - 1:GLA_CHUNK]
        k = k_ref[rows, :]
        qe = q_ref[rows, :] * (GLA_DK ** -0.5) * jnp.exp(cum)
        ke = k * jnp.exp(-cum)
        kl = k * jnp.exp(tot - cum)
        dec = jnp.exp(tot)
        for h in range(GLA_HEADS):
            pr = slice((h // 2) * LANES, (h // 2 + 1) * LANES)
            hv = slice(h * GLA_DV, (h + 1) * GLA_DV)
            msk = head_mask[h % 2]
            qe_h = jnp.where(msk, qe[:, pr], 0.0).astype(BF16)
            kl_h = jnp.where(msk, kl[:, pr], 0.0).astype(BF16)
            sc = lax.dot_general(qe_h, ke[:, pr].astype(BF16), nt, preferred_element_type=F32)
            sc = jnp.where(keep, sc, 0.0).astype(BF16)
            v_h = v_ref[rows, hv].astype(BF16)
            st = st_ref[h]
            o_h = (jnp.dot(sc, v_h, preferred_element_type=F32)
                   + lax.dot_general(qe_h, st.astype(BF16), nt, preferred_element_type=F32))
            st_ref[h] = st * dec[:, pr] + lax.dot_general(v_h, kl_h, tn, preferred_element_type=F32)
            ob_ref[rows, hv] = o_h
        return carry

    lax.fori_loop(0, nc, chunk, 0)

    if combine:
        for h in range(GLA_HEADS):
            hv = slice(h * GLA_DV, (h + 1) * GLA_DV)
            o = of_ref[:, hv] + ob_ref[:, hv]
            ms = jnp.mean(o * o, axis=-1, keepdims=True)
            o = o * lax.rsqrt(ms + NORM_EPS) * gout_ref[...]
            o_ref[:, hv] = (o * _silu(gate_ref[:, hv])).astype(o_ref.dtype)

    @pl.when(pl.program_id(0) == pl.num_programs(0) - 1)
    def _():
        sfin_ref[...] = st_ref[...]


_COL_K, _COL_Q, _COL_V, _COL_G, _COL_R = 24, 28, 32, 40, 48


def _gla_pass(p, wg_pad, bg, s0, reverse, of=None, gout=None):
    l = p.shape[0]
    rb = min(l, 512)
    nblk = l // rb
    qk = GLA_HEADS * GLA_DK
    vw = GLA_HEADS * GLA_DV
    combine = of is not None
    rmap = (lambda i: nblk - 1 - i) if reverse else (lambda i: i)

    def cols(width, off128):
        return pl.BlockSpec((rb, width), lambda i: (rmap(i), off128 * LANES // width))

    state = pl.BlockSpec((GLA_HEADS, GLA_DV, LANES), lambda i: (0, 0, 0))
    in_specs = [cols(qk, _COL_K), cols(qk, _COL_Q), cols(vw, _COL_V), cols(LANES, _COL_R),
                pl.BlockSpec((LANES, qk), lambda i: (0, 0)), pl.BlockSpec((1, qk), lambda i: (0, 0)), state]
    args = [p, p, p, p, wg_pad, bg.reshape(1, qk), s0]
    scratch = [pltpu.VMEM((GLA_HEADS, GLA_DV, LANES), F32)]
    if combine:
        in_specs += [pl.BlockSpec((rb, vw), lambda i: (rmap(i), 0)), cols(vw, _COL_G),
                     pl.BlockSpec((1, GLA_DV), lambda i: (0, 0))]
        args += [of, p, gout.reshape(1, GLA_DV)]
        scratch.append(pltpu.VMEM((rb, vw), F32))
    out_dtype = BF16 if combine else F32
    return pl.pallas_call(
        functools.partial(_gla_kernel, reverse=reverse, combine=combine),
        grid=(nblk,),
        in_specs=in_specs,
        out_specs=[pl.BlockSpec((rb, vw), lambda i: (rmap(i), 0)), state],
        out_shape=[jax.ShapeDtypeStruct((l, vw), out_dtype),
                   jax.ShapeDtypeStruct((GLA_HEADS, GLA_DV, LANES), F32)],
        scratch_shapes=scratch,
        compiler_params=_params("arbitrary"),
        name="gla_bwd" if reverse else "gla_fwd",
    )(*args)


def _out_proj_kernel(hy_ref, o_ref, w_ref, x_ref, ga_ref, g_ref, sh_ref, sc_ref, xo_ref, h_ref):
    half = hy_ref.shape[1]
    mix = (jnp.dot(hy_ref[...].astype(BF16), w_ref[:half, :], preferred_element_type=F32)
           + jnp.dot(o_ref[...], w_ref[half:, :], preferred_element_type=F32))
    xn = x_ref[...] + ga_ref[...] * mix
    xo_ref[...] = xn
    h_ref[...] = _norm_mod(xn, g_ref[...], sh_ref[...], sc_ref[...]).astype(BF16)


def _out_proj(hy, o, w, x, ga, g, shift, scale):
    m, d = x.shape
    half = hy.shape[1]
    tm = min(m, 256)
    vec = pl.BlockSpec((1, d), lambda i: (0, 0))
    row = lambda width: pl.BlockSpec((tm, width), lambda i: (i, 0))
    return pl.pallas_call(
        _out_proj_kernel,
        grid=(m // tm,),
        in_specs=[row(half), row(half), pl.BlockSpec((2 * half, d), lambda i: (0, 0)), row(d), vec, vec, vec, vec],
        out_specs=[row(d), row(d)],
        out_shape=[jax.ShapeDtypeStruct((m, d), F32), jax.ShapeDtypeStruct((m, d), BF16)],
        compiler_params=_params("parallel"),
        name="out_proj",
    )(hy, o, w, x, ga, g, shift, scale)


def _matmul_kernel(x_ref, w_ref, o_ref):
    o_ref[...] = jnp.dot(x_ref[...], w_ref[...], preferred_element_type=F32).astype(o_ref.dtype)


def _ffn_up(h, w):
    m, d = h.shape
    n = w.shape[1]
    tm = min(m, 1024)
    tn = 1024
    return pl.pallas_call(
        _matmul_kernel,
        grid=(m // tm, n // tn),
        in_specs=[pl.BlockSpec((tm, d), lambda i, j: (i, 0)), pl.BlockSpec((d, tn), lambda i, j: (0, j))],
        out_specs=pl.BlockSpec((tm, tn), lambda i, j: (i, j)),
        out_shape=jax.ShapeDtypeStruct((m, n), BF16),
        compiler_params=_params("parallel", "parallel"),
        name="ffn_up",
    )(h, w)


_HALO = GRID_W + SUBLANES


def _ffn_mid_kernel(a_ref, ap_ref, an_ref, u_ref, w_ref, o_ref, s_ref, *, vertical, cols):
    i = pl.program_id(0)
    tt = a_ref.shape[0]
    first = i == 0
    last = i == pl.num_programs(0) - 1
    s_ref[pl.ds(0, SUBLANES), :] = jnp.zeros((SUBLANES, s_ref.shape[1]), F32)
    s_ref[pl.ds(_HALO + tt + GRID_W, SUBLANES), :] = jnp.zeros((SUBLANES, s_ref.shape[1]), F32)
    s_ref[pl.ds(SUBLANES, GRID_W), :] = jnp.where(first, 0.0, ap_ref[...].astype(F32))
    s_ref[pl.ds(_HALO, tt), :] = a_ref[...].astype(F32)
    s_ref[pl.ds(_HALO + tt, GRID_W), :] = jnp.where(last, 0.0, an_ref[...].astype(F32))
    col = lax.broadcasted_iota(jnp.int32, o_ref.shape, 0) & (cols - 1)
    acc = None
    for dc in (-1, 0, 1):
        part = None
        for dr in ((-1, 0, 1) if vertical else (0,)):
            tap = w_ref[3 * (dr + 1) + (dc + 1):3 * (dr + 1) + (dc + 1) + 1, :]
            term = tap * s_ref[pl.ds(_HALO + GRID_W * dr + dc, tt), :]
            part = term if part is None else part + term
        if dc == -1:
            part = jnp.where(col == 0, 0.0, part)
        elif dc == 1:
            part = jnp.where(col == cols - 1, 0.0, part)
        acc = part if acc is None else acc + part
    o_ref[...] = (_silu(acc) * u_ref[...].astype(F32)).astype(o_ref.dtype)


def _ffn_mid(up, conv_w, cols):
    m = up.shape[0]
    dff = conv_w.shape[-1]
    vertical = m > cols
    assert cols & (cols - 1) == 0 and (cols == GRID_W or not vertical)
    tt = min(m, 512)
    assert tt % cols == 0
    cb = 512
    ncb = dff // cb
    t64 = tt // GRID_W
    n64 = m // GRID_W
    return pl.pallas_call(
        functools.partial(_ffn_mid_kernel, vertical=vertical, cols=cols),
        grid=(m // tt, ncb),
        in_specs=[pl.BlockSpec((tt, cb), lambda i, j: (i, j)),
                  pl.BlockSpec((GRID_W, cb), lambda i, j: (jnp.maximum(i * t64 - 1, 0), j)),
                  pl.BlockSpec((GRID_W, cb), lambda i, j: (jnp.minimum((i + 1) * t64, n64 - 1), j)),
                  pl.BlockSpec((tt, cb), lambda i, j: (i, ncb + j)),
                  pl.BlockSpec((9, cb), lambda i, j: (0, j))],
        out_specs=pl.BlockSpec((tt, cb), lambda i, j: (i, j)),
        out_shape=jax.ShapeDtypeStruct((m, dff), BF16),
        scratch_shapes=[pltpu.VMEM((tt + 2 * _HALO, cb), F32)],
        compiler_params=_params("parallel", "parallel"),
        name="ffn_mid",
    )(up, up, up, up, conv_w.reshape(9, dff))


def _ffn_down_kernel(g_ref, w_ref, x_ref, gate_ref, gf_ref, o_ref, acc_ref, *, final_norm):
    kk = pl.program_id(1)

    @pl.when(kk == 0)
    def _():
        acc_ref[...] = jnp.zeros_like(acc_ref)

    acc_ref[...] += jnp.dot(g_ref[...], w_ref[...], preferred_element_type=F32)

    @pl.when(kk == pl.num_programs(1) - 1)
    def _():
        xn = x_ref[...] + gate_ref[...] * acc_ref[...]
        if final_norm:
            ms = jnp.mean(xn * xn, axis=-1, keepdims=True)
            xn = xn * lax.rsqrt(ms + NORM_EPS) * gf_ref[...]
        o_ref[...] = xn


def _ffn_down(g, w, x, gate, g_final, final_norm):
    m, d = x.shape
    dff = g.shape[1]
    tm = min(m, 512)
    tk = 512
    vec = pl.BlockSpec((1, d), lambda i, k: (0, 0))
    return pl.pallas_call(
        functools.partial(_ffn_down_kernel, final_norm=final_norm),
        grid=(m // tm, dff // tk),
        in_specs=[pl.BlockSpec((tm, tk), lambda i, k: (i, k)), pl.BlockSpec((tk, d), lambda i, k: (k, 0)),
                  pl.BlockSpec((tm, d), lambda i, k: (i, 0)), vec, vec],
        out_specs=pl.BlockSpec((tm, d), lambda i, k: (i, 0)),
        out_shape=jax.ShapeDtypeStruct((m, d), F32),
        scratch_shapes=[pltpu.VMEM((tm, d), F32)],
        compiler_params=_params("parallel", "arbitrary"),
        name="ffn_down",
    )(g, w, x, gate, g_final)


def _permute_w_in(w_in, hy_proj):
    qk = GLA_HEADS * GLA_DK
    vw = GLA_HEADS * GLA_DV
    g0 = hy_proj
    k = w_in[:, g0:g0 + qk]
    v = w_in[:, g0 + qk:g0 + qk + vw]
    r = w_in[:, g0 + qk + vw:g0 + qk + vw + 2 * GLA_RANK]
    q = w_in[:, g0 + qk + vw + 2 * GLA_RANK:g0 + 2 * qk + vw + 2 * GLA_RANK]
    g = w_in[:, g0 + 2 * qk + vw + 2 * GLA_RANK:]
    pad = jnp.zeros((w_in.shape[0], LANES - 2 * GLA_RANK), w_in.dtype)
    return jnp.concatenate([w_in[:, :hy_proj], k, q, v, g, r, pad], axis=1).astype(BF16)


def _pad_gate_w(wg, lane0):
    return jnp.zeros((LANES, wg.shape[1]), F32).at[lane0:lane0 + GLA_RANK].set(wg)


def _mixer(p, lp, s0_f, s0_b, want_output):
    l = p.shape[0]
    width = lp['hy_bias'].shape[0]
    o_f, s_f = _gla_pass(p, lp['wg_f_pad'], lp['gla_bg_f'], s0_f, reverse=False)
    o, s_b = _gla_pass(p, lp['wg_b_pad'], lp['gla_bg_b'], s0_b, reverse=True, of=o_f, gout=lp['gla_out_g'])
    if not want_output:
        return None, None, s_f, s_b
    z, x0c = _hy_pre(p, lp['hy_conv'], width)
    k2 = _hyena_filter2(l, lp['hy_w1'], lp['hy_b1'], lp['hy_freq'], lp['hy_w2'], lp['hy_b2'], lp['hy_w3'], width)
    if 2 * l // DFT_B >= 2 * SUBLANES:
        hy = _hyena_long(z, x0c, k2, lp['hy_bias'], lp['hy_out_g'])
    else:
        hy = _hyena_short(z, x0c, k2, lp['hy_bias'], lp['hy_out_g'])
    return hy, o, s_f, s_b


def _ffn(h2, x, gate, lp, cols, g_final, final_norm):
    up = _ffn_up(h2, lp['ffn_w_up'])
    gact = _ffn_mid(up, lp['ffn_conv'], cols)
    return _ffn_down(gact, lp['ffn_w_down'], x, gate, g_final, final_norm)


def kernel(x, c, ctx, c_ctx, w_mod, b_mod, g_attn, w_in, hy_conv, hy_w1, hy_b1, hy_freq, hy_w2, hy_b2, hy_w3,
           hy_bias, hy_out_g, gla_wg_f, gla_bg_f, gla_wg_b, gla_bg_b, gla_out_g, w_out, g_ffn, ffn_w_up,
           ffn_conv, ffn_w_down, g_final):
    depth = w_mod.shape[0]
    batch, seq, d = x.shape
    assert batch == 1 and c.shape[0] == 1 and ctx.shape[0] == 1
    hy_proj = hy_conv.shape[-1]
    xs = x[0]
    cs = ctx[0]
    cc = jnp.zeros((SUBLANES, d), F32).at[0].set(c[0]).at[1].set(c_ctx)
    zero_state = jnp.zeros((GLA_HEADS, GLA_DV, LANES), F32)
    gfin = g_final.reshape(1, d)
    for l in range(depth):
        last = l == depth - 1
        lp = {
            'hy_conv': hy_conv[l], 'hy_w1': hy_w1[l], 'hy_b1': hy_b1[l], 'hy_freq': hy_freq[l],
            'hy_w2': hy_w2[l], 'hy_b2': hy_b2[l], 'hy_w3': hy_w3[l], 'hy_bias': hy_bias[l],
            'hy_out_g': hy_out_g[l], 'gla_bg_f': gla_bg_f[l], 'gla_bg_b': gla_bg_b[l],
            'gla_out_g': gla_out_g[l],
            'wg_f_pad': _pad_gate_w(gla_wg_f[l], 0), 'wg_b_pad': _pad_gate_w(gla_wg_b[l], GLA_RANK),
            'ffn_w_up': ffn_w_up[l].astype(BF16), 'ffn_conv': ffn_conv[l], 'ffn_w_down': ffn_w_down[l].astype(BF16),
        }
        w_in_p = _permute_w_in(w_in[l], hy_proj)
        w_out_b = w_out[l].astype(BF16)
        mod = _adaln(cc, w_mod[l], b_mod[l])
        sa, sca, ga, sf, scf, gf = [mod[0:1, i * d:(i + 1) * d] for i in range(6)]
        csa, csca, cga, csf, cscf, cgf = [mod[1:2, i * d:(i + 1) * d] for i in range(6)]
        ga_row = g_attn[l].reshape(1, d)
        gf_row = g_ffn[l].reshape(1, d)

        pc = _in_proj(cs, ga_row, csa, csca, w_in_p)
        hy_c, o_c, s_f, s_b = _mixer(pc, lp, zero_state, zero_state, want_output=not last)
        if not last:
            cs, h2c = _out_proj(hy_c, o_c, w_out_b, cs, cga, gf_row, csf, cscf)
            cs = _ffn(h2c, cs, cgf, lp, cs.shape[0], gfin, False)

        px = _in_proj(xs, ga_row, sa, sca, w_in_p)
        hy_x, o_x, _, _ = _mixer(px, lp, s_f, s_b, want_output=True)
        xs, h2 = _out_proj(hy_x, o_x, w_out_b, xs, ga, gf_row, sf, scf)
        xs = _ffn(h2, xs, gf, lp, GRID_W, gfin, last)
    return xs[None]
```

```python
import functools
import math

import numpy as np
import jax
import jax.numpy as jnp
from jax import lax
from jax.experimental import pallas as pl
from jax.experimental.pallas import tpu as pltpu

F32 = jnp.float32
BF16 = jnp.bfloat16
HIGHEST = lax.Precision.HIGHEST

LANES = 128
SUBLANES = 8
VMEM_BYTES_V7X = 64 * 1024 * 1024
VMEM_LIMIT = VMEM_BYTES_V7X - 12 * 1024 * 1024

NORM_EPS = 1e-6
HYENA_GROUP = 128
FILTER_BANDS = 16
FILTER_HIDDEN = 64
FILTER_TARGET = 1e-2
FILTER_FAST_PCT = 0.3
FILTER_SLOW_PCT = 1.5
GLA_HEADS = 8
GLA_DK = 64
GLA_DV = 128
GLA_RANK = 16
GLA_GATE_TAU = 16.0
GLA_CHUNK = 64
GRID_W = 64
DFT_B = 128


def _params(*sem):
    return pltpu.CompilerParams(dimension_semantics=sem, vmem_limit_bytes=VMEM_LIMIT)


def _silu(x):
    return x * jax.nn.sigmoid(x)


def _adaln_kernel(c_ref, w_ref, b_ref, o_ref):
    s = _silu(c_ref[...]).astype(BF16)
    o_ref[...] = jnp.dot(s, w_ref[...].astype(BF16), preferred_element_type=F32) + b_ref[...]


def _adaln(cc, w_mod, b_mod, layer):
    depth, d, n = w_mod.shape
    tn = 1024
    return pl.pallas_call(
        _adaln_kernel,
        grid=(n // tn,),
        in_specs=[pl.BlockSpec((SUBLANES, d), lambda j: (0, 0)),
                  pl.BlockSpec((None, d, tn), lambda j: (layer, 0, j)),
                  pl.BlockSpec((None, 1, tn), lambda j: (layer, 0, j))],
        out_specs=pl.BlockSpec((SUBLANES, tn), lambda j: (0, j)),
        out_shape=jax.ShapeDtypeStruct((SUBLANES, n), F32),
        compiler_params=_params("arbitrary"),
        name="adaln",
    )(cc, w_mod, b_mod.reshape(depth, 1, n))


def _norm_mod(x, g, shift, scale):
    ms = jnp.mean(x * x, axis=-1, keepdims=True)
    return (x * lax.rsqrt(ms + NORM_EPS) * g) * (1.0 + scale) + shift


def _in_proj_kernel(x_ref, g_ref, sh_ref, sc_ref, w_ref, o_ref, h_ref):
    @pl.when(pl.program_id(1) == 0)
    def _():
        h_ref[...] = _norm_mod(x_ref[...], g_ref[...], sh_ref[...], sc_ref[...]).astype(BF16)

    o_ref[...] = jnp.dot(h_ref[...], w_ref[...], preferred_element_type=F32)


def _in_proj(x, g, shift, scale, w, layer):
    m, d = x.shape
    n = w.shape[2]
    tm = min(m, 512)
    tn = 896
    vec = pl.BlockSpec((1, d), lambda i, j: (0, 0))
    return pl.pallas_call(
        _in_proj_kernel,
        grid=(m // tm, n // tn),
        in_specs=[pl.BlockSpec((tm, d), lambda i, j: (i, 0)), vec, vec, vec,
                  pl.BlockSpec((None, d, tn), lambda i, j: (layer, 0, j))],
        out_specs=pl.BlockSpec((tm, tn), lambda i, j: (i, j)),
        out_shape=jax.ShapeDtypeStruct((m, n), F32),
        scratch_shapes=[pltpu.VMEM((tm, d), BF16)],
        compiler_params=_params("parallel", "arbitrary"),
        name="in_proj",
    )(x, g, shift, scale, w)


def _hy_pre_kernel(x0_ref, x1_ref, v_ref, x0p_ref, x1p_ref, vp_ref, x0n_ref, x1n_ref, vn_ref,
                   w0_ref, w1_ref, wv_ref, z_ref, x0c_ref):
    i = pl.program_id(0)
    first = i == 0
    last = i == pl.num_programs(0) - 1
    tt = x0_ref.shape[0]
    row = lax.broadcasted_iota(jnp.int32, x0_ref.shape, 0)

    def conv(ref, prev_ref, next_ref, w_ref):
        x = ref[...]
        prev_row = jnp.where(first, 0.0, prev_ref[SUBLANES - 1:SUBLANES, :])
        next_row = jnp.where(last, 0.0, next_ref[0:1, :])
        below = jnp.where(row == 0, prev_row, pltpu.roll(x, 1, 0))
        above = jnp.where(row == tt - 1, next_row, pltpu.roll(x, tt - 1, 0))
        return w_ref[0:1, :] * below + w_ref[1:2, :] * x + w_ref[2:3, :] * above

    x0c_ref[...] = conv(x0_ref, x0p_ref, x0n_ref, w0_ref)
    z_ref[...] = conv(v_ref, vp_ref, vn_ref, wv_ref) * conv(x1_ref, x1p_ref, x1n_ref, w1_ref)


def _hy_pre(p, hy_conv, width):
    l = p.shape[0]
    tt = min(l, 1024)
    cb = 512
    ncb = width // cb
    nb8 = l // SUBLANES
    t8 = tt // SUBLANES

    def main(g):
        return pl.BlockSpec((tt, cb), lambda i, j: (i, g * ncb + j))

    def prev(g):
        return pl.BlockSpec((SUBLANES, cb), lambda i, j: (jnp.maximum(i * t8 - 1, 0), g * ncb + j))

    def nxt(g):
        return pl.BlockSpec((SUBLANES, cb), lambda i, j: (jnp.minimum((i + 1) * t8, nb8 - 1), g * ncb + j))

    def wspec(g):
        return pl.BlockSpec((3, cb), lambda i, j: (0, g * ncb + j))

    out = pl.BlockSpec((tt, cb), lambda i, j: (i, j))
    return pl.pallas_call(
        _hy_pre_kernel,
        grid=(l // tt, ncb),
        in_specs=[main(0), main(1), main(2), prev(0), prev(1), prev(2), nxt(0), nxt(1), nxt(2),
                  wspec(0), wspec(1), wspec(2)],
        out_specs=[out, out],
        out_shape=[jax.ShapeDtypeStruct((l, width), F32), jax.ShapeDtypeStruct((l, width), F32)],
        compiler_params=_params("parallel", "parallel"),
        name="hy_pre",
    )(p, p, p, p, p, p, p, p, p, hy_conv, hy_conv, hy_conv)


def _filter_hidden(j, seq_len, w1t_ref, w1c_ref, w1s_ref, b1_ref, fr_ref, w2_ref, b2_ref):
    pos_i = jnp.where(j < seq_len, j, 2 * seq_len - j)
    pos = pos_i.astype(F32)
    t = pos / float(max(seq_len - 1, 1))
    lane = lax.broadcasted_iota(jnp.int32, j.shape, 1)
    band = jnp.where(lane < FILTER_BANDS, lane + 1, 0).astype(F32)
    ang = (2.0 * math.pi / seq_len) * pos * band
    fr = fr_ref[...]
    pre = (t[:, 0:1] * w1t_ref[...]
           + jnp.dot(jnp.cos(ang), w1c_ref[...], precision=HIGHEST, preferred_element_type=F32)
           + jnp.dot(jnp.sin(ang), w1s_ref[...], precision=HIGHEST, preferred_element_type=F32)
           + b1_ref[...])
    h = jnp.sin(fr * pre)
    h = jnp.sin(fr * (jnp.dot(h, w2_ref[...], precision=HIGHEST, preferred_element_type=F32) + b2_ref[...]))
    return h, pos_i, t


def _filter_kernel(w1t_ref, w1c_ref, w1s_ref, b1_ref, fr_ref, w2_ref, b2_ref, w3_ref, delta_ref, o_ref, *, seq_len):
    tr = o_ref.shape[0]
    j = pl.program_id(0) * tr + lax.broadcasted_iota(jnp.int32, (tr, LANES), 0)
    h, pos_i, t = _filter_hidden(j, seq_len, w1t_ref, w1c_ref, w1s_ref, b1_ref, fr_ref, w2_ref, b2_ref)
    hh = jnp.dot(h.astype(BF16), w3_ref[...].astype(BF16), preferred_element_type=F32)
    decay = jnp.exp(-t[:, 0:1] * delta_ref[...])
    valid = pos_i[:, 0:1] != seq_len
    o_ref[...] = jnp.where(valid, hh * decay, 0.0)


def _filter_hidden_kernel(w1t_ref, w1c_ref, w1s_ref, b1_ref, fr_ref, w2_ref, b2_ref, o_ref, *, seq_len, n_a):
    tr = o_ref.shape[0]
    r = pl.program_id(0) * tr + lax.broadcasted_iota(jnp.int32, (tr, LANES), 0)
    shift = n_a.bit_length() - 1
    j = ((r & (n_a - 1)) * DFT_B) + (r >> shift)
    h, _, _ = _filter_hidden(j, seq_len, w1t_ref, w1c_ref, w1s_ref, b1_ref, fr_ref, w2_ref, b2_ref)
    o_ref[...] = h


def _filter_mlp_args(w1, b1, freq, w2, b2):
    hid = FILTER_HIDDEN
    w1c = jnp.zeros((LANES, hid), F32).at[:FILTER_BANDS].set(w1[1:1 + FILTER_BANDS])
    w1s = jnp.zeros((LANES, hid), F32).at[:FILTER_BANDS].set(w1[1 + FILTER_BANDS:1 + 2 * FILTER_BANDS])
    full = lambda shape: pl.BlockSpec(shape, lambda i: (0, 0))
    specs = [full((1, hid)), full((LANES, hid)), full((LANES, hid)), full((1, hid)), full((1, hid)),
             full((hid, hid)), full((1, hid))]
    args = (w1[0:1], w1c, w1s, b1.reshape(1, hid), freq.reshape(1, hid), w2, b2.reshape(1, hid))
    return specs, args


def _filter_delta(width):
    return jnp.abs(jnp.linspace(math.log(FILTER_TARGET) / FILTER_SLOW_PCT,
                                math.log(FILTER_TARGET) / FILTER_FAST_PCT, width, dtype=F32)).reshape(1, width)


def _hyena_filter2(seq_len, w1, b1, freq, w2, b2, w3, width):
    n2 = 2 * seq_len
    tr = min(seq_len, 1024)
    nfwd = seq_len // tr
    specs, args = _filter_mlp_args(w1, b1, freq, w2, b2)
    return pl.pallas_call(
        functools.partial(_filter_kernel, seq_len=seq_len),
        grid=(n2 // tr,),
        in_specs=specs + [pl.BlockSpec((FILTER_HIDDEN, width), lambda i: (0, jnp.where(i < nfwd, 0, 1))),
                          pl.BlockSpec((1, width), lambda i: (0, 0))],
        out_specs=pl.BlockSpec((tr, width), lambda i: (i, 0)),
        out_shape=jax.ShapeDtypeStruct((n2, width), F32),
        compiler_params=_params("parallel"),
        name="hyena_filter",
    )(*args, w3, _filter_delta(width))


def _filter_hidden_permuted(seq_len, w1, b1, freq, w2, b2):
    n2 = 2 * seq_len
    n_a = n2 // DFT_B
    tr = min(n2, 1024)
    specs, args = _filter_mlp_args(w1, b1, freq, w2, b2)
    return pl.pallas_call(
        functools.partial(_filter_hidden_kernel, seq_len=seq_len, n_a=n_a),
        grid=(n2 // tr,),
        in_specs=specs,
        out_specs=pl.BlockSpec((tr, FILTER_HIDDEN), lambda i: (i, 0)),
        out_shape=jax.ShapeDtypeStruct((n2, FILTER_HIDDEN), F32),
        compiler_params=_params("parallel"),
        name="hyena_filter_hidden",
    )(*args)


def _dft_tables(n_a, a_in):
    n = n_a * DFT_B
    b = np.arange(DFT_B)[:, None, None]
    k1 = np.arange(n_a)[None, :, None]
    a = np.arange(a_in)[None, None, :]
    e = np.exp(-2j * np.pi * ((k1 * (DFT_B * a + b)) % n) / n)
    g = np.concatenate([e.real, e.imag], axis=1)
    return jnp.asarray(g, F32)


def _idft_tables(n_a, a_out):
    n = n_a * DFT_B
    b = np.arange(DFT_B)[:, None, None]
    a = np.arange(a_out)[None, :, None]
    k1 = np.arange(n_a)[None, None, :]
    e = np.exp(2j * np.pi * ((k1 * (DFT_B * a + b)) % n) / n) / n
    h = np.concatenate([e.real, -e.imag], axis=2)
    return jnp.asarray(h, F32)


def _dft128_embed():
    i = np.arange(DFT_B)
    f = np.exp(-2j * np.pi * ((i[:, None] * i[None, :]) % DFT_B) / DFT_B)
    fwd = np.block([[f.real, -f.imag], [f.imag, f.real]])
    inv = np.block([[f.real, f.imag], [-f.imag, f.real]])
    return jnp.asarray(fwd, F32), jnp.asarray(inv, F32)


_PITCH = DFT_B + SUBLANES
_K1_BLOCK = 8


def _stage_rows(k1):
    return pl.ds(pl.multiple_of(k1 * _PITCH, SUBLANES), DFT_B)


def _single(shape, index_map):
    return pl.BlockSpec(shape, index_map, pipeline_mode=pl.Buffered(1))


def _spectrum_kernel(h_ref, w3f_ref, w3b_ref, delta_ref, g_ref, wf_ref, o_ref, sr_ref, si_ref, *, seq_len, n_a):
    s = pl.program_id(1)
    half = n_a // 2

    @pl.when(s == 0)
    def _():
        a_i = lax.broadcasted_iota(jnp.int32, (n_a, LANES), 0)

        def body(b, carry):
            base = pl.multiple_of(b * n_a, SUBLANES)
            hf = jnp.dot(h_ref[pl.ds(base, half), :].astype(BF16), w3f_ref[...].astype(BF16),
                         preferred_element_type=F32)
            hb = jnp.dot(h_ref[pl.ds(base + half, half), :].astype(BF16), w3b_ref[...].astype(BF16),
                         preferred_element_type=F32)
            hh = jnp.concatenate([hf, hb], axis=0)
            j = a_i * DFT_B + b
            pos_i = jnp.where(j < seq_len, j, 2 * seq_len - j)
            t = pos_i.astype(F32) / float(max(seq_len - 1, 1))
            k2 = jnp.where(pos_i != seq_len, hh * jnp.exp(-t * delta_ref[...]), 0.0)
            r = jnp.dot(g_ref[b], k2.astype(BF16), preferred_element_type=F32)
            sr_ref[pl.ds(b, n_a, stride=_PITCH), :] = r[:n_a]
            si_ref[pl.ds(b, n_a, stride=_PITCH), :] = r[n_a:]
            return carry

        lax.fori_loop(0, DFT_B, body, 0)

    @pl.when(s > 0)
    def _():
        k0 = (s - 1) * o_ref.shape[1]
        for i in range(o_ref.shape[1]):
            rows = _stage_rows(k0 + i)
            x = jnp.concatenate([sr_ref[rows, :], si_ref[rows, :]], axis=0).astype(BF16)
            z = jnp.dot(wf_ref[...], x, preferred_element_type=F32)
            o_ref[0, i] = z[:DFT_B]
            o_ref[1, i] = z[DFT_B:]


def _filter_spectrum(hperm, w3, seq_len, width, g_k, wf):
    n_a = 2 * seq_len // DFT_B
    kb = min(n_a, _K1_BLOCK)
    ncb = width // LANES
    hid = FILTER_HIDDEN
    return pl.pallas_call(
        functools.partial(_spectrum_kernel, seq_len=seq_len, n_a=n_a),
        grid=(ncb, 1 + n_a // kb),
        in_specs=[_single(hperm.shape, lambda ci, s: (0, 0)),
                  pl.BlockSpec((hid, LANES), lambda ci, s: (0, ci)),
                  pl.BlockSpec((hid, LANES), lambda ci, s: (0, ncb + ci)),
                  pl.BlockSpec((1, LANES), lambda ci, s: (0, ci)),
                  _single(g_k.shape, lambda ci, s: (0, 0, 0)),
                  _single(wf.shape, lambda ci, s: (0, 0))],
        out_specs=pl.BlockSpec((2, kb, DFT_B, LANES), lambda ci, s: (0, jnp.maximum(s - 1, 0), 0, ci)),
        out_shape=jax.ShapeDtypeStruct((2, n_a, DFT_B, width), F32),
        scratch_shapes=[pltpu.VMEM((n_a * _PITCH, LANES), F32), pltpu.VMEM((n_a * _PITCH, LANES), F32)],
        compiler_params=_params("parallel", "arbitrary"),
        name="filter_spectrum",
    )(hperm, w3, w3, _filter_delta(width), g_k, wf)


def _hyena_epilogue(y, z, x0, bias, gain):
    yv = (y + z * bias) * x0
    ms = jnp.mean(yv * yv, axis=-1, keepdims=True)
    return yv * lax.rsqrt(ms + NORM_EPS) * gain


_EPI_ROWS = 512


def _long_conv_kernel(z_ref, x0_ref, kf_ref, g_ref, h_ref, wf_ref, wi_ref, bias_ref, gain_ref, o_ref,
                      sr_ref, si_ref, *, n_a):
    s = pl.program_id(1)
    half = n_a // 2
    kb = kf_ref.shape[1]
    nkb = n_a // kb

    @pl.when(s == 0)
    def _():
        def body(b, carry):
            xs = z_ref[pl.ds(b, half, stride=DFT_B), :].astype(BF16)
            r = jnp.dot(g_ref[b], xs, preferred_element_type=F32)
            sr_ref[pl.ds(b, n_a, stride=_PITCH), :] = r[:n_a]
            si_ref[pl.ds(b, n_a, stride=_PITCH), :] = r[n_a:]
            return carry

        lax.fori_loop(0, DFT_B, body, 0)

    @pl.when((s > 0) & (s <= nkb))
    def _():
        k0 = (s - 1) * kb
        for i in range(kb):
            rows = _stage_rows(k0 + i)
            x = jnp.concatenate([sr_ref[rows, :], si_ref[rows, :]], axis=0).astype(BF16)
            zz = jnp.dot(wf_ref[...], x, preferred_element_type=F32)
            zr, zi = zz[:DFT_B], zz[DFT_B:]
            kr, ki = kf_ref[0, i], kf_ref[1, i]
            y = jnp.concatenate([zr * kr - zi * ki, zr * ki + zi * kr], axis=0).astype(BF16)
            cc = jnp.dot(wi_ref[...], y, preferred_element_type=F32)
            sr_ref[rows, :] = cc[:DFT_B]
            si_ref[rows, :] = cc[DFT_B:]

    @pl.when(s == nkb + 1)
    def _():
        def body(b, carry):
            x = jnp.concatenate([sr_ref[pl.ds(b, n_a, stride=_PITCH), :],
                                 si_ref[pl.ds(b, n_a, stride=_PITCH), :]], axis=0).astype(BF16)
            o_ref[pl.ds(b, half, stride=DFT_B), :] = jnp.dot(h_ref[b], x, preferred_element_type=F32)
            return carry

        lax.fori_loop(0, DFT_B, body, 0)

        step = min(_EPI_ROWS, o_ref.shape[0])

        def epi(i, carry):
            rows = pl.ds(pl.multiple_of(i * step, step), step)
            o_ref[rows, :] = _hyena_epilogue(o_ref[rows, :], z_ref[rows, :], x0_ref[rows, :],
                                             bias_ref[...], gain_ref[...])
            return carry

        lax.fori_loop(0, o_ref.shape[0] // step, epi, 0)


def _long_conv(z, x0c, kf, g_z, h_tab, wf, wi, bias, gain):
    l, c = z.shape
    n_a = 2 * l // DFT_B
    kb = min(n_a, _K1_BLOCK)
    nkb = n_a // kb
    col = lambda ci, s: (0, ci)
    return pl.pallas_call(
        functools.partial(_long_conv_kernel, n_a=n_a),
        grid=(c // LANES, nkb + 2),
        in_specs=[_single((l, LANES), col), _single((l, LANES), col),
                  pl.BlockSpec((2, kb, DFT_B, LANES), lambda ci, s: (0, jnp.clip(s - 1, 0, nkb - 1), 0, ci)),
                  _single(g_z.shape, lambda ci, s: (0, 0, 0)), _single(h_tab.shape, lambda ci, s: (0, 0, 0)),
                  _single(wf.shape, lambda ci, s: (0, 0)), _single(wi.shape, lambda ci, s: (0, 0)),
                  pl.BlockSpec((1, LANES), col), pl.BlockSpec((1, LANES), col)],
        out_specs=pl.BlockSpec((l, LANES), col),
        out_shape=jax.ShapeDtypeStruct((l, c), F32),
        scratch_shapes=[pltpu.VMEM((n_a * _PITCH, LANES), F32), pltpu.VMEM((n_a * _PITCH, LANES), F32)],
        compiler_params=_params("parallel", "arbitrary"),
        name="long_conv",
    )(z, x0c, kf, g_z, h_tab, wf, wi, bias.reshape(1, c), gain.reshape(1, c))


def _hyena_long(z, x0c, filt, bias, gain):
    l, c = z.shape
    n_a = 2 * l // DFT_B
    wf, wi = _dft128_embed()
    wf, wi = wf.astype(BF16), wi.astype(BF16)
    g_z = _dft_tables(n_a, n_a // 2).astype(BF16)
    g_k = _dft_tables(n_a, n_a).astype(BF16)
    h_tab = _idft_tables(n_a, n_a // 2).astype(BF16)
    hperm = _filter_hidden_permuted(l, filt['hy_w1'], filt['hy_b1'], filt['hy_freq'], filt['hy_w2'], filt['hy_b2'])
    kf = _filter_spectrum(hperm, filt['hy_w3'], l, c, g_k, wf)
    return _long_conv(z, x0c, kf, g_z, h_tab, wf, wi, bias, gain)


def _ctx_conv_kernel(z_ref, k_ref, x0_ref, fz_ref, fk_ref, hi_ref, bias_ref, gain_ref, o_ref):
    n = k_ref.shape[0]
    zs = jnp.dot(fz_ref[...], z_ref[...].astype(BF16), preferred_element_type=F32)
    ks = jnp.dot(fk_ref[...], k_ref[...].astype(BF16), preferred_element_type=F32)
    zr, zi, kr, ki = zs[:n], zs[n:], ks[:n], ks[n:]
    y = jnp.concatenate([zr * kr - zi * ki, zr * ki + zi * kr], axis=0).astype(BF16)
    yt = jnp.dot(hi_ref[...], y, preferred_element_type=F32)
    o_ref[...] = _hyena_epilogue(yt, z_ref[...], x0_ref[...], bias_ref[...], gain_ref[...]).astype(o_ref.dtype)


def _hyena_short(z, x0c, k2, bias, gain):
    l, c = z.shape
    n = 2 * l
    k = np.arange(n)[:, None]
    t = np.arange(n)[None, :]
    e = np.exp(-2j * np.pi * ((k * t) % n) / n)
    fk = jnp.asarray(np.concatenate([e.real, e.imag], axis=0), F32).astype(BF16)
    fz = fk[:, :l]
    ei = np.conj(e[:, :l]).T / n
    hi = jnp.asarray(np.concatenate([ei.real, -ei.imag], axis=1), F32).astype(BF16)
    col = lambda rows: pl.BlockSpec((rows, LANES), lambda ci: (0, ci))
    full = lambda a: pl.BlockSpec(a.shape, lambda ci: (0, 0))
    return pl.pallas_call(
        _ctx_conv_kernel,
        grid=(c // LANES,),
        in_specs=[col(l), col(n), col(l), full(fz), full(fk), full(hi), col(1), col(1)],
        out_specs=col(l),
        out_shape=jax.ShapeDtypeStruct((l, c), F32),
        compiler_params=_params("parallel"),
        name="hyena_short",
    )(z, k2, x0c, fz, fk, hi, bias.reshape(1, c), gain.reshape(1, c))


def _gla_kernel(*refs, reverse, combine):
    if combine:
        (k_ref, q_ref, v_ref, r_ref, wg_ref, bg_ref, s0_ref, of_ref, gate_ref, gout_ref,
         o_ref, sfin_ref, st_ref, ob_ref) = refs
    else:
        k_ref, q_ref, v_ref, r_ref, wg_ref, bg_ref, s0_ref, o_ref, sfin_ref, st_ref = refs
        ob_ref = o_ref
    rows_blk = k_ref.shape[0]
    nc = rows_blk // GLA_CHUNK

    @pl.when(pl.program_id(0) == 0)
    def _():
        st_ref[...] = s0_ref[...]

    ri = lax.broadcasted_iota(jnp.int32, (GLA_CHUNK, GLA_CHUNK), 0)
    ci = lax.broadcasted_iota(jnp.int32, (GLA_CHUNK, GLA_CHUNK), 1)
    if reverse:
        tri = (ci >= ri).astype(F32)
        keep = ci > ri
    else:
        tri = (ci <= ri).astype(F32)
        keep = ci <= ri
    lane = lax.broadcasted_iota(jnp.int32, (1, LANES), 1)
    head_mask = (lane < GLA_DK, lane >= GLA_DK)
    nt = (((1,), (1,)), ((), ()))
    tn = (((0,), (0,)), ((), ()))

    def chunk(idx, carry):
        cidx = (nc - 1 - idx) if reverse else idx
        rows = pl.ds(pl.multiple_of(cidx * GLA_CHUNK, GLA_CHUNK), GLA_CHUNK)
        pre = jnp.dot(r_ref[rows, :], wg_ref[...], precision=HIGHEST, preferred_element_type=F32) + bg_ref[...]
        la = jax.nn.log_sigmoid(pre) / GLA_GATE_TAU
        cum = jnp.dot(tri, la, precision=HIGHEST, preferred_element_type=F32)
        tot = cum[0:1] if reverse else cum[GLA_CHUNK - 1:GLA_CHUNK]
        k = k_ref[rows, :]
        qe = q_ref[rows, :] * (GLA_DK ** -0.5) * jnp.exp(cum)
        ke = k * jnp.exp(-cum)
        kl = k * jnp.exp(tot - cum)
        dec = jnp.exp(tot)
        for h in range(GLA_HEADS):
            pr = slice((h // 2) * LANES, (h // 2 + 1) * LANES)
            hv = slice(h * GLA_DV, (h + 1) * GLA_DV)
            msk = head_mask[h % 2]
            qe_h = jnp.where(msk, qe[:, pr], 0.0).astype(BF16)
            kl_h = jnp.where(msk, kl[:, pr], 0.0).astype(BF16)
            sc = lax.dot_general(qe_h, ke[:, pr].astype(BF16), nt, preferred_element_type=F32)
            sc = jnp.where(keep, sc, 0.0).astype(BF16)
            v_h = v_ref[rows, hv].astype(BF16)
            st = st_ref[h]
            o_h = (jnp.dot(sc, v_h, preferred_element_type=F32)
                   + lax.dot_general(qe_h, st.astype(BF16), nt, preferred_element_type=F32))
            st_ref[h] = st * dec[:, pr] + lax.dot_general(v_h, kl_h, tn, preferred_element_type=F32)
            ob_ref[rows, hv] = o_h
        return carry

    lax.fori_loop(0, nc, chunk, 0)

    if combine:
        for h in range(GLA_HEADS):
            hv = slice(h * GLA_DV, (h + 1) * GLA_DV)
            o = of_ref[:, hv] + ob_ref[:, hv]
            ms = jnp.mean(o * o, axis=-1, keepdims=True)
            o = o * lax.rsqrt(ms + NORM_EPS) * gout_ref[...]
            o_ref[:, hv] = (o * _silu(gate_ref[:, hv])).astype(o_ref.dtype)

    @pl.when(pl.program_id(0) == pl.num_programs(0) - 1)
    def _():
        sfin_ref[...] = st_ref[...]


_COL_K, _COL_Q, _COL_V, _COL_G, _COL_R = 24, 28, 32, 40, 48


def _gla_pass(p, wg_pad, bg, s0, reverse, of=None, gout=None):
    l = p.shape[0]
    rb = min(l, 512)
    nblk = l // rb
    qk = GLA_HEADS * GLA_DK
    vw = GLA_HEADS * GLA_DV
    combine = of is not None
    rmap = (lambda i: nblk - 1 - i) if reverse else (lambda i: i)

    def cols(width, off128):
        return pl.BlockSpec((rb, width), lambda i: (rmap(i), off128 * LANES // width))

    state = pl.BlockSpec((GLA_HEADS, GLA_DV, LANES), lambda i: (0, 0, 0))
    in_specs = [cols(qk, _COL_K), cols(qk, _COL_Q), cols(vw, _COL_V), cols(LANES, _COL_R),
                pl.BlockSpec((LANES, qk), lambda i: (0, 0)), pl.BlockSpec((1, qk), lambda i: (0, 0)), state]
    args = [p, p, p, p, wg_pad, bg.reshape(1, qk), s0]
    scratch = [pltpu.VMEM((GLA_HEADS, GLA_DV, LANES), F32)]
    if combine:
        in_specs += [pl.BlockSpec((rb, vw), lambda i: (rmap(i), 0)), cols(vw, _COL_G),
                     pl.BlockSpec((1, GLA_DV), lambda i: (0, 0))]
        args += [of, p, gout.reshape(1, GLA_DV)]
        scratch.append(pltpu.VMEM((rb, vw), F32))
    out_dtype = BF16 if combine else F32
    return pl.pallas_call(
        functools.partial(_gla_kernel, reverse=reverse, combine=combine),
        grid=(nblk,),
        in_specs=in_specs,
        out_specs=[pl.BlockSpec((rb, vw), lambda i: (rmap(i), 0)), state],
        out_shape=[jax.ShapeDtypeStruct((l, vw), out_dtype),
                   jax.ShapeDtypeStruct((GLA_HEADS, GLA_DV, LANES), F32)],
        scratch_shapes=scratch,
        compiler_params=_params("arbitrary"),
        name="gla_bwd" if reverse else "gla_fwd",
    )(*args)


def _out_proj_kernel(hy_ref, o_ref, w_ref, x_ref, ga_ref, g_ref, sh_ref, sc_ref, xo_ref, h_ref):
    half = hy_ref.shape[1]
    mix = (jnp.dot(hy_ref[...].astype(BF16), w_ref[:half, :], preferred_element_type=F32)
           + jnp.dot(o_ref[...], w_ref[half:, :], preferred_element_type=F32))
    xn = x_ref[...] + ga_ref[...] * mix
    xo_ref[...] = xn
    h_ref[...] = _norm_mod(xn, g_ref[...], sh_ref[...], sc_ref[...]).astype(BF16)


def _out_proj(hy, o, w, x, ga, g, shift, scale, layer):
    m, d = x.shape
    half = hy.shape[1]
    tm = min(m, 256)
    vec = pl.BlockSpec((1, d), lambda i: (0, 0))
    row = lambda width: pl.BlockSpec((tm, width), lambda i: (i, 0))
    return pl.pallas_call(
        _out_proj_kernel,
        grid=(m // tm,),
        in_specs=[row(half), row(half), pl.BlockSpec((None, 2 * half, d), lambda i: (layer, 0, 0)), row(d),
                  vec, vec, vec, vec],
        out_specs=[row(d), row(d)],
        out_shape=[jax.ShapeDtypeStruct((m, d), F32), jax.ShapeDtypeStruct((m, d), BF16)],
        compiler_params=_params("parallel"),
        name="out_proj",
    )(hy, o, w, x, ga, g, shift, scale)


def _matmul_kernel(x_ref, w_ref, o_ref):
    o_ref[...] = jnp.dot(x_ref[...], w_ref[...].astype(BF16), preferred_element_type=F32).astype(o_ref.dtype)


def _ffn_up(h, w, layer):
    m, d = h.shape
    n = w.shape[2]
    tm = min(m, 2048)
    tn = 512
    return pl.pallas_call(
        _matmul_kernel,
        grid=(m // tm, n // tn),
        in_specs=[pl.BlockSpec((tm, d), lambda i, j: (i, 0)),
                  pl.BlockSpec((None, d, tn), lambda i, j: (layer, 0, j))],
        out_specs=pl.BlockSpec((tm, tn), lambda i, j: (i, j)),
        out_shape=jax.ShapeDtypeStruct((m, n), BF16),
        compiler_params=_params("parallel", "parallel"),
        name="ffn_up",
    )(h, w)


_HALO = GRID_W + SUBLANES


def _ffn_mid_kernel(a_ref, ap_ref, an_ref, u_ref, w_ref, o_ref, s_ref, *, vertical, cols):
    i = pl.program_id(0)
    tt = a_ref.shape[0]
    first = i == 0
    last = i == pl.num_programs(0) - 1
    s_ref[pl.ds(0, SUBLANES), :] = jnp.zeros((SUBLANES, s_ref.shape[1]), F32)
    s_ref[pl.ds(_HALO + tt + GRID_W, SUBLANES), :] = jnp.zeros((SUBLANES, s_ref.shape[1]), F32)
    s_ref[pl.ds(SUBLANES, GRID_W), :] = jnp.where(first, 0.0, ap_ref[...].astype(F32))
    s_ref[pl.ds(_HALO, tt), :] = a_ref[...].astype(F32)
    s_ref[pl.ds(_HALO + tt, GRID_W), :] = jnp.where(last, 0.0, an_ref[...].astype(F32))
    col = lax.broadcasted_iota(jnp.int32, o_ref.shape, 0) & (cols - 1)
    acc = None
    for dc in (-1, 0, 1):
        part = None
        for dr in ((-1, 0, 1) if vertical else (0,)):
            tap = w_ref[3 * (dr + 1) + (dc + 1):3 * (dr + 1) + (dc + 1) + 1, :]
            term = tap * s_ref[pl.ds(_HALO + GRID_W * dr + dc, tt), :]
            part = term if part is None else part + term
        if dc == -1:
            part = jnp.where(col == 0, 0.0, part)
        elif dc == 1:
            part = jnp.where(col == cols - 1, 0.0, part)
        acc = part if acc is None else acc + part
    o_ref[...] = (_silu(acc) * u_ref[...].astype(F32)).astype(o_ref.dtype)


def _ffn_mid(up, conv_w, cols):
    m = up.shape[0]
    dff = conv_w.shape[-1]
    vertical = m > cols
    assert cols & (cols - 1) == 0 and (cols == GRID_W or not vertical)
    tt = min(m, 512)
    assert tt % cols == 0
    cb = 512
    ncb = dff // cb
    t64 = tt // GRID_W
    n64 = m // GRID_W
    return pl.pallas_call(
        functools.partial(_ffn_mid_kernel, vertical=vertical, cols=cols),
        grid=(m // tt, ncb),
        in_specs=[pl.BlockSpec((tt, cb), lambda i, j: (i, j)),
                  pl.BlockSpec((GRID_W, cb), lambda i, j: (jnp.maximum(i * t64 - 1, 0), j)),
                  pl.BlockSpec((GRID_W, cb), lambda i, j: (jnp.minimum((i + 1) * t64, n64 - 1), j)),
                  pl.BlockSpec((tt, cb), lambda i, j: (i, ncb + j)),
                  pl.BlockSpec((9, cb), lambda i, j: (0, j))],
        out_specs=pl.BlockSpec((tt, cb), lambda i, j: (i, j)),
        out_shape=jax.ShapeDtypeStruct((m, dff), BF16),
        scratch_shapes=[pltpu.VMEM((tt + 2 * _HALO, cb), F32)],
        compiler_params=_params("parallel", "parallel"),
        name="ffn_mid",
    )(up, up, up, up, conv_w.reshape(9, dff))


def _ffn_down_kernel(g_ref, w_ref, x_ref, gate_ref, gf_ref, o_ref, acc_ref, *, final_norm):
    kk = pl.program_id(1)

    @pl.when(kk == 0)
    def _():
        acc_ref[...] = jnp.zeros_like(acc_ref)

    acc_ref[...] += jnp.dot(g_ref[...], w_ref[...].astype(BF16), preferred_element_type=F32)

    @pl.when(kk == pl.num_programs(1) - 1)
    def _():
        xn = x_ref[...] + gate_ref[...] * acc_ref[...]
        if final_norm:
            ms = jnp.mean(xn * xn, axis=-1, keepdims=True)
            xn = xn * lax.rsqrt(ms + NORM_EPS) * gf_ref[...]
        o_ref[...] = xn


def _ffn_down(g, w, x, gate, g_final, final_norm, layer):
    m, d = x.shape
    dff = g.shape[1]
    tm = min(m, 1024)
    tk = 512
    vec = pl.BlockSpec((1, d), lambda i, k: (0, 0))
    return pl.pallas_call(
        functools.partial(_ffn_down_kernel, final_norm=final_norm),
        grid=(m // tm, dff // tk),
        in_specs=[pl.BlockSpec((tm, tk), lambda i, k: (i, k)),
                  pl.BlockSpec((None, tk, d), lambda i, k: (layer, k, 0)),
                  pl.BlockSpec((tm, d), lambda i, k: (i, 0), pipeline_mode=pl.Buffered(1)), vec, vec],
        out_specs=pl.BlockSpec((tm, d), lambda i, k: (i, 0)),
        out_shape=jax.ShapeDtypeStruct((m, d), F32),
        scratch_shapes=[pltpu.VMEM((tm, d), F32)],
        compiler_params=_params("parallel", "arbitrary"),
        name="ffn_down",
    )(g, w, x, gate, g_final)


def _permute_w_in(w_in, hy_proj):
    qk = GLA_HEADS * GLA_DK
    vw = GLA_HEADS * GLA_DV
    g0 = hy_proj
    k = w_in[..., g0:g0 + qk]
    v = w_in[..., g0 + qk:g0 + qk + vw]
    r = w_in[..., g0 + qk + vw:g0 + qk + vw + 2 * GLA_RANK]
    q = w_in[..., g0 + qk + vw + 2 * GLA_RANK:g0 + 2 * qk + vw + 2 * GLA_RANK]
    g = w_in[..., g0 + 2 * qk + vw + 2 * GLA_RANK:]
    pad = jnp.zeros(w_in.shape[:-1] + (LANES - 2 * GLA_RANK,), w_in.dtype)
    return jnp.concatenate([w_in[..., :hy_proj], k, q, v, g, r, pad], axis=-1).astype(BF16)


def _pad_gate_w(wg, lane0):
    return jnp.zeros((LANES, wg.shape[1]), F32).at[lane0:lane0 + GLA_RANK].set(wg)


def _mixer(p, lp, s0_f, s0_b, want_output):
    l = p.shape[0]
    width = lp['hy_bias'].shape[0]
    o_f, s_f = _gla_pass(p, lp['wg_f_pad'], lp['gla_bg_f'], s0_f, reverse=False)
    o, s_b = _gla_pass(p, lp['wg_b_pad'], lp['gla_bg_b'], s0_b, reverse=True, of=o_f, gout=lp['gla_out_g'])
    if not want_output:
        return None, None, s_f, s_b
    z, x0c = _hy_pre(p, lp['hy_conv'], width)
    if 2 * l // DFT_B >= 2 * SUBLANES:
        hy = _hyena_long(z, x0c, lp, lp['hy_bias'], lp['hy_out_g'])
    else:
        k2 = _hyena_filter2(l, lp['hy_w1'], lp['hy_b1'], lp['hy_freq'], lp['hy_w2'], lp['hy_b2'], lp['hy_w3'], width)
        hy = _hyena_short(z, x0c, k2, lp['hy_bias'], lp['hy_out_g'])
    return hy, o, s_f, s_b


def _ffn(h2, x, gate, lp, cols, g_final, final_norm, layer):
    up = _ffn_up(h2, lp['ffn_w_up'], layer)
    gact = _ffn_mid(up, lp['ffn_conv'], cols)
    return _ffn_down(gact, lp['ffn_w_down'], x, gate, g_final, final_norm, layer)


def kernel(x, c, ctx, c_ctx, w_mod, b_mod, g_attn, w_in, hy_conv, hy_w1, hy_b1, hy_freq, hy_w2, hy_b2, hy_w3,
           hy_bias, hy_out_g, gla_wg_f, gla_bg_f, gla_wg_b, gla_bg_b, gla_out_g, w_out, g_ffn, ffn_w_up,
           ffn_conv, ffn_w_down, g_final):
    depth = w_mod.shape[0]
    batch, seq, d = x.shape
    assert batch == 1 and c.shape[0] == 1 and ctx.shape[0] == 1
    hy_proj = hy_conv.shape[-1]
    xs = x[0]
    cs = ctx[0]
    cc = jnp.zeros((SUBLANES, d), F32).at[0].set(c[0]).at[1].set(c_ctx)
    zero_state = jnp.zeros((GLA_HEADS, GLA_DV, LANES), F32)
    gfin = g_final.reshape(1, d)
    w_in_p = _permute_w_in(w_in, hy_proj)
    w_out_b = w_out.astype(BF16)
    for l in range(depth):
        last = l == depth - 1
        lp = {
            'hy_conv': hy_conv[l], 'hy_w1': hy_w1[l], 'hy_b1': hy_b1[l], 'hy_freq': hy_freq[l],
            'hy_w2': hy_w2[l], 'hy_b2': hy_b2[l], 'hy_w3': hy_w3[l], 'hy_bias': hy_bias[l],
            'hy_out_g': hy_out_g[l], 'gla_bg_f': gla_bg_f[l], 'gla_bg_b': gla_bg_b[l],
            'gla_out_g': gla_out_g[l],
            'wg_f_pad': _pad_gate_w(gla_wg_f[l], 0), 'wg_b_pad': _pad_gate_w(gla_wg_b[l], GLA_RANK),
            'ffn_w_up': ffn_w_up, 'ffn_conv': ffn_conv[l], 'ffn_w_down': ffn_w_down,
        }
        mod = _adaln(cc, w_mod, b_mod, l)
        sa, sca, ga, sf, scf, gf = [mod[0:1, i * d:(i + 1) * d] for i in range(6)]
        csa, csca, cga, csf, cscf, cgf = [mod[1:2, i * d:(i + 1) * d] for i in range(6)]
        ga_row = g_attn[l].reshape(1, d)
        gf_row = g_ffn[l].reshape(1, d)

        pc = _in_proj(cs, ga_row, csa, csca, w_in_p, l)
        hy_c, o_c, s_f, s_b = _mixer(pc, lp, zero_state, zero_state, want_output=not last)
        if not last:
            cs, h2c = _out_proj(hy_c, o_c, w_out_b, cs, cga, gf_row, csf, cscf, l)
            cs = _ffn(h2c, cs, cgf, lp, cs.shape[0], gfin, False, l)

        px = _in_proj(xs, ga_row, sa, sca, w_in_p, l)
        hy_x, o_x, _, _ = _mixer(px, lp, s_f, s_b, want_output=True)
        xs, h2 = _out_proj(hy_x, o_x, w_out_b, xs, ga, gf_row, sf, scf, l)
        xs = _ffn(h2, xs, gf, lp, GRID_W, gfin, last, l)
    return xs[None]
```

```python
import functools
import math

import numpy as np
import jax
import jax.numpy as jnp
from jax import lax
from jax.experimental import pallas as pl
from jax.experimental.pallas import tpu as pltpu

F32 = jnp.float32
BF16 = jnp.bfloat16
HIGHEST = lax.Precision.HIGHEST

LANES = 128
SUBLANES = 8
VMEM_BYTES_V7X = 64 * 1024 * 1024
VMEM_LIMIT = VMEM_BYTES_V7X - 12 * 1024 * 1024

NORM_EPS = 1e-6
HYENA_GROUP = 128
FILTER_BANDS = 16
FILTER_HIDDEN = 64
FILTER_TARGET = 1e-2
FILTER_FAST_PCT = 0.3
FILTER_SLOW_PCT = 1.5
GLA_HEADS = 8
GLA_DK = 64
GLA_DV = 128
GLA_RANK = 16
GLA_GATE_TAU = 16.0
GLA_CHUNK = 64
GRID_W = 64
DFT_B = 128


def _params(*sem):
    return pltpu.CompilerParams(dimension_semantics=sem, vmem_limit_bytes=VMEM_LIMIT)


def _silu(x):
    return x * jax.nn.sigmoid(x)


def _adaln_kernel(c_ref, w_ref, b_ref, o_ref):
    s = _silu(c_ref[...]).astype(BF16)
    o_ref[...] = jnp.dot(s, w_ref[...].astype(BF16), preferred_element_type=F32) + b_ref[...]


def _adaln(cc, w_mod, b_mod, layer):
    depth, d, n = w_mod.shape
    tn = 1024
    return pl.pallas_call(
        _adaln_kernel,
        grid=(n // tn,),
        in_specs=[pl.BlockSpec((SUBLANES, d), lambda j: (0, 0)),
                  pl.BlockSpec((None, d, tn), lambda j: (layer, 0, j)),
                  pl.BlockSpec((None, 1, tn), lambda j: (layer, 0, j))],
        out_specs=pl.BlockSpec((SUBLANES, tn), lambda j: (0, j)),
        out_shape=jax.ShapeDtypeStruct((SUBLANES, n), F32),
        compiler_params=_params("arbitrary"),
        name="adaln",
    )(cc, w_mod, b_mod.reshape(depth, 1, n))


def _norm_mod(x, g, shift, scale):
    ms = jnp.mean(x * x, axis=-1, keepdims=True)
    return (x * lax.rsqrt(ms + NORM_EPS) * g) * (1.0 + scale) + shift


def _in_proj_kernel(x_ref, g_ref, sh_ref, sc_ref, w_ref, o_ref, h_ref):
    @pl.when(pl.program_id(1) == 0)
    def _():
        h_ref[...] = _norm_mod(x_ref[...], g_ref[...], sh_ref[...], sc_ref[...]).astype(BF16)

    o_ref[...] = jnp.dot(h_ref[...], w_ref[...], preferred_element_type=F32)


def _in_proj(x, g, shift, scale, w, layer):
    m, d = x.shape
    n = w.shape[2]
    tm = min(m, 512)
    tn = 896
    vec = pl.BlockSpec((1, d), lambda i, j: (0, 0))
    return pl.pallas_call(
        _in_proj_kernel,
        grid=(m // tm, n // tn),
        in_specs=[pl.BlockSpec((tm, d), lambda i, j: (i, 0)), vec, vec, vec,
                  pl.BlockSpec((None, d, tn), lambda i, j: (layer, 0, j))],
        out_specs=pl.BlockSpec((tm, tn), lambda i, j: (i, j)),
        out_shape=jax.ShapeDtypeStruct((m, n), F32),
        scratch_shapes=[pltpu.VMEM((tm, d), BF16)],
        compiler_params=_params("parallel", "arbitrary"),
        name="in_proj",
    )(x, g, shift, scale, w)


def _hy_pre_kernel(x0_ref, x1_ref, v_ref, x0p_ref, x1p_ref, vp_ref, x0n_ref, x1n_ref, vn_ref,
                   w0_ref, w1_ref, wv_ref, z_ref, x0c_ref):
    i = pl.program_id(0)
    first = i == 0
    last = i == pl.num_programs(0) - 1
    tt = x0_ref.shape[0]
    row = lax.broadcasted_iota(jnp.int32, x0_ref.shape, 0)

    def conv(ref, prev_ref, next_ref, w_ref):
        x = ref[...]
        prev_row = jnp.where(first, 0.0, prev_ref[SUBLANES - 1:SUBLANES, :])
        next_row = jnp.where(last, 0.0, next_ref[0:1, :])
        below = jnp.where(row == 0, prev_row, pltpu.roll(x, 1, 0))
        above = jnp.where(row == tt - 1, next_row, pltpu.roll(x, tt - 1, 0))
        return w_ref[0:1, :] * below + w_ref[1:2, :] * x + w_ref[2:3, :] * above

    x0c_ref[...] = conv(x0_ref, x0p_ref, x0n_ref, w0_ref)
    z_ref[...] = conv(v_ref, vp_ref, vn_ref, wv_ref) * conv(x1_ref, x1p_ref, x1n_ref, w1_ref)


def _hy_pre(p, hy_conv, width):
    l = p.shape[0]
    tt = min(l, 1024)
    cb = 512
    ncb = width // cb
    nb8 = l // SUBLANES
    t8 = tt // SUBLANES

    def main(g):
        return pl.BlockSpec((tt, cb), lambda i, j: (i, g * ncb + j))

    def prev(g):
        return pl.BlockSpec((SUBLANES, cb), lambda i, j: (jnp.maximum(i * t8 - 1, 0), g * ncb + j))

    def nxt(g):
        return pl.BlockSpec((SUBLANES, cb), lambda i, j: (jnp.minimum((i + 1) * t8, nb8 - 1), g * ncb + j))

    def wspec(g):
        return pl.BlockSpec((3, cb), lambda i, j: (0, g * ncb + j))

    out = pl.BlockSpec((tt, cb), lambda i, j: (i, j))
    return pl.pallas_call(
        _hy_pre_kernel,
        grid=(l // tt, ncb),
        in_specs=[main(0), main(1), main(2), prev(0), prev(1), prev(2), nxt(0), nxt(1), nxt(2),
                  wspec(0), wspec(1), wspec(2)],
        out_specs=[out, out],
        out_shape=[jax.ShapeDtypeStruct((l, width), F32), jax.ShapeDtypeStruct((l, width), F32)],
        compiler_params=_params("parallel", "parallel"),
        name="hy_pre",
    )(p, p, p, p, p, p, p, p, p, hy_conv, hy_conv, hy_conv)


def _filter_hidden(j, seq_len, w1t_ref, w1c_ref, w1s_ref, b1_ref, fr_ref, w2_ref, b2_ref):
    pos_i = jnp.where(j < seq_len, j, 2 * seq_len - j)
    pos = pos_i.astype(F32)
    t = pos / float(max(seq_len - 1, 1))
    lane = lax.broadcasted_iota(jnp.int32, j.shape, 1)
    band = jnp.where(lane < FILTER_BANDS, lane + 1, 0).astype(F32)
    ang = (2.0 * math.pi / seq_len) * pos * band
    fr = fr_ref[...]
    pre = (t[:, 0:1] * w1t_ref[...]
           + jnp.dot(jnp.cos(ang), w1c_ref[...], precision=HIGHEST, preferred_element_type=F32)
           + jnp.dot(jnp.sin(ang), w1s_ref[...], precision=HIGHEST, preferred_element_type=F32)
           + b1_ref[...])
    h = jnp.sin(fr * pre)
    h = jnp.sin(fr * (jnp.dot(h, w2_ref[...], precision=HIGHEST, preferred_element_type=F32) + b2_ref[...]))
    return h, pos_i, t


def _filter_kernel(w1t_ref, w1c_ref, w1s_ref, b1_ref, fr_ref, w2_ref, b2_ref, w3_ref, delta_ref, o_ref, *, seq_len):
    tr = o_ref.shape[0]
    j = pl.program_id(0) * tr + lax.broadcasted_iota(jnp.int32, (tr, LANES), 0)
    h, pos_i, t = _filter_hidden(j, seq_len, w1t_ref, w1c_ref, w1s_ref, b1_ref, fr_ref, w2_ref, b2_ref)
    hh = jnp.dot(h.astype(BF16), w3_ref[...].astype(BF16), preferred_element_type=F32)
    decay = jnp.exp(-t[:, 0:1] * delta_ref[...])
    valid = pos_i[:, 0:1] != seq_len
    o_ref[...] = jnp.where(valid, hh * decay, 0.0)


def _filter_hidden_kernel(w1t_ref, w1c_ref, w1s_ref, b1_ref, fr_ref, w2_ref, b2_ref, o_ref, *, seq_len, n_a):
    tr = o_ref.shape[0]
    r = pl.program_id(0) * tr + lax.broadcasted_iota(jnp.int32, (tr, LANES), 0)
    shift = n_a.bit_length() - 1
    j = ((r & (n_a - 1)) * DFT_B) + (r >> shift)
    h, _, _ = _filter_hidden(j, seq_len, w1t_ref, w1c_ref, w1s_ref, b1_ref, fr_ref, w2_ref, b2_ref)
    o_ref[...] = h


def _filter_mlp_args(w1, b1, freq, w2, b2):
    hid = FILTER_HIDDEN
    w1c = jnp.zeros((LANES, hid), F32).at[:FILTER_BANDS].set(w1[1:1 + FILTER_BANDS])
    w1s = jnp.zeros((LANES, hid), F32).at[:FILTER_BANDS].set(w1[1 + FILTER_BANDS:1 + 2 * FILTER_BANDS])
    full = lambda shape: pl.BlockSpec(shape, lambda i: (0, 0))
    specs = [full((1, hid)), full((LANES, hid)), full((LANES, hid)), full((1, hid)), full((1, hid)),
             full((hid, hid)), full((1, hid))]
    args = (w1[0:1], w1c, w1s, b1.reshape(1, hid), freq.reshape(1, hid), w2, b2.reshape(1, hid))
    return specs, args


def _filter_delta(width):
    return jnp.abs(jnp.linspace(math.log(FILTER_TARGET) / FILTER_SLOW_PCT,
                                math.log(FILTER_TARGET) / FILTER_FAST_PCT, width, dtype=F32)).reshape(1, width)


def _hyena_filter2(seq_len, w1, b1, freq, w2, b2, w3, width):
    n2 = 2 * seq_len
    tr = min(seq_len, 1024)
    nfwd = seq_len // tr
    specs, args = _filter_mlp_args(w1, b1, freq, w2, b2)
    return pl.pallas_call(
        functools.partial(_filter_kernel, seq_len=seq_len),
        grid=(n2 // tr,),
        in_specs=specs + [pl.BlockSpec((FILTER_HIDDEN, width), lambda i: (0, jnp.where(i < nfwd, 0, 1))),
                          pl.BlockSpec((1, width), lambda i: (0, 0))],
        out_specs=pl.BlockSpec((tr, width), lambda i: (i, 0)),
        out_shape=jax.ShapeDtypeStruct((n2, width), F32),
        compiler_params=_params("parallel"),
        name="hyena_filter",
    )(*args, w3, _filter_delta(width))


def _filter_hidden_permuted(seq_len, w1, b1, freq, w2, b2):
    n2 = 2 * seq_len
    n_a = n2 // DFT_B
    tr = min(n2, 1024)
    specs, args = _filter_mlp_args(w1, b1, freq, w2, b2)
    return pl.pallas_call(
        functools.partial(_filter_hidden_kernel, seq_len=seq_len, n_a=n_a),
        grid=(n2 // tr,),
        in_specs=specs,
        out_specs=pl.BlockSpec((tr, FILTER_HIDDEN), lambda i: (i, 0)),
        out_shape=jax.ShapeDtypeStruct((n2, FILTER_HIDDEN), F32),
        compiler_params=_params("parallel"),
        name="hyena_filter_hidden",
    )(*args)


def _dft_tables(n_a, a_in):
    n = n_a * DFT_B
    b = np.arange(DFT_B)[:, None, None]
    k1 = np.arange(n_a)[None, :, None]
    a = np.arange(a_in)[None, None, :]
    e = np.exp(-2j * np.pi * ((k1 * (DFT_B * a + b)) % n) / n)
    g = np.concatenate([e.real, e.imag], axis=1)
    return jnp.asarray(g, F32)


def _idft_tables(n_a, a_out):
    n = n_a * DFT_B
    b = np.arange(DFT_B)[:, None, None]
    a = np.arange(a_out)[None, :, None]
    k1 = np.arange(n_a)[None, None, :]
    e = np.exp(2j * np.pi * ((k1 * (DFT_B * a + b)) % n) / n) / n
    h = np.concatenate([e.real, -e.imag], axis=2)
    return jnp.asarray(h, F32)


def _dft128_embed():
    i = np.arange(DFT_B)
    f = np.exp(-2j * np.pi * ((i[:, None] * i[None, :]) % DFT_B) / DFT_B)
    fwd = np.block([[f.real, -f.imag], [f.imag, f.real]])
    inv = np.block([[f.real, f.imag], [-f.imag, f.real]])
    return jnp.asarray(fwd, F32), jnp.asarray(inv, F32)


_PITCH = DFT_B + SUBLANES
_K1_BLOCK = 8
_B_UNROLL = 8


def _stage_rows(k1):
    return pl.ds(pl.multiple_of(k1 * _PITCH, SUBLANES), DFT_B)


def _single(shape, index_map):
    return pl.BlockSpec(shape, index_map, pipeline_mode=pl.Buffered(1))


def _spectrum_kernel(h_ref, w3f_ref, w3b_ref, delta_ref, g_ref, wf_ref, o_ref, sr_ref, si_ref, *, seq_len, n_a):
    s = pl.program_id(1)
    half = n_a // 2

    @pl.when(s == 0)
    def _():
        a_i = lax.broadcasted_iota(jnp.int32, (n_a, LANES), 0)

        def body(b, carry):
            base = pl.multiple_of(b * n_a, SUBLANES)
            hf = jnp.dot(h_ref[pl.ds(base, half), :].astype(BF16), w3f_ref[...].astype(BF16),
                         preferred_element_type=F32)
            hb = jnp.dot(h_ref[pl.ds(base + half, half), :].astype(BF16), w3b_ref[...].astype(BF16),
                         preferred_element_type=F32)
            hh = jnp.concatenate([hf, hb], axis=0)
            j = a_i * DFT_B + b
            pos_i = jnp.where(j < seq_len, j, 2 * seq_len - j)
            t = pos_i.astype(F32) / float(max(seq_len - 1, 1))
            k2 = jnp.where(pos_i != seq_len, hh * jnp.exp(-t * delta_ref[...]), 0.0)
            r = jnp.dot(g_ref[b], k2.astype(BF16), preferred_element_type=F32)
            sr_ref[pl.ds(b, n_a, stride=_PITCH), :] = r[:n_a]
            si_ref[pl.ds(b, n_a, stride=_PITCH), :] = r[n_a:]
            return carry

        lax.fori_loop(0, DFT_B, body, 0, unroll=_B_UNROLL)

    @pl.when(s > 0)
    def _():
        k0 = (s - 1) * o_ref.shape[1]
        for i in range(o_ref.shape[1]):
            rows = _stage_rows(k0 + i)
            x = jnp.concatenate([sr_ref[rows, :], si_ref[rows, :]], axis=0).astype(BF16)
            z = jnp.dot(wf_ref[...], x, preferred_element_type=F32)
            o_ref[0, i] = z[:DFT_B]
            o_ref[1, i] = z[DFT_B:]


def _filter_spectrum(hperm, w3, seq_len, width, g_k, wf):
    n_a = 2 * seq_len // DFT_B
    kb = min(n_a, _K1_BLOCK)
    ncb = width // LANES
    hid = FILTER_HIDDEN
    return pl.pallas_call(
        functools.partial(_spectrum_kernel, seq_len=seq_len, n_a=n_a),
        grid=(ncb, 1 + n_a // kb),
        in_specs=[_single(hperm.shape, lambda ci, s: (0, 0)),
                  pl.BlockSpec((hid, LANES), lambda ci, s: (0, ci)),
                  pl.BlockSpec((hid, LANES), lambda ci, s: (0, ncb + ci)),
                  pl.BlockSpec((1, LANES), lambda ci, s: (0, ci)),
                  _single(g_k.shape, lambda ci, s: (0, 0, 0)),
                  _single(wf.shape, lambda ci, s: (0, 0))],
        out_specs=pl.BlockSpec((2, kb, DFT_B, LANES), lambda ci, s: (0, jnp.maximum(s - 1, 0), 0, ci)),
        out_shape=jax.ShapeDtypeStruct((2, n_a, DFT_B, width), F32),
        scratch_shapes=[pltpu.VMEM((n_a * _PITCH, LANES), F32), pltpu.VMEM((n_a * _PITCH, LANES), F32)],
        compiler_params=_params("parallel", "arbitrary"),
        name="filter_spectrum",
    )(hperm, w3, w3, _filter_delta(width), g_k, wf)


def _hyena_epilogue(y, z, x0, bias, gain):
    yv = (y + z * bias) * x0
    ms = jnp.mean(yv * yv, axis=-1, keepdims=True)
    return yv * lax.rsqrt(ms + NORM_EPS) * gain


_EPI_ROWS = 512


def _long_conv_kernel(z_ref, x0_ref, kf_ref, g_ref, h_ref, wf_ref, wi_ref, bias_ref, gain_ref, o_ref,
                      sr_ref, si_ref, *, n_a):
    s = pl.program_id(1)
    half = n_a // 2
    kb = kf_ref.shape[1]
    nkb = n_a // kb

    @pl.when(s == 0)
    def _():
        def body(b, carry):
            xs = z_ref[pl.ds(b, half, stride=DFT_B), :].astype(BF16)
            r = jnp.dot(g_ref[b], xs, preferred_element_type=F32)
            sr_ref[pl.ds(b, n_a, stride=_PITCH), :] = r[:n_a]
            si_ref[pl.ds(b, n_a, stride=_PITCH), :] = r[n_a:]
            return carry

        lax.fori_loop(0, DFT_B, body, 0, unroll=_B_UNROLL)

    @pl.when((s > 0) & (s <= nkb))
    def _():
        k0 = (s - 1) * kb
        for i in range(kb):
            rows = _stage_rows(k0 + i)
            x = jnp.concatenate([sr_ref[rows, :], si_ref[rows, :]], axis=0).astype(BF16)
            zz = jnp.dot(wf_ref[...], x, preferred_element_type=F32)
            zr, zi = zz[:DFT_B], zz[DFT_B:]
            kr, ki = kf_ref[0, i], kf_ref[1, i]
            y = jnp.concatenate([zr * kr - zi * ki, zr * ki + zi * kr], axis=0).astype(BF16)
            cc = jnp.dot(wi_ref[...], y, preferred_element_type=F32)
            sr_ref[rows, :] = cc[:DFT_B]
            si_ref[rows, :] = cc[DFT_B:]

    @pl.when(s == nkb + 1)
    def _():
        def body(b, carry):
            x = jnp.concatenate([sr_ref[pl.ds(b, n_a, stride=_PITCH), :],
                                 si_ref[pl.ds(b, n_a, stride=_PITCH), :]], axis=0).astype(BF16)
            o_ref[pl.ds(b, half, stride=DFT_B), :] = jnp.dot(h_ref[b], x, preferred_element_type=F32)
            return carry

        lax.fori_loop(0, DFT_B, body, 0, unroll=_B_UNROLL)

        step = min(_EPI_ROWS, o_ref.shape[0])

        def epi(i, carry):
            rows = pl.ds(pl.multiple_of(i * step, step), step)
            o_ref[rows, :] = _hyena_epilogue(o_ref[rows, :], z_ref[rows, :], x0_ref[rows, :],
                                             bias_ref[...], gain_ref[...])
            return carry

        lax.fori_loop(0, o_ref.shape[0] // step, epi, 0)


def _long_conv(z, x0c, kf, g_z, h_tab, wf, wi, bias, gain):
    l, c = z.shape
    n_a = 2 * l // DFT_B
    kb = min(n_a, _K1_BLOCK)
    nkb = n_a // kb
    col = lambda ci, s: (0, ci)
    return pl.pallas_call(
        functools.partial(_long_conv_kernel, n_a=n_a),
        grid=(c // LANES, nkb + 2),
        in_specs=[_single((l, LANES), col), _single((l, LANES), col),
                  pl.BlockSpec((2, kb, DFT_B, LANES), lambda ci, s: (0, jnp.clip(s - 1, 0, nkb - 1), 0, ci)),
                  _single(g_z.shape, lambda ci, s: (0, 0, 0)), _single(h_tab.shape, lambda ci, s: (0, 0, 0)),
                  _single(wf.shape, lambda ci, s: (0, 0)), _single(wi.shape, lambda ci, s: (0, 0)),
                  pl.BlockSpec((1, LANES), col), pl.BlockSpec((1, LANES), col)],
        out_specs=pl.BlockSpec((l, LANES), col),
        out_shape=jax.ShapeDtypeStruct((l, c), F32),
        scratch_shapes=[pltpu.VMEM((n_a * _PITCH, LANES), F32), pltpu.VMEM((n_a * _PITCH, LANES), F32)],
        compiler_params=_params("parallel", "arbitrary"),
        name="long_conv",
    )(z, x0c, kf, g_z, h_tab, wf, wi, bias.reshape(1, c), gain.reshape(1, c))


def _hyena_long(z, x0c, filt, bias, gain):
    l, c = z.shape
    n_a = 2 * l // DFT_B
    wf, wi = _dft128_embed()
    wf, wi = wf.astype(BF16), wi.astype(BF16)
    g_z = _dft_tables(n_a, n_a // 2).astype(BF16)
    g_k = _dft_tables(n_a, n_a).astype(BF16)
    h_tab = _idft_tables(n_a, n_a // 2).astype(BF16)
    hperm = _filter_hidden_permuted(l, filt['hy_w1'], filt['hy_b1'], filt['hy_freq'], filt['hy_w2'], filt['hy_b2'])
    kf = _filter_spectrum(hperm, filt['hy_w3'], l, c, g_k, wf)
    return _long_conv(z, x0c, kf, g_z, h_tab, wf, wi, bias, gain)


def _ctx_conv_kernel(z_ref, k_ref, x0_ref, fz_ref, fk_ref, hi_ref, bias_ref, gain_ref, o_ref):
    n = k_ref.shape[0]
    zs = jnp.dot(fz_ref[...], z_ref[...].astype(BF16), preferred_element_type=F32)
    ks = jnp.dot(fk_ref[...], k_ref[...].astype(BF16), preferred_element_type=F32)
    zr, zi, kr, ki = zs[:n], zs[n:], ks[:n], ks[n:]
    y = jnp.concatenate([zr * kr - zi * ki, zr * ki + zi * kr], axis=0).astype(BF16)
    yt = jnp.dot(hi_ref[...], y, preferred_element_type=F32)
    o_ref[...] = _hyena_epilogue(yt, z_ref[...], x0_ref[...], bias_ref[...], gain_ref[...]).astype(o_ref.dtype)


def _hyena_short(z, x0c, k2, bias, gain):
    l, c = z.shape
    n = 2 * l
    k = np.arange(n)[:, None]
    t = np.arange(n)[None, :]
    e = np.exp(-2j * np.pi * ((k * t) % n) / n)
    fk = jnp.asarray(np.concatenate([e.real, e.imag], axis=0), F32).astype(BF16)
    fz = fk[:, :l]
    ei = np.conj(e[:, :l]).T / n
    hi = jnp.asarray(np.concatenate([ei.real, -ei.imag], axis=1), F32).astype(BF16)
    col = lambda rows: pl.BlockSpec((rows, LANES), lambda ci: (0, ci))
    full = lambda a: pl.BlockSpec(a.shape, lambda ci: (0, 0))
    return pl.pallas_call(
        _ctx_conv_kernel,
        grid=(c // LANES,),
        in_specs=[col(l), col(n), col(l), full(fz), full(fk), full(hi), col(1), col(1)],
        out_specs=col(l),
        out_shape=jax.ShapeDtypeStruct((l, c), F32),
        compiler_params=_params("parallel"),
        name="hyena_short",
    )(z, k2, x0c, fz, fk, hi, bias.reshape(1, c), gain.reshape(1, c))


def _gla_kernel(*refs, reverse, combine):
    if combine:
        (k_ref, q_ref, v_ref, r_ref, wg_ref, bg_ref, s0_ref, of_ref, gate_ref, gout_ref,
         o_ref, sfin_ref, st_ref, ob_ref) = refs
    else:
        k_ref, q_ref, v_ref, r_ref, wg_ref, bg_ref, s0_ref, o_ref, sfin_ref, st_ref = refs
        ob_ref = o_ref
    rows_blk = k_ref.shape[0]
    nc = rows_blk // GLA_CHUNK

    @pl.when(pl.program_id(0) == 0)
    def _():
        st_ref[...] = s0_ref[...]

    ri = lax.broadcasted_iota(jnp.int32, (GLA_CHUNK, GLA_CHUNK), 0)
    ci = lax.broadcasted_iota(jnp.int32, (GLA_CHUNK, GLA_CHUNK), 1)
    if reverse:
        tri = (ci >= ri).astype(F32)
        keep = ci > ri
    else:
        tri = (ci <= ri).astype(F32)
        keep = ci <= ri
    lane = lax.broadcasted_iota(jnp.int32, (1, LANES), 1)
    head_mask = (lane < GLA_DK, lane >= GLA_DK)
    nt = (((1,), (1,)), ((), ()))
    tn = (((0,), (0,)), ((), ()))

    def chunk(idx, carry):
        cidx = (nc - 1 - idx) if reverse else idx
        rows = pl.ds(pl.multiple_of(cidx * GLA_CHUNK, GLA_CHUNK), GLA_CHUNK)
        pre = jnp.dot(r_ref[rows, :], wg_ref[...], precision=HIGHEST, preferred_element_type=F32) + bg_ref[...]
        la = jax.nn.log_sigmoid(pre) / GLA_GATE_TAU
        cum = jnp.dot(tri, la, precision=HIGHEST, preferred_element_type=F32)
        tot = cum[0:1] if reverse else cum[GLA_CHUNK - 1:GLA_CHUNK]
        k = k_ref[rows, :]
        qe = q_ref[rows, :] * (GLA_DK ** -0.5) * jnp.exp(cum)
        ke = k * jnp.exp(-cum)
        kl = k * jnp.exp(tot - cum)
        dec = jnp.exp(tot)
        for h in range(GLA_HEADS):
            pr = slice((h // 2) * LANES, (h // 2 + 1) * LANES)
            hv = slice(h * GLA_DV, (h + 1) * GLA_DV)
            msk = head_mask[h % 2]
            qe_h = jnp.where(msk, qe[:, pr], 0.0).astype(BF16)
            kl_h = jnp.where(msk, kl[:, pr], 0.0).astype(BF16)
            sc = lax.dot_general(qe_h, ke[:, pr].astype(BF16), nt, preferred_element_type=F32)
            sc = jnp.where(keep, sc, 0.0).astype(BF16)
            v_h = v_ref[rows, hv].astype(BF16)
            st = st_ref[h]
            o_h = (jnp.dot(sc, v_h, preferred_element_type=F32)
                   + lax.dot_general(qe_h, st.astype(BF16), nt, preferred_element_type=F32))
            st_ref[h] = st * dec[:, pr] + lax.dot_general(v_h, kl_h, tn, preferred_element_type=F32)
            ob_ref[rows, hv] = o_h
        return carry

    lax.fori_loop(0, nc, chunk, 0)

    if combine:
        for h in range(GLA_HEADS):
            hv = slice(h * GLA_DV, (h + 1) * GLA_DV)
            o = of_ref[:, hv] + ob_ref[:, hv]
            ms = jnp.mean(o * o, axis=-1, keepdims=True)
            o = o * lax.rsqrt(ms + NORM_EPS) * gout_ref[...]
            o_ref[:, hv] = (o * _silu(gate_ref[:, hv])).astype(o_ref.dtype)

    @pl.when(pl.program_id(0) == pl.num_programs(0) - 1)
    def _():
        sfin_ref[...] = st_ref[...]


_COL_K, _COL_Q, _COL_V, _COL_G, _COL_R = 24, 28, 32, 40, 48


def _gla_pass(p, wg_pad, bg, s0, reverse, of=None, gout=None):
    l = p.shape[0]
    rb = min(l, 512)
    nblk = l // rb
    qk = GLA_HEADS * GLA_DK
    vw = GLA_HEADS * GLA_DV
    combine = of is not None
    rmap = (lambda i: nblk - 1 - i) if reverse else (lambda i: i)

    def cols(width, off128):
        return pl.BlockSpec((rb, width), lambda i: (rmap(i), off128 * LANES // width))

    state = pl.BlockSpec((GLA_HEADS, GLA_DV, LANES), lambda i: (0, 0, 0))
    in_specs = [cols(qk, _COL_K), cols(qk, _COL_Q), cols(vw, _COL_V), cols(LANES, _COL_R),
                pl.BlockSpec((LANES, qk), lambda i: (0, 0)), pl.BlockSpec((1, qk), lambda i: (0, 0)), state]
    args = [p, p, p, p, wg_pad, bg.reshape(1, qk), s0]
    scratch = [pltpu.VMEM((GLA_HEADS, GLA_DV, LANES), F32)]
    if combine:
        in_specs += [pl.BlockSpec((rb, vw), lambda i: (rmap(i), 0)), cols(vw, _COL_G),
                     pl.BlockSpec((1, GLA_DV), lambda i: (0, 0))]
        args += [of, p, gout.reshape(1, GLA_DV)]
        scratch.append(pltpu.VMEM((rb, vw), F32))
    out_dtype = BF16 if combine else F32
    return pl.pallas_call(
        functools.partial(_gla_kernel, reverse=reverse, combine=combine),
        grid=(nblk,),
        in_specs=in_specs,
        out_specs=[pl.BlockSpec((rb, vw), lambda i: (rmap(i), 0)), state],
        out_shape=[jax.ShapeDtypeStruct((l, vw), out_dtype),
                   jax.ShapeDtypeStruct((GLA_HEADS, GLA_DV, LANES), F32)],
        scratch_shapes=scratch,
        compiler_params=_params("arbitrary"),
        name="gla_bwd" if reverse else "gla_fwd",
    )(*args)


def _out_proj_kernel(hy_ref, o_ref, w_ref, x_ref, ga_ref, g_ref, sh_ref, sc_ref, xo_ref, h_ref):
    half = hy_ref.shape[1]
    mix = (jnp.dot(hy_ref[...].astype(BF16), w_ref[:half, :], preferred_element_type=F32)
           + jnp.dot(o_ref[...], w_ref[half:, :], preferred_element_type=F32))
    xn = x_ref[...] + ga_ref[...] * mix
    xo_ref[...] = xn
    h_ref[...] = _norm_mod(xn, g_ref[...], sh_ref[...], sc_ref[...]).astype(BF16)


def _out_proj(hy, o, w, x, ga, g, shift, scale, layer):
    m, d = x.shape
    half = hy.shape[1]
    tm = min(m, 256)
    vec = pl.BlockSpec((1, d), lambda i: (0, 0))
    row = lambda width: pl.BlockSpec((tm, width), lambda i: (i, 0))
    return pl.pallas_call(
        _out_proj_kernel,
        grid=(m // tm,),
        in_specs=[row(half), row(half), pl.BlockSpec((None, 2 * half, d), lambda i: (layer, 0, 0)), row(d),
                  vec, vec, vec, vec],
        out_specs=[row(d), row(d)],
        out_shape=[jax.ShapeDtypeStruct((m, d), F32), jax.ShapeDtypeStruct((m, d), BF16)],
        compiler_params=_params("parallel"),
        name="out_proj",
    )(hy, o, w, x, ga, g, shift, scale)


def _matmul_kernel(x_ref, w_ref, o_ref):
    o_ref[...] = jnp.dot(x_ref[...], w_ref[...].astype(BF16), preferred_element_type=F32).astype(o_ref.dtype)


def _ffn_up(h, w, layer):
    m, d = h.shape
    n = w.shape[2]
    tm = min(m, 2048)
    tn = 512
    return pl.pallas_call(
        _matmul_kernel,
        grid=(m // tm, n // tn),
        in_specs=[pl.BlockSpec((tm, d), lambda i, j: (i, 0)),
                  pl.BlockSpec((None, d, tn), lambda i, j: (layer, 0, j))],
        out_specs=pl.BlockSpec((tm, tn), lambda i, j: (i, j)),
        out_shape=jax.ShapeDtypeStruct((m, n), BF16),
        compiler_params=_params("parallel", "parallel"),
        name="ffn_up",
    )(h, w)


_HALO = GRID_W + SUBLANES


def _ffn_mid_kernel(a_ref, ap_ref, an_ref, u_ref, w_ref, o_ref, s_ref, *, vertical, cols):
    i = pl.program_id(0)
    tt = a_ref.shape[0]
    first = i == 0
    last = i == pl.num_programs(0) - 1
    s_ref[pl.ds(0, SUBLANES), :] = jnp.zeros((SUBLANES, s_ref.shape[1]), F32)
    s_ref[pl.ds(_HALO + tt + GRID_W, SUBLANES), :] = jnp.zeros((SUBLANES, s_ref.shape[1]), F32)
    s_ref[pl.ds(SUBLANES, GRID_W), :] = jnp.where(first, 0.0, ap_ref[...].astype(F32))
    s_ref[pl.ds(_HALO, tt), :] = a_ref[...].astype(F32)
    s_ref[pl.ds(_HALO + tt, GRID_W), :] = jnp.where(last, 0.0, an_ref[...].astype(F32))
    col = lax.broadcasted_iota(jnp.int32, o_ref.shape, 0) & (cols - 1)
    acc = None
    for dc in (-1, 0, 1):
        part = None
        for dr in ((-1, 0, 1) if vertical else (0,)):
            tap = w_ref[3 * (dr + 1) + (dc + 1):3 * (dr + 1) + (dc + 1) + 1, :]
            term = tap * s_ref[pl.ds(_HALO + GRID_W * dr + dc, tt), :]
            part = term if part is None else part + term
        if dc == -1:
            part = jnp.where(col == 0, 0.0, part)
        elif dc == 1:
            part = jnp.where(col == cols - 1, 0.0, part)
        acc = part if acc is None else acc + part
    o_ref[...] = (_silu(acc) * u_ref[...].astype(F32)).astype(o_ref.dtype)


def _ffn_mid(up, conv_w, cols):
    m = up.shape[0]
    dff = conv_w.shape[-1]
    vertical = m > cols
    assert cols & (cols - 1) == 0 and (cols == GRID_W or not vertical)
    tt = min(m, 512)
    assert tt % cols == 0
    cb = 512
    ncb = dff // cb
    t64 = tt // GRID_W
    n64 = m // GRID_W
    return pl.pallas_call(
        functools.partial(_ffn_mid_kernel, vertical=vertical, cols=cols),
        grid=(m // tt, ncb),
        in_specs=[pl.BlockSpec((tt, cb), lambda i, j: (i, j)),
                  pl.BlockSpec((GRID_W, cb), lambda i, j: (jnp.maximum(i * t64 - 1, 0), j)),
                  pl.BlockSpec((GRID_W, cb), lambda i, j: (jnp.minimum((i + 1) * t64, n64 - 1), j)),
                  pl.BlockSpec((tt, cb), lambda i, j: (i, ncb + j)),
                  pl.BlockSpec((9, cb), lambda i, j: (0, j))],
        out_specs=pl.BlockSpec((tt, cb), lambda i, j: (i, j)),
        out_shape=jax.ShapeDtypeStruct((m, dff), BF16),
        scratch_shapes=[pltpu.VMEM((tt + 2 * _HALO, cb), F32)],
        compiler_params=_params("parallel", "parallel"),
        name="ffn_mid",
    )(up, up, up, up, conv_w.reshape(9, dff))


def _ffn_down_kernel(g_ref, w_ref, x_ref, gate_ref, gf_ref, o_ref, acc_ref, *, final_norm):
    kk = pl.program_id(1)

    @pl.when(kk == 0)
    def _():
        acc_ref[...] = jnp.zeros_like(acc_ref)

    acc_ref[...] += jnp.dot(g_ref[...], w_ref[...].astype(BF16), preferred_element_type=F32)

    @pl.when(kk == pl.num_programs(1) - 1)
    def _():
        xn = x_ref[...] + gate_ref[...] * acc_ref[...]
        if final_norm:
            ms = jnp.mean(xn * xn, axis=-1, keepdims=True)
            xn = xn * lax.rsqrt(ms + NORM_EPS) * gf_ref[...]
        o_ref[...] = xn


def _ffn_down(g, w, x, gate, g_final, final_norm, layer):
    m, d = x.shape
    dff = g.shape[1]
    tm = min(m, 1024)
    tk = 512
    vec = pl.BlockSpec((1, d), lambda i, k: (0, 0))
    return pl.pallas_call(
        functools.partial(_ffn_down_kernel, final_norm=final_norm),
        grid=(m // tm, dff // tk),
        in_specs=[pl.BlockSpec((tm, tk), lambda i, k: (i, k)),
                  pl.BlockSpec((None, tk, d), lambda i, k: (layer, k, 0)),
                  pl.BlockSpec((tm, d), lambda i, k: (i, 0), pipeline_mode=pl.Buffered(1)), vec, vec],
        out_specs=pl.BlockSpec((tm, d), lambda i, k: (i, 0)),
        out_shape=jax.ShapeDtypeStruct((m, d), F32),
        scratch_shapes=[pltpu.VMEM((tm, d), F32)],
        compiler_params=_params("parallel", "arbitrary"),
        name="ffn_down",
    )(g, w, x, gate, g_final)


def _permute_w_in(w_in, hy_proj):
    qk = GLA_HEADS * GLA_DK
    vw = GLA_HEADS * GLA_DV
    g0 = hy_proj
    k = w_in[..., g0:g0 + qk]
    v = w_in[..., g0 + qk:g0 + qk + vw]
    r = w_in[..., g0 + qk + vw:g0 + qk + vw + 2 * GLA_RANK]
    q = w_in[..., g0 + qk + vw + 2 * GLA_RANK:g0 + 2 * qk + vw + 2 * GLA_RANK]
    g = w_in[..., g0 + 2 * qk + vw + 2 * GLA_RANK:]
    pad = jnp.zeros(w_in.shape[:-1] + (LANES - 2 * GLA_RANK,), w_in.dtype)
    return jnp.concatenate([w_in[..., :hy_proj], k, q, v, g, r, pad], axis=-1).astype(BF16)


def _pad_gate_w(wg, lane0):
    return jnp.zeros((LANES, wg.shape[1]), F32).at[lane0:lane0 + GLA_RANK].set(wg)


def _mixer(p, lp, s0_f, s0_b, want_output):
    l = p.shape[0]
    width = lp['hy_bias'].shape[0]
    o_f, s_f = _gla_pass(p, lp['wg_f_pad'], lp['gla_bg_f'], s0_f, reverse=False)
    o, s_b = _gla_pass(p, lp['wg_b_pad'], lp['gla_bg_b'], s0_b, reverse=True, of=o_f, gout=lp['gla_out_g'])
    if not want_output:
        return None, None, s_f, s_b
    z, x0c = _hy_pre(p, lp['hy_conv'], width)
    if 2 * l // DFT_B >= 2 * SUBLANES:
        hy = _hyena_long(z, x0c, lp, lp['hy_bias'], lp['hy_out_g'])
    else:
        k2 = _hyena_filter2(l, lp['hy_w1'], lp['hy_b1'], lp['hy_freq'], lp['hy_w2'], lp['hy_b2'], lp['hy_w3'], width)
        hy = _hyena_short(z, x0c, k2, lp['hy_bias'], lp['hy_out_g'])
    return hy, o, s_f, s_b


def _ffn(h2, x, gate, lp, cols, g_final, final_norm, layer):
    up = _ffn_up(h2, lp['ffn_w_up'], layer)
    gact = _ffn_mid(up, lp['ffn_conv'], cols)
    return _ffn_down(gact, lp['ffn_w_down'], x, gate, g_final, final_norm, layer)


def kernel(x, c, ctx, c_ctx, w_mod, b_mod, g_attn, w_in, hy_conv, hy_w1, hy_b1, hy_freq, hy_w2, hy_b2, hy_w3,
           hy_bias, hy_out_g, gla_wg_f, gla_bg_f, gla_wg_b, gla_bg_b, gla_out_g, w_out, g_ffn, ffn_w_up,
           ffn_conv, ffn_w_down, g_final):
    depth = w_mod.shape[0]
    batch, seq, d = x.shape
    assert batch == 1 and c.shape[0] == 1 and ctx.shape[0] == 1
    hy_proj = hy_conv.shape[-1]
    xs = x[0]
    cs = ctx[0]
    cc = jnp.zeros((SUBLANES, d), F32).at[0].set(c[0]).at[1].set(c_ctx)
    zero_state = jnp.zeros((GLA_HEADS, GLA_DV, LANES), F32)
    gfin = g_final.reshape(1, d)
    w_in_p = _permute_w_in(w_in, hy_proj)
    w_out_b = w_out.astype(BF16)
    for l in range(depth):
        last = l == depth - 1
        lp = {
            'hy_conv': hy_conv[l], 'hy_w1': hy_w1[l], 'hy_b1': hy_b1[l], 'hy_freq': hy_freq[l],
            'hy_w2': hy_w2[l], 'hy_b2': hy_b2[l], 'hy_w3': hy_w3[l], 'hy_bias': hy_bias[l],
            'hy_out_g': hy_out_g[l], 'gla_bg_f': gla_bg_f[l], 'gla_bg_b': gla_bg_b[l],
            'gla_out_g': gla_out_g[l],
            'wg_f_pad': _pad_gate_w(gla_wg_f[l], 0), 'wg_b_pad': _pad_gate_w(gla_wg_b[l], GLA_RANK),
            'ffn_w_up': ffn_w_up, 'ffn_conv': ffn_conv[l], 'ffn_w_down': ffn_w_down,
        }
        mod = _adaln(cc, w_mod, b_mod, l)
        sa, sca, ga, sf, scf, gf = [mod[0:1, i * d:(i + 1) * d] for i in range(6)]
        csa, csca, cga, csf, cscf, cgf = [mod[1:2, i * d:(i + 1) * d] for i in range(6)]
        ga_row = g_attn[l].reshape(1, d)
        gf_row = g_ffn[l].reshape(1, d)

        pc = _in_proj(cs, ga_row, csa, csca, w_in_p, l)
        hy_c, o_c, s_f, s_b = _mixer(pc, lp, zero_state, zero_state, want_output=not last)
        if not last:
            cs, h2c = _out_proj(hy_c, o_c, w_out_b, cs, cga, gf_row, csf, cscf, l)
            cs = _ffn(h2c, cs, cgf, lp, cs.shape[0], gfin, False, l)

        px = _in_proj(xs, ga_row, sa, sca, w_in_p, l)
        hy_x, o_x, _, _ = _mixer(px, lp, s_f, s_b, want_output=True)
        xs, h2 = _out_proj(hy_x, o_x, w_out_b, xs, ga, gf_row, sf, scf, l)
        xs = _ffn(h2, xs, gf, lp, GRID_W, gfin, last, l)
    return xs[None]
```

```python
import functools
import math

import numpy as np
import jax
import jax.numpy as jnp
from jax import lax
from jax.experimental import pallas as pl
from jax.experimental.pallas import tpu as pltpu

F32 = jnp.float32
BF16 = jnp.bfloat16
HIGHEST = lax.Precision.HIGHEST

LANES = 128
SUBLANES = 8
VMEM_BYTES_V7X = 64 * 1024 * 1024
VMEM_LIMIT = VMEM_BYTES_V7X - 12 * 1024 * 1024

NORM_EPS = 1e-6
HYENA_GROUP = 128
FILTER_BANDS = 16
FILTER_HIDDEN = 64
FILTER_TARGET = 1e-2
FILTER_FAST_PCT = 0.3
FILTER_SLOW_PCT = 1.5
GLA_HEADS = 8
GLA_DK = 64
GLA_DV = 128
GLA_RANK = 16
GLA_GATE_TAU = 16.0
GLA_CHUNK = 64
GRID_W = 64
DFT_B = 128


def _params(*sem):
    return pltpu.CompilerParams(dimension_semantics=sem, vmem_limit_bytes=VMEM_LIMIT)


def _silu(x):
    return x * jax.nn.sigmoid(x)


def _adaln_kernel(c_ref, w_ref, b_ref, o_ref):
    s = _silu(c_ref[...]).astype(BF16)
    o_ref[...] = jnp.dot(s, w_ref[...].astype(BF16), preferred_element_type=F32) + b_ref[...]


def _adaln(cc, w_mod, b_mod, layer):
    depth, d, n = w_mod.shape
    tn = 1024
    return pl.pallas_call(
        _adaln_kernel,
        grid=(n // tn,),
        in_specs=[pl.BlockSpec((SUBLANES, d), lambda j: (0, 0)),
                  pl.BlockSpec((None, d, tn), lambda j: (layer, 0, j)),
                  pl.BlockSpec((None, 1, tn), lambda j: (layer, 0, j))],
        out_specs=pl.BlockSpec((SUBLANES, tn), lambda j: (0, j)),
        out_shape=jax.ShapeDtypeStruct((SUBLANES, n), F32),
        compiler_params=_params("arbitrary"),
        name="adaln",
    )(cc, w_mod, b_mod.reshape(depth, 1, n))


def _norm_mod(x, g, shift, scale):
    ms = jnp.mean(x * x, axis=-1, keepdims=True)
    return (x * lax.rsqrt(ms + NORM_EPS) * g) * (1.0 + scale) + shift


def _norm_mod_kernel(x_ref, g_ref, sh_ref, sc_ref, o_ref):
    o_ref[...] = _norm_mod(x_ref[...], g_ref[...], sh_ref[...], sc_ref[...]).astype(o_ref.dtype)


def _norm_mod_call(x, g, shift, scale):
    m, d = x.shape
    tm = min(m, 512)
    vec = pl.BlockSpec((1, d), lambda i: (0, 0))
    return pl.pallas_call(
        _norm_mod_kernel,
        grid=(m // tm,),
        in_specs=[pl.BlockSpec((tm, d), lambda i: (i, 0)), vec, vec, vec],
        out_specs=pl.BlockSpec((tm, d), lambda i: (i, 0)),
        out_shape=jax.ShapeDtypeStruct((m, d), BF16),
        compiler_params=_params("parallel"),
        name="norm_mod",
    )(x, g, shift, scale)


def _in_proj_kernel(h_ref, w_ref, o_ref):
    o_ref[...] = jnp.dot(h_ref[...], w_ref[...], preferred_element_type=F32)


def _in_proj(h, w, layer):
    m, d = h.shape
    n = w.shape[2]
    tm = min(m, 1024)
    tn = 1280
    return pl.pallas_call(
        _in_proj_kernel,
        grid=(m // tm, n // tn),
        in_specs=[pl.BlockSpec((tm, d), lambda i, j: (i, 0)),
                  pl.BlockSpec((None, d, tn), lambda i, j: (layer, 0, j))],
        out_specs=pl.BlockSpec((tm, tn), lambda i, j: (i, j)),
        out_shape=jax.ShapeDtypeStruct((m, n), F32),
        compiler_params=_params("parallel", "parallel"),
        name="in_proj",
    )(h, w)


def _hy_pre_kernel(x0_ref, x1_ref, v_ref, x0p_ref, x1p_ref, vp_ref, x0n_ref, x1n_ref, vn_ref,
                   w0_ref, w1_ref, wv_ref, z_ref, x0c_ref):
    i = pl.program_id(0)
    first = i == 0
    last = i == pl.num_programs(0) - 1
    tt = x0_ref.shape[0]
    row = lax.broadcasted_iota(jnp.int32, x0_ref.shape, 0)

    def conv(ref, prev_ref, next_ref, w_ref):
        x = ref[...]
        prev_row = jnp.where(first, 0.0, prev_ref[SUBLANES - 1:SUBLANES, :])
        next_row = jnp.where(last, 0.0, next_ref[0:1, :])
        below = jnp.where(row == 0, prev_row, pltpu.roll(x, 1, 0))
        above = jnp.where(row == tt - 1, next_row, pltpu.roll(x, tt - 1, 0))
        return w_ref[0:1, :] * below + w_ref[1:2, :] * x + w_ref[2:3, :] * above

    x0c_ref[...] = conv(x0_ref, x0p_ref, x0n_ref, w0_ref)
    z_ref[...] = conv(v_ref, vp_ref, vn_ref, wv_ref) * conv(x1_ref, x1p_ref, x1n_ref, w1_ref)


def _hy_pre(p, hy_conv, width):
    l = p.shape[0]
    tt = min(l, 1024)
    cb = 512
    ncb = width // cb
    nb8 = l // SUBLANES
    t8 = tt // SUBLANES

    def main(g):
        return pl.BlockSpec((tt, cb), lambda i, j: (i, g * ncb + j))

    def prev(g):
        return pl.BlockSpec((SUBLANES, cb), lambda i, j: (jnp.maximum(i * t8 - 1, 0), g * ncb + j))

    def nxt(g):
        return pl.BlockSpec((SUBLANES, cb), lambda i, j: (jnp.minimum((i + 1) * t8, nb8 - 1), g * ncb + j))

    def wspec(g):
        return pl.BlockSpec((3, cb), lambda i, j: (0, g * ncb + j))

    out = pl.BlockSpec((tt, cb), lambda i, j: (i, j))
    return pl.pallas_call(
        _hy_pre_kernel,
        grid=(l // tt, ncb),
        in_specs=[main(0), main(1), main(2), prev(0), prev(1), prev(2), nxt(0), nxt(1), nxt(2),
                  wspec(0), wspec(1), wspec(2)],
        out_specs=[out, out],
        out_shape=[jax.ShapeDtypeStruct((l, width), F32), jax.ShapeDtypeStruct((l, width), F32)],
        compiler_params=_params("parallel", "parallel"),
        name="hy_pre",
    )(p, p, p, p, p, p, p, p, p, hy_conv, hy_conv, hy_conv)


def _filter_hidden(j, seq_len, w1t_ref, w1c_ref, w1s_ref, b1_ref, fr_ref, w2_ref, b2_ref):
    pos_i = jnp.where(j < seq_len, j, 2 * seq_len - j)
    pos = pos_i.astype(F32)
    t = pos / float(max(seq_len - 1, 1))
    lane = lax.broadcasted_iota(jnp.int32, j.shape, 1)
    band = jnp.where(lane < FILTER_BANDS, lane + 1, 0).astype(F32)
    ang = (2.0 * math.pi / seq_len) * pos * band
    fr = fr_ref[...]
    pre = (t[:, 0:1] * w1t_ref[...]
           + jnp.dot(jnp.cos(ang), w1c_ref[...], precision=HIGHEST, preferred_element_type=F32)
           + jnp.dot(jnp.sin(ang), w1s_ref[...], precision=HIGHEST, preferred_element_type=F32)
           + b1_ref[...])
    h = jnp.sin(fr * pre)
    h = jnp.sin(fr * (jnp.dot(h, w2_ref[...], precision=HIGHEST, preferred_element_type=F32) + b2_ref[...]))
    return h, pos_i, t


def _filter_kernel(w1t_ref, w1c_ref, w1s_ref, b1_ref, fr_ref, w2_ref, b2_ref, w3_ref, delta_ref, o_ref, *, seq_len):
    tr = o_ref.shape[0]
    j = pl.program_id(0) * tr + lax.broadcasted_iota(jnp.int32, (tr, LANES), 0)
    h, pos_i, t = _filter_hidden(j, seq_len, w1t_ref, w1c_ref, w1s_ref, b1_ref, fr_ref, w2_ref, b2_ref)
    hh = jnp.dot(h.astype(BF16), w3_ref[...].astype(BF16), preferred_element_type=F32)
    decay = jnp.exp(-t[:, 0:1] * delta_ref[...])
    valid = pos_i[:, 0:1] != seq_len
    o_ref[...] = jnp.where(valid, hh * decay, 0.0)


def _filter_hidden_kernel(w1t_ref, w1c_ref, w1s_ref, b1_ref, fr_ref, w2_ref, b2_ref, o_ref, *, seq_len, n_a):
    tr = o_ref.shape[0]
    r = pl.program_id(0) * tr + lax.broadcasted_iota(jnp.int32, (tr, LANES), 0)
    shift = n_a.bit_length() - 1
    j = ((r & (n_a - 1)) * DFT_B) + (r >> shift)
    h, _, _ = _filter_hidden(j, seq_len, w1t_ref, w1c_ref, w1s_ref, b1_ref, fr_ref, w2_ref, b2_ref)
    o_ref[...] = h


def _filter_mlp_args(w1, b1, freq, w2, b2):
    hid = FILTER_HIDDEN
    w1c = jnp.zeros((LANES, hid), F32).at[:FILTER_BANDS].set(w1[1:1 + FILTER_BANDS])
    w1s = jnp.zeros((LANES, hid), F32).at[:FILTER_BANDS].set(w1[1 + FILTER_BANDS:1 + 2 * FILTER_BANDS])
    full = lambda shape: pl.BlockSpec(shape, lambda i: (0, 0))
    specs = [full((1, hid)), full((LANES, hid)), full((LANES, hid)), full((1, hid)), full((1, hid)),
             full((hid, hid)), full((1, hid))]
    args = (w1[0:1], w1c, w1s, b1.reshape(1, hid), freq.reshape(1, hid), w2, b2.reshape(1, hid))
    return specs, args


def _filter_delta(width):
    return jnp.abs(jnp.linspace(math.log(FILTER_TARGET) / FILTER_SLOW_PCT,
                                math.log(FILTER_TARGET) / FILTER_FAST_PCT, width, dtype=F32)).reshape(1, width)


def _hyena_filter2(seq_len, w1, b1, freq, w2, b2, w3, width):
    n2 = 2 * seq_len
    tr = min(seq_len, 1024)
    nfwd = seq_len // tr
    specs, args = _filter_mlp_args(w1, b1, freq, w2, b2)
    return pl.pallas_call(
        functools.partial(_filter_kernel, seq_len=seq_len),
        grid=(n2 // tr,),
        in_specs=specs + [pl.BlockSpec((FILTER_HIDDEN, width), lambda i: (0, jnp.where(i < nfwd, 0, 1))),
                          pl.BlockSpec((1, width), lambda i: (0, 0))],
        out_specs=pl.BlockSpec((tr, width), lambda i: (i, 0)),
        out_shape=jax.ShapeDtypeStruct((n2, width), F32),
        compiler_params=_params("parallel"),
        name="hyena_filter",
    )(*args, w3, _filter_delta(width))


def _filter_hidden_permuted(seq_len, w1, b1, freq, w2, b2):
    n2 = 2 * seq_len
    n_a = n2 // DFT_B
    tr = min(n2, 1024)
    specs, args = _filter_mlp_args(w1, b1, freq, w2, b2)
    return pl.pallas_call(
        functools.partial(_filter_hidden_kernel, seq_len=seq_len, n_a=n_a),
        grid=(n2 // tr,),
        in_specs=specs,
        out_specs=pl.BlockSpec((tr, FILTER_HIDDEN), lambda i: (i, 0)),
        out_shape=jax.ShapeDtypeStruct((n2, FILTER_HIDDEN), F32),
        compiler_params=_params("parallel"),
        name="hyena_filter_hidden",
    )(*args)


def _dft_tables(n_a, a_in):
    n = n_a * DFT_B
    b = np.arange(DFT_B)[:, None, None]
    k1 = np.arange(n_a)[None, :, None]
    a = np.arange(a_in)[None, None, :]
    e = np.exp(-2j * np.pi * ((k1 * (DFT_B * a + b)) % n) / n)
    g = np.concatenate([e.real, e.imag], axis=1)
    return jnp.asarray(g, F32)


def _idft_tables(n_a, a_out):
    n = n_a * DFT_B
    b = np.arange(DFT_B)[:, None, None]
    a = np.arange(a_out)[None, :, None]
    k1 = np.arange(n_a)[None, None, :]
    e = np.exp(2j * np.pi * ((k1 * (DFT_B * a + b)) % n) / n) / n
    h = np.concatenate([e.real, -e.imag], axis=2)
    return jnp.asarray(h, F32)


def _dft128_embed():
    i = np.arange(DFT_B)
    f = np.exp(-2j * np.pi * ((i[:, None] * i[None, :]) % DFT_B) / DFT_B)
    fwd = np.block([[f.real, -f.imag], [f.imag, f.real]])
    inv = np.block([[f.real, f.imag], [-f.imag, f.real]])
    return jnp.asarray(fwd, F32), jnp.asarray(inv, F32)


_PITCH = DFT_B + SUBLANES
_K1_BLOCK = 8
_B_UNROLL = 8


def _stage_rows(k1):
    return pl.ds(pl.multiple_of(k1 * _PITCH, SUBLANES), DFT_B)


def _single(shape, index_map):
    return pl.BlockSpec(shape, index_map, pipeline_mode=pl.Buffered(1))


def _spectrum_kernel(h_ref, w3f_ref, w3b_ref, delta_ref, g_ref, wf_ref, o_ref, sr_ref, si_ref, *, seq_len, n_a):
    s = pl.program_id(1)
    half = n_a // 2

    @pl.when(s == 0)
    def _():
        a_i = lax.broadcasted_iota(jnp.int32, (n_a, LANES), 0)

        def body(b, carry):
            base = pl.multiple_of(b * n_a, SUBLANES)
            hf = jnp.dot(h_ref[pl.ds(base, half), :].astype(BF16), w3f_ref[...].astype(BF16),
                         preferred_element_type=F32)
            hb = jnp.dot(h_ref[pl.ds(base + half, half), :].astype(BF16), w3b_ref[...].astype(BF16),
                         preferred_element_type=F32)
            hh = jnp.concatenate([hf, hb], axis=0)
            j = a_i * DFT_B + b
            pos_i = jnp.where(j < seq_len, j, 2 * seq_len - j)
            t = pos_i.astype(F32) / float(max(seq_len - 1, 1))
            k2 = jnp.where(pos_i != seq_len, hh * jnp.exp(-t * delta_ref[...]), 0.0)
            r = jnp.dot(g_ref[b], k2.astype(BF16), preferred_element_type=F32)
            sr_ref[pl.ds(b, n_a, stride=_PITCH), :] = r[:n_a]
            si_ref[pl.ds(b, n_a, stride=_PITCH), :] = r[n_a:]
            return carry

        lax.fori_loop(0, DFT_B, body, 0, unroll=_B_UNROLL)

    @pl.when(s > 0)
    def _():
        k0 = (s - 1) * o_ref.shape[1]
        for i in range(o_ref.shape[1]):
            rows = _stage_rows(k0 + i)
            x = jnp.concatenate([sr_ref[rows, :], si_ref[rows, :]], axis=0).astype(BF16)
            z = jnp.dot(wf_ref[...], x, preferred_element_type=F32)
            o_ref[0, i] = z[:DFT_B]
            o_ref[1, i] = z[DFT_B:]


def _filter_spectrum(hperm, w3, seq_len, width, g_k, wf):
    n_a = 2 * seq_len // DFT_B
    kb = min(n_a, _K1_BLOCK)
    ncb = width // LANES
    hid = FILTER_HIDDEN
    return pl.pallas_call(
        functools.partial(_spectrum_kernel, seq_len=seq_len, n_a=n_a),
        grid=(ncb, 1 + n_a // kb),
        in_specs=[_single(hperm.shape, lambda ci, s: (0, 0)),
                  pl.BlockSpec((hid, LANES), lambda ci, s: (0, ci)),
                  pl.BlockSpec((hid, LANES), lambda ci, s: (0, ncb + ci)),
                  pl.BlockSpec((1, LANES), lambda ci, s: (0, ci)),
                  _single(g_k.shape, lambda ci, s: (0, 0, 0)),
                  _single(wf.shape, lambda ci, s: (0, 0))],
        out_specs=pl.BlockSpec((2, kb, DFT_B, LANES), lambda ci, s: (0, jnp.maximum(s - 1, 0), 0, ci)),
        out_shape=jax.ShapeDtypeStruct((2, n_a, DFT_B, width), F32),
        scratch_shapes=[pltpu.VMEM((n_a * _PITCH, LANES), F32), pltpu.VMEM((n_a * _PITCH, LANES), F32)],
        compiler_params=_params("parallel", "arbitrary"),
        name="filter_spectrum",
    )(hperm, w3, w3, _filter_delta(width), g_k, wf)


def _hyena_epilogue(y, z, x0, bias, gain):
    yv = (y + z * bias) * x0
    ms = jnp.mean(yv * yv, axis=-1, keepdims=True)
    return yv * lax.rsqrt(ms + NORM_EPS) * gain


_EPI_ROWS = 512


def _long_conv_kernel(z_ref, x0_ref, kf_ref, g_ref, h_ref, wf_ref, wi_ref, bias_ref, gain_ref, o_ref,
                      sr_ref, si_ref, *, n_a):
    s = pl.program_id(1)
    half = n_a // 2
    kb = kf_ref.shape[1]
    nkb = n_a // kb

    @pl.when(s == 0)
    def _():
        def body(b, carry):
            xs = z_ref[pl.ds(b, half, stride=DFT_B), :].astype(BF16)
            r = jnp.dot(g_ref[b], xs, preferred_element_type=F32)
            sr_ref[pl.ds(b, n_a, stride=_PITCH), :] = r[:n_a]
            si_ref[pl.ds(b, n_a, stride=_PITCH), :] = r[n_a:]
            return carry

        lax.fori_loop(0, DFT_B, body, 0, unroll=_B_UNROLL)

    @pl.when((s > 0) & (s <= nkb))
    def _():
        k0 = (s - 1) * kb
        for i in range(kb):
            rows = _stage_rows(k0 + i)
            x = jnp.concatenate([sr_ref[rows, :], si_ref[rows, :]], axis=0).astype(BF16)
            zz = jnp.dot(wf_ref[...], x, preferred_element_type=F32)
            zr, zi = zz[:DFT_B], zz[DFT_B:]
            kr, ki = kf_ref[0, i], kf_ref[1, i]
            y = jnp.concatenate([zr * kr - zi * ki, zr * ki + zi * kr], axis=0).astype(BF16)
            cc = jnp.dot(wi_ref[...], y, preferred_element_type=F32)
            sr_ref[rows, :] = cc[:DFT_B]
            si_ref[rows, :] = cc[DFT_B:]

    @pl.when(s == nkb + 1)
    def _():
        def body(b, carry):
            x = jnp.concatenate([sr_ref[pl.ds(b, n_a, stride=_PITCH), :],
                                 si_ref[pl.ds(b, n_a, stride=_PITCH), :]], axis=0).astype(BF16)
            o_ref[pl.ds(b, half, stride=DFT_B), :] = jnp.dot(h_ref[b], x, preferred_element_type=F32)
            return carry

        lax.fori_loop(0, DFT_B, body, 0, unroll=_B_UNROLL)

        step = min(_EPI_ROWS, o_ref.shape[0])

        def epi(i, carry):
            rows = pl.ds(pl.multiple_of(i * step, step), step)
            o_ref[rows, :] = _hyena_epilogue(o_ref[rows, :], z_ref[rows, :], x0_ref[rows, :],
                                             bias_ref[...], gain_ref[...])
            return carry

        lax.fori_loop(0, o_ref.shape[0] // step, epi, 0)


def _long_conv(z, x0c, kf, g_z, h_tab, wf, wi, bias, gain):
    l, c = z.shape
    n_a = 2 * l // DFT_B
    kb = min(n_a, _K1_BLOCK)
    nkb = n_a // kb
    col = lambda ci, s: (0, ci)
    return pl.pallas_call(
        functools.partial(_long_conv_kernel, n_a=n_a),
        grid=(c // LANES, nkb + 2),
        in_specs=[_single((l, LANES), col), _single((l, LANES), col),
                  pl.BlockSpec((2, kb, DFT_B, LANES), lambda ci, s: (0, jnp.clip(s - 1, 0, nkb - 1), 0, ci)),
                  _single(g_z.shape, lambda ci, s: (0, 0, 0)), _single(h_tab.shape, lambda ci, s: (0, 0, 0)),
                  _single(wf.shape, lambda ci, s: (0, 0)), _single(wi.shape, lambda ci, s: (0, 0)),
                  pl.BlockSpec((1, LANES), col), pl.BlockSpec((1, LANES), col)],
        out_specs=pl.BlockSpec((l, LANES), col),
        out_shape=jax.ShapeDtypeStruct((l, c), F32),
        scratch_shapes=[pltpu.VMEM((n_a * _PITCH, LANES), F32), pltpu.VMEM((n_a * _PITCH, LANES), F32)],
        compiler_params=_params("parallel", "arbitrary"),
        name="long_conv",
    )(z, x0c, kf, g_z, h_tab, wf, wi, bias.reshape(1, c), gain.reshape(1, c))


def _hyena_long(z, x0c, filt, bias, gain):
    l, c = z.shape
    n_a = 2 * l // DFT_B
    wf, wi = _dft128_embed()
    wf, wi = wf.astype(BF16), wi.astype(BF16)
    g_z = _dft_tables(n_a, n_a // 2).astype(BF16)
    g_k = _dft_tables(n_a, n_a).astype(BF16)
    h_tab = _idft_tables(n_a, n_a // 2).astype(BF16)
    hperm = _filter_hidden_permuted(l, filt['hy_w1'], filt['hy_b1'], filt['hy_freq'], filt['hy_w2'], filt['hy_b2'])
    kf = _filter_spectrum(hperm, filt['hy_w3'], l, c, g_k, wf)
    return _long_conv(z, x0c, kf, g_z, h_tab, wf, wi, bias, gain)


def _ctx_conv_kernel(z_ref, k_ref, x0_ref, fz_ref, fk_ref, hi_ref, bias_ref, gain_ref, o_ref):
    n = k_ref.shape[0]
    zs = jnp.dot(fz_ref[...], z_ref[...].astype(BF16), preferred_element_type=F32)
    ks = jnp.dot(fk_ref[...], k_ref[...].astype(BF16), preferred_element_type=F32)
    zr, zi, kr, ki = zs[:n], zs[n:], ks[:n], ks[n:]
    y = jnp.concatenate([zr * kr - zi * ki, zr * ki + zi * kr], axis=0).astype(BF16)
    yt = jnp.dot(hi_ref[...], y, preferred_element_type=F32)
    o_ref[...] = _hyena_epilogue(yt, z_ref[...], x0_ref[...], bias_ref[...], gain_ref[...]).astype(o_ref.dtype)


def _hyena_short(z, x0c, k2, bias, gain):
    l, c = z.shape
    n = 2 * l
    k = np.arange(n)[:, None]
    t = np.arange(n)[None, :]
    e = np.exp(-2j * np.pi * ((k * t) % n) / n)
    fk = jnp.asarray(np.concatenate([e.real, e.imag], axis=0), F32).astype(BF16)
    fz = fk[:, :l]
    ei = np.conj(e[:, :l]).T / n
    hi = jnp.asarray(np.concatenate([ei.real, -ei.imag], axis=1), F32).astype(BF16)
    col = lambda rows: pl.BlockSpec((rows, LANES), lambda ci: (0, ci))
    full = lambda a: pl.BlockSpec(a.shape, lambda ci: (0, 0))
    return pl.pallas_call(
        _ctx_conv_kernel,
        grid=(c // LANES,),
        in_specs=[col(l), col(n), col(l), full(fz), full(fk), full(hi), col(1), col(1)],
        out_specs=col(l),
        out_shape=jax.ShapeDtypeStruct((l, c), F32),
        compiler_params=_params("parallel"),
        name="hyena_short",
    )(z, k2, x0c, fz, fk, hi, bias.reshape(1, c), gain.reshape(1, c))


def _split3(x):
    hi = x.astype(BF16)
    r1 = x - hi.astype(F32)
    mid = r1.astype(BF16)
    lo = (r1 - mid.astype(F32)).astype(BF16)
    return hi, mid, lo


def _gla_kernel(*refs, reverse, combine):
    if combine:
        (k_ref, q_ref, v_ref, r_ref, wg_ref, bg_ref, s0_ref, of_ref, gate_ref, gout_ref,
         o_ref, sfin_ref, st_ref, la_ref, qe_ref, upd_ref, dec_ref, ob_ref) = refs
    else:
        (k_ref, q_ref, v_ref, r_ref, wg_ref, bg_ref, s0_ref,
         o_ref, sfin_ref, st_ref, la_ref, qe_ref, upd_ref, dec_ref) = refs
        ob_ref = o_ref
    rows_blk = k_ref.shape[0]
    nc = rows_blk // GLA_CHUNK
    qk = k_ref.shape[1]

    @pl.when(pl.program_id(0) == 0)
    def _():
        st_ref[...] = s0_ref[...]

    ri = lax.broadcasted_iota(jnp.int32, (2 * GLA_CHUNK, GLA_CHUNK), 0) % GLA_CHUNK
    ci = lax.broadcasted_iota(jnp.int32, (2 * GLA_CHUNK, GLA_CHUNK), 1)
    r3 = lax.broadcasted_iota(jnp.int32, (GLA_CHUNK, 3 * GLA_CHUNK), 0)
    c3 = lax.broadcasted_iota(jnp.int32, (GLA_CHUNK, 3 * GLA_CHUNK), 1) % GLA_CHUNK
    if reverse:
        tri3 = (c3 >= r3).astype(BF16)
        keep = ci > ri
    else:
        tri3 = (c3 <= r3).astype(BF16)
        keep = ci <= ri
    lane = lax.broadcasted_iota(jnp.int32, (1, LANES), 1)
    first = (lane < GLA_DK).astype(F32)
    srow = lax.broadcasted_iota(jnp.int32, (2 * GLA_DV, LANES), 0)
    scol = lax.broadcasted_iota(jnp.int32, (2 * GLA_DV, LANES), 1)
    own = (srow < GLA_DV) == (scol < GLA_DK)
    nt = (((1,), (1,)), ((), ()))
    tn = (((0,), (0,)), ((), ()))
    npair = GLA_HEADS // 2

    pre = jnp.dot(r_ref[...], wg_ref[...], precision=HIGHEST, preferred_element_type=F32) + bg_ref[...]
    la_ref[...] = jax.nn.log_sigmoid(pre) / GLA_GATE_TAU

    def local(c, carry):
        rows = pl.ds(pl.multiple_of(c * GLA_CHUNK, GLA_CHUNK), GLA_CHUNK)
        hi, mid, lo = _split3(la_ref[rows, :])
        cum = jnp.dot(tri3, jnp.concatenate([hi, mid, lo], axis=0), preferred_element_type=F32)
        tot = cum[0:1] if reverse else cum[GLA_CHUNK - 1:GLA_CHUNK]
        k = k_ref[rows, :]
        qe = q_ref[rows, :] * (GLA_DK ** -0.5) * jnp.exp(cum)
        ke = (k * jnp.exp(-cum)).astype(BF16)
        kl = (k * jnp.exp(tot - cum)).astype(BF16)
        dec_ref[pl.ds(c, 1), :] = jnp.exp(tot)
        qe_ref[rows, :] = qe.astype(BF16)
        for p in range(npair):
            pr = slice(p * LANES, (p + 1) * LANES)
            pv = slice(2 * p * GLA_DV, 2 * (p + 1) * GLA_DV)
            q2 = jnp.concatenate([qe[:, pr] * first, qe[:, pr] * (1.0 - first)], axis=0).astype(BF16)
            sc = lax.dot_general(q2, ke[:, pr], nt, preferred_element_type=F32)
            sc = jnp.where(keep, sc, 0.0).astype(BF16)
            v_p = v_ref[rows, pv].astype(BF16)
            o2 = jnp.dot(sc, v_p, preferred_element_type=F32)
            ob_ref[rows, pl.ds(2 * p * GLA_DV, GLA_DV)] = o2[:GLA_CHUNK, :GLA_DV]
            ob_ref[rows, pl.ds((2 * p + 1) * GLA_DV, GLA_DV)] = o2[GLA_CHUNK:, GLA_DV:]
            u2 = lax.dot_general(v_p, kl[:, pr], tn, preferred_element_type=F32)
            upd_ref[c * npair + p] = jnp.where(own, u2, 0.0)
        return carry

    lax.fori_loop(0, nc, local, 0, unroll=2)

    def carried(idx, carry):
        c = (nc - 1 - idx) if reverse else idx
        rows = pl.ds(pl.multiple_of(c * GLA_CHUNK, GLA_CHUNK), GLA_CHUNK)
        dec = dec_ref[pl.ds(c, 1), :]
        for p in range(npair):
            pr = slice(p * LANES, (p + 1) * LANES)
            pv = slice(2 * p * GLA_DV, 2 * (p + 1) * GLA_DV)
            st = st_ref[p]
            ob_ref[rows, pv] += lax.dot_general(qe_ref[rows, pr], st.astype(BF16), nt, preferred_element_type=F32)
            st_ref[p] = st * dec[:, pr] + upd_ref[c * npair + p]
        return carry

    lax.fori_loop(0, nc, carried, 0)

    if combine:
        for h in range(GLA_HEADS):
            hv = slice(h * GLA_DV, (h + 1) * GLA_DV)
            o = of_ref[:, hv] + ob_ref[:, hv]
            ms = jnp.mean(o * o, axis=-1, keepdims=True)
            o = o * lax.rsqrt(ms + NORM_EPS) * gout_ref[...]
            o_ref[:, hv] = (o * _silu(gate_ref[:, hv])).astype(o_ref.dtype)

    @pl.when(pl.program_id(0) == pl.num_programs(0) - 1)
    def _():
        sfin_ref[...] = st_ref[...]


_GLA_STATE = (GLA_HEADS // 2, 2 * GLA_DV, LANES)

_COL_K, _COL_Q, _COL_V, _COL_G, _COL_R = 24, 28, 32, 40, 48


def _gla_pass(p, wg_pad, bg, s0, reverse, of=None, gout=None):
    l = p.shape[0]
    rb = min(l, 512)
    nblk = l // rb
    qk = GLA_HEADS * GLA_DK
    vw = GLA_HEADS * GLA_DV
    combine = of is not None
    rmap = (lambda i: nblk - 1 - i) if reverse else (lambda i: i)

    def cols(width, off128):
        return pl.BlockSpec((rb, width), lambda i: (rmap(i), off128 * LANES // width))

    state = pl.BlockSpec(_GLA_STATE, lambda i: (0, 0, 0))
    in_specs = [cols(qk, _COL_K), cols(qk, _COL_Q), cols(vw, _COL_V), cols(LANES, _COL_R),
                pl.BlockSpec((LANES, qk), lambda i: (0, 0)), pl.BlockSpec((1, qk), lambda i: (0, 0)), state]
    args = [p, p, p, p, wg_pad, bg.reshape(1, qk), s0]
    nc = rb // GLA_CHUNK
    scratch = [pltpu.VMEM(_GLA_STATE, F32),
               pltpu.VMEM((rb, qk), F32),
               pltpu.VMEM((rb, qk), BF16),
               pltpu.VMEM((nc * _GLA_STATE[0],) + _GLA_STATE[1:], F32),
               pltpu.VMEM((nc, qk), F32)]
    if combine:
        in_specs += [pl.BlockSpec((rb, vw), lambda i: (rmap(i), 0)), cols(vw, _COL_G),
                     pl.BlockSpec((1, GLA_DV), lambda i: (0, 0))]
        args += [of, p, gout.reshape(1, GLA_DV)]
        scratch.append(pltpu.VMEM((rb, vw), F32))
    out_dtype = BF16 if combine else F32
    return pl.pallas_call(
        functools.partial(_gla_kernel, reverse=reverse, combine=combine),
        grid=(nblk,),
        in_specs=in_specs,
        out_specs=[pl.BlockSpec((rb, vw), lambda i: (rmap(i), 0)), state],
        out_shape=[jax.ShapeDtypeStruct((l, vw), out_dtype),
                   jax.ShapeDtypeStruct(_GLA_STATE, F32)],
        scratch_shapes=scratch,
        compiler_params=_params("arbitrary"),
        name="gla_bwd" if reverse else "gla_fwd",
    )(*args)


def _out_proj_kernel(hy_ref, o_ref, w_ref, x_ref, ga_ref, g_ref, sh_ref, sc_ref, xo_ref, h_ref):
    half = hy_ref.shape[1]
    mix = (jnp.dot(hy_ref[...].astype(BF16), w_ref[:half, :], preferred_element_type=F32)
           + jnp.dot(o_ref[...], w_ref[half:, :], preferred_element_type=F32))
    xn = x_ref[...] + ga_ref[...] * mix
    xo_ref[...] = xn
    h_ref[...] = _norm_mod(xn, g_ref[...], sh_ref[...], sc_ref[...]).astype(BF16)


def _out_proj(hy, o, w, x, ga, g, shift, scale, layer):
    m, d = x.shape
    half = hy.shape[1]
    tm = min(m, 256)
    vec = pl.BlockSpec((1, d), lambda i: (0, 0))
    row = lambda width: pl.BlockSpec((tm, width), lambda i: (i, 0))
    return pl.pallas_call(
        _out_proj_kernel,
        grid=(m // tm,),
        in_specs=[row(half), row(half), pl.BlockSpec((None, 2 * half, d), lambda i: (layer, 0, 0)), row(d),
                  vec, vec, vec, vec],
        out_specs=[row(d), row(d)],
        out_shape=[jax.ShapeDtypeStruct((m, d), F32), jax.ShapeDtypeStruct((m, d), BF16)],
        compiler_params=_params("parallel"),
        name="out_proj",
    )(hy, o, w, x, ga, g, shift, scale)


def _matmul_kernel(x_ref, w_ref, o_ref):
    o_ref[...] = jnp.dot(x_ref[...], w_ref[...].astype(BF16), preferred_element_type=F32).astype(o_ref.dtype)


def _ffn_up(h, w, layer):
    m, d = h.shape
    n = w.shape[2]
    tm = min(m, 2048)
    tn = 512
    return pl.pallas_call(
        _matmul_kernel,
        grid=(m // tm, n // tn),
        in_specs=[pl.BlockSpec((tm, d), lambda i, j: (i, 0)),
                  pl.BlockSpec((None, d, tn), lambda i, j: (layer, 0, j))],
        out_specs=pl.BlockSpec((tm, tn), lambda i, j: (i, j)),
        out_shape=jax.ShapeDtypeStruct((m, n), BF16),
        compiler_params=_params("parallel", "parallel"),
        name="ffn_up",
    )(h, w)


_HALO = GRID_W + SUBLANES


def _ffn_mid_kernel(a_ref, ap_ref, an_ref, u_ref, w_ref, o_ref, s_ref, *, vertical, cols):
    i = pl.program_id(0)
    tt = a_ref.shape[0]
    first = i == 0
    last = i == pl.num_programs(0) - 1
    s_ref[pl.ds(0, SUBLANES), :] = jnp.zeros((SUBLANES, s_ref.shape[1]), F32)
    s_ref[pl.ds(_HALO + tt + GRID_W, SUBLANES), :] = jnp.zeros((SUBLANES, s_ref.shape[1]), F32)
    s_ref[pl.ds(SUBLANES, GRID_W), :] = jnp.where(first, 0.0, ap_ref[...].astype(F32))
    s_ref[pl.ds(_HALO, tt), :] = a_ref[...].astype(F32)
    s_ref[pl.ds(_HALO + tt, GRID_W), :] = jnp.where(last, 0.0, an_ref[...].astype(F32))
    col = lax.broadcasted_iota(jnp.int32, o_ref.shape, 0) & (cols - 1)
    acc = None
    for dc in (-1, 0, 1):
        part = None
        for dr in ((-1, 0, 1) if vertical else (0,)):
            tap = w_ref[3 * (dr + 1) + (dc + 1):3 * (dr + 1) + (dc + 1) + 1, :]
            term = tap * s_ref[pl.ds(_HALO + GRID_W * dr + dc, tt), :]
            part = term if part is None else part + term
        if dc == -1:
            part = jnp.where(col == 0, 0.0, part)
        elif dc == 1:
            part = jnp.where(col == cols - 1, 0.0, part)
        acc = part if acc is None else acc + part
    o_ref[...] = (_silu(acc) * u_ref[...].astype(F32)).astype(o_ref.dtype)


def _ffn_mid(up, conv_w, cols):
    m = up.shape[0]
    dff = conv_w.shape[-1]
    vertical = m > cols
    assert cols & (cols - 1) == 0 and (cols == GRID_W or not vertical)
    tt = min(m, 512)
    assert tt % cols == 0
    cb = 512
    ncb = dff // cb
    t64 = tt // GRID_W
    n64 = m // GRID_W
    return pl.pallas_call(
        functools.partial(_ffn_mid_kernel, vertical=vertical, cols=cols),
        grid=(m // tt, ncb),
        in_specs=[pl.BlockSpec((tt, cb), lambda i, j: (i, j)),
                  pl.BlockSpec((GRID_W, cb), lambda i, j: (jnp.maximum(i * t64 - 1, 0), j)),
                  pl.BlockSpec((GRID_W, cb), lambda i, j: (jnp.minimum((i + 1) * t64, n64 - 1), j)),
                  pl.BlockSpec((tt, cb), lambda i, j: (i, ncb + j)),
                  pl.BlockSpec((9, cb), lambda i, j: (0, j))],
        out_specs=pl.BlockSpec((tt, cb), lambda i, j: (i, j)),
        out_shape=jax.ShapeDtypeStruct((m, dff), BF16),
        scratch_shapes=[pltpu.VMEM((tt + 2 * _HALO, cb), F32)],
        compiler_params=_params("parallel", "parallel"),
        name="ffn_mid",
    )(up, up, up, up, conv_w.reshape(9, dff))


def _ffn_down_kernel(g_ref, w_ref, x_ref, gate_ref, ng_ref, nsh_ref, nsc_ref, o_ref, *h_ref, post):
    xn = x_ref[...] + gate_ref[...] * jnp.dot(g_ref[...], w_ref[...], preferred_element_type=F32)
    if post == 'final':
        ms = jnp.mean(xn * xn, axis=-1, keepdims=True)
        xn = xn * lax.rsqrt(ms + NORM_EPS) * ng_ref[...]
    o_ref[...] = xn
    if post == 'next':
        h_ref[0][...] = _norm_mod(xn, ng_ref[...], nsh_ref[...], nsc_ref[...]).astype(BF16)


def _ffn_down(g, w, x, gate, norm, post, layer):
    m, d = x.shape
    dff = g.shape[1]
    tm = min(m, 256)
    vec = pl.BlockSpec((1, d), lambda i: (0, 0))
    row = pl.BlockSpec((tm, d), lambda i: (i, 0))
    out_specs, out_shape = [row], [jax.ShapeDtypeStruct((m, d), F32)]
    if post == 'next':
        out_specs.append(row)
        out_shape.append(jax.ShapeDtypeStruct((m, d), BF16))
    return pl.pallas_call(
        functools.partial(_ffn_down_kernel, post=post),
        grid=(m // tm,),
        in_specs=[pl.BlockSpec((tm, dff), lambda i: (i, 0)),
                  _single((None, dff, d), lambda i: (layer, 0, 0)),
                  row, vec, vec, vec, vec],
        out_specs=out_specs,
        out_shape=out_shape,
        compiler_params=_params("parallel"),
        name="ffn_down",
    )(g, w, x, gate, *norm)


def _permute_w_in(w_in, hy_proj):
    qk = GLA_HEADS * GLA_DK
    vw = GLA_HEADS * GLA_DV
    g0 = hy_proj
    k = w_in[..., g0:g0 + qk]
    v = w_in[..., g0 + qk:g0 + qk + vw]
    r = w_in[..., g0 + qk + vw:g0 + qk + vw + 2 * GLA_RANK]
    q = w_in[..., g0 + qk + vw + 2 * GLA_RANK:g0 + 2 * qk + vw + 2 * GLA_RANK]
    g = w_in[..., g0 + 2 * qk + vw + 2 * GLA_RANK:]
    pad = jnp.zeros(w_in.shape[:-1] + (2 * LANES - 2 * GLA_RANK,), w_in.dtype)
    return jnp.concatenate([w_in[..., :hy_proj], k, q, v, g, r, pad], axis=-1).astype(BF16)


def _pad_gate_w(wg, lane0):
    return jnp.zeros((LANES, wg.shape[1]), F32).at[lane0:lane0 + GLA_RANK].set(wg)


def _mixer(p, lp, s0_f, s0_b, want_output):
    l = p.shape[0]
    width = lp['hy_bias'].shape[0]
    o_f, s_f = _gla_pass(p, lp['wg_f_pad'], lp['gla_bg_f'], s0_f, reverse=False)
    o, s_b = _gla_pass(p, lp['wg_b_pad'], lp['gla_bg_b'], s0_b, reverse=True, of=o_f, gout=lp['gla_out_g'])
    if not want_output:
        return None, None, s_f, s_b
    z, x0c = _hy_pre(p, lp['hy_conv'], width)
    if 2 * l // DFT_B >= 2 * SUBLANES:
        hy = _hyena_long(z, x0c, lp, lp['hy_bias'], lp['hy_out_g'])
    else:
        k2 = _hyena_filter2(l, lp['hy_w1'], lp['hy_b1'], lp['hy_freq'], lp['hy_w2'], lp['hy_b2'], lp['hy_w3'], width)
        hy = _hyena_short(z, x0c, k2, lp['hy_bias'], lp['hy_out_g'])
    return hy, o, s_f, s_b


def _ffn(h2, x, gate, lp, cols, norm, post, layer):
    up = _ffn_up(h2, lp['ffn_w_up'], layer)
    gact = _ffn_mid(up, lp['ffn_conv'], cols)
    return _ffn_down(gact, lp['ffn_w_down'], x, gate, norm, post, layer)


def kernel(x, c, ctx, c_ctx, w_mod, b_mod, g_attn, w_in, hy_conv, hy_w1, hy_b1, hy_freq, hy_w2, hy_b2, hy_w3,
           hy_bias, hy_out_g, gla_wg_f, gla_bg_f, gla_wg_b, gla_bg_b, gla_out_g, w_out, g_ffn, ffn_w_up,
           ffn_conv, ffn_w_down, g_final):
    depth = w_mod.shape[0]
    batch, seq, d = x.shape
    assert batch == 1 and c.shape[0] == 1 and ctx.shape[0] == 1
    hy_proj = hy_conv.shape[-1]
    xs = x[0]
    cs = ctx[0]
    cc = jnp.zeros((SUBLANES, d), F32).at[0].set(c[0]).at[1].set(c_ctx)
    zero_state = jnp.zeros(_GLA_STATE, F32)
    gfin = g_final.reshape(1, d)
    w_in_p = _permute_w_in(w_in, hy_proj)
    w_out_b = w_out.astype(BF16)
    w_down_b = ffn_w_down.astype(BF16)
    mods =[_adaln(cc, w_mod, b_mod, l) for l in range(depth)]

    def attn_norm(l, row):
        return (g_attn[l].reshape(1, d), mods[l][row:row + 1, 0:d], mods[l][row:row + 1, d:2 * d])

    hx = _norm_mod_call(xs, *attn_norm(0, 0))
    hc = _norm_mod_call(cs, *attn_norm(0, 1))
    for l in range(depth):
        last = l == depth - 1
        lp = {
            'hy_conv': hy_conv[l], 'hy_w1': hy_w1[l], 'hy_b1': hy_b1[l], 'hy_freq': hy_freq[l],
            'hy_w2': hy_w2[l], 'hy_b2': hy_b2[l], 'hy_w3': hy_w3[l], 'hy_bias': hy_bias[l],
            'hy_out_g': hy_out_g[l], 'gla_bg_f': gla_bg_f[l], 'gla_bg_b': gla_bg_b[l],
            'gla_out_g': gla_out_g[l],
            'wg_f_pad': _pad_gate_w(gla_wg_f[l], 0), 'wg_b_pad': _pad_gate_w(gla_wg_b[l], GLA_RANK),
            'ffn_w_up': ffn_w_up, 'ffn_conv': ffn_conv[l], 'ffn_w_down': w_down_b,
        }
        mod = mods[l]
        _, _, ga, sf, scf, gf = [mod[0:1, i * d:(i + 1) * d] for i in range(6)]
        _, _, cga, csf, cscf, cgf = [mod[1:2, i * d:(i + 1) * d] for i in range(6)]
        gf_row = g_ffn[l].reshape(1, d)

        pc = _in_proj(hc, w_in_p, l)
        hy_c, o_c, s_f, s_b = _mixer(pc, lp, zero_state, zero_state, want_output=not last)
        if not last:
            cs, h2c = _out_proj(hy_c, o_c, w_out_b, cs, cga, gf_row, csf, cscf, l)
            cs, hc = _ffn(h2c, cs, cgf, lp, cs.shape[0], attn_norm(l + 1, 1), 'next', l)

        px = _in_proj(hx, w_in_p, l)
        hy_x, o_x, _, _ = _mixer(px, lp, s_f, s_b, want_output=True)
        xs, h2 = _out_proj(hy_x, o_x, w_out_b, xs, ga, gf_row, sf, scf, l)
        if last:
            (xs,) = _ffn(h2, xs, gf, lp, GRID_W, (gfin, gfin, gfin), 'final', l)
        else:
            xs, hx = _ffn(h2, xs, gf, lp, GRID_W, attn_norm(l + 1, 0), 'next', l)
    return xs[None]
```

```python
import functools
import math

import numpy as np
import jax
import jax.numpy as jnp
from jax import lax
from jax.experimental import pallas as pl
from jax.experimental.pallas import tpu as pltpu

F32 = jnp.float32
BF16 = jnp.bfloat16
HIGHEST = lax.Precision.HIGHEST

LANES = 128
SUBLANES = 8
VMEM_BYTES_V7X = 64 * 1024 * 1024
VMEM_LIMIT = VMEM_BYTES_V7X - 12 * 1024 * 1024

NORM_EPS = 1e-6
HYENA_GROUP = 128
FILTER_BANDS = 16
FILTER_HIDDEN = 64
FILTER_TARGET = 1e-2
FILTER_FAST_PCT = 0.3
FILTER_SLOW_PCT = 1.5
GLA_HEADS = 8
GLA_DK = 64
GLA_DV = 128
GLA_RANK = 16
GLA_GATE_TAU = 16.0
GLA_CHUNK = 64
GRID_W = 64
DFT_B = 128


def _params(*sem):
    return pltpu.CompilerParams(dimension_semantics=sem, vmem_limit_bytes=VMEM_LIMIT)


def _silu(x):
    return x * jax.nn.sigmoid(x)


def _adaln_kernel(c_ref, w_ref, b_ref, o_ref):
    s = _silu(c_ref[...]).astype(BF16)
    o_ref[...] = jnp.dot(s, w_ref[...].astype(BF16), preferred_element_type=F32) + b_ref[...]


def _adaln(cc, w_mod, b_mod, layer):
    depth, d, n = w_mod.shape
    tn = 1024
    return pl.pallas_call(
        _adaln_kernel,
        grid=(n // tn,),
        in_specs=[pl.BlockSpec((SUBLANES, d), lambda j: (0, 0)),
                  pl.BlockSpec((None, d, tn), lambda j: (layer, 0, j)),
                  pl.BlockSpec((None, 1, tn), lambda j: (layer, 0, j))],
        out_specs=pl.BlockSpec((SUBLANES, tn), lambda j: (0, j)),
        out_shape=jax.ShapeDtypeStruct((SUBLANES, n), F32),
        compiler_params=_params("arbitrary"),
        name="adaln",
    )(cc, w_mod, b_mod.reshape(depth, 1, n))


def _norm_mod(x, g, shift, scale):
    ms = jnp.mean(x * x, axis=-1, keepdims=True)
    return (x * lax.rsqrt(ms + NORM_EPS) * g) * (1.0 + scale) + shift


def _norm_mod_kernel(x_ref, g_ref, sh_ref, sc_ref, o_ref):
    o_ref[...] = _norm_mod(x_ref[...], g_ref[...], sh_ref[...], sc_ref[...]).astype(o_ref.dtype)


def _norm_mod_call(x, g, shift, scale):
    m, d = x.shape
    tm = min(m, 512)
    vec = pl.BlockSpec((1, d), lambda i: (0, 0))
    return pl.pallas_call(
        _norm_mod_kernel,
        grid=(m // tm,),
        in_specs=[pl.BlockSpec((tm, d), lambda i: (i, 0)), vec, vec, vec],
        out_specs=pl.BlockSpec((tm, d), lambda i: (i, 0)),
        out_shape=jax.ShapeDtypeStruct((m, d), BF16),
        compiler_params=_params("parallel"),
        name="norm_mod",
    )(x, g, shift, scale)


def _in_proj_kernel(h_ref, w_ref, o_ref):
    o_ref[...] = jnp.dot(h_ref[...], w_ref[...], preferred_element_type=F32)


def _in_proj(h, w, layer):
    m, d = h.shape
    n = w.shape[2]
    tm = min(m, 1024)
    tn = 1280
    return pl.pallas_call(
        _in_proj_kernel,
        grid=(m // tm, n // tn),
        in_specs=[pl.BlockSpec((tm, d), lambda i, j: (i, 0)),
                  pl.BlockSpec((None, d, tn), lambda i, j: (layer, 0, j))],
        out_specs=pl.BlockSpec((tm, tn), lambda i, j: (i, j)),
        out_shape=jax.ShapeDtypeStruct((m, n), F32),
        compiler_params=_params("parallel", "parallel"),
        name="in_proj",
    )(h, w)


def _hy_pre_kernel(x0_ref, x1_ref, v_ref, x0p_ref, x1p_ref, vp_ref, x0n_ref, x1n_ref, vn_ref,
                   w0_ref, w1_ref, wv_ref, z_ref, x0c_ref):
    i = pl.program_id(0)
    first = i == 0
    last = i == pl.num_programs(0) - 1
    tt = x0_ref.shape[0]
    row = lax.broadcasted_iota(jnp.int32, x0_ref.shape, 0)

    def conv(ref, prev_ref, next_ref, w_ref):
        x = ref[...]
        prev_row = jnp.where(first, 0.0, prev_ref[SUBLANES - 1:SUBLANES, :])
        next_row = jnp.where(last, 0.0, next_ref[0:1, :])
        below = jnp.where(row == 0, prev_row, pltpu.roll(x, 1, 0))
        above = jnp.where(row == tt - 1, next_row, pltpu.roll(x, tt - 1, 0))
        return w_ref[0:1, :] * below + w_ref[1:2, :] * x + w_ref[2:3, :] * above

    x0c_ref[...] = conv(x0_ref, x0p_ref, x0n_ref, w0_ref)
    z_ref[...] = conv(v_ref, vp_ref, vn_ref, wv_ref) * conv(x1_ref, x1p_ref, x1n_ref, w1_ref)


def _hy_pre(p, hy_conv, width):
    l = p.shape[0]
    tt = min(l, 1024)
    cb = 512
    ncb = width // cb
    nb8 = l // SUBLANES
    t8 = tt // SUBLANES

    def main(g):
        return pl.BlockSpec((tt, cb), lambda i, j: (i, g * ncb + j))

    def prev(g):
        return pl.BlockSpec((SUBLANES, cb), lambda i, j: (jnp.maximum(i * t8 - 1, 0), g * ncb + j))

    def nxt(g):
        return pl.BlockSpec((SUBLANES, cb), lambda i, j: (jnp.minimum((i + 1) * t8, nb8 - 1), g * ncb + j))

    def wspec(g):
        return pl.BlockSpec((3, cb), lambda i, j: (0, g * ncb + j))

    out = pl.BlockSpec((tt, cb), lambda i, j: (i, j))
    return pl.pallas_call(
        _hy_pre_kernel,
        grid=(l // tt, ncb),
        in_specs=[main(0), main(1), main(2), prev(0), prev(1), prev(2), nxt(0), nxt(1), nxt(2),
                  wspec(0), wspec(1), wspec(2)],
        out_specs=[out, out],
        out_shape=[jax.ShapeDtypeStruct((l, width), F32), jax.ShapeDtypeStruct((l, width), F32)],
        compiler_params=_params("parallel", "parallel"),
        name="hy_pre",
    )(p, p, p, p, p, p, p, p, p, hy_conv, hy_conv, hy_conv)


def _filter_hidden(j, seq_len, w1t_ref, w1c_ref, w1s_ref, b1_ref, fr_ref, w2_ref, b2_ref):
    pos_i = jnp.where(j < seq_len, j, 2 * seq_len - j)
    pos = pos_i.astype(F32)
    t = pos / float(max(seq_len - 1, 1))
    lane = lax.broadcasted_iota(jnp.int32, j.shape, 1)
    band = jnp.where(lane < FILTER_BANDS, lane + 1, 0).astype(F32)
    ang = (2.0 * math.pi / seq_len) * pos * band
    fr = fr_ref[...]
    pre = (t[:, 0:1] * w1t_ref[...]
           + jnp.dot(jnp.cos(ang), w1c_ref[...], precision=HIGHEST, preferred_element_type=F32)
           + jnp.dot(jnp.sin(ang), w1s_ref[...], precision=HIGHEST, preferred_element_type=F32)
           + b1_ref[...])
    h = jnp.sin(fr * pre)
    h = jnp.sin(fr * (jnp.dot(h, w2_ref[...], precision=HIGHEST, preferred_element_type=F32) + b2_ref[...]))
    return h, pos_i, t


def _filter_kernel(w1t_ref, w1c_ref, w1s_ref, b1_ref, fr_ref, w2_ref, b2_ref, w3_ref, delta_ref, o_ref, *, seq_len):
    tr = o_ref.shape[0]
    j = pl.program_id(0) * tr + lax.broadcasted_iota(jnp.int32, (tr, LANES), 0)
    h, pos_i, t = _filter_hidden(j, seq_len, w1t_ref, w1c_ref, w1s_ref, b1_ref, fr_ref, w2_ref, b2_ref)
    hh = jnp.dot(h.astype(BF16), w3_ref[...].astype(BF16), preferred_element_type=F32)
    decay = jnp.exp(-t[:, 0:1] * delta_ref[...])
    valid = pos_i[:, 0:1] != seq_len
    o_ref[...] = jnp.where(valid, hh * decay, 0.0)


def _filter_hidden_kernel(w1t_ref, w1c_ref, w1s_ref, b1_ref, fr_ref, w2_ref, b2_ref, o_ref, *, seq_len, n_a):
    tr = o_ref.shape[0]
    r = pl.program_id(0) * tr + lax.broadcasted_iota(jnp.int32, (tr, LANES), 0)
    shift = n_a.bit_length() - 1
    j = ((r & (n_a - 1)) * DFT_B) + (r >> shift)
    h, _, _ = _filter_hidden(j, seq_len, w1t_ref, w1c_ref, w1s_ref, b1_ref, fr_ref, w2_ref, b2_ref)
    o_ref[...] = h


def _filter_mlp_args(w1, b1, freq, w2, b2):
    hid = FILTER_HIDDEN
    w1c = jnp.zeros((LANES, hid), F32).at[:FILTER_BANDS].set(w1[1:1 + FILTER_BANDS])
    w1s = jnp.zeros((LANES, hid), F32).at[:FILTER_BANDS].set(w1[1 + FILTER_BANDS:1 + 2 * FILTER_BANDS])
    full = lambda shape: pl.BlockSpec(shape, lambda i: (0, 0))
    specs = [full((1, hid)), full((LANES, hid)), full((LANES, hid)), full((1, hid)), full((1, hid)),
             full((hid, hid)), full((1, hid))]
    args = (w1[0:1], w1c, w1s, b1.reshape(1, hid), freq.reshape(1, hid), w2, b2.reshape(1, hid))
    return specs, args


def _filter_delta(width):
    return jnp.abs(jnp.linspace(math.log(FILTER_TARGET) / FILTER_SLOW_PCT,
                                math.log(FILTER_TARGET) / FILTER_FAST_PCT, width, dtype=F32)).reshape(1, width)


def _hyena_filter2(seq_len, w1, b1, freq, w2, b2, w3, width):
    n2 = 2 * seq_len
    tr = min(seq_len, 1024)
    nfwd = seq_len // tr
    specs, args = _filter_mlp_args(w1, b1, freq, w2, b2)
    return pl.pallas_call(
        functools.partial(_filter_kernel, seq_len=seq_len),
        grid=(n2 // tr,),
        in_specs=specs + [pl.BlockSpec((FILTER_HIDDEN, width), lambda i: (0, jnp.where(i < nfwd, 0, 1))),
                          pl.BlockSpec((1, width), lambda i: (0, 0))],
        out_specs=pl.BlockSpec((tr, width), lambda i: (i, 0)),
        out_shape=jax.ShapeDtypeStruct((n2, width), F32),
        compiler_params=_params("parallel"),
        name="hyena_filter",
    )(*args, w3, _filter_delta(width))


def _filter_hidden_permuted(seq_len, w1, b1, freq, w2, b2):
    n2 = 2 * seq_len
    n_a = n2 // DFT_B
    tr = min(n2, 1024)
    specs, args = _filter_mlp_args(w1, b1, freq, w2, b2)
    return pl.pallas_call(
        functools.partial(_filter_hidden_kernel, seq_len=seq_len, n_a=n_a),
        grid=(n2 // tr,),
        in_specs=specs,
        out_specs=pl.BlockSpec((tr, FILTER_HIDDEN), lambda i: (i, 0)),
        out_shape=jax.ShapeDtypeStruct((n2, FILTER_HIDDEN), F32),
        compiler_params=_params("parallel"),
        name="hyena_filter_hidden",
    )(*args)


def _dft_tables(n_a, a_in):
    n = n_a * DFT_B
    b = np.arange(DFT_B)[:, None, None]
    k1 = np.arange(n_a)[None, :, None]
    a = np.arange(a_in)[None, None, :]
    e = np.exp(-2j * np.pi * ((k1 * (DFT_B * a + b)) % n) / n)
    g = np.concatenate([e.real, e.imag], axis=1)
    return jnp.asarray(g, F32)


def _idft_tables(n_a, a_out):
    n = n_a * DFT_B
    b = np.arange(DFT_B)[:, None, None]
    a = np.arange(a_out)[None, :, None]
    k1 = np.arange(n_a)[None, None, :]
    e = np.exp(2j * np.pi * ((k1 * (DFT_B * a + b)) % n) / n) / n
    h = np.concatenate([e.real, -e.imag], axis=2)
    return jnp.asarray(h, F32)


def _dft128_embed():
    i = np.arange(DFT_B)
    f = np.exp(-2j * np.pi * ((i[:, None] * i[None, :]) % DFT_B) / DFT_B)
    fwd = np.block([[f.real, -f.imag], [f.imag, f.real]])
    inv = np.block([[f.real, f.imag], [-f.imag, f.real]])
    return jnp.asarray(fwd, F32), jnp.asarray(inv, F32)


_PITCH = DFT_B + SUBLANES
_K1_BLOCK = 8
_B_UNROLL = 8


def _stage_rows(k1):
    return pl.ds(pl.multiple_of(k1 * _PITCH, SUBLANES), DFT_B)


def _single(shape, index_map):
    return pl.BlockSpec(shape, index_map, pipeline_mode=pl.Buffered(1))


def _spectrum_kernel(h_ref, w3f_ref, w3b_ref, delta_ref, g_ref, wf_ref, o_ref, sr_ref, si_ref, *, seq_len, n_a):
    s = pl.program_id(1)
    half = n_a // 2

    @pl.when(s == 0)
    def _():
        a_i = lax.broadcasted_iota(jnp.int32, (n_a, LANES), 0)

        def body(b, carry):
            base = pl.multiple_of(b * n_a, SUBLANES)
            hf = jnp.dot(h_ref[pl.ds(base, half), :].astype(BF16), w3f_ref[...].astype(BF16),
                         preferred_element_type=F32)
            hb = jnp.dot(h_ref[pl.ds(base + half, half), :].astype(BF16), w3b_ref[...].astype(BF16),
                         preferred_element_type=F32)
            hh = jnp.concatenate([hf, hb], axis=0)
            j = a_i * DFT_B + b
            pos_i = jnp.where(j < seq_len, j, 2 * seq_len - j)
            t = pos_i.astype(F32) / float(max(seq_len - 1, 1))
            k2 = jnp.where(pos_i != seq_len, hh * jnp.exp(-t * delta_ref[...]), 0.0)
            r = jnp.dot(g_ref[b], k2.astype(BF16), preferred_element_type=F32)
            sr_ref[pl.ds(b, n_a, stride=_PITCH), :] = r[:n_a]
            si_ref[pl.ds(b, n_a, stride=_PITCH), :] = r[n_a:]
            return carry

        lax.fori_loop(0, DFT_B, body, 0, unroll=_B_UNROLL)

    @pl.when(s > 0)
    def _():
        k0 = (s - 1) * o_ref.shape[1]
        for i in range(o_ref.shape[1]):
            rows = _stage_rows(k0 + i)
            x = jnp.concatenate([sr_ref[rows, :], si_ref[rows, :]], axis=0).astype(BF16)
            z = jnp.dot(wf_ref[...], x, preferred_element_type=F32)
            o_ref[0, i] = z[:DFT_B]
            o_ref[1, i] = z[DFT_B:]


def _filter_spectrum(hperm, w3, seq_len, width, g_k, wf):
    n_a = 2 * seq_len // DFT_B
    kb = min(n_a, _K1_BLOCK)
    ncb = width // LANES
    hid = FILTER_HIDDEN
    return pl.pallas_call(
        functools.partial(_spectrum_kernel, seq_len=seq_len, n_a=n_a),
        grid=(ncb, 1 + n_a // kb),
        in_specs=[_single(hperm.shape, lambda ci, s: (0, 0)),
                  pl.BlockSpec((hid, LANES), lambda ci, s: (0, ci)),
                  pl.BlockSpec((hid, LANES), lambda ci, s: (0, ncb + ci)),
                  pl.BlockSpec((1, LANES), lambda ci, s: (0, ci)),
                  _single(g_k.shape, lambda ci, s: (0, 0, 0)),
                  _single(wf.shape, lambda ci, s: (0, 0))],
        out_specs=pl.BlockSpec((2, kb, DFT_B, LANES), lambda ci, s: (0, jnp.maximum(s - 1, 0), 0, ci)),
        out_shape=jax.ShapeDtypeStruct((2, n_a, DFT_B, width), F32),
        scratch_shapes=[pltpu.VMEM((n_a * _PITCH, LANES), F32), pltpu.VMEM((n_a * _PITCH, LANES), F32)],
        compiler_params=_params("parallel", "arbitrary"),
        name="filter_spectrum",
    )(hperm, w3, w3, _filter_delta(width), g_k, wf)


def _hyena_epilogue(y, z, x0, bias, gain):
    yv = (y + z * bias) * x0
    ms = jnp.mean(yv * yv, axis=-1, keepdims=True)
    return yv * lax.rsqrt(ms + NORM_EPS) * gain


_EPI_ROWS = 512


def _long_conv_kernel(z_ref, x0_ref, kf_ref, g_ref, h_ref, wf_ref, wi_ref, bias_ref, gain_ref, o_ref,
                      sr_ref, si_ref, *, n_a):
    s = pl.program_id(1)
    half = n_a // 2
    kb = kf_ref.shape[1]
    nkb = n_a // kb

    @pl.when(s == 0)
    def _():
        def body(b, carry):
            xs = z_ref[pl.ds(b, half, stride=DFT_B), :].astype(BF16)
            r = jnp.dot(g_ref[b], xs, preferred_element_type=F32)
            sr_ref[pl.ds(b, n_a, stride=_PITCH), :] = r[:n_a]
            si_ref[pl.ds(b, n_a, stride=_PITCH), :] = r[n_a:]
            return carry

        lax.fori_loop(0, DFT_B, body, 0, unroll=_B_UNROLL)

    @pl.when((s > 0) & (s <= nkb))
    def _():
        k0 = (s - 1) * kb
        for i in range(kb):
            rows = _stage_rows(k0 + i)
            x = jnp.concatenate([sr_ref[rows, :], si_ref[rows, :]], axis=0).astype(BF16)
            zz = jnp.dot(wf_ref[...], x, preferred_element_type=F32)
            zr, zi = zz[:DFT_B], zz[DFT_B:]
            kr, ki = kf_ref[0, i], kf_ref[1, i]
            y = jnp.concatenate([zr * kr - zi * ki, zr * ki + zi * kr], axis=0).astype(BF16)
            cc = jnp.dot(wi_ref[...], y, preferred_element_type=F32)
            sr_ref[rows, :] = cc[:DFT_B]
            si_ref[rows, :] = cc[DFT_B:]

    @pl.when(s == nkb + 1)
    def _():
        def body(b, carry):
            x = jnp.concatenate([sr_ref[pl.ds(b, n_a, stride=_PITCH), :],
                                 si_ref[pl.ds(b, n_a, stride=_PITCH), :]], axis=0).astype(BF16)
            o_ref[pl.ds(b, half, stride=DFT_B), :] = jnp.dot(h_ref[b], x, preferred_element_type=F32)
            return carry

        lax.fori_loop(0, DFT_B, body, 0, unroll=_B_UNROLL)

        step = min(_EPI_ROWS, o_ref.shape[0])

        def epi(i, carry):
            rows = pl.ds(pl.multiple_of(i * step, step), step)
            o_ref[rows, :] = _hyena_epilogue(o_ref[rows, :], z_ref[rows, :], x0_ref[rows, :],
                                             bias_ref[...], gain_ref[...])
            return carry

        lax.fori_loop(0, o_ref.shape[0] // step, epi, 0)


def _long_conv(z, x0c, kf, g_z, h_tab, wf, wi, bias, gain):
    l, c = z.shape
    n_a = 2 * l // DFT_B
    kb = min(n_a, _K1_BLOCK)
    nkb = n_a // kb
    col = lambda ci, s: (0, ci)
    return pl.pallas_call(
        functools.partial(_long_conv_kernel, n_a=n_a),
        grid=(c // LANES, nkb + 2),
        in_specs=[_single((l, LANES), col), _single((l, LANES), col),
                  pl.BlockSpec((2, kb, DFT_B, LANES), lambda ci, s: (0, jnp.clip(s - 1, 0, nkb - 1), 0, ci)),
                  _single(g_z.shape, lambda ci, s: (0, 0, 0)), _single(h_tab.shape, lambda ci, s: (0, 0, 0)),
                  _single(wf.shape, lambda ci, s: (0, 0)), _single(wi.shape, lambda ci, s: (0, 0)),
                  pl.BlockSpec((1, LANES), col), pl.BlockSpec((1, LANES), col)],
        out_specs=pl.BlockSpec((l, LANES), col),
        out_shape=jax.ShapeDtypeStruct((l, c), F32),
        scratch_shapes=[pltpu.VMEM((n_a * _PITCH, LANES), F32), pltpu.VMEM((n_a * _PITCH, LANES), F32)],
        compiler_params=_params("parallel", "arbitrary"),
        name="long_conv",
    )(z, x0c, kf, g_z, h_tab, wf, wi, bias.reshape(1, c), gain.reshape(1, c))


def _hyena_long(z, x0c, filt, bias, gain):
    l, c = z.shape
    n_a = 2 * l // DFT_B
    wf, wi = _dft128_embed()
    wf, wi = wf.astype(BF16), wi.astype(BF16)
    g_z = _dft_tables(n_a, n_a // 2).astype(BF16)
    g_k = _dft_tables(n_a, n_a).astype(BF16)
    h_tab = _idft_tables(n_a, n_a // 2).astype(BF16)
    hperm = _filter_hidden_permuted(l, filt['hy_w1'], filt['hy_b1'], filt['hy_freq'], filt['hy_w2'], filt['hy_b2'])
    kf = _filter_spectrum(hperm, filt['hy_w3'], l, c, g_k, wf)
    return _long_conv(z, x0c, kf, g_z, h_tab, wf, wi, bias, gain)


def _ctx_conv_kernel(z_ref, k_ref, x0_ref, fz_ref, fk_ref, hi_ref, bias_ref, gain_ref, o_ref):
    n = k_ref.shape[0]
    zs = jnp.dot(fz_ref[...], z_ref[...].astype(BF16), preferred_element_type=F32)
    ks = jnp.dot(fk_ref[...], k_ref[...].astype(BF16), preferred_element_type=F32)
    zr, zi, kr, ki = zs[:n], zs[n:], ks[:n], ks[n:]
    y = jnp.concatenate([zr * kr - zi * ki, zr * ki + zi * kr], axis=0).astype(BF16)
    yt = jnp.dot(hi_ref[...], y, preferred_element_type=F32)
    o_ref[...] = _hyena_epilogue(yt, z_ref[...], x0_ref[...], bias_ref[...], gain_ref[...]).astype(o_ref.dtype)


def _hyena_short(z, x0c, k2, bias, gain):
    l, c = z.shape
    n = 2 * l
    k = np.arange(n)[:, None]
    t = np.arange(n)[None, :]
    e = np.exp(-2j * np.pi * ((k * t) % n) / n)
    fk = jnp.asarray(np.concatenate([e.real, e.imag], axis=0), F32).astype(BF16)
    fz = fk[:, :l]
    ei = np.conj(e[:, :l]).T / n
    hi = jnp.asarray(np.concatenate([ei.real, -ei.imag], axis=1), F32).astype(BF16)
    col = lambda rows: pl.BlockSpec((rows, LANES), lambda ci: (0, ci))
    full = lambda a: pl.BlockSpec(a.shape, lambda ci: (0, 0))
    return pl.pallas_call(
        _ctx_conv_kernel,
        grid=(c // LANES,),
        in_specs=[col(l), col(n), col(l), full(fz), full(fk), full(hi), col(1), col(1)],
        out_specs=col(l),
        out_shape=jax.ShapeDtypeStruct((l, c), F32),
        compiler_params=_params("parallel"),
        name="hyena_short",
    )(z, k2, x0c, fz, fk, hi, bias.reshape(1, c), gain.reshape(1, c))


def _split3(x):
    hi = x.astype(BF16)
    r1 = x - hi.astype(F32)
    mid = r1.astype(BF16)
    lo = (r1 - mid.astype(F32)).astype(BF16)
    return hi, mid, lo


def _gla_kernel(*refs, reverse, combine):
    if combine:
        (k_ref, q_ref, v_ref, r_ref, wg_ref, bg_ref, s0_ref, of_ref, gate_ref, gout_ref,
         o_ref, sfin_ref, st_ref, la_ref, qe_ref, upd_ref, dec_ref, ob_ref) = refs
    else:
        (k_ref, q_ref, v_ref, r_ref, wg_ref, bg_ref, s0_ref,
         o_ref, sfin_ref, st_ref, la_ref, qe_ref, upd_ref, dec_ref) = refs
        ob_ref = o_ref
    rows_blk = k_ref.shape[0]
    nc = rows_blk // GLA_CHUNK
    qk = k_ref.shape[1]

    @pl.when(pl.program_id(0) == 0)
    def _():
        st_ref[...] = s0_ref[...]

    ri = lax.broadcasted_iota(jnp.int32, (2 * GLA_CHUNK, GLA_CHUNK), 0) % GLA_CHUNK
    ci = lax.broadcasted_iota(jnp.int32, (2 * GLA_CHUNK, GLA_CHUNK), 1)
    r3 = lax.broadcasted_iota(jnp.int32, (GLA_CHUNK, 3 * GLA_CHUNK), 0)
    c3 = lax.broadcasted_iota(jnp.int32, (GLA_CHUNK, 3 * GLA_CHUNK), 1) % GLA_CHUNK
    if reverse:
        tri3 = (c3 >= r3).astype(BF16)
        keep = ci > ri
    else:
        tri3 = (c3 <= r3).astype(BF16)
        keep = ci <= ri
    lane = lax.broadcasted_iota(jnp.int32, (1, LANES), 1)
    first = (lane < GLA_DK).astype(F32)
    srow = lax.broadcasted_iota(jnp.int32, (2 * GLA_DV, LANES), 0)
    scol = lax.broadcasted_iota(jnp.int32, (2 * GLA_DV, LANES), 1)
    own = (srow < GLA_DV) == (scol < GLA_DK)
    nt = (((1,), (1,)), ((), ()))
    tn = (((0,), (0,)), ((), ()))
    npair = GLA_HEADS // 2

    pre = jnp.dot(r_ref[...], wg_ref[...], precision=HIGHEST, preferred_element_type=F32) + bg_ref[...]
    la_ref[...] = jax.nn.log_sigmoid(pre) / GLA_GATE_TAU

    def local(c, carry):
        rows = pl.ds(pl.multiple_of(c * GLA_CHUNK, GLA_CHUNK), GLA_CHUNK)
        hi, mid, lo = _split3(la_ref[rows, :])
        cum = jnp.dot(tri3, jnp.concatenate([hi, mid, lo], axis=0), preferred_element_type=F32)
        tot = cum[0:1] if reverse else cum[GLA_CHUNK - 1:GLA_CHUNK]
        k = k_ref[rows, :]
        qe = q_ref[rows, :] * (GLA_DK ** -0.5) * jnp.exp(cum)
        ke = (k * jnp.exp(-cum)).astype(BF16)
        kl = (k * jnp.exp(tot - cum)).astype(BF16)
        dec_ref[pl.ds(c, 1), :] = jnp.exp(tot)
        qe_ref[rows, :] = qe.astype(BF16)
        for p in range(npair):
            pr = slice(p * LANES, (p + 1) * LANES)
            pv = slice(2 * p * GLA_DV, 2 * (p + 1) * GLA_DV)
            q2 = jnp.concatenate([qe[:, pr] * first, qe[:, pr] * (1.0 - first)], axis=0).astype(BF16)
            sc = lax.dot_general(q2, ke[:, pr], nt, preferred_element_type=F32)
            sc = jnp.where(keep, sc, 0.0).astype(BF16)
            v_p = v_ref[rows, pv].astype(BF16)
            o2 = jnp.dot(sc, v_p, preferred_element_type=F32)
            ob_ref[rows, pl.ds(2 * p * GLA_DV, GLA_DV)] = o2[:GLA_CHUNK, :GLA_DV]
            ob_ref[rows, pl.ds((2 * p + 1) * GLA_DV, GLA_DV)] = o2[GLA_CHUNK:, GLA_DV:]
            u2 = lax.dot_general(v_p, kl[:, pr], tn, preferred_element_type=F32)
            upd_ref[c * npair + p] = jnp.where(own, u2, 0.0)
        return carry

    lax.fori_loop(0, nc, local, 0, unroll=2)

    def carried(idx, carry):
        c = (nc - 1 - idx) if reverse else idx
        rows = pl.ds(pl.multiple_of(c * GLA_CHUNK, GLA_CHUNK), GLA_CHUNK)
        dec = dec_ref[pl.ds(c, 1), :]
        for p in range(npair):
            pr = slice(p * LANES, (p + 1) * LANES)
            pv = slice(2 * p * GLA_DV, 2 * (p + 1) * GLA_DV)
            st = st_ref[p]
            ob_ref[rows, pv] += lax.dot_general(qe_ref[rows, pr], st.astype(BF16), nt, preferred_element_type=F32)
            st_ref[p] = st * dec[:, pr] + upd_ref[c * npair + p]
        return carry

    lax.fori_loop(0, nc, carried, 0)

    if combine:
        for h in range(GLA_HEADS):
            hv = slice(h * GLA_DV, (h + 1) * GLA_DV)
            o = of_ref[:, hv] + ob_ref[:, hv]
            ms = jnp.mean(o * o, axis=-1, keepdims=True)
            o = o * lax.rsqrt(ms + NORM_EPS) * gout_ref[...]
            o_ref[:, hv] = (o * _silu(gate_ref[:, hv])).astype(o_ref.dtype)

    @pl.when(pl.program_id(0) == pl.num_programs(0) - 1)
    def _():
        sfin_ref[...] = st_ref[...]


_GLA_STATE = (GLA_HEADS // 2, 2 * GLA_DV, LANES)

_COL_K, _COL_Q, _COL_V, _COL_G, _COL_R = 24, 28, 32, 40, 48


def _gla_pass(p, wg_pad, bg, s0, reverse, of=None, gout=None):
    l = p.shape[0]
    rb = min(l, 512)
    nblk = l // rb
    qk = GLA_HEADS * GLA_DK
    vw = GLA_HEADS * GLA_DV
    combine = of is not None
    rmap = (lambda i: nblk - 1 - i) if reverse else (lambda i: i)

    def cols(width, off128):
        return pl.BlockSpec((rb, width), lambda i: (rmap(i), off128 * LANES // width))

    state = pl.BlockSpec(_GLA_STATE, lambda i: (0, 0, 0))
    in_specs = [cols(qk, _COL_K), cols(qk, _COL_Q), cols(vw, _COL_V), cols(LANES, _COL_R),
                pl.BlockSpec((LANES, qk), lambda i: (0, 0)), pl.BlockSpec((1, qk), lambda i: (0, 0)), state]
    args = [p, p, p, p, wg_pad, bg.reshape(1, qk), s0]
    nc = rb // GLA_CHUNK
    scratch = [pltpu.VMEM(_GLA_STATE, F32),
               pltpu.VMEM((rb, qk), F32),
               pltpu.VMEM((rb, qk), BF16),
               pltpu.VMEM((nc * _GLA_STATE[0],) + _GLA_STATE[1:], F32),
               pltpu.VMEM((nc, qk), F32)]
    if combine:
        in_specs += [pl.BlockSpec((rb, vw), lambda i: (rmap(i), 0)), cols(vw, _COL_G),
                     pl.BlockSpec((1, GLA_DV), lambda i: (0, 0))]
        args += [of, p, gout.reshape(1, GLA_DV)]
        scratch.append(pltpu.VMEM((rb, vw), F32))
    out_dtype = BF16 if combine else F32
    return pl.pallas_call(
        functools.partial(_gla_kernel, reverse=reverse, combine=combine),
        grid=(nblk,),
        in_specs=in_specs,
        out_specs=[pl.BlockSpec((rb, vw), lambda i: (rmap(i), 0)), state],
        out_shape=[jax.ShapeDtypeStruct((l, vw), out_dtype),
                   jax.ShapeDtypeStruct(_GLA_STATE, F32)],
        scratch_shapes=scratch,
        compiler_params=_params("arbitrary"),
        name="gla_bwd" if reverse else "gla_fwd",
    )(*args)


def _out_proj_kernel(hy_ref, o_ref, w_ref, x_ref, ga_ref, g_ref, sh_ref, sc_ref, xo_ref, h_ref):
    half = hy_ref.shape[1]
    mix = (jnp.dot(hy_ref[...].astype(BF16), w_ref[:half, :], preferred_element_type=F32)
           + jnp.dot(o_ref[...], w_ref[half:, :], preferred_element_type=F32))
    xn = x_ref[...] + ga_ref[...] * mix
    xo_ref[...] = xn
    h_ref[...] = _norm_mod(xn, g_ref[...], sh_ref[...], sc_ref[...]).astype(BF16)


def _out_proj(hy, o, w, x, ga, g, shift, scale, layer):
    m, d = x.shape
    half = hy.shape[1]
    tm = min(m, 256)
    vec = pl.BlockSpec((1, d), lambda i: (0, 0))
    row = lambda width: pl.BlockSpec((tm, width), lambda i: (i, 0))
    return pl.pallas_call(
        _out_proj_kernel,
        grid=(m // tm,),
        in_specs=[row(half), row(half), pl.BlockSpec((None, 2 * half, d), lambda i: (layer, 0, 0)), row(d),
                  vec, vec, vec, vec],
        out_specs=[row(d), row(d)],
        out_shape=[jax.ShapeDtypeStruct((m, d), F32), jax.ShapeDtypeStruct((m, d), BF16)],
        compiler_params=_params("parallel"),
        name="out_proj",
    )(hy, o, w, x, ga, g, shift, scale)


_HALO = GRID_W + SUBLANES


def _ffn_up_conv_kernel(x_ref, wa_ref, wu_ref, cw_ref, o_ref, a_ref, u_ref, *, vertical, cols, nm):
    j = pl.program_id(0)
    i = pl.program_id(1)
    tt = x_ref.shape[0]
    cur = i % 2
    prv = 1 - cur

    @pl.when((j == 0) & (i == 0))
    def _():
        a_ref[...] = jnp.zeros_like(a_ref)
        u_ref[...] = jnp.zeros_like(u_ref)

    def conv_previous_tile():
        col = lax.broadcasted_iota(jnp.int32, o_ref.shape, 0) & (cols - 1)
        acc = None
        for dc in (-1, 0, 1):
            part = None
            for dr in ((-1, 0, 1) if vertical else (0,)):
                tap = cw_ref[3 * (dr + 1) + (dc + 1):3 * (dr + 1) + (dc + 1) + 1, :]
                term = tap * a_ref[prv, pl.ds(_HALO + GRID_W * dr + dc, tt), :]
                part = term if part is None else part + term
            if dc == -1:
                part = jnp.where(col == 0, 0.0, part)
            elif dc == 1:
                part = jnp.where(col == cols - 1, 0.0, part)
            acc = part if acc is None else acc + part
        o_ref[...] = (_silu(acc) * u_ref[prv].astype(F32)).astype(o_ref.dtype)

    @pl.when(i < nm)
    def _():
        x = x_ref[...]
        a_new = jnp.dot(x, wa_ref[...].astype(BF16), preferred_element_type=F32)
        u_ref[cur] = jnp.dot(x, wu_ref[...].astype(BF16), preferred_element_type=F32).astype(BF16)
        upper = a_ref[prv, pl.ds(_HALO + tt - GRID_W, GRID_W), :]
        a_ref[cur, pl.ds(SUBLANES, GRID_W), :] = jnp.where(i == 0, 0.0, upper)
        a_ref[cur, pl.ds(_HALO, tt), :] = a_new
        a_ref[prv, pl.ds(_HALO + tt, GRID_W), :] = a_new[:GRID_W]
        conv_previous_tile()

    @pl.when(i == nm)
    def _():
        a_ref[prv, pl.ds(_HALO + tt, GRID_W), :] = jnp.zeros((GRID_W, a_ref.shape[2]), F32)
        conv_previous_tile()


def _ffn_up_conv(h, w_up, conv_w, cols, layer):
    m, d = h.shape
    dff = conv_w.shape[-1]
    vertical = m > cols
    assert cols & (cols - 1) == 0 and (cols == GRID_W or not vertical)
    tt = min(m, 1024)
    assert tt % cols == 0
    cb = 512
    ncb = dff // cb
    nm = m // tt
    return pl.pallas_call(
        functools.partial(_ffn_up_conv_kernel, vertical=vertical, cols=cols, nm=nm),
        grid=(ncb, nm + 1),
        in_specs=[pl.BlockSpec((tt, d), lambda j, i: (jnp.minimum(i, nm - 1), 0)),
                  pl.BlockSpec((None, d, cb), lambda j, i: (layer, 0, j)),
                  pl.BlockSpec((None, d, cb), lambda j, i: (layer, 0, ncb + j)),
                  pl.BlockSpec((9, cb), lambda j, i: (0, j))],
        out_specs=pl.BlockSpec((tt, cb), lambda j, i: (jnp.maximum(i - 1, 0), j)),
        out_shape=jax.ShapeDtypeStruct((m, dff), BF16),
        scratch_shapes=[pltpu.VMEM((2, tt + 2 * _HALO, cb), F32), pltpu.VMEM((2, tt, cb), BF16)],
        compiler_params=_params("arbitrary", "arbitrary"),
        name="ffn_up_conv",
    )(h, w_up, w_up, conv_w.reshape(9, dff))


def _ffn_down_kernel(g_ref, w_ref, x_ref, gate_ref, ng_ref, nsh_ref, nsc_ref, o_ref, *h_ref, post):
    xn = x_ref[...] + gate_ref[...] * jnp.dot(g_ref[...], w_ref[...], preferred_element_type=F32)
    if post == 'final':
        ms = jnp.mean(xn * xn, axis=-1, keepdims=True)
        xn = xn * lax.rsqrt(ms + NORM_EPS) * ng_ref[...]
    o_ref[...] = xn
    if post == 'next':
        h_ref[0][...] = _norm_mod(xn, ng_ref[...], nsh_ref[...], nsc_ref[...]).astype(BF16)


def _ffn_down(g, w, x, gate, norm, post, layer):
    m, d = x.shape
    dff = g.shape[1]
    tm = min(m, 256)
    vec = pl.BlockSpec((1, d), lambda i: (0, 0))
    row = pl.BlockSpec((tm, d), lambda i: (i, 0))
    out_specs, out_shape = [row], [jax.ShapeDtypeStruct((m, d), F32)]
    if post == 'next':
        out_specs.append(row)
        out_shape.append(jax.ShapeDtypeStruct((m, d), BF16))
    return pl.pallas_call(
        functools.partial(_ffn_down_kernel, post=post),
        grid=(m // tm,),
        in_specs=[pl.BlockSpec((tm, dff), lambda i: (i, 0)),
                  _single((None, dff, d), lambda i: (layer, 0, 0)),
                  row, vec, vec, vec, vec],
        out_specs=out_specs,
        out_shape=out_shape,
        compiler_params=_params("parallel"),
        name="ffn_down",
    )(g, w, x, gate, *norm)


def _permute_w_in(w_in, hy_proj):
    qk = GLA_HEADS * GLA_DK
    vw = GLA_HEADS * GLA_DV
    g0 = hy_proj
    k = w_in[..., g0:g0 + qk]
    v = w_in[..., g0 + qk:g0 + qk + vw]
    r = w_in[..., g0 + qk + vw:g0 + qk + vw + 2 * GLA_RANK]
    q = w_in[..., g0 + qk + vw + 2 * GLA_RANK:g0 + 2 * qk + vw + 2 * GLA_RANK]
    g = w_in[..., g0 + 2 * qk + vw + 2 * GLA_RANK:]
    pad = jnp.zeros(w_in.shape[:-1] + (2 * LANES - 2 * GLA_RANK,), w_in.dtype)
    return jnp.concatenate([w_in[..., :hy_proj], k, q, v, g, r, pad], axis=-1).astype(BF16)


def _pad_gate_w(wg, lane0):
    return jnp.zeros((LANES, wg.shape[1]), F32).at[lane0:lane0 + GLA_RANK].set(wg)


def _mixer(p, lp, s0_f, s0_b, want_output):
    l = p.shape[0]
    width = lp['hy_bias'].shape[0]
    o_f, s_f = _gla_pass(p, lp['wg_f_pad'], lp['gla_bg_f'], s0_f, reverse=False)
    o, s_b = _gla_pass(p, lp['wg_b_pad'], lp['gla_bg_b'], s0_b, reverse=True, of=o_f, gout=lp['gla_out_g'])
    if not want_output:
        return None, None, s_f, s_b
    z, x0c = _hy_pre(p, lp['hy_conv'], width)
    if 2 * l // DFT_B >= 2 * SUBLANES:
        hy = _hyena_long(z, x0c, lp, lp['hy_bias'], lp['hy_out_g'])
    else:
        k2 = _hyena_filter2(l, lp['hy_w1'], lp['hy_b1'], lp['hy_freq'], lp['hy_w2'], lp['hy_b2'], lp['hy_w3'], width)
        hy = _hyena_short(z, x0c, k2, lp['hy_bias'], lp['hy_out_g'])
    return hy, o, s_f, s_b


def _ffn(h2, x, gate, lp, cols, norm, post, layer):
    gact = _ffn_up_conv(h2, lp['ffn_w_up'], lp['ffn_conv'], cols, layer)
    return _ffn_down(gact, lp['ffn_w_down'], x, gate, norm, post, layer)


def kernel(x, c, ctx, c_ctx, w_mod, b_mod, g_attn, w_in, hy_conv, hy_w1, hy_b1, hy_freq, hy_w2, hy_b2, hy_w3,
           hy_bias, hy_out_g, gla_wg_f, gla_bg_f, gla_wg_b, gla_bg_b, gla_out_g, w_out, g_ffn, ffn_w_up,
           ffn_conv, ffn_w_down, g_final):
    depth = w_mod.shape[0]
    batch, seq, d = x.shape
    assert batch == 1 and c.shape[0] == 1 and ctx.shape[0] == 1
    hy_proj = hy_conv.shape[-1]
    xs = x[0]
    cs = ctx[0]
    cc = jnp.zeros((SUBLANES, d), F32).at[0].set(c[0]).at[1].set(c_ctx)
    zero_state = jnp.zeros(_GLA_STATE, F32)
    gfin = g_final.reshape(1, d)
    w_in_p = _permute_w_in(w_in, hy_proj)
    w_out_b = w_out.astype(BF16)
    w_down_b = ffn_w_down.astype(BF16)
    mods =[_adaln(cc, w_mod, b_mod, l) for l in range(depth)]

    def attn_norm(l, row):
        return (g_attn[l].reshape(1, d), mods[l][row:row + 1, 0:d], mods[l][row:row + 1, d:2 * d])

    hx = _norm_mod_call(xs, *attn_norm(0, 0))
    hc = _norm_mod_call(cs, *attn_norm(0, 1))
    for l in range(depth):
        last = l == depth - 1
        lp = {
            'hy_conv': hy_conv[l], 'hy_w1': hy_w1[l], 'hy_b1': hy_b1[l], 'hy_freq': hy_freq[l],
            'hy_w2': hy_w2[l], 'hy_b2': hy_b2[l], 'hy_w3': hy_w3[l], 'hy_bias': hy_bias[l],
            'hy_out_g': hy_out_g[l], 'gla_bg_f': gla_bg_f[l], 'gla_bg_b': gla_bg_b[l],
            'gla_out_g': gla_out_g[l],
            'wg_f_pad': _pad_gate_w(gla_wg_f[l], 0), 'wg_b_pad': _pad_gate_w(gla_wg_b[l], GLA_RANK),
            'ffn_w_up': ffn_w_up, 'ffn_conv': ffn_conv[l], 'ffn_w_down': w_down_b,
        }
        mod = mods[l]
        _, _, ga, sf, scf, gf = [mod[0:1, i * d:(i + 1) * d] for i in range(6)]
        _, _, cga, csf, cscf, cgf = [mod[1:2, i * d:(i + 1) * d] for i in range(6)]
        gf_row = g_ffn[l].reshape(1, d)

        pc = _in_proj(hc, w_in_p, l)
        hy_c, o_c, s_f, s_b = _mixer(pc, lp, zero_state, zero_state, want_output=not last)
        if not last:
            cs, h2c = _out_proj(hy_c, o_c, w_out_b, cs, cga, gf_row, csf, cscf, l)
            cs, hc = _ffn(h2c, cs, cgf, lp, cs.shape[0], attn_norm(l + 1, 1), 'next', l)

        px = _in_proj(hx, w_in_p, l)
        hy_x, o_x, _, _ = _mixer(px, lp, s_f, s_b, want_output=True)
        xs, h2 = _out_proj(hy_x, o_x, w_out_b, xs, ga, gf_row, sf, scf, l)
        if last:
            (xs,) = _ffn(h2, xs, gf, lp, GRID_W, (gfin, gfin, gfin), 'final', l)
        else:
            xs, hx = _ffn(h2, xs, gf, lp, GRID_W, attn_norm(l + 1, 0), 'next', l)
    return xs[None]
```

```python
import functools
import math

import numpy as np
import jax
import jax.numpy as jnp
from jax import lax
from jax.experimental import pallas as pl
from jax.experimental.pallas import tpu as pltpu

F32 = jnp.float32
BF16 = jnp.bfloat16
HIGHEST = lax.Precision.HIGHEST

LANES = 128
SUBLANES = 8
VMEM_BYTES_V7X = 64 * 1024 * 1024
VMEM_LIMIT = VMEM_BYTES_V7X - 12 * 1024 * 1024

NORM_EPS = 1e-6
HYENA_GROUP = 128
FILTER_BANDS = 16
FILTER_HIDDEN = 64
FILTER_TARGET = 1e-2
FILTER_FAST_PCT = 0.3
FILTER_SLOW_PCT = 1.5
GLA_HEADS = 8
GLA_DK = 64
GLA_DV = 128
GLA_RANK = 16
GLA_GATE_TAU = 16.0
GLA_CHUNK = 64
GRID_W = 64
DFT_B = 128


def _params(*sem):
    return pltpu.CompilerParams(dimension_semantics=sem, vmem_limit_bytes=VMEM_LIMIT)


def _silu(x):
    return x * jax.nn.sigmoid(x)


def _adaln_kernel(c_ref, w_ref, b_ref, o_ref):
    s = _silu(c_ref[...]).astype(BF16)
    o_ref[...] = jnp.dot(s, w_ref[...].astype(BF16), preferred_element_type=F32) + b_ref[...]


def _adaln(cc, w_mod, b_mod, layer):
    depth, d, n = w_mod.shape
    tn = 1024
    return pl.pallas_call(
        _adaln_kernel,
        grid=(n // tn,),
        in_specs=[pl.BlockSpec((SUBLANES, d), lambda j: (0, 0)),
                  pl.BlockSpec((None, d, tn), lambda j: (layer, 0, j)),
                  pl.BlockSpec((None, 1, tn), lambda j: (layer, 0, j))],
        out_specs=pl.BlockSpec((SUBLANES, tn), lambda j: (0, j)),
        out_shape=jax.ShapeDtypeStruct((SUBLANES, n), F32),
        compiler_params=_params("arbitrary"),
        name="adaln",
    )(cc, w_mod, b_mod.reshape(depth, 1, n))


def _norm_mod(x, g, shift, scale):
    ms = jnp.mean(x * x, axis=-1, keepdims=True)
    return (x * lax.rsqrt(ms + NORM_EPS) * g) * (1.0 + scale) + shift


def _norm_mod_kernel(x_ref, g_ref, sh_ref, sc_ref, o_ref):
    o_ref[...] = _norm_mod(x_ref[...], g_ref[...], sh_ref[...], sc_ref[...]).astype(o_ref.dtype)


def _norm_mod_call(x, g, shift, scale):
    m, d = x.shape
    tm = min(m, 512)
    vec = pl.BlockSpec((1, d), lambda i: (0, 0))
    return pl.pallas_call(
        _norm_mod_kernel,
        grid=(m // tm,),
        in_specs=[pl.BlockSpec((tm, d), lambda i: (i, 0)), vec, vec, vec],
        out_specs=pl.BlockSpec((tm, d), lambda i: (i, 0)),
        out_shape=jax.ShapeDtypeStruct((m, d), BF16),
        compiler_params=_params("parallel"),
        name="norm_mod",
    )(x, g, shift, scale)


def _in_proj_kernel(h_ref, w_ref, o_ref):
    o_ref[...] = jnp.dot(h_ref[...], w_ref[...], preferred_element_type=F32)


def _in_proj(h, w, layer):
    m, d = h.shape
    n = w.shape[2]
    tm = min(m, 1024)
    tn = 1280
    return pl.pallas_call(
        _in_proj_kernel,
        grid=(m // tm, n // tn),
        in_specs=[pl.BlockSpec((tm, d), lambda i, j: (i, 0)),
                  pl.BlockSpec((None, d, tn), lambda i, j: (layer, 0, j))],
        out_specs=pl.BlockSpec((tm, tn), lambda i, j: (i, j)),
        out_shape=jax.ShapeDtypeStruct((m, n), F32),
        compiler_params=_params("parallel", "parallel"),
        name="in_proj",
    )(h, w)


def _hy_pre_kernel(x0_ref, x1_ref, v_ref, x0p_ref, x1p_ref, vp_ref, x0n_ref, x1n_ref, vn_ref,
                   w0_ref, w1_ref, wv_ref, z_ref, x0c_ref):
    i = pl.program_id(0)
    first = i == 0
    last = i == pl.num_programs(0) - 1
    tt = x0_ref.shape[0]
    row = lax.broadcasted_iota(jnp.int32, x0_ref.shape, 0)

    def conv(ref, prev_ref, next_ref, w_ref):
        x = ref[...]
        prev_row = jnp.where(first, 0.0, prev_ref[SUBLANES - 1:SUBLANES, :])
        next_row = jnp.where(last, 0.0, next_ref[0:1, :])
        below = jnp.where(row == 0, prev_row, pltpu.roll(x, 1, 0))
        above = jnp.where(row == tt - 1, next_row, pltpu.roll(x, tt - 1, 0))
        return w_ref[0:1, :] * below + w_ref[1:2, :] * x + w_ref[2:3, :] * above

    x0c_ref[...] = conv(x0_ref, x0p_ref, x0n_ref, w0_ref)
    z_ref[...] = conv(v_ref, vp_ref, vn_ref, wv_ref) * conv(x1_ref, x1p_ref, x1n_ref, w1_ref)


def _hy_pre(p, hy_conv, width):
    l = p.shape[0]
    tt = min(l, 1024)
    cb = 512
    ncb = width // cb
    nb8 = l // SUBLANES
    t8 = tt // SUBLANES

    def main(g):
        return pl.BlockSpec((tt, cb), lambda i, j: (i, g * ncb + j))

    def prev(g):
        return pl.BlockSpec((SUBLANES, cb), lambda i, j: (jnp.maximum(i * t8 - 1, 0), g * ncb + j))

    def nxt(g):
        return pl.BlockSpec((SUBLANES, cb), lambda i, j: (jnp.minimum((i + 1) * t8, nb8 - 1), g * ncb + j))

    def wspec(g):
        return pl.BlockSpec((3, cb), lambda i, j: (0, g * ncb + j))

    out = pl.BlockSpec((tt, cb), lambda i, j: (i, j))
    return pl.pallas_call(
        _hy_pre_kernel,
        grid=(l // tt, ncb),
        in_specs=[main(0), main(1), main(2), prev(0), prev(1), prev(2), nxt(0), nxt(1), nxt(2),
                  wspec(0), wspec(1), wspec(2)],
        out_specs=[out, out],
        out_shape=[jax.ShapeDtypeStruct((l, width), F32), jax.ShapeDtypeStruct((l, width), F32)],
        compiler_params=_params("parallel", "parallel"),
        name="hy_pre",
    )(p, p, p, p, p, p, p, p, p, hy_conv, hy_conv, hy_conv)


def _filter_hidden(j, seq_len, w1t_ref, w1c_ref, w1s_ref, b1_ref, fr_ref, w2_ref, b2_ref):
    pos_i = jnp.where(j < seq_len, j, 2 * seq_len - j)
    pos = pos_i.astype(F32)
    t = pos / float(max(seq_len - 1, 1))
    lane = lax.broadcasted_iota(jnp.int32, j.shape, 1)
    band = jnp.where(lane < FILTER_BANDS, lane + 1, 0).astype(F32)
    ang = (2.0 * math.pi / seq_len) * pos * band
    fr = fr_ref[...]
    pre = (t[:, 0:1] * w1t_ref[...]
           + jnp.dot(jnp.cos(ang), w1c_ref[...], precision=HIGHEST, preferred_element_type=F32)
           + jnp.dot(jnp.sin(ang), w1s_ref[...], precision=HIGHEST, preferred_element_type=F32)
           + b1_ref[...])
    h = jnp.sin(fr * pre)
    h = jnp.sin(fr * (jnp.dot(h, w2_ref[...], precision=HIGHEST, preferred_element_type=F32) + b2_ref[...]))
    return h, pos_i, t


def _filter_kernel(w1t_ref, w1c_ref, w1s_ref, b1_ref, fr_ref, w2_ref, b2_ref, w3_ref, delta_ref, o_ref, *, seq_len):
    tr = o_ref.shape[0]
    j = pl.program_id(0) * tr + lax.broadcasted_iota(jnp.int32, (tr, LANES), 0)
    h, pos_i, t = _filter_hidden(j, seq_len, w1t_ref, w1c_ref, w1s_ref, b1_ref, fr_ref, w2_ref, b2_ref)
    hh = jnp.dot(h.astype(BF16), w3_ref[...].astype(BF16), preferred_element_type=F32)
    decay = jnp.exp(-t[:, 0:1] * delta_ref[...])
    valid = pos_i[:, 0:1] != seq_len
    o_ref[...] = jnp.where(valid, hh * decay, 0.0)


_FILTER_PACK = LANES // FILTER_BANDS


def _filter_hidden_kernel(w1t_ref, w1c_ref, w1s_ref, b1_ref, fr_ref, w2_ref, b2_ref, o_ref, *, seq_len, n_a):
    tp = o_ref.shape[0]
    shift = n_a.bit_length() - 1

    def position(shape, lanes_per_pos):
        prow = pl.program_id(0) * tp + lax.broadcasted_iota(jnp.int32, shape, 0)
        r = prow * _FILTER_PACK + lax.broadcasted_iota(jnp.int32, shape, 1) // lanes_per_pos
        j = ((r & (n_a - 1)) * DFT_B) + (r >> shift)
        return jnp.where(j < seq_len, j, 2 * seq_len - j).astype(F32)

    pos = position((tp, LANES), FILTER_BANDS)
    band = (lax.broadcasted_iota(jnp.int32, (tp, LANES), 1) % FILTER_BANDS + 1).astype(F32)
    ang = (2.0 * math.pi / seq_len) * pos * band
    t = position(o_ref.shape, FILTER_HIDDEN) / float(max(seq_len - 1, 1))
    fr = fr_ref[...]
    pre = (t * w1t_ref[...]
           + jnp.dot(jnp.cos(ang), w1c_ref[...], precision=HIGHEST, preferred_element_type=F32)
           + jnp.dot(jnp.sin(ang), w1s_ref[...], precision=HIGHEST, preferred_element_type=F32)
           + b1_ref[...])
    h = jnp.sin(fr * pre)
    o_ref[...] = jnp.sin(fr * (jnp.dot(h, w2_ref[...], precision=HIGHEST, preferred_element_type=F32) + b2_ref[...]))


def _filter_mlp_args(w1, b1, freq, w2, b2):
    hid = FILTER_HIDDEN
    w1c = jnp.zeros((LANES, hid), F32).at[:FILTER_BANDS].set(w1[1:1 + FILTER_BANDS])
    w1s = jnp.zeros((LANES, hid), F32).at[:FILTER_BANDS].set(w1[1 + FILTER_BANDS:1 + 2 * FILTER_BANDS])
    full = lambda shape: pl.BlockSpec(shape, lambda i: (0, 0))
    specs = [full((1, hid)), full((LANES, hid)), full((LANES, hid)), full((1, hid)), full((1, hid)),
             full((hid, hid)), full((1, hid))]
    args = (w1[0:1], w1c, w1s, b1.reshape(1, hid), freq.reshape(1, hid), w2, b2.reshape(1, hid))
    return specs, args


def _filter_delta(width):
    return jnp.abs(jnp.linspace(math.log(FILTER_TARGET) / FILTER_SLOW_PCT,
                                math.log(FILTER_TARGET) / FILTER_FAST_PCT, width, dtype=F32)).reshape(1, width)


def _hyena_filter2(seq_len, w1, b1, freq, w2, b2, w3, width):
    n2 = 2 * seq_len
    tr = min(seq_len, 1024)
    nfwd = seq_len // tr
    specs, args = _filter_mlp_args(w1, b1, freq, w2, b2)
    return pl.pallas_call(
        functools.partial(_filter_kernel, seq_len=seq_len),
        grid=(n2 // tr,),
        in_specs=specs + [pl.BlockSpec((FILTER_HIDDEN, width), lambda i: (0, jnp.where(i < nfwd, 0, 1))),
                          pl.BlockSpec((1, width), lambda i: (0, 0))],
        out_specs=pl.BlockSpec((tr, width), lambda i: (i, 0)),
        out_shape=jax.ShapeDtypeStruct((n2, width), F32),
        compiler_params=_params("parallel"),
        name="hyena_filter",
    )(*args, w3, _filter_delta(width))


def _filter_hidden_permuted(seq_len, w1, b1, freq, w2, b2):
    n2 = 2 * seq_len
    n_a = n2 // DFT_B
    hid = FILTER_HIDDEN
    wide = _FILTER_PACK * hid
    rows = n2 // _FILTER_PACK
    tp = min(rows, 256)
    eye = jnp.eye(_FILTER_PACK, dtype=F32)
    tile = lambda v: jnp.tile(v.reshape(1, hid), (1, _FILTER_PACK))
    args = (tile(w1[0]), jnp.kron(eye, w1[1:1 + FILTER_BANDS]),
            jnp.kron(eye, w1[1 + FILTER_BANDS:1 + 2 * FILTER_BANDS]), tile(b1), tile(freq),
            jnp.kron(eye, w2), tile(b2))
    packed = pl.pallas_call(
        functools.partial(_filter_hidden_kernel, seq_len=seq_len, n_a=n_a),
        grid=(rows // tp,),
        in_specs=[pl.BlockSpec(a.shape, lambda i: (0, 0)) for a in args],
        out_specs=pl.BlockSpec((tp, wide), lambda i: (i, 0)),
        out_shape=jax.ShapeDtypeStruct((rows, wide), F32),
        compiler_params=_params("parallel"),
        name="hyena_filter_hidden",
    )(*args)
    return packed.reshape(n2, hid)


def _dft_tables(n_a, a_in):
    n = n_a * DFT_B
    b = np.arange(DFT_B)[:, None, None]
    k1 = np.arange(n_a)[None, :, None]
    a = np.arange(a_in)[None, None, :]
    e = np.exp(-2j * np.pi * ((k1 * (DFT_B * a + b)) % n) / n)
    g = np.concatenate([e.real, e.imag], axis=1)
    return jnp.asarray(g, F32)


def _idft_tables(n_a, a_out):
    n = n_a * DFT_B
    b = np.arange(DFT_B)[:, None, None]
    a = np.arange(a_out)[None, :, None]
    k1 = np.arange(n_a)[None, None, :]
    e = np.exp(2j * np.pi * ((k1 * (DFT_B * a + b)) % n) / n) / n
    h = np.concatenate([e.real, -e.imag], axis=2)
    return jnp.asarray(h, F32)


def _dft128_embed():
    i = np.arange(DFT_B)
    f = np.exp(-2j * np.pi * ((i[:, None] * i[None, :]) % DFT_B) / DFT_B)
    fwd = np.block([[f.real, -f.imag], [f.imag, f.real]])
    inv = np.block([[f.real, f.imag], [-f.imag, f.real]])
    return jnp.asarray(fwd, F32), jnp.asarray(inv, F32)


_PITCH = DFT_B + SUBLANES
_K1_BLOCK = 8
_B_UNROLL = 8


def _stage_rows(k1):
    return pl.ds(pl.multiple_of(k1 * _PITCH, SUBLANES), DFT_B)


def _single(shape, index_map):
    return pl.BlockSpec(shape, index_map, pipeline_mode=pl.Buffered(1))


def _pack_pair(re, im):
    hi = lax.bitcast_convert_type(re.astype(BF16).astype(F32), jnp.uint32)
    lo = lax.bitcast_convert_type(im.astype(BF16).astype(F32), jnp.uint32)
    return hi | (lo >> 16)


def _unpack_pair(w):
    re = lax.bitcast_convert_type(w & jnp.uint32(0xFFFF0000), F32)
    im = lax.bitcast_convert_type(w << 16, F32)
    return re.astype(BF16), im.astype(BF16)


def _spectrum_kernel(h_ref, w3f_ref, w3b_ref, delta_ref, g_ref, wf_ref, o_ref, s0_ref, s1_ref, *, seq_len, n_a):
    s = pl.program_id(1)
    half = n_a // 2
    cw = o_ref.shape[3]

    @pl.when(s == 0)
    def _():
        a_i = lax.broadcasted_iota(jnp.int32, (n_a, cw), 0)

        def body(b, carry):
            base = pl.multiple_of(b * n_a, SUBLANES)
            hf = jnp.dot(h_ref[pl.ds(base, half), :].astype(BF16), w3f_ref[...].astype(BF16),
                         preferred_element_type=F32)
            hb = jnp.dot(h_ref[pl.ds(base + half, half), :].astype(BF16), w3b_ref[...].astype(BF16),
                         preferred_element_type=F32)
            hh = jnp.concatenate([hf, hb], axis=0)
            j = a_i * DFT_B + b
            pos_i = jnp.where(j < seq_len, j, 2 * seq_len - j)
            t = pos_i.astype(F32) / float(max(seq_len - 1, 1))
            k2 = jnp.where(pos_i != seq_len, hh * jnp.exp(-t * delta_ref[...]), 0.0)
            r = jnp.dot(g_ref[b], k2.astype(BF16), preferred_element_type=F32)
            w = _pack_pair(r[:n_a], r[n_a:])
            s0_ref[pl.ds(b, n_a, stride=_PITCH), :] = w[:, :LANES]
            s1_ref[pl.ds(b, n_a, stride=_PITCH), :] = w[:, LANES:]
            return carry

        lax.fori_loop(0, DFT_B, body, 0, unroll=_B_UNROLL)

    @pl.when(s > 0)
    def _():
        k0 = (s - 1) * o_ref.shape[1]
        for i in range(o_ref.shape[1]):
            rows = _stage_rows(k0 + i)
            re, im = _unpack_pair(jnp.concatenate([s0_ref[rows, :], s1_ref[rows, :]], axis=1))
            z = jnp.dot(wf_ref[...], jnp.concatenate([re, im], axis=0), preferred_element_type=F32)
            o_ref[0, i] = z[:DFT_B]
            o_ref[1, i] = z[DFT_B:]


def _filter_spectrum(hperm, w3, seq_len, width, g_k, wf):
    n_a = 2 * seq_len // DFT_B
    kb = min(n_a, _K1_BLOCK)
    cw = 2 * LANES
    ncb = width // cw
    hid = FILTER_HIDDEN
    return pl.pallas_call(
        functools.partial(_spectrum_kernel, seq_len=seq_len, n_a=n_a),
        grid=(ncb, 1 + n_a // kb),
        in_specs=[_single(hperm.shape, lambda ci, s: (0, 0)),
                  pl.BlockSpec((hid, cw), lambda ci, s: (0, ci)),
                  pl.BlockSpec((hid, cw), lambda ci, s: (0, ncb + ci)),
                  pl.BlockSpec((1, cw), lambda ci, s: (0, ci)),
                  _single(g_k.shape, lambda ci, s: (0, 0, 0)),
                  _single(wf.shape, lambda ci, s: (0, 0))],
        out_specs=pl.BlockSpec((2, kb, DFT_B, cw), lambda ci, s: (0, jnp.maximum(s - 1, 0), 0, ci)),
        out_shape=jax.ShapeDtypeStruct((2, n_a, DFT_B, width), F32),
        scratch_shapes=[pltpu.VMEM((n_a * _PITCH, LANES), jnp.uint32), pltpu.VMEM((n_a * _PITCH, LANES), jnp.uint32)],
        compiler_params=_params("parallel", "arbitrary"),
        name="filter_spectrum",
    )(hperm, w3, w3, _filter_delta(width), g_k, wf)


def _hyena_epilogue(y, z, x0, bias, gain):
    yv = (y + z * bias) * x0
    ms = jnp.mean(yv * yv, axis=-1, keepdims=True)
    return yv * lax.rsqrt(ms + NORM_EPS) * gain


_EPI_ROWS = 512


def _long_conv_kernel(z_ref, x0_ref, kf_ref, g_ref, h_ref, wf_ref, wi_ref, bias_ref, gain_ref, o_ref,
                      sr_ref, si_ref, *, n_a):
    s = pl.program_id(1)
    half = n_a // 2
    kb = kf_ref.shape[1]
    nkb = n_a // kb

    @pl.when(s == 0)
    def _():
        def body(b, carry):
            xs = z_ref[pl.ds(b, half, stride=DFT_B), :].astype(BF16)
            r = jnp.dot(g_ref[b], xs, preferred_element_type=F32)
            sr_ref[pl.ds(b, n_a, stride=_PITCH), :] = r[:n_a]
            si_ref[pl.ds(b, n_a, stride=_PITCH), :] = r[n_a:]
            return carry

        lax.fori_loop(0, DFT_B, body, 0, unroll=_B_UNROLL)

    @pl.when((s > 0) & (s <= nkb))
    def _():
        k0 = (s - 1) * kb
        for i in range(kb):
            rows = _stage_rows(k0 + i)
            x = jnp.concatenate([sr_ref[rows, :], si_ref[rows, :]], axis=0).astype(BF16)
            zz = jnp.dot(wf_ref[...], x, preferred_element_type=F32)
            zr, zi = zz[:DFT_B], zz[DFT_B:]
            kr, ki = kf_ref[0, i], kf_ref[1, i]
            y = jnp.concatenate([zr * kr - zi * ki, zr * ki + zi * kr], axis=0).astype(BF16)
            cc = jnp.dot(wi_ref[...], y, preferred_element_type=F32)
            sr_ref[rows, :] = cc[:DFT_B]
            si_ref[rows, :] = cc[DFT_B:]

    @pl.when(s == nkb + 1)
    def _():
        def body(b, carry):
            x = jnp.concatenate([sr_ref[pl.ds(b, n_a, stride=_PITCH), :],
                                 si_ref[pl.ds(b, n_a, stride=_PITCH), :]], axis=0).astype(BF16)
            o_ref[pl.ds(b, half, stride=DFT_B), :] = jnp.dot(h_ref[b], x, preferred_element_type=F32)
            return carry

        lax.fori_loop(0, DFT_B, body, 0, unroll=_B_UNROLL)

        step = min(_EPI_ROWS, o_ref.shape[0])

        def epi(i, carry):
            rows = pl.ds(pl.multiple_of(i * step, step), step)
            o_ref[rows, :] = _hyena_epilogue(o_ref[rows, :], z_ref[rows, :], x0_ref[rows, :],
                                             bias_ref[...], gain_ref[...])
            return carry

        lax.fori_loop(0, o_ref.shape[0] // step, epi, 0)


def _long_conv(z, x0c, kf, g_z, h_tab, wf, wi, bias, gain):
    l, c = z.shape
    n_a = 2 * l // DFT_B
    kb = min(n_a, _K1_BLOCK)
    nkb = n_a // kb
    col = lambda ci, s: (0, ci)
    return pl.pallas_call(
        functools.partial(_long_conv_kernel, n_a=n_a),
        grid=(c // LANES, nkb + 2),
        in_specs=[_single((l, LANES), col), _single((l, LANES), col),
                  pl.BlockSpec((2, kb, DFT_B, LANES), lambda ci, s: (0, jnp.clip(s - 1, 0, nkb - 1), 0, ci)),
                  _single(g_z.shape, lambda ci, s: (0, 0, 0)), _single(h_tab.shape, lambda ci, s: (0, 0, 0)),
                  _single(wf.shape, lambda ci, s: (0, 0)), _single(wi.shape, lambda ci, s: (0, 0)),
                  pl.BlockSpec((1, LANES), col), pl.BlockSpec((1, LANES), col)],
        out_specs=pl.BlockSpec((l, LANES), col),
        out_shape=jax.ShapeDtypeStruct((l, c), F32),
        scratch_shapes=[pltpu.VMEM((n_a * _PITCH, LANES), F32), pltpu.VMEM((n_a * _PITCH, LANES), F32)],
        compiler_params=_params("parallel", "arbitrary"),
        name="long_conv",
    )(z, x0c, kf, g_z, h_tab, wf, wi, bias.reshape(1, c), gain.reshape(1, c))


def _hyena_long(z, x0c, filt, bias, gain):
    l, c = z.shape
    n_a = 2 * l // DFT_B
    wf, wi = _dft128_embed()
    wf, wi = wf.astype(BF16), wi.astype(BF16)
    g_z = _dft_tables(n_a, n_a // 2).astype(BF16)
    g_k = _dft_tables(n_a, n_a).astype(BF16)
    h_tab = _idft_tables(n_a, n_a // 2).astype(BF16)
    hperm = _filter_hidden_permuted(l, filt['hy_w1'], filt['hy_b1'], filt['hy_freq'], filt['hy_w2'], filt['hy_b2'])
    kf = _filter_spectrum(hperm, filt['hy_w3'], l, c, g_k, wf)
    return _long_conv(z, x0c, kf, g_z, h_tab, wf, wi, bias, gain)


def _ctx_conv_kernel(z_ref, k_ref, x0_ref, fz_ref, fk_ref, hi_ref, bias_ref, gain_ref, o_ref):
    n = k_ref.shape[0]
    zs = jnp.dot(fz_ref[...], z_ref[...].astype(BF16), preferred_element_type=F32)
    ks = jnp.dot(fk_ref[...], k_ref[...].astype(BF16), preferred_element_type=F32)
    zr, zi, kr, ki = zs[:n], zs[n:], ks[:n], ks[n:]
    y = jnp.concatenate([zr * kr - zi * ki, zr * ki + zi * kr], axis=0).astype(BF16)
    yt = jnp.dot(hi_ref[...], y, preferred_element_type=F32)
    o_ref[...] = _hyena_epilogue(yt, z_ref[...], x0_ref[...], bias_ref[...], gain_ref[...]).astype(o_ref.dtype)


def _hyena_short(z, x0c, k2, bias, gain):
    l, c = z.shape
    n = 2 * l
    k = np.arange(n)[:, None]
    t = np.arange(n)[None, :]
    e = np.exp(-2j * np.pi * ((k * t) % n) / n)
    fk = jnp.asarray(np.concatenate([e.real, e.imag], axis=0), F32).astype(BF16)
    fz = fk[:, :l]
    ei = np.conj(e[:, :l]).T / n
    hi = jnp.asarray(np.concatenate([ei.real, -ei.imag], axis=1), F32).astype(BF16)
    col = lambda rows: pl.BlockSpec((rows, LANES), lambda ci: (0, ci))
    full = lambda a: pl.BlockSpec(a.shape, lambda ci: (0, 0))
    return pl.pallas_call(
        _ctx_conv_kernel,
        grid=(c // LANES,),
        in_specs=[col(l), col(n), col(l), full(fz), full(fk), full(hi), col(1), col(1)],
        out_specs=col(l),
        out_shape=jax.ShapeDtypeStruct((l, c), F32),
        compiler_params=_params("parallel"),
        name="hyena_short",
    )(z, k2, x0c, fz, fk, hi, bias.reshape(1, c), gain.reshape(1, c))


def _split3(x):
    hi = x.astype(BF16)
    r1 = x - hi.astype(F32)
    mid = r1.astype(BF16)
    lo = (r1 - mid.astype(F32)).astype(BF16)
    return hi, mid, lo


def _gla_kernel(*refs, reverse, combine):
    if combine:
        (k_ref, q_ref, v_ref, r_ref, wg_ref, bg_ref, s0_ref, of_ref, gate_ref, gout_ref,
         o_ref, sfin_ref, st_ref, la_ref, qe_ref, upd_ref, dec_ref, ob_ref) = refs
    else:
        (k_ref, q_ref, v_ref, r_ref, wg_ref, bg_ref, s0_ref,
         o_ref, sfin_ref, st_ref, la_ref, qe_ref, upd_ref, dec_ref) = refs
        ob_ref = o_ref
    rows_blk = k_ref.shape[0]
    nc = rows_blk // GLA_CHUNK
    qk = k_ref.shape[1]

    @pl.when(pl.program_id(0) == 0)
    def _():
        st_ref[...] = s0_ref[...]

    ri = lax.broadcasted_iota(jnp.int32, (2 * GLA_CHUNK, GLA_CHUNK), 0) % GLA_CHUNK
    ci = lax.broadcasted_iota(jnp.int32, (2 * GLA_CHUNK, GLA_CHUNK), 1)
    r3 = lax.broadcasted_iota(jnp.int32, (GLA_CHUNK, 3 * GLA_CHUNK), 0)
    c3 = lax.broadcasted_iota(jnp.int32, (GLA_CHUNK, 3 * GLA_CHUNK), 1) % GLA_CHUNK
    if reverse:
        tri3 = (c3 >= r3).astype(BF16)
        keep = ci > ri
    else:
        tri3 = (c3 <= r3).astype(BF16)
        keep = ci <= ri
    lane = lax.broadcasted_iota(jnp.int32, (1, LANES), 1)
    first = (lane < GLA_DK).astype(F32)
    srow = lax.broadcasted_iota(jnp.int32, (2 * GLA_DV, LANES), 0)
    scol = lax.broadcasted_iota(jnp.int32, (2 * GLA_DV, LANES), 1)
    own = (srow < GLA_DV) == (scol < GLA_DK)
    nt = (((1,), (1,)), ((), ()))
    tn = (((0,), (0,)), ((), ()))
    npair = GLA_HEADS // 2

    rh, rm, _ = _split3(r_ref[...])
    wh, wm, _ = _split3(wg_ref[...])
    pre = jnp.dot(jnp.concatenate([rh, rh, rm], axis=1), jnp.concatenate([wh, wm, wh], axis=0),
                  preferred_element_type=F32) + bg_ref[...]
    la_ref[...] = jax.nn.log_sigmoid(pre) / GLA_GATE_TAU

    def local(c, carry):
        rows = pl.ds(pl.multiple_of(c * GLA_CHUNK, GLA_CHUNK), GLA_CHUNK)
        hi, mid, lo = _split3(la_ref[rows, :])
        cum = jnp.dot(tri3, jnp.concatenate([hi, mid, lo], axis=0), preferred_element_type=F32)
        tot = cum[0:1] if reverse else cum[GLA_CHUNK - 1:GLA_CHUNK]
        k = k_ref[rows, :]
        qe = q_ref[rows, :] * (GLA_DK ** -0.5) * jnp.exp(cum)
        ke = (k * jnp.exp(-cum)).astype(BF16)
        kl = (k * jnp.exp(tot - cum)).astype(BF16)
        dec_ref[pl.ds(c, 1), :] = jnp.exp(tot)
        qe_ref[rows, :] = qe.astype(BF16)
        for p in range(npair):
            pr = slice(p * LANES, (p + 1) * LANES)
            pv = slice(2 * p * GLA_DV, 2 * (p + 1) * GLA_DV)
            q2 = jnp.concatenate([qe[:, pr] * first, qe[:, pr] * (1.0 - first)], axis=0).astype(BF16)
            sc = lax.dot_general(q2, ke[:, pr], nt, preferred_element_type=F32)
            sc = jnp.where(keep, sc, 0.0).astype(BF16)
            v_p = v_ref[rows, pv].astype(BF16)
            o2 = jnp.dot(sc, v_p, preferred_element_type=F32)
            ob_ref[rows, pl.ds(2 * p * GLA_DV, GLA_DV)] = o2[:GLA_CHUNK, :GLA_DV]
            ob_ref[rows, pl.ds((2 * p + 1) * GLA_DV, GLA_DV)] = o2[GLA_CHUNK:, GLA_DV:]
            u2 = lax.dot_general(v_p, kl[:, pr], tn, preferred_element_type=F32)
            upd_ref[c * npair + p] = jnp.where(own, u2, 0.0)
        return carry

    lax.fori_loop(0, nc, local, 0, unroll=2)

    def carried(idx, carry):
        c = (nc - 1 - idx) if reverse else idx
        rows = pl.ds(pl.multiple_of(c * GLA_CHUNK, GLA_CHUNK), GLA_CHUNK)
        dec = dec_ref[pl.ds(c, 1), :]
        for p in range(npair):
            pr = slice(p * LANES, (p + 1) * LANES)
            pv = slice(2 * p * GLA_DV, 2 * (p + 1) * GLA_DV)
            st = st_ref[p]
            ob_ref[rows, pv] += lax.dot_general(qe_ref[rows, pr], st.astype(BF16), nt, preferred_element_type=F32)
            st_ref[p] = st * dec[:, pr] + upd_ref[c * npair + p]
        return carry

    lax.fori_loop(0, nc, carried, 0)

    if combine:
        for h in range(GLA_HEADS):
            hv = slice(h * GLA_DV, (h + 1) * GLA_DV)
            o = of_ref[:, hv] + ob_ref[:, hv]
            ms = jnp.mean(o * o, axis=-1, keepdims=True)
            o = o * lax.rsqrt(ms + NORM_EPS) * gout_ref[...]
            o_ref[:, hv] = (o * _silu(gate_ref[:, hv])).astype(o_ref.dtype)

    @pl.when(pl.program_id(0) == pl.num_programs(0) - 1)
    def _():
        sfin_ref[...] = st_ref[...]


_GLA_STATE = (GLA_HEADS // 2, 2 * GLA_DV, LANES)

_COL_K, _COL_Q, _COL_V, _COL_G, _COL_R = 24, 28, 32, 40, 48


def _gla_pass(p, wg_pad, bg, s0, reverse, of=None, gout=None):
    l = p.shape[0]
    rb = min(l, 512)
    nblk = l // rb
    qk = GLA_HEADS * GLA_DK
    vw = GLA_HEADS * GLA_DV
    combine = of is not None
    rmap = (lambda i: nblk - 1 - i) if reverse else (lambda i: i)

    def cols(width, off128):
        return pl.BlockSpec((rb, width), lambda i: (rmap(i), off128 * LANES // width))

    state = pl.BlockSpec(_GLA_STATE, lambda i: (0, 0, 0))
    in_specs = [cols(qk, _COL_K), cols(qk, _COL_Q), cols(vw, _COL_V), cols(LANES, _COL_R),
                pl.BlockSpec((LANES, qk), lambda i: (0, 0)), pl.BlockSpec((1, qk), lambda i: (0, 0)), state]
    args = [p, p, p, p, wg_pad, bg.reshape(1, qk), s0]
    nc = rb // GLA_CHUNK
    scratch = [pltpu.VMEM(_GLA_STATE, F32),
               pltpu.VMEM((rb, qk), F32),
               pltpu.VMEM((rb, qk), BF16),
               pltpu.VMEM((nc * _GLA_STATE[0],) + _GLA_STATE[1:], F32),
               pltpu.VMEM((nc, qk), F32)]
    if combine:
        in_specs += [pl.BlockSpec((rb, vw), lambda i: (rmap(i), 0)), cols(vw, _COL_G),
                     pl.BlockSpec((1, GLA_DV), lambda i: (0, 0))]
        args += [of, p, gout.reshape(1, GLA_DV)]
        scratch.append(pltpu.VMEM((rb, vw), F32))
    out_dtype = BF16 if combine else F32
    return pl.pallas_call(
        functools.partial(_gla_kernel, reverse=reverse, combine=combine),
        grid=(nblk,),
        in_specs=in_specs,
        out_specs=[pl.BlockSpec((rb, vw), lambda i: (rmap(i), 0)), state],
        out_shape=[jax.ShapeDtypeStruct((l, vw), out_dtype),
                   jax.ShapeDtypeStruct(_GLA_STATE, F32)],
        scratch_shapes=scratch,
        compiler_params=_params("arbitrary"),
        name="gla_bwd" if reverse else "gla_fwd",
    )(*args)


def _out_proj_kernel(hy_ref, o_ref, w_ref, x_ref, ga_ref, g_ref, sh_ref, sc_ref, xo_ref, h_ref):
    half = hy_ref.shape[1]
    mix = (jnp.dot(hy_ref[...].astype(BF16), w_ref[:half, :], preferred_element_type=F32)
           + jnp.dot(o_ref[...], w_ref[half:, :], preferred_element_type=F32))
    xn = x_ref[...] + ga_ref[...] * mix
    xo_ref[...] = xn
    h_ref[...] = _norm_mod(xn, g_ref[...], sh_ref[...], sc_ref[...]).astype(BF16)


def _out_proj(hy, o, w, x, ga, g, shift, scale, layer):
    m, d = x.shape
    half = hy.shape[1]
    tm = min(m, 256)
    vec = pl.BlockSpec((1, d), lambda i: (0, 0))
    row = lambda width: pl.BlockSpec((tm, width), lambda i: (i, 0))
    return pl.pallas_call(
        _out_proj_kernel,
        grid=(m // tm,),
        in_specs=[row(half), row(half), pl.BlockSpec((None, 2 * half, d), lambda i: (layer, 0, 0)), row(d),
                  vec, vec, vec, vec],
        out_specs=[row(d), row(d)],
        out_shape=[jax.ShapeDtypeStruct((m, d), F32), jax.ShapeDtypeStruct((m, d), BF16)],
        compiler_params=_params("parallel"),
        name="out_proj",
    )(hy, o, w, x, ga, g, shift, scale)


_HALO = GRID_W + SUBLANES


def _ffn_up_conv_kernel(x_ref, wa_ref, wu_ref, cw_ref, o_ref, a_ref, u_ref, *, vertical, cols, nm):
    j = pl.program_id(0)
    i = pl.program_id(1)
    tt = x_ref.shape[0]
    cur = i % 2
    prv = 1 - cur

    @pl.when((j == 0) & (i == 0))
    def _():
        a_ref[...] = jnp.zeros_like(a_ref)
        u_ref[...] = jnp.zeros_like(u_ref)

    def conv_previous_tile():
        col = lax.broadcasted_iota(jnp.int32, o_ref.shape, 0) & (cols - 1)
        acc = None
        for dc in (-1, 0, 1):
            part = None
            for dr in ((-1, 0, 1) if vertical else (0,)):
                tap = cw_ref[3 * (dr + 1) + (dc + 1):3 * (dr + 1) + (dc + 1) + 1, :]
                term = tap * a_ref[prv, pl.ds(_HALO + GRID_W * dr + dc, tt), :]
                part = term if part is None else part + term
            if dc == -1:
                part = jnp.where(col == 0, 0.0, part)
            elif dc == 1:
                part = jnp.where(col == cols - 1, 0.0, part)
            acc = part if acc is None else acc + part
        o_ref[...] = (_silu(acc) * u_ref[prv].astype(F32)).astype(o_ref.dtype)

    @pl.when(i < nm)
    def _():
        x = x_ref[...]
        a_new = jnp.dot(x, wa_ref[...].astype(BF16), preferred_element_type=F32)
        u_ref[cur] = jnp.dot(x, wu_ref[...].astype(BF16), preferred_element_type=F32).astype(BF16)
        upper = a_ref[prv, pl.ds(_HALO + tt - GRID_W, GRID_W), :]
        a_ref[cur, pl.ds(SUBLANES, GRID_W), :] = jnp.where(i == 0, 0.0, upper)
        a_ref[cur, pl.ds(_HALO, tt), :] = a_new
        a_ref[prv, pl.ds(_HALO + tt, GRID_W), :] = a_new[:GRID_W]
        conv_previous_tile()

    @pl.when(i == nm)
    def _():
        a_ref[prv, pl.ds(_HALO + tt, GRID_W), :] = jnp.zeros((GRID_W, a_ref.shape[2]), F32)
        conv_previous_tile()


def _ffn_up_conv(h, w_up, conv_w, cols, layer):
    m, d = h.shape
    dff = conv_w.shape[-1]
    vertical = m > cols
    assert cols & (cols - 1) == 0 and (cols == GRID_W or not vertical)
    tt = min(m, 1024)
    assert tt % cols == 0
    cb = 512
    ncb = dff // cb
    nm = m // tt
    return pl.pallas_call(
        functools.partial(_ffn_up_conv_kernel, vertical=vertical, cols=cols, nm=nm),
        grid=(ncb, nm + 1),
        in_specs=[pl.BlockSpec((tt, d), lambda j, i: (jnp.minimum(i, nm - 1), 0)),
                  pl.BlockSpec((None, d, cb), lambda j, i: (layer, 0, j)),
                  pl.BlockSpec((None, d, cb), lambda j, i: (layer, 0, ncb + j)),
                  pl.BlockSpec((9, cb), lambda j, i: (0, j))],
        out_specs=pl.BlockSpec((tt, cb), lambda j, i: (jnp.maximum(i - 1, 0), j)),
        out_shape=jax.ShapeDtypeStruct((m, dff), BF16),
        scratch_shapes=[pltpu.VMEM((2, tt + 2 * _HALO, cb), F32), pltpu.VMEM((2, tt, cb), BF16)],
        compiler_params=_params("arbitrary", "arbitrary"),
        name="ffn_up_conv",
    )(h, w_up, w_up, conv_w.reshape(9, dff))


def _ffn_down_kernel(g_ref, w_ref, x_ref, gate_ref, ng_ref, nsh_ref, nsc_ref, o_ref, *h_ref, post):
    xn = x_ref[...] + gate_ref[...] * jnp.dot(g_ref[...], w_ref[...], preferred_element_type=F32)
    if post == 'final':
        ms = jnp.mean(xn * xn, axis=-1, keepdims=True)
        xn = xn * lax.rsqrt(ms + NORM_EPS) * ng_ref[...]
    o_ref[...] = xn
    if post == 'next':
        h_ref[0][...] = _norm_mod(xn, ng_ref[...], nsh_ref[...], nsc_ref[...]).astype(BF16)


def _ffn_down(g, w, x, gate, norm, post, layer):
    m, d = x.shape
    dff = g.shape[1]
    tm = min(m, 256)
    vec = pl.BlockSpec((1, d), lambda i: (0, 0))
    row = pl.BlockSpec((tm, d), lambda i: (i, 0))
    out_specs, out_shape = [row], [jax.ShapeDtypeStruct((m, d), F32)]
    if post == 'next':
        out_specs.append(row)
        out_shape.append(jax.ShapeDtypeStruct((m, d), BF16))
    return pl.pallas_call(
        functools.partial(_ffn_down_kernel, post=post),
        grid=(m // tm,),
        in_specs=[pl.BlockSpec((tm, dff), lambda i: (i, 0)),
                  _single((None, dff, d), lambda i: (layer, 0, 0)),
                  row, vec, vec, vec, vec],
        out_specs=out_specs,
        out_shape=out_shape,
        compiler_params=_params("parallel"),
        name="ffn_down",
    )(g, w, x, gate, *norm)


def _permute_w_in(w_in, hy_proj):
    qk = GLA_HEADS * GLA_DK
    vw = GLA_HEADS * GLA_DV
    g0 = hy_proj
    k = w_in[..., g0:g0 + qk]
    v = w_in[..., g0 + qk:g0 + qk + vw]
    r = w_in[..., g0 + qk + vw:g0 + qk + vw + 2 * GLA_RANK]
    q = w_in[..., g0 + qk + vw + 2 * GLA_RANK:g0 + 2 * qk + vw + 2 * GLA_RANK]
    g = w_in[..., g0 + 2 * qk + vw + 2 * GLA_RANK:]
    pad = jnp.zeros(w_in.shape[:-1] + (2 * LANES - 2 * GLA_RANK,), w_in.dtype)
    return jnp.concatenate([w_in[..., :hy_proj], k, q, v, g, r, pad], axis=-1).astype(BF16)


def _pad_gate_w(wg, lane0):
    return jnp.zeros((LANES, wg.shape[1]), F32).at[lane0:lane0 + GLA_RANK].set(wg)


def _mixer(p, lp, s0_f, s0_b, want_output):
    l = p.shape[0]
    width = lp['hy_bias'].shape[0]
    o_f, s_f = _gla_pass(p, lp['wg_f_pad'], lp['gla_bg_f'], s0_f, reverse=False)
    o, s_b = _gla_pass(p, lp['wg_b_pad'], lp['gla_bg_b'], s0_b, reverse=True, of=o_f, gout=lp['gla_out_g'])
    if not want_output:
        return None, None, s_f, s_b
    z, x0c = _hy_pre(p, lp['hy_conv'], width)
    if 2 * l // DFT_B >= 2 * SUBLANES:
        hy = _hyena_long(z, x0c, lp, lp['hy_bias'], lp['hy_out_g'])
    else:
        k2 = _hyena_filter2(l, lp['hy_w1'], lp['hy_b1'], lp['hy_freq'], lp['hy_w2'], lp['hy_b2'], lp['hy_w3'], width)
        hy = _hyena_short(z, x0c, k2, lp['hy_bias'], lp['hy_out_g'])
    return hy, o, s_f, s_b


def _ffn(h2, x, gate, lp, cols, norm, post, layer):
    gact = _ffn_up_conv(h2, lp['ffn_w_up'], lp['ffn_conv'], cols, layer)
    return _ffn_down(gact, lp['ffn_w_down'], x, gate, norm, post, layer)


def kernel(x, c, ctx, c_ctx, w_mod, b_mod, g_attn, w_in, hy_conv, hy_w1, hy_b1, hy_freq, hy_w2, hy_b2, hy_w3,
           hy_bias, hy_out_g, gla_wg_f, gla_bg_f, gla_wg_b, gla_bg_b, gla_out_g, w_out, g_ffn, ffn_w_up,
           ffn_conv, ffn_w_down, g_final):
    depth = w_mod.shape[0]
    batch, seq, d = x.shape
    assert batch == 1 and c.shape[0] == 1 and ctx.shape[0] == 1
    hy_proj = hy_conv.shape[-1]
    xs = x[0]
    cs = ctx[0]
    cc = jnp.zeros((SUBLANES, d), F32).at[0].set(c[0]).at[1].set(c_ctx)
    zero_state = jnp.zeros(_GLA_STATE, F32)
    gfin = g_final.reshape(1, d)
    w_in_p = _permute_w_in(w_in, hy_proj)
    w_out_b = w_out.astype(BF16)
    w_down_b = ffn_w_down.astype(BF16)
    mods =[_adaln(cc, w_mod, b_mod, l) for l in range(depth)]

    def attn_norm(l, row):
        return (g_attn[l].reshape(1, d), mods[l][row:row + 1, 0:d], mods[l][row:row + 1, d:2 * d])

    hx = _norm_mod_call(xs, *attn_norm(0, 0))
    hc = _norm_mod_call(cs, *attn_norm(0, 1))
    for l in range(depth):
        last = l == depth - 1
        lp = {
            'hy_conv': hy_conv[l], 'hy_w1': hy_w1[l], 'hy_b1': hy_b1[l], 'hy_freq': hy_freq[l],
            'hy_w2': hy_w2[l], 'hy_b2': hy_b2[l], 'hy_w3': hy_w3[l], 'hy_bias': hy_bias[l],
            'hy_out_g': hy_out_g[l], 'gla_bg_f': gla_bg_f[l], 'gla_bg_b': gla_bg_b[l],
            'gla_out_g': gla_out_g[l],
            'wg_f_pad': _pad_gate_w(gla_wg_f[l], 0), 'wg_b_pad': _pad_gate_w(gla_wg_b[l], GLA_RANK),
            'ffn_w_up': ffn_w_up, 'ffn_conv': ffn_conv[l], 'ffn_w_down': w_down_b,
        }
        mod = mods[l]
        _, _, ga, sf, scf, gf = [mod[0:1, i * d:(i + 1) * d] for i in range(6)]
        _, _, cga, csf, cscf, cgf = [mod[1:2, i * d:(i + 1) * d] for i in range(6)]
        gf_row = g_ffn[l].reshape(1, d)

        pc = _in_proj(hc, w_in_p, l)
        hy_c, o_c, s_f, s_b = _mixer(pc, lp, zero_state, zero_state, want_output=not last)
        if not last:
            cs, h2c = _out_proj(hy_c, o_c, w_out_b, cs, cga, gf_row, csf, cscf, l)
            cs, hc = _ffn(h2c, cs, cgf, lp, cs.shape[0], attn_norm(l + 1, 1), 'next', l)

        px = _in_proj(hx, w_in_p, l)
        hy_x, o_x, _, _ = _mixer(px, lp, s_f, s_b, want_output=True)
        xs, h2 = _out_proj(hy_x, o_x, w_out_b, xs, ga, gf_row, sf, scf, l)
        if last:
            (xs,) = _ffn(h2, xs, gf, lp, GRID_W, (gfin, gfin, gfin), 'final', l)
        else:
            xs, hx = _ffn(h2, xs, gf, lp, GRID_W, attn_norm(l + 1, 0), 'next', l)
    return xs[None]
```

```python
import functools
import math

import numpy as np
import jax
import jax.numpy as jnp
from jax import lax
from jax.experimental import pallas as pl
from jax.experimental.pallas import tpu as pltpu

F32 = jnp.float32
BF16 = jnp.bfloat16
HIGHEST = lax.Precision.HIGHEST

LANES = 128
SUBLANES = 8
VMEM_BYTES_V7X = 64 * 1024 * 1024
VMEM_LIMIT = VMEM_BYTES_V7X - 12 * 1024 * 1024

NORM_EPS = 1e-6
HYENA_GROUP = 128
FILTER_BANDS = 16
FILTER_HIDDEN = 64
FILTER_TARGET = 1e-2
FILTER_FAST_PCT = 0.3
FILTER_SLOW_PCT = 1.5
GLA_HEADS = 8
GLA_DK = 64
GLA_DV = 128
GLA_RANK = 16
GLA_GATE_TAU = 16.0
GLA_CHUNK = 64
GRID_W = 64
DFT_B = 128


def _params(*sem):
    return pltpu.CompilerParams(dimension_semantics=sem, vmem_limit_bytes=VMEM_LIMIT)


def _silu(x):
    return x * jax.nn.sigmoid(x)


def _adaln_kernel(c_ref, w_ref, b_ref, o_ref):
    s = _silu(c_ref[...]).astype(BF16)
    o_ref[...] = jnp.dot(s, w_ref[...].astype(BF16), preferred_element_type=F32) + b_ref[...]


def _adaln(cc, w_mod, b_mod, layer):
    depth, d, n = w_mod.shape
    tn = 1024
    return pl.pallas_call(
        _adaln_kernel,
        grid=(n // tn,),
        in_specs=[pl.BlockSpec((SUBLANES, d), lambda j: (0, 0)),
                  pl.BlockSpec((None, d, tn), lambda j: (layer, 0, j)),
                  pl.BlockSpec((None, 1, tn), lambda j: (layer, 0, j))],
        out_specs=pl.BlockSpec((SUBLANES, tn), lambda j: (0, j)),
        out_shape=jax.ShapeDtypeStruct((SUBLANES, n), F32),
        compiler_params=_params("arbitrary"),
        name="adaln",
    )(cc, w_mod, b_mod.reshape(depth, 1, n))


def _norm_mod(x, g, shift, scale):
    ms = jnp.mean(x * x, axis=-1, keepdims=True)
    return (x * lax.rsqrt(ms + NORM_EPS) * g) * (1.0 + scale) + shift


def _norm_mod_kernel(x_ref, g_ref, sh_ref, sc_ref, o_ref):
    o_ref[...] = _norm_mod(x_ref[...], g_ref[...], sh_ref[...], sc_ref[...]).astype(o_ref.dtype)


def _norm_mod_call(x, g, shift, scale):
    m, d = x.shape
    tm = min(m, 512)
    vec = pl.BlockSpec((1, d), lambda i: (0, 0))
    return pl.pallas_call(
        _norm_mod_kernel,
        grid=(m // tm,),
        in_specs=[pl.BlockSpec((tm, d), lambda i: (i, 0)), vec, vec, vec],
        out_specs=pl.BlockSpec((tm, d), lambda i: (i, 0)),
        out_shape=jax.ShapeDtypeStruct((m, d), BF16),
        compiler_params=_params("parallel"),
        name="norm_mod",
    )(x, g, shift, scale)


_IN_TILE = 768


_NT = (((1,), (1,)), ((), ()))


def _in_proj_kernel(h_ref, w_ref, wt_ref, o_ref, *, n_head):
    j = pl.program_id(1)

    @pl.when(j < n_head)
    def _():
        o_ref[...] = lax.dot_general(h_ref[...], w_ref[...].astype(BF16), _NT, preferred_element_type=F32)

    @pl.when(j >= n_head)
    def _():
        o_ref[...] = lax.dot_general(h_ref[...], wt_ref[...].astype(BF16), _NT, preferred_element_type=F32)


def _in_proj(h, w_in_t, w_tail_t, n_head_cols, layer):
    m, d = h.shape
    tn = _IN_TILE
    n_head = n_head_cols // tn
    n_tail = w_tail_t.shape[1] // tn
    assert n_head * tn == n_head_cols and n_tail * tn == w_tail_t.shape[1]
    tm = min(m, 1024)
    return pl.pallas_call(
        functools.partial(_in_proj_kernel, n_head=n_head),
        grid=(m // tm, n_head + n_tail),
        in_specs=[pl.BlockSpec((tm, d), lambda i, j: (i, 0)),
                  pl.BlockSpec((None, tn, d), lambda i, j: (layer, jnp.minimum(j, n_head - 1), 0)),
                  pl.BlockSpec((None, tn, d), lambda i, j: (layer, jnp.maximum(j - n_head, 0), 0))],
        out_specs=pl.BlockSpec((tm, tn), lambda i, j: (i, j)),
        out_shape=jax.ShapeDtypeStruct((m, n_head_cols + w_tail_t.shape[1]), F32),
        compiler_params=_params("parallel", "arbitrary"),
        name="in_proj",
    )(h, w_in_t, w_tail_t)


def _gate_rank_kernel(h_ref, w_ref, o_ref):
    o_ref[...] = lax.dot_general(h_ref[...], w_ref[...].astype(BF16), _NT, preferred_element_type=F32)


def _gate_rank_proj(h, w_r_t, layer):
    m, d = h.shape
    tm = min(m, 2048)
    return pl.pallas_call(
        _gate_rank_kernel,
        grid=(m // tm,),
        in_specs=[pl.BlockSpec((tm, d), lambda i: (i, 0)), pl.BlockSpec((None, LANES, d), lambda i: (layer, 0, 0))],
        out_specs=pl.BlockSpec((tm, LANES), lambda i: (i, 0)),
        out_shape=jax.ShapeDtypeStruct((m, LANES), F32),
        compiler_params=_params("parallel"),
        name="gate_rank_proj",
    )(h, w_r_t)


def _hy_pre_kernel(x0_ref, x1_ref, v_ref, x0p_ref, x1p_ref, vp_ref, x0n_ref, x1n_ref, vn_ref,
                   w0_ref, w1_ref, wv_ref, z_ref, x0c_ref):
    i = pl.program_id(0)
    first = i == 0
    last = i == pl.num_programs(0) - 1
    tt = x0_ref.shape[0]
    row = lax.broadcasted_iota(jnp.int32, x0_ref.shape, 0)

    def conv(ref, prev_ref, next_ref, w_ref):
        x = ref[...]
        prev_row = jnp.where(first, 0.0, prev_ref[SUBLANES - 1:SUBLANES, :])
        next_row = jnp.where(last, 0.0, next_ref[0:1, :])
        below = jnp.where(row == 0, prev_row, pltpu.roll(x, 1, 0))
        above = jnp.where(row == tt - 1, next_row, pltpu.roll(x, tt - 1, 0))
        return w_ref[0:1, :] * below + w_ref[1:2, :] * x + w_ref[2:3, :] * above

    x0c_ref[...] = conv(x0_ref, x0p_ref, x0n_ref, w0_ref)
    z_ref[...] = conv(v_ref, vp_ref, vn_ref, wv_ref) * conv(x1_ref, x1p_ref, x1n_ref, w1_ref)


def _hy_pre(p, hy_conv, width):
    l = p.shape[0]
    tt = min(l, 1024)
    cb = 512
    ncb = width // cb
    nb8 = l // SUBLANES
    t8 = tt // SUBLANES

    def main(g):
        return pl.BlockSpec((tt, cb), lambda i, j: (i, g * ncb + j))

    def prev(g):
        return pl.BlockSpec((SUBLANES, cb), lambda i, j: (jnp.maximum(i * t8 - 1, 0), g * ncb + j))

    def nxt(g):
        return pl.BlockSpec((SUBLANES, cb), lambda i, j: (jnp.minimum((i + 1) * t8, nb8 - 1), g * ncb + j))

    def wspec(g):
        return pl.BlockSpec((3, cb), lambda i, j: (0, g * ncb + j))

    out = pl.BlockSpec((tt, cb), lambda i, j: (i, j))
    return pl.pallas_call(
        _hy_pre_kernel,
        grid=(l // tt, ncb),
        in_specs=[main(0), main(1), main(2), prev(0), prev(1), prev(2), nxt(0), nxt(1), nxt(2),
                  wspec(0), wspec(1), wspec(2)],
        out_specs=[out, out],
        out_shape=[jax.ShapeDtypeStruct((l, width), F32), jax.ShapeDtypeStruct((l, width), F32)],
        compiler_params=_params("parallel", "parallel"),
        name="hy_pre",
    )(p, p, p, p, p, p, p, p, p, hy_conv, hy_conv, hy_conv)


def _filter_hidden(j, seq_len, w1t_ref, w1c_ref, w1s_ref, b1_ref, fr_ref, w2_ref, b2_ref):
    pos_i = jnp.where(j < seq_len, j, 2 * seq_len - j)
    pos = pos_i.astype(F32)
    t = pos / float(max(seq_len - 1, 1))
    lane = lax.broadcasted_iota(jnp.int32, j.shape, 1)
    band = jnp.where(lane < FILTER_BANDS, lane + 1, 0).astype(F32)
    ang = (2.0 * math.pi / seq_len) * pos * band
    fr = fr_ref[...]
    pre = (t[:, 0:1] * w1t_ref[...]
           + jnp.dot(jnp.cos(ang), w1c_ref[...], precision=HIGHEST, preferred_element_type=F32)
           + jnp.dot(jnp.sin(ang), w1s_ref[...], precision=HIGHEST, preferred_element_type=F32)
           + b1_ref[...])
    h = jnp.sin(fr * pre)
    h = jnp.sin(fr * (jnp.dot(h, w2_ref[...], precision=HIGHEST, preferred_element_type=F32) + b2_ref[...]))
    return h, pos_i, t


def _filter_kernel(w1t_ref, w1c_ref, w1s_ref, b1_ref, fr_ref, w2_ref, b2_ref, w3_ref, delta_ref, o_ref, *, seq_len):
    tr = o_ref.shape[0]
    j = pl.program_id(0) * tr + lax.broadcasted_iota(jnp.int32, (tr, LANES), 0)
    h, pos_i, t = _filter_hidden(j, seq_len, w1t_ref, w1c_ref, w1s_ref, b1_ref, fr_ref, w2_ref, b2_ref)
    hh = jnp.dot(h.astype(BF16), w3_ref[...].astype(BF16), preferred_element_type=F32)
    decay = jnp.exp(-t[:, 0:1] * delta_ref[...])
    valid = pos_i[:, 0:1] != seq_len
    o_ref[...] = jnp.where(valid, hh * decay, 0.0)


_FILTER_PACK = LANES // FILTER_BANDS


def _filter_hidden_kernel(w1t_ref, w1c_ref, w1s_ref, b1_ref, fr_ref, w2_ref, b2_ref, o_ref, *, seq_len, n_a):
    tp = o_ref.shape[0]
    shift = n_a.bit_length() - 1

    def position(shape, lanes_per_pos):
        prow = pl.program_id(0) * tp + lax.broadcasted_iota(jnp.int32, shape, 0)
        r = prow * _FILTER_PACK + lax.broadcasted_iota(jnp.int32, shape, 1) // lanes_per_pos
        j = ((r & (n_a - 1)) * DFT_B) + (r >> shift)
        return jnp.where(j < seq_len, j, 2 * seq_len - j).astype(F32)

    pos = position((tp, LANES), FILTER_BANDS)
    band = (lax.broadcasted_iota(jnp.int32, (tp, LANES), 1) % FILTER_BANDS + 1).astype(F32)
    ang = (2.0 * math.pi / seq_len) * pos * band
    t = position(o_ref.shape, FILTER_HIDDEN) / float(max(seq_len - 1, 1))
    fr = fr_ref[...]
    pre = (t * w1t_ref[...]
           + jnp.dot(jnp.cos(ang), w1c_ref[...], precision=HIGHEST, preferred_element_type=F32)
           + jnp.dot(jnp.sin(ang), w1s_ref[...], precision=HIGHEST, preferred_element_type=F32)
           + b1_ref[...])
    h = jnp.sin(fr * pre)
    o_ref[...] = jnp.sin(fr * (jnp.dot(h, w2_ref[...], precision=HIGHEST, preferred_element_type=F32) + b2_ref[...]))


def _filter_mlp_args(w1, b1, freq, w2, b2):
    hid = FILTER_HIDDEN
    w1c = jnp.zeros((LANES, hid), F32).at[:FILTER_BANDS].set(w1[1:1 + FILTER_BANDS])
    w1s = jnp.zeros((LANES, hid), F32).at[:FILTER_BANDS].set(w1[1 + FILTER_BANDS:1 + 2 * FILTER_BANDS])
    full = lambda shape: pl.BlockSpec(shape, lambda i: (0, 0))
    specs = [full((1, hid)), full((LANES, hid)), full((LANES, hid)), full((1, hid)), full((1, hid)),
             full((hid, hid)), full((1, hid))]
    args = (w1[0:1], w1c, w1s, b1.reshape(1, hid), freq.reshape(1, hid), w2, b2.reshape(1, hid))
    return specs, args


def _filter_delta(width):
    return jnp.abs(jnp.linspace(math.log(FILTER_TARGET) / FILTER_SLOW_PCT,
                                math.log(FILTER_TARGET) / FILTER_FAST_PCT, width, dtype=F32)).reshape(1, width)


def _hyena_filter2(seq_len, w1, b1, freq, w2, b2, w3, width):
    n2 = 2 * seq_len
    tr = min(seq_len, 1024)
    nfwd = seq_len // tr
    specs, args = _filter_mlp_args(w1, b1, freq, w2, b2)
    return pl.pallas_call(
        functools.partial(_filter_kernel, seq_len=seq_len),
        grid=(n2 // tr,),
        in_specs=specs + [pl.BlockSpec((FILTER_HIDDEN, width), lambda i: (0, jnp.where(i < nfwd, 0, 1))),
                          pl.BlockSpec((1, width), lambda i: (0, 0))],
        out_specs=pl.BlockSpec((tr, width), lambda i: (i, 0)),
        out_shape=jax.ShapeDtypeStruct((n2, width), F32),
        compiler_params=_params("parallel"),
        name="hyena_filter",
    )(*args, w3, _filter_delta(width))


def _filter_hidden_permuted(seq_len, w1, b1, freq, w2, b2):
    n2 = 2 * seq_len
    n_a = n2 // DFT_B
    hid = FILTER_HIDDEN
    wide = _FILTER_PACK * hid
    rows = n2 // _FILTER_PACK
    tp = min(rows, 256)
    eye = jnp.eye(_FILTER_PACK, dtype=F32)
    tile = lambda v: jnp.tile(v.reshape(1, hid), (1, _FILTER_PACK))
    args = (tile(w1[0]), jnp.kron(eye, w1[1:1 + FILTER_BANDS]),
            jnp.kron(eye, w1[1 + FILTER_BANDS:1 + 2 * FILTER_BANDS]), tile(b1), tile(freq),
            jnp.kron(eye, w2), tile(b2))
    packed = pl.pallas_call(
        functools.partial(_filter_hidden_kernel, seq_len=seq_len, n_a=n_a),
        grid=(rows // tp,),
        in_specs=[pl.BlockSpec(a.shape, lambda i: (0, 0)) for a in args],
        out_specs=pl.BlockSpec((tp, wide), lambda i: (i, 0)),
        out_shape=jax.ShapeDtypeStruct((rows, wide), F32),
        compiler_params=_params("parallel"),
        name="hyena_filter_hidden",
    )(*args)
    return packed.reshape(n2, hid)


def _dft_tables(n_a, a_in):
    n = n_a * DFT_B
    b = np.arange(DFT_B)[:, None, None]
    k1 = np.arange(n_a)[None, :, None]
    a = np.arange(a_in)[None, None, :]
    e = np.exp(-2j * np.pi * ((k1 * (DFT_B * a + b)) % n) / n)
    g = np.concatenate([e.real, e.imag], axis=1)
    return jnp.asarray(g, F32)


def _idft_tables(n_a, a_out):
    n = n_a * DFT_B
    b = np.arange(DFT_B)[:, None, None]
    a = np.arange(a_out)[None, :, None]
    k1 = np.arange(n_a)[None, None, :]
    e = np.exp(2j * np.pi * ((k1 * (DFT_B * a + b)) % n) / n) / n
    h = np.concatenate([e.real, -e.imag], axis=2)
    return jnp.asarray(h, F32)


def _dft128_embed():
    i = np.arange(DFT_B)
    f = np.exp(-2j * np.pi * ((i[:, None] * i[None, :]) % DFT_B) / DFT_B)
    fwd = np.block([[f.real, -f.imag], [f.imag, f.real]])
    inv = np.block([[f.real, f.imag], [-f.imag, f.real]])
    return jnp.asarray(fwd, F32), jnp.asarray(inv, F32)


_PITCH = DFT_B + SUBLANES
_K1_BLOCK = 8
_B_UNROLL = 8


def _stage_rows(k1):
    return pl.ds(pl.multiple_of(k1 * _PITCH, SUBLANES), DFT_B)


def _single(shape, index_map):
    return pl.BlockSpec(shape, index_map, pipeline_mode=pl.Buffered(1))


def _pack_pair(re, im):
    hi = lax.bitcast_convert_type(re.astype(BF16).astype(F32), jnp.uint32)
    lo = lax.bitcast_convert_type(im.astype(BF16).astype(F32), jnp.uint32)
    return hi | (lo >> 16)


def _unpack_pair(w):
    re = lax.bitcast_convert_type(w & jnp.uint32(0xFFFF0000), F32)
    im = lax.bitcast_convert_type(w << 16, F32)
    return re.astype(BF16), im.astype(BF16)


def _spectrum_kernel(h_ref, w3f_ref, w3b_ref, delta_ref, g_ref, wf_ref, o_ref, s0_ref, s1_ref, *, seq_len, n_a):
    s = pl.program_id(1)
    half = n_a // 2
    cw = o_ref.shape[3]

    @pl.when(s == 0)
    def _():
        a_i = lax.broadcasted_iota(jnp.int32, (n_a, cw), 0)

        def body(b, carry):
            base = pl.multiple_of(b * n_a, SUBLANES)
            hf = jnp.dot(h_ref[pl.ds(base, half), :].astype(BF16), w3f_ref[...].astype(BF16),
                         preferred_element_type=F32)
            hb = jnp.dot(h_ref[pl.ds(base + half, half), :].astype(BF16), w3b_ref[...].astype(BF16),
                         preferred_element_type=F32)
            hh = jnp.concatenate([hf, hb], axis=0)
            j = a_i * DFT_B + b
            pos_i = jnp.where(j < seq_len, j, 2 * seq_len - j)
            t = pos_i.astype(F32) / float(max(seq_len - 1, 1))
            k2 = jnp.where(pos_i != seq_len, hh * jnp.exp(-t * delta_ref[...]), 0.0)
            r = jnp.dot(g_ref[b], k2.astype(BF16), preferred_element_type=F32)
            w = _pack_pair(r[:n_a], r[n_a:])
            s0_ref[pl.ds(b, n_a, stride=_PITCH), :] = w[:, :LANES]
            s1_ref[pl.ds(b, n_a, stride=_PITCH), :] = w[:, LANES:]
            return carry

        lax.fori_loop(0, DFT_B, body, 0, unroll=_B_UNROLL)

    @pl.when(s > 0)
    def _():
        k0 = (s - 1) * o_ref.shape[1]
        for i in range(o_ref.shape[1]):
            rows = _stage_rows(k0 + i)
            re, im = _unpack_pair(jnp.concatenate([s0_ref[rows, :], s1_ref[rows, :]], axis=1))
            z = jnp.dot(wf_ref[...], jnp.concatenate([re, im], axis=0), preferred_element_type=F32)
            o_ref[0, i] = z[:DFT_B]
            o_ref[1, i] = z[DFT_B:]


def _filter_spectrum(hperm, w3, seq_len, width, g_k, wf):
    n_a = 2 * seq_len // DFT_B
    kb = min(n_a, _K1_BLOCK)
    cw = 2 * LANES
    ncb = width // cw
    hid = FILTER_HIDDEN
    return pl.pallas_call(
        functools.partial(_spectrum_kernel, seq_len=seq_len, n_a=n_a),
        grid=(ncb, 1 + n_a // kb),
        in_specs=[_single(hperm.shape, lambda ci, s: (0, 0)),
                  pl.BlockSpec((hid, cw), lambda ci, s: (0, ci)),
                  pl.BlockSpec((hid, cw), lambda ci, s: (0, ncb + ci)),
                  pl.BlockSpec((1, cw), lambda ci, s: (0, ci)),
                  _single(g_k.shape, lambda ci, s: (0, 0, 0)),
                  _single(wf.shape, lambda ci, s: (0, 0))],
        out_specs=pl.BlockSpec((2, kb, DFT_B, cw), lambda ci, s: (0, jnp.maximum(s - 1, 0), 0, ci)),
        out_shape=jax.ShapeDtypeStruct((2, n_a, DFT_B, width), F32),
        scratch_shapes=[pltpu.VMEM((n_a * _PITCH, LANES), jnp.uint32), pltpu.VMEM((n_a * _PITCH, LANES), jnp.uint32)],
        compiler_params=_params("parallel", "arbitrary"),
        name="filter_spectrum",
    )(hperm, w3, w3, _filter_delta(width), g_k, wf)


def _hyena_epilogue(y, z, x0, bias, gain):
    yv = (y + z * bias) * x0
    ms = jnp.mean(yv * yv, axis=-1, keepdims=True)
    return yv * lax.rsqrt(ms + NORM_EPS) * gain


_EPI_ROWS = 512


def _long_conv_kernel(z_ref, x0_ref, kf_ref, g_ref, h_ref, wf_ref, wi_ref, bias_ref, gain_ref, o_ref,
                      sr_ref, si_ref, *, n_a):
    s = pl.program_id(1)
    half = n_a // 2
    kb = kf_ref.shape[1]
    nkb = n_a // kb

    @pl.when(s == 0)
    def _():
        def body(b, carry):
            xs = z_ref[pl.ds(b, half, stride=DFT_B), :].astype(BF16)
            r = jnp.dot(g_ref[b], xs, preferred_element_type=F32)
            sr_ref[pl.ds(b, n_a, stride=_PITCH), :] = r[:n_a]
            si_ref[pl.ds(b, n_a, stride=_PITCH), :] = r[n_a:]
            return carry

        lax.fori_loop(0, DFT_B, body, 0, unroll=_B_UNROLL)

    @pl.when((s > 0) & (s <= nkb))
    def _():
        k0 = (s - 1) * kb
        for i in range(kb):
            rows = _stage_rows(k0 + i)
            x = jnp.concatenate([sr_ref[rows, :], si_ref[rows, :]], axis=0).astype(BF16)
            zz = jnp.dot(wf_ref[...], x, preferred_element_type=F32)
            zr, zi = zz[:DFT_B], zz[DFT_B:]
            kr, ki = kf_ref[0, i], kf_ref[1, i]
            y = jnp.concatenate([zr * kr - zi * ki, zr * ki + zi * kr], axis=0).astype(BF16)
            cc = jnp.dot(wi_ref[...], y, preferred_element_type=F32)
            sr_ref[rows, :] = cc[:DFT_B]
            si_ref[rows, :] = cc[DFT_B:]

    @pl.when(s == nkb + 1)
    def _():
        def body(b, carry):
            x = jnp.concatenate([sr_ref[pl.ds(b, n_a, stride=_PITCH), :],
                                 si_ref[pl.ds(b, n_a, stride=_PITCH), :]], axis=0).astype(BF16)
            o_ref[pl.ds(b, half, stride=DFT_B), :] = jnp.dot(h_ref[b], x, preferred_element_type=F32)
            return carry

        lax.fori_loop(0, DFT_B, body, 0, unroll=_B_UNROLL)

        step = min(_EPI_ROWS, o_ref.shape[0])

        def epi(i, carry):
            rows = pl.ds(pl.multiple_of(i * step, step), step)
            o_ref[rows, :] = _hyena_epilogue(o_ref[rows, :], z_ref[rows, :], x0_ref[rows, :],
                                             bias_ref[...], gain_ref[...])
            return carry

        lax.fori_loop(0, o_ref.shape[0] // step, epi, 0)


def _long_conv(z, x0c, kf, g_z, h_tab, wf, wi, bias, gain):
    l, c = z.shape
    n_a = 2 * l // DFT_B
    kb = min(n_a, _K1_BLOCK)
    nkb = n_a // kb
    col = lambda ci, s: (0, ci)
    return pl.pallas_call(
        functools.partial(_long_conv_kernel, n_a=n_a),
        grid=(c // LANES, nkb + 2),
        in_specs=[_single((l, LANES), col), _single((l, LANES), col),
                  pl.BlockSpec((2, kb, DFT_B, LANES), lambda ci, s: (0, jnp.clip(s - 1, 0, nkb - 1), 0, ci)),
                  _single(g_z.shape, lambda ci, s: (0, 0, 0)), _single(h_tab.shape, lambda ci, s: (0, 0, 0)),
                  _single(wf.shape, lambda ci, s: (0, 0)), _single(wi.shape, lambda ci, s: (0, 0)),
                  pl.BlockSpec((1, LANES), col), pl.BlockSpec((1, LANES), col)],
        out_specs=pl.BlockSpec((l, LANES), col),
        out_shape=jax.ShapeDtypeStruct((l, c), F32),
        scratch_shapes=[pltpu.VMEM((n_a * _PITCH, LANES), F32), pltpu.VMEM((n_a * _PITCH, LANES), F32)],
        compiler_params=_params("parallel", "arbitrary"),
        name="long_conv",
    )(z, x0c, kf, g_z, h_tab, wf, wi, bias.reshape(1, c), gain.reshape(1, c))


def _hyena_long(z, x0c, filt, bias, gain):
    l, c = z.shape
    n_a = 2 * l // DFT_B
    wf, wi = _dft128_embed()
    wf, wi = wf.astype(BF16), wi.astype(BF16)
    g_z = _dft_tables(n_a, n_a // 2).astype(BF16)
    g_k = _dft_tables(n_a, n_a).astype(BF16)
    h_tab = _idft_tables(n_a, n_a // 2).astype(BF16)
    hperm = _filter_hidden_permuted(l, filt['hy_w1'], filt['hy_b1'], filt['hy_freq'], filt['hy_w2'], filt['hy_b2'])
    kf = _filter_spectrum(hperm, filt['hy_w3'], l, c, g_k, wf)
    return _long_conv(z, x0c, kf, g_z, h_tab, wf, wi, bias, gain)


def _ctx_conv_kernel(z_ref, k_ref, x0_ref, fz_ref, fk_ref, hi_ref, bias_ref, gain_ref, o_ref):
    n = k_ref.shape[0]
    zs = jnp.dot(fz_ref[...], z_ref[...].astype(BF16), preferred_element_type=F32)
    ks = jnp.dot(fk_ref[...], k_ref[...].astype(BF16), preferred_element_type=F32)
    zr, zi, kr, ki = zs[:n], zs[n:], ks[:n], ks[n:]
    y = jnp.concatenate([zr * kr - zi * ki, zr * ki + zi * kr], axis=0).astype(BF16)
    yt = jnp.dot(hi_ref[...], y, preferred_element_type=F32)
    o_ref[...] = _hyena_epilogue(yt, z_ref[...], x0_ref[...], bias_ref[...], gain_ref[...]).astype(o_ref.dtype)


def _hyena_short(z, x0c, k2, bias, gain):
    l, c = z.shape
    n = 2 * l
    k = np.arange(n)[:, None]
    t = np.arange(n)[None, :]
    e = np.exp(-2j * np.pi * ((k * t) % n) / n)
    fk = jnp.asarray(np.concatenate([e.real, e.imag], axis=0), F32).astype(BF16)
    fz = fk[:, :l]
    ei = np.conj(e[:, :l]).T / n
    hi = jnp.asarray(np.concatenate([ei.real, -ei.imag], axis=1), F32).astype(BF16)
    col = lambda rows: pl.BlockSpec((rows, LANES), lambda ci: (0, ci))
    full = lambda a: pl.BlockSpec(a.shape, lambda ci: (0, 0))
    return pl.pallas_call(
        _ctx_conv_kernel,
        grid=(c // LANES,),
        in_specs=[col(l), col(n), col(l), full(fz), full(fk), full(hi), col(1), col(1)],
        out_specs=col(l),
        out_shape=jax.ShapeDtypeStruct((l, c), F32),
        compiler_params=_params("parallel"),
        name="hyena_short",
    )(z, k2, x0c, fz, fk, hi, bias.reshape(1, c), gain.reshape(1, c))


def _split3(x):
    hi = x.astype(BF16)
    r1 = x - hi.astype(F32)
    mid = r1.astype(BF16)
    lo = (r1 - mid.astype(F32)).astype(BF16)
    return hi, mid, lo


def _gla_kernel(*refs, reverse, combine):
    if combine:
        (k_ref, q_ref, vlo_ref, vhi_ref, r_ref, wg_ref, bg_ref, s0_ref, of_ref, gate_ref, gout_ref,
         o_ref, sfin_ref, st_ref, la_ref, qe_ref, upd_ref, dec_ref, ke_ref, kl_ref, q2_ref, sc_ref, ob_ref) = refs
    else:
        (k_ref, q_ref, vlo_ref, vhi_ref, r_ref, wg_ref, bg_ref, s0_ref,
         o_ref, sfin_ref, st_ref, la_ref, qe_ref, upd_ref, dec_ref, ke_ref, kl_ref, q2_ref, sc_ref) = refs
        ob_ref = o_ref
    v_half = (vlo_ref, vhi_ref)
    rows_blk = k_ref.shape[0]
    nc = rows_blk // GLA_CHUNK
    qk = k_ref.shape[1]

    @pl.when(pl.program_id(0) == 0)
    def _():
        st_ref[...] = s0_ref[...]

    ri = lax.broadcasted_iota(jnp.int32, (2 * GLA_CHUNK, GLA_CHUNK), 0) % GLA_CHUNK
    ci = lax.broadcasted_iota(jnp.int32, (2 * GLA_CHUNK, GLA_CHUNK), 1)
    r3 = lax.broadcasted_iota(jnp.int32, (GLA_CHUNK, 3 * GLA_CHUNK), 0)
    c3 = lax.broadcasted_iota(jnp.int32, (GLA_CHUNK, 3 * GLA_CHUNK), 1) % GLA_CHUNK
    if reverse:
        tri3 = (c3 >= r3).astype(BF16)
        keep = ci > ri
    else:
        tri3 = (c3 <= r3).astype(BF16)
        keep = ci <= ri
    lane = lax.broadcasted_iota(jnp.int32, (1, LANES), 1)
    first = (lane < GLA_DK).astype(F32)
    srow = lax.broadcasted_iota(jnp.int32, (2 * GLA_DV, LANES), 0)
    scol = lax.broadcasted_iota(jnp.int32, (2 * GLA_DV, LANES), 1)
    own = (srow < GLA_DV) == (scol < GLA_DK)
    nt = (((1,), (1,)), ((), ()))
    tn = (((0,), (0,)), ((), ()))
    npair = GLA_HEADS // 2

    rh, rm, _ = _split3(r_ref[...])
    wh, wm, _ = _split3(wg_ref[...])
    pre = jnp.dot(jnp.concatenate([rh, rh, rm], axis=1), jnp.concatenate([wh, wm, wh], axis=0),
                  preferred_element_type=F32) + bg_ref[...]
    la_ref[...] = jax.nn.log_sigmoid(pre) / GLA_GATE_TAU

    def chunk_rows(c):
        return pl.ds(pl.multiple_of(c * GLA_CHUNK, GLA_CHUNK), GLA_CHUNK)

    def cum_pass(c, carry):
        rows = chunk_rows(c)
        hi, mid, lo = _split3(la_ref[rows, :])
        la_ref[rows, :] = jnp.dot(tri3, jnp.concatenate([hi, mid, lo], axis=0), preferred_element_type=F32)
        return carry

    def decay_pass(c, carry):
        rows = chunk_rows(c)
        cum = la_ref[rows, :]
        tot = cum[0:1] if reverse else cum[GLA_CHUNK - 1:GLA_CHUNK]
        k = k_ref[rows, :]
        qe = q_ref[rows, :] * (GLA_DK ** -0.5) * jnp.exp(cum)
        ke_ref[rows, :] = (k * jnp.exp(-cum)).astype(BF16)
        kl_ref[rows, :] = (k * jnp.exp(tot - cum)).astype(BF16)
        dec_ref[pl.ds(c, 1), :] = jnp.exp(tot)
        qe_ref[rows, :] = qe.astype(BF16)
        for p in range(npair):
            pr = slice(p * LANES, (p + 1) * LANES)
            q2_ref[c * npair + p] = jnp.concatenate([qe[:, pr] * first, qe[:, pr] * (1.0 - first)],
                                                    axis=0).astype(BF16)
        return carry

    def score_pass(c, carry):
        rows = chunk_rows(c)
        for p in range(npair):
            pr = slice(p * LANES, (p + 1) * LANES)
            sc = lax.dot_general(q2_ref[c * npair + p], ke_ref[rows, pr], nt, preferred_element_type=F32)
            sc_ref[c * npair + p] = jnp.where(keep, sc, 0.0).astype(BF16)
        return carry

    def value_pass(c, carry):
        rows = chunk_rows(c)
        for p in range(npair):
            pr = slice(p * LANES, (p + 1) * LANES)
            pv = slice(2 * p * GLA_DV, 2 * (p + 1) * GLA_DV)
            v_p = v_half[p // 2][rows, pl.ds((p % 2) * 2 * GLA_DV, 2 * GLA_DV)].astype(BF16)
            o2 = jnp.dot(sc_ref[c * npair + p], v_p, preferred_element_type=F32)
            ob_ref[rows, pl.ds(2 * p * GLA_DV, GLA_DV)] = o2[:GLA_CHUNK, :GLA_DV]
            ob_ref[rows, pl.ds((2 * p + 1) * GLA_DV, GLA_DV)] = o2[GLA_CHUNK:, GLA_DV:]
            u2 = lax.dot_general(v_p, kl_ref[rows, pr], tn, preferred_element_type=F32)
            upd_ref[c * npair + p] = jnp.where(own, u2, 0.0)
        return carry

    lax.fori_loop(0, nc, cum_pass, 0, unroll=4)
    lax.fori_loop(0, nc, decay_pass, 0)
    lax.fori_loop(0, nc, score_pass, 0, unroll=2)
    lax.fori_loop(0, nc, value_pass, 0, unroll=2)

    def carried(idx, carry):
        c = (nc - 1 - idx) if reverse else idx
        rows = pl.ds(pl.multiple_of(c * GLA_CHUNK, GLA_CHUNK), GLA_CHUNK)
        dec = dec_ref[pl.ds(c, 1), :]
        for p in range(npair):
            pr = slice(p * LANES, (p + 1) * LANES)
            pv = slice(2 * p * GLA_DV, 2 * (p + 1) * GLA_DV)
            st = st_ref[p]
            ob_ref[rows, pv] += lax.dot_general(qe_ref[rows, pr], st.astype(BF16), nt, preferred_element_type=F32)
            st_ref[p] = st * dec[:, pr] + upd_ref[c * npair + p]
        return carry

    lax.fori_loop(0, nc, carried, 0)

    if combine:
        for h in range(GLA_HEADS):
            hv = slice(h * GLA_DV, (h + 1) * GLA_DV)
            o = of_ref[:, hv] + ob_ref[:, hv]
            ms = jnp.mean(o * o, axis=-1, keepdims=True)
            o = o * lax.rsqrt(ms + NORM_EPS) * gout_ref[...]
            o_ref[:, hv] = (o * _silu(gate_ref[:, hv])).astype(o_ref.dtype)

    @pl.when(pl.program_id(0) == pl.num_programs(0) - 1)
    def _():
        sfin_ref[...] = st_ref[...]


_GLA_STATE = (GLA_HEADS // 2, 2 * GLA_DV, LANES)

_COL_K, _COL_V, _COL_Q, _COL_G = 24, 28, 36, 40


def _gla_pass(p, p_r, wg_pad, bg, s0, reverse, of=None, gout=None):
    l = p.shape[0]
    rb = min(l, 512)
    nblk = l // rb
    qk = GLA_HEADS * GLA_DK
    vw = GLA_HEADS * GLA_DV
    combine = of is not None
    rmap = (lambda i: nblk - 1 - i) if reverse else (lambda i: i)

    def cols(width, off128):
        return pl.BlockSpec((rb, width), lambda i: (rmap(i), off128 * LANES // width))

    state = pl.BlockSpec(_GLA_STATE, lambda i: (0, 0, 0))
    in_specs = [cols(qk, _COL_K), cols(qk, _COL_Q), cols(vw // 2, _COL_V), cols(vw // 2, _COL_V + vw // 2 // LANES),
                pl.BlockSpec((rb, LANES), lambda i: (rmap(i), 0)),
                pl.BlockSpec((LANES, qk), lambda i: (0, 0)), pl.BlockSpec((1, qk), lambda i: (0, 0)), state]
    args = [p, p, p, p, p_r, wg_pad, bg.reshape(1, qk), s0]
    nc = rb // GLA_CHUNK
    npair = _GLA_STATE[0]
    scratch = [pltpu.VMEM(_GLA_STATE, F32),
               pltpu.VMEM((rb, qk), F32),
               pltpu.VMEM((rb, qk), BF16),
               pltpu.VMEM((nc * npair,) + _GLA_STATE[1:], F32),
               pltpu.VMEM((nc, qk), F32),
               pltpu.VMEM((rb, qk), BF16),
               pltpu.VMEM((rb, qk), BF16),
               pltpu.VMEM((nc * npair, 2 * GLA_CHUNK, LANES), BF16),
               pltpu.VMEM((nc * npair, 2 * GLA_CHUNK, GLA_CHUNK), BF16)]
    if combine:
        in_specs += [pl.BlockSpec((rb, vw), lambda i: (rmap(i), 0)), cols(vw, _COL_G),
                     pl.BlockSpec((1, GLA_DV), lambda i: (0, 0))]
        args += [of, p, gout.reshape(1, GLA_DV)]
        scratch.append(pltpu.VMEM((rb, vw), F32))
    out_dtype = BF16 if combine else F32
    return pl.pallas_call(
        functools.partial(_gla_kernel, reverse=reverse, combine=combine),
        grid=(nblk,),
        in_specs=in_specs,
        out_specs=[pl.BlockSpec((rb, vw), lambda i: (rmap(i), 0)), state],
        out_shape=[jax.ShapeDtypeStruct((l, vw), out_dtype),
                   jax.ShapeDtypeStruct(_GLA_STATE, F32)],
        scratch_shapes=scratch,
        compiler_params=_params("arbitrary"),
        name="gla_bwd" if reverse else "gla_fwd",
    )(*args)


def _out_proj_kernel(hy_ref, o_ref, w_ref, x_ref, ga_ref, g_ref, sh_ref, sc_ref, xo_ref, h_ref):
    half = hy_ref.shape[1]
    mix = (jnp.dot(hy_ref[...].astype(BF16), w_ref[:half, :], preferred_element_type=F32)
           + jnp.dot(o_ref[...], w_ref[half:, :], preferred_element_type=F32))
    xn = x_ref[...] + ga_ref[...] * mix
    xo_ref[...] = xn
    h_ref[...] = _norm_mod(xn, g_ref[...], sh_ref[...], sc_ref[...]).astype(BF16)


def _out_proj(hy, o, w, x, ga, g, shift, scale, layer):
    m, d = x.shape
    half = hy.shape[1]
    tm = min(m, 256)
    vec = pl.BlockSpec((1, d), lambda i: (0, 0))
    row = lambda width: pl.BlockSpec((tm, width), lambda i: (i, 0))
    return pl.pallas_call(
        _out_proj_kernel,
        grid=(m // tm,),
        in_specs=[row(half), row(half), pl.BlockSpec((None, 2 * half, d), lambda i: (layer, 0, 0)), row(d),
                  vec, vec, vec, vec],
        out_specs=[row(d), row(d)],
        out_shape=[jax.ShapeDtypeStruct((m, d), F32), jax.ShapeDtypeStruct((m, d), BF16)],
        compiler_params=_params("parallel"),
        name="out_proj",
    )(hy, o, w, x, ga, g, shift, scale)


_HALO = GRID_W + SUBLANES


def _ffn_up_conv_kernel(x_ref, wa_ref, wu_ref, cw_ref, o_ref, a_ref, u_ref, *, vertical, cols, nm):
    j = pl.program_id(0)
    i = pl.program_id(1)
    tt = x_ref.shape[0]
    cur = i % 2
    prv = 1 - cur

    @pl.when((j == 0) & (i == 0))
    def _():
        a_ref[...] = jnp.zeros_like(a_ref)
        u_ref[...] = jnp.zeros_like(u_ref)

    def conv_previous_tile():
        col = lax.broadcasted_iota(jnp.int32, o_ref.shape, 0) & (cols - 1)
        acc = None
        for dc in (-1, 0, 1):
            part = None
            for dr in ((-1, 0, 1) if vertical else (0,)):
                tap = cw_ref[3 * (dr + 1) + (dc + 1):3 * (dr + 1) + (dc + 1) + 1, :]
                term = tap * a_ref[prv, pl.ds(_HALO + GRID_W * dr + dc, tt), :]
                part = term if part is None else part + term
            if dc == -1:
                part = jnp.where(col == 0, 0.0, part)
            elif dc == 1:
                part = jnp.where(col == cols - 1, 0.0, part)
            acc = part if acc is None else acc + part
        o_ref[...] = (_silu(acc) * u_ref[prv].astype(F32)).astype(o_ref.dtype)

    @pl.when(i < nm)
    def _():
        x = x_ref[...]
        a_new = jnp.dot(x, wa_ref[...].astype(BF16), preferred_element_type=F32)
        u_ref[cur] = jnp.dot(x, wu_ref[...].astype(BF16), preferred_element_type=F32).astype(BF16)
        upper = a_ref[prv, pl.ds(_HALO + tt - GRID_W, GRID_W), :]
        a_ref[cur, pl.ds(SUBLANES, GRID_W), :] = jnp.where(i == 0, 0.0, upper)
        a_ref[cur, pl.ds(_HALO, tt), :] = a_new
        a_ref[prv, pl.ds(_HALO + tt, GRID_W), :] = a_new[:GRID_W]
        conv_previous_tile()

    @pl.when(i == nm)
    def _():
        a_ref[prv, pl.ds(_HALO + tt, GRID_W), :] = jnp.zeros((GRID_W, a_ref.shape[2]), F32)
        conv_previous_tile()


def _ffn_up_conv(h, w_up, conv_w, cols, layer):
    m, d = h.shape
    dff = conv_w.shape[-1]
    vertical = m > cols
    assert cols & (cols - 1) == 0 and (cols == GRID_W or not vertical)
    tt = min(m, 1024)
    assert tt % cols == 0
    cb = 512
    ncb = dff // cb
    nm = m // tt
    return pl.pallas_call(
        functools.partial(_ffn_up_conv_kernel, vertical=vertical, cols=cols, nm=nm),
        grid=(ncb, nm + 1),
        in_specs=[pl.BlockSpec((tt, d), lambda j, i: (jnp.minimum(i, nm - 1), 0)),
                  pl.BlockSpec((None, d, cb), lambda j, i: (layer, 0, j)),
                  pl.BlockSpec((None, d, cb), lambda j, i: (layer, 0, ncb + j)),
                  pl.BlockSpec((9, cb), lambda j, i: (0, j))],
        out_specs=pl.BlockSpec((tt, cb), lambda j, i: (jnp.maximum(i - 1, 0), j)),
        out_shape=jax.ShapeDtypeStruct((m, dff), BF16),
        scratch_shapes=[pltpu.VMEM((2, tt + 2 * _HALO, cb), F32), pltpu.VMEM((2, tt, cb), BF16)],
        compiler_params=_params("arbitrary", "arbitrary"),
        name="ffn_up_conv",
    )(h, w_up, w_up, conv_w.reshape(9, dff))


def _ffn_down_kernel(g_ref, w_ref, x_ref, gate_ref, ng_ref, nsh_ref, nsc_ref, o_ref, *h_ref, post):
    xn = x_ref[...] + gate_ref[...] * jnp.dot(g_ref[...], w_ref[...], preferred_element_type=F32)
    if post == 'final':
        ms = jnp.mean(xn * xn, axis=-1, keepdims=True)
        xn = xn * lax.rsqrt(ms + NORM_EPS) * ng_ref[...]
    o_ref[...] = xn
    if post == 'next':
        h_ref[0][...] = _norm_mod(xn, ng_ref[...], nsh_ref[...], nsc_ref[...]).astype(BF16)


def _ffn_down(g, w, x, gate, norm, post, layer):
    m, d = x.shape
    dff = g.shape[1]
    tm = min(m, 256)
    vec = pl.BlockSpec((1, d), lambda i: (0, 0))
    row = pl.BlockSpec((tm, d), lambda i: (i, 0))
    out_specs, out_shape = [row], [jax.ShapeDtypeStruct((m, d), F32)]
    if post == 'next':
        out_specs.append(row)
        out_shape.append(jax.ShapeDtypeStruct((m, d), BF16))
    return pl.pallas_call(
        functools.partial(_ffn_down_kernel, post=post),
        grid=(m // tm,),
        in_specs=[pl.BlockSpec((tm, dff), lambda i: (i, 0)),
                  _single((None, dff, d), lambda i: (layer, 0, 0)),
                  row, vec, vec, vec, vec],
        out_specs=out_specs,
        out_shape=out_shape,
        compiler_params=_params("parallel"),
        name="ffn_down",
    )(g, w, x, gate, *norm)


def _regroup_w_in(w_in_t, hy_proj):
    qk = GLA_HEADS * GLA_DK
    vw = GLA_HEADS * GLA_DV
    head = hy_proj + qk + vw
    r = w_in_t[:, head:head + 2 * GLA_RANK]
    tail = w_in_t[:, head + 2 * GLA_RANK:]
    pad = jnp.zeros((w_in_t.shape[0], LANES - 2 * GLA_RANK, w_in_t.shape[2]), w_in_t.dtype)
    return head, tail, jnp.concatenate([r, pad], axis=1)


def _pad_gate_w(wg, lane0):
    return jnp.zeros((LANES, wg.shape[1]), F32).at[lane0:lane0 + GLA_RANK].set(wg)


def _mixer(p, p_r, lp, s0_f, s0_b, want_output):
    l = p.shape[0]
    width = lp['hy_bias'].shape[0]
    o_f, s_f = _gla_pass(p, p_r, lp['wg_f_pad'], lp['gla_bg_f'], s0_f, reverse=False)
    o, s_b = _gla_pass(p, p_r, lp['wg_b_pad'], lp['gla_bg_b'], s0_b, reverse=True, of=o_f, gout=lp['gla_out_g'])
    if not want_output:
        return None, None, s_f, s_b
    z, x0c = _hy_pre(p, lp['hy_conv'], width)
    if 2 * l // DFT_B >= 2 * SUBLANES:
        hy = _hyena_long(z, x0c, lp, lp['hy_bias'], lp['hy_out_g'])
    else:
        k2 = _hyena_filter2(l, lp['hy_w1'], lp['hy_b1'], lp['hy_freq'], lp['hy_w2'], lp['hy_b2'], lp['hy_w3'], width)
        hy = _hyena_short(z, x0c, k2, lp['hy_bias'], lp['hy_out_g'])
    return hy, o, s_f, s_b


def _ffn(h2, x, gate, lp, cols, norm, post, layer):
    gact = _ffn_up_conv(h2, lp['ffn_w_up'], lp['ffn_conv'], cols, layer)
    return _ffn_down(gact, lp['ffn_w_down'], x, gate, norm, post, layer)


def kernel(x, c, ctx, c_ctx, w_mod, b_mod, g_attn, w_in, hy_conv, hy_w1, hy_b1, hy_freq, hy_w2, hy_b2, hy_w3,
           hy_bias, hy_out_g, gla_wg_f, gla_bg_f, gla_wg_b, gla_bg_b, gla_out_g, w_out, g_ffn, ffn_w_up,
           ffn_conv, ffn_w_down, g_final):
    depth = w_mod.shape[0]
    batch, seq, d = x.shape
    assert batch == 1 and c.shape[0] == 1 and ctx.shape[0] == 1
    hy_proj = hy_conv.shape[-1]
    xs = x[0]
    cs = ctx[0]
    cc = jnp.zeros((SUBLANES, d), F32).at[0].set(c[0]).at[1].set(c_ctx)
    zero_state = jnp.zeros(_GLA_STATE, F32)
    gfin = g_final.reshape(1, d)
    w_in = jnp.swapaxes(w_in, 1, 2)
    n_head_cols, w_tail, w_rank = _regroup_w_in(w_in, hy_proj)
    w_out_b = w_out.astype(BF16)
    w_down_b = ffn_w_down.astype(BF16)
    mods =[_adaln(cc, w_mod, b_mod, l) for l in range(depth)]

    def attn_norm(l, row):
        return (g_attn[l].reshape(1, d), mods[l][row:row + 1, 0:d], mods[l][row:row + 1, d:2 * d])

    hx = _norm_mod_call(xs, *attn_norm(0, 0))
    hc = _norm_mod_call(cs, *attn_norm(0, 1))
    for l in range(depth):
        last = l == depth - 1
        lp = {
            'hy_conv': hy_conv[l], 'hy_w1': hy_w1[l], 'hy_b1': hy_b1[l], 'hy_freq': hy_freq[l],
            'hy_w2': hy_w2[l], 'hy_b2': hy_b2[l], 'hy_w3': hy_w3[l], 'hy_bias': hy_bias[l],
            'hy_out_g': hy_out_g[l], 'gla_bg_f': gla_bg_f[l], 'gla_bg_b': gla_bg_b[l],
            'gla_out_g': gla_out_g[l],
            'wg_f_pad': _pad_gate_w(gla_wg_f[l], 0), 'wg_b_pad': _pad_gate_w(gla_wg_b[l], GLA_RANK),
            'ffn_w_up': ffn_w_up, 'ffn_conv': ffn_conv[l], 'ffn_w_down': w_down_b,
        }
        mod = mods[l]
        _, _, ga, sf, scf, gf = [mod[0:1, i * d:(i + 1) * d] for i in range(6)]
        _, _, cga, csf, cscf, cgf = [mod[1:2, i * d:(i + 1) * d] for i in range(6)]
        gf_row = g_ffn[l].reshape(1, d)

        pc = _in_proj(hc, w_in, w_tail, n_head_cols, l)
        hy_c, o_c, s_f, s_b = _mixer(pc, _gate_rank_proj(hc, w_rank, l), lp, zero_state, zero_state,
                                     want_output=not last)
        if not last:
            cs, h2c = _out_proj(hy_c, o_c, w_out_b, cs, cga, gf_row, csf, cscf, l)
            cs, hc = _ffn(h2c, cs, cgf, lp, cs.shape[0], attn_norm(l + 1, 1), 'next', l)

        px = _in_proj(hx, w_in, w_tail, n_head_cols, l)
        hy_x, o_x, _, _ = _mixer(px, _gate_rank_proj(hx, w_rank, l), lp, s_f, s_b, want_output=True)
        xs, h2 = _out_proj(hy_x, o_x, w_out_b, xs, ga, gf_row, sf, scf, l)
        if last:
            (xs,) = _ffn(h2, xs, gf, lp, GRID_W, (gfin, gfin, gfin), 'final', l)
        else:
            xs, hx = _ffn(h2, xs, gf, lp, GRID_W, attn_norm(l + 1, 0), 'next', l)
    return xs[None]
```

```python
import functools
import math

import numpy as np
import jax
import jax.numpy as jnp
from jax import lax
from jax.experimental import pallas as pl
from jax.experimental.pallas import tpu as pltpu

F32 = jnp.float32
BF16 = jnp.bfloat16
HIGHEST = lax.Precision.HIGHEST

LANES = 128
SUBLANES = 8
VMEM_BYTES_V7X = 64 * 1024 * 1024
VMEM_LIMIT = VMEM_BYTES_V7X - 12 * 1024 * 1024

NORM_EPS = 1e-6
HYENA_GROUP = 128
FILTER_BANDS = 16
FILTER_HIDDEN = 64
FILTER_TARGET = 1e-2
FILTER_FAST_PCT = 0.3
FILTER_SLOW_PCT = 1.5
GLA_HEADS = 8
GLA_DK = 64
GLA_DV = 128
GLA_RANK = 16
GLA_GATE_TAU = 16.0
GLA_CHUNK = 64
GRID_W = 64
DFT_B = 128


def _params(*sem):
    return pltpu.CompilerParams(dimension_semantics=sem, vmem_limit_bytes=VMEM_LIMIT)


def _silu(x):
    return x * jax.nn.sigmoid(x)


def _adaln_kernel(c_ref, w_ref, b_ref, o_ref):
    s = _silu(c_ref[...]).astype(BF16)
    o_ref[...] = jnp.dot(s, w_ref[...].astype(BF16), preferred_element_type=F32) + b_ref[...]


def _adaln(cc, w_mod, b_mod, layer):
    depth, d, n = w_mod.shape
    tn = 1024
    return pl.pallas_call(
        _adaln_kernel,
        grid=(n // tn,),
        in_specs=[pl.BlockSpec((SUBLANES, d), lambda j: (0, 0)),
                  pl.BlockSpec((None, d, tn), lambda j: (layer, 0, j)),
                  pl.BlockSpec((None, 1, tn), lambda j: (layer, 0, j))],
        out_specs=pl.BlockSpec((SUBLANES, tn), lambda j: (0, j)),
        out_shape=jax.ShapeDtypeStruct((SUBLANES, n), F32),
        compiler_params=_params("arbitrary"),
        name="adaln",
    )(cc, w_mod, b_mod.reshape(depth, 1, n))


def _norm_mod(x, g, shift, scale):
    ms = jnp.mean(x * x, axis=-1, keepdims=True)
    return (x * lax.rsqrt(ms + NORM_EPS) * g) * (1.0 + scale) + shift


def _norm_mod_kernel(x_ref, g_ref, sh_ref, sc_ref, o_ref):
    o_ref[...] = _norm_mod(x_ref[...], g_ref[...], sh_ref[...], sc_ref[...]).astype(o_ref.dtype)


def _norm_mod_call(x, g, shift, scale):
    m, d = x.shape
    tm = min(m, 512)
    vec = pl.BlockSpec((1, d), lambda i: (0, 0))
    return pl.pallas_call(
        _norm_mod_kernel,
        grid=(m // tm,),
        in_specs=[pl.BlockSpec((tm, d), lambda i: (i, 0)), vec, vec, vec],
        out_specs=pl.BlockSpec((tm, d), lambda i: (i, 0)),
        out_shape=jax.ShapeDtypeStruct((m, d), BF16),
        compiler_params=_params("parallel"),
        name="norm_mod",
    )(x, g, shift, scale)


_IN_TILE = 768


_NT = (((1,), (1,)), ((), ()))


def _in_proj_kernel(h_ref, w_ref, wt_ref, o_ref, *, n_head):
    j = pl.program_id(1)

    @pl.when(j < n_head)
    def _():
        o_ref[...] = lax.dot_general(h_ref[...], w_ref[...].astype(BF16), _NT, preferred_element_type=F32)

    @pl.when(j >= n_head)
    def _():
        o_ref[...] = lax.dot_general(h_ref[...], wt_ref[...].astype(BF16), _NT, preferred_element_type=F32)


def _in_proj(h, w_in_t, w_tail_t, n_head_cols, layer):
    m, d = h.shape
    tn = _IN_TILE
    n_head = n_head_cols // tn
    n_tail = w_tail_t.shape[1] // tn
    assert n_head * tn == n_head_cols and n_tail * tn == w_tail_t.shape[1]
    tm = min(m, 1024)
    return pl.pallas_call(
        functools.partial(_in_proj_kernel, n_head=n_head),
        grid=(m // tm, n_head + n_tail),
        in_specs=[pl.BlockSpec((tm, d), lambda i, j: (i, 0)),
                  pl.BlockSpec((None, tn, d), lambda i, j: (layer, jnp.minimum(j, n_head - 1), 0)),
                  pl.BlockSpec((None, tn, d), lambda i, j: (layer, jnp.maximum(j - n_head, 0), 0))],
        out_specs=pl.BlockSpec((tm, tn), lambda i, j: (i, j)),
        out_shape=jax.ShapeDtypeStruct((m, n_head_cols + w_tail_t.shape[1]), F32),
        compiler_params=_params("parallel", "arbitrary"),
        name="in_proj",
    )(h, w_in_t, w_tail_t)


def _gate_rank_kernel(h_ref, w_ref, o_ref):
    o_ref[...] = lax.dot_general(h_ref[...], w_ref[...].astype(BF16), _NT, preferred_element_type=F32)


def _gate_rank_proj(h, w_r_t, layer):
    m, d = h.shape
    tm = min(m, 2048)
    return pl.pallas_call(
        _gate_rank_kernel,
        grid=(m // tm,),
        in_specs=[pl.BlockSpec((tm, d), lambda i: (i, 0)), pl.BlockSpec((None, LANES, d), lambda i: (layer, 0, 0))],
        out_specs=pl.BlockSpec((tm, LANES), lambda i: (i, 0)),
        out_shape=jax.ShapeDtypeStruct((m, LANES), F32),
        compiler_params=_params("parallel"),
        name="gate_rank_proj",
    )(h, w_r_t)


def _hy_pre_kernel(x0_ref, x1_ref, v_ref, x0p_ref, x1p_ref, vp_ref, x0n_ref, x1n_ref, vn_ref,
                   w0_ref, w1_ref, wv_ref, zx_ref):
    i = pl.program_id(0)
    first = i == 0
    last = i == pl.num_programs(0) - 1
    tt = x0_ref.shape[0]
    row = lax.broadcasted_iota(jnp.int32, x0_ref.shape, 0)

    def conv(ref, prev_ref, next_ref, w_ref):
        x = ref[...]
        prev_row = jnp.where(first, 0.0, prev_ref[SUBLANES - 1:SUBLANES, :])
        next_row = jnp.where(last, 0.0, next_ref[0:1, :])
        below = jnp.where(row == 0, prev_row, pltpu.roll(x, 1, 0))
        above = jnp.where(row == tt - 1, next_row, pltpu.roll(x, tt - 1, 0))
        return w_ref[0:1, :] * below + w_ref[1:2, :] * x + w_ref[2:3, :] * above

    z = conv(v_ref, vp_ref, vn_ref, wv_ref) * conv(x1_ref, x1p_ref, x1n_ref, w1_ref)
    zx_ref[...] = _pack_pair(z, conv(x0_ref, x0p_ref, x0n_ref, w0_ref))


def _hy_pre(p, hy_conv, width):
    l = p.shape[0]
    tt = min(l, 1024)
    cb = 512
    ncb = width // cb
    nb8 = l // SUBLANES
    t8 = tt // SUBLANES

    def main(g):
        return pl.BlockSpec((tt, cb), lambda i, j: (i, g * ncb + j))

    def prev(g):
        return pl.BlockSpec((SUBLANES, cb), lambda i, j: (jnp.maximum(i * t8 - 1, 0), g * ncb + j))

    def nxt(g):
        return pl.BlockSpec((SUBLANES, cb), lambda i, j: (jnp.minimum((i + 1) * t8, nb8 - 1), g * ncb + j))

    def wspec(g):
        return pl.BlockSpec((3, cb), lambda i, j: (0, g * ncb + j))

    out = pl.BlockSpec((tt, cb), lambda i, j: (i, j))
    return pl.pallas_call(
        _hy_pre_kernel,
        grid=(l // tt, ncb),
        in_specs=[main(0), main(1), main(2), prev(0), prev(1), prev(2), nxt(0), nxt(1), nxt(2),
                  wspec(0), wspec(1), wspec(2)],
        out_specs=out,
        out_shape=jax.ShapeDtypeStruct((l, width), jnp.uint32),
        compiler_params=_params("parallel", "parallel"),
        name="hy_pre",
    )(p, p, p, p, p, p, p, p, p, hy_conv, hy_conv, hy_conv)


def _filter_hidden(j, seq_len, w1t_ref, w1c_ref, w1s_ref, b1_ref, fr_ref, w2_ref, b2_ref):
    pos_i = jnp.where(j < seq_len, j, 2 * seq_len - j)
    pos = pos_i.astype(F32)
    t = pos / float(max(seq_len - 1, 1))
    lane = lax.broadcasted_iota(jnp.int32, j.shape, 1)
    band = jnp.where(lane < FILTER_BANDS, lane + 1, 0).astype(F32)
    ang = (2.0 * math.pi / seq_len) * pos * band
    fr = fr_ref[...]
    pre = (t[:, 0:1] * w1t_ref[...]
           + jnp.dot(jnp.cos(ang), w1c_ref[...], precision=HIGHEST, preferred_element_type=F32)
           + jnp.dot(jnp.sin(ang), w1s_ref[...], precision=HIGHEST, preferred_element_type=F32)
           + b1_ref[...])
    h = jnp.sin(fr * pre)
    h = jnp.sin(fr * (jnp.dot(h, w2_ref[...], precision=HIGHEST, preferred_element_type=F32) + b2_ref[...]))
    return h, pos_i, t


def _filter_kernel(w1t_ref, w1c_ref, w1s_ref, b1_ref, fr_ref, w2_ref, b2_ref, w3_ref, delta_ref, o_ref, *, seq_len):
    tr = o_ref.shape[0]
    j = pl.program_id(0) * tr + lax.broadcasted_iota(jnp.int32, (tr, LANES), 0)
    h, pos_i, t = _filter_hidden(j, seq_len, w1t_ref, w1c_ref, w1s_ref, b1_ref, fr_ref, w2_ref, b2_ref)
    hh = jnp.dot(h.astype(BF16), w3_ref[...].astype(BF16), preferred_element_type=F32)
    decay = jnp.exp(-t[:, 0:1] * delta_ref[...])
    valid = pos_i[:, 0:1] != seq_len
    o_ref[...] = jnp.where(valid, hh * decay, 0.0)


_FILTER_PACK = LANES // FILTER_BANDS


def _filter_hidden_kernel(w1t_ref, w1c_ref, w1s_ref, b1_ref, fr_ref, w2_ref, b2_ref, o_ref, *, seq_len, n_a):
    tp = o_ref.shape[0]
    shift = n_a.bit_length() - 1

    def position(shape, lanes_per_pos):
        prow = pl.program_id(0) * tp + lax.broadcasted_iota(jnp.int32, shape, 0)
        r = prow * _FILTER_PACK + lax.broadcasted_iota(jnp.int32, shape, 1) // lanes_per_pos
        j = ((r & (n_a - 1)) * DFT_B) + (r >> shift)
        return jnp.where(j < seq_len, j, 2 * seq_len - j).astype(F32)

    pos = position((tp, LANES), FILTER_BANDS)
    band = (lax.broadcasted_iota(jnp.int32, (tp, LANES), 1) % FILTER_BANDS + 1).astype(F32)
    ang = (2.0 * math.pi / seq_len) * pos * band
    t = position(o_ref.shape, FILTER_HIDDEN) / float(max(seq_len - 1, 1))
    fr = fr_ref[...]
    pre = (t * w1t_ref[...]
           + jnp.dot(jnp.cos(ang), w1c_ref[...], precision=HIGHEST, preferred_element_type=F32)
           + jnp.dot(jnp.sin(ang), w1s_ref[...], precision=HIGHEST, preferred_element_type=F32)
           + b1_ref[...])
    h = jnp.sin(fr * pre)
    o_ref[...] = jnp.sin(fr * (jnp.dot(h, w2_ref[...], precision=HIGHEST, preferred_element_type=F32) + b2_ref[...]))


def _filter_mlp_args(w1, b1, freq, w2, b2):
    hid = FILTER_HIDDEN
    w1c = jnp.zeros((LANES, hid), F32).at[:FILTER_BANDS].set(w1[1:1 + FILTER_BANDS])
    w1s = jnp.zeros((LANES, hid), F32).at[:FILTER_BANDS].set(w1[1 + FILTER_BANDS:1 + 2 * FILTER_BANDS])
    full = lambda shape: pl.BlockSpec(shape, lambda i: (0, 0))
    specs = [full((1, hid)), full((LANES, hid)), full((LANES, hid)), full((1, hid)), full((1, hid)),
             full((hid, hid)), full((1, hid))]
    args = (w1[0:1], w1c, w1s, b1.reshape(1, hid), freq.reshape(1, hid), w2, b2.reshape(1, hid))
    return specs, args


def _filter_delta(width):
    return jnp.abs(jnp.linspace(math.log(FILTER_TARGET) / FILTER_SLOW_PCT,
                                math.log(FILTER_TARGET) / FILTER_FAST_PCT, width, dtype=F32)).reshape(1, width)


def _hyena_filter2(seq_len, w1, b1, freq, w2, b2, w3, width):
    n2 = 2 * seq_len
    tr = min(seq_len, 1024)
    nfwd = seq_len // tr
    specs, args = _filter_mlp_args(w1, b1, freq, w2, b2)
    return pl.pallas_call(
        functools.partial(_filter_kernel, seq_len=seq_len),
        grid=(n2 // tr,),
        in_specs=specs + [pl.BlockSpec((FILTER_HIDDEN, width), lambda i: (0, jnp.where(i < nfwd, 0, 1))),
                          pl.BlockSpec((1, width), lambda i: (0, 0))],
        out_specs=pl.BlockSpec((tr, width), lambda i: (i, 0)),
        out_shape=jax.ShapeDtypeStruct((n2, width), F32),
        compiler_params=_params("parallel"),
        name="hyena_filter",
    )(*args, w3, _filter_delta(width))


def _filter_hidden_permuted(seq_len, w1, b1, freq, w2, b2):
    n2 = 2 * seq_len
    n_a = n2 // DFT_B
    hid = FILTER_HIDDEN
    wide = _FILTER_PACK * hid
    rows = n2 // _FILTER_PACK
    tp = min(rows, 256)
    eye = jnp.eye(_FILTER_PACK, dtype=F32)
    tile = lambda v: jnp.tile(v.reshape(1, hid), (1, _FILTER_PACK))
    args = (tile(w1[0]), jnp.kron(eye, w1[1:1 + FILTER_BANDS]),
            jnp.kron(eye, w1[1 + FILTER_BANDS:1 + 2 * FILTER_BANDS]), tile(b1), tile(freq),
            jnp.kron(eye, w2), tile(b2))
    packed = pl.pallas_call(
        functools.partial(_filter_hidden_kernel, seq_len=seq_len, n_a=n_a),
        grid=(rows // tp,),
        in_specs=[pl.BlockSpec(a.shape, lambda i: (0, 0)) for a in args],
        out_specs=pl.BlockSpec((tp, wide), lambda i: (i, 0)),
        out_shape=jax.ShapeDtypeStruct((rows, wide), F32),
        compiler_params=_params("parallel"),
        name="hyena_filter_hidden",
    )(*args)
    return packed.reshape(n2, hid)


def _dft_tables(n_a, a_in):
    n = n_a * DFT_B
    b = np.arange(DFT_B)[:, None, None]
    k1 = np.arange(n_a)[None, :, None]
    a = np.arange(a_in)[None, None, :]
    e = np.exp(-2j * np.pi * ((k1 * (DFT_B * a + b)) % n) / n)
    g = np.concatenate([e.real, e.imag], axis=1)
    return jnp.asarray(g, F32)


def _idft_tables(n_a, a_out):
    n = n_a * DFT_B
    b = np.arange(DFT_B)[:, None, None]
    a = np.arange(a_out)[None, :, None]
    k1 = np.arange(n_a)[None, None, :]
    e = np.exp(2j * np.pi * ((k1 * (DFT_B * a + b)) % n) / n) / n
    h = np.concatenate([e.real, -e.imag], axis=2)
    return jnp.asarray(h, F32)


def _dft128_embed():
    i = np.arange(DFT_B)
    f = np.exp(-2j * np.pi * ((i[:, None] * i[None, :]) % DFT_B) / DFT_B)
    fwd = np.block([[f.real, -f.imag], [f.imag, f.real]])
    inv = np.block([[f.real, f.imag], [-f.imag, f.real]])
    return jnp.asarray(fwd, F32), jnp.asarray(inv, F32)


_PITCH = DFT_B + SUBLANES
_K1_BLOCK = 8
_B_UNROLL = 8


def _stage_rows(k1):
    return pl.ds(pl.multiple_of(k1 * _PITCH, SUBLANES), DFT_B)


def _single(shape, index_map):
    return pl.BlockSpec(shape, index_map, pipeline_mode=pl.Buffered(1))


def _pack_pair(re, im):
    hi = lax.bitcast_convert_type(re.astype(BF16).astype(F32), jnp.uint32)
    lo = lax.bitcast_convert_type(im.astype(BF16).astype(F32), jnp.uint32)
    return hi | (lo >> 16)


def _unpack_pair(w):
    re = lax.bitcast_convert_type(w & jnp.uint32(0xFFFF0000), F32)
    im = lax.bitcast_convert_type(w << 16, F32)
    return re.astype(BF16), im.astype(BF16)


def _spectrum_kernel(h_ref, w3f_ref, w3b_ref, delta_ref, g_ref, wf_ref, o_ref, s0_ref, s1_ref, *, seq_len, n_a):
    s = pl.program_id(1)
    half = n_a // 2
    cw = o_ref.shape[3]

    @pl.when(s == 0)
    def _():
        a_i = lax.broadcasted_iota(jnp.int32, (n_a, cw), 0)

        def body(b, carry):
            base = pl.multiple_of(b * n_a, SUBLANES)
            hf = jnp.dot(h_ref[pl.ds(base, half), :].astype(BF16), w3f_ref[...].astype(BF16),
                         preferred_element_type=F32)
            hb = jnp.dot(h_ref[pl.ds(base + half, half), :].astype(BF16), w3b_ref[...].astype(BF16),
                         preferred_element_type=F32)
            hh = jnp.concatenate([hf, hb], axis=0)
            j = a_i * DFT_B + b
            pos_i = jnp.where(j < seq_len, j, 2 * seq_len - j)
            t = pos_i.astype(F32) / float(max(seq_len - 1, 1))
            k2 = jnp.where(pos_i != seq_len, hh * jnp.exp(-t * delta_ref[...]), 0.0)
            r = jnp.dot(g_ref[b], k2.astype(BF16), preferred_element_type=F32)
            w = _pack_pair(r[:n_a], r[n_a:])
            s0_ref[pl.ds(b, n_a, stride=_PITCH), :] = w[:, :LANES]
            s1_ref[pl.ds(b, n_a, stride=_PITCH), :] = w[:, LANES:]
            return carry

        lax.fori_loop(0, DFT_B, body, 0, unroll=_B_UNROLL)

    @pl.when(s > 0)
    def _():
        k0 = (s - 1) * o_ref.shape[1]
        for i in range(o_ref.shape[1]):
            rows = _stage_rows(k0 + i)
            re, im = _unpack_pair(jnp.concatenate([s0_ref[rows, :], s1_ref[rows, :]], axis=1))
            z = jnp.dot(wf_ref[...], jnp.concatenate([re, im], axis=0), preferred_element_type=F32)
            o_ref[0, i] = z[:DFT_B]
            o_ref[1, i] = z[DFT_B:]


def _filter_spectrum(hperm, w3, seq_len, width, g_k, wf):
    n_a = 2 * seq_len // DFT_B
    kb = min(n_a, _K1_BLOCK)
    cw = 2 * LANES
    ncb = width // cw
    hid = FILTER_HIDDEN
    return pl.pallas_call(
        functools.partial(_spectrum_kernel, seq_len=seq_len, n_a=n_a),
        grid=(ncb, 1 + n_a // kb),
        in_specs=[_single(hperm.shape, lambda ci, s: (0, 0)),
                  pl.BlockSpec((hid, cw), lambda ci, s: (0, ci)),
                  pl.BlockSpec((hid, cw), lambda ci, s: (0, ncb + ci)),
                  pl.BlockSpec((1, cw), lambda ci, s: (0, ci)),
                  _single(g_k.shape, lambda ci, s: (0, 0, 0)),
                  _single(wf.shape, lambda ci, s: (0, 0))],
        out_specs=pl.BlockSpec((2, kb, DFT_B, cw), lambda ci, s: (0, jnp.maximum(s - 1, 0), 0, ci)),
        out_shape=jax.ShapeDtypeStruct((2, n_a, DFT_B, width), F32),
        scratch_shapes=[pltpu.VMEM((n_a * _PITCH, LANES), jnp.uint32), pltpu.VMEM((n_a * _PITCH, LANES), jnp.uint32)],
        compiler_params=_params("parallel", "arbitrary"),
        name="filter_spectrum",
    )(hperm, w3, w3, _filter_delta(width), g_k, wf)


def _hyena_epilogue(y, z, x0, bias, gain):
    yv = (y + z * bias) * x0
    ms = jnp.mean(yv * yv, axis=-1, keepdims=True)
    return yv * lax.rsqrt(ms + NORM_EPS) * gain


def _pack_groups(y, zx, bias, gain):
    z, x0 = _unpack_pair(zx)
    out = _hyena_epilogue_groups(y, z.astype(F32), x0.astype(F32), bias, gain)
    return _pack_pair(out[:, :LANES], out[:, LANES:])


def _hyena_epilogue_groups(y, z, x0, bias, gain):
    return jnp.concatenate([_hyena_epilogue(y[:, g * LANES:(g + 1) * LANES], z[:, g * LANES:(g + 1) * LANES],
                                            x0[:, g * LANES:(g + 1) * LANES], bias[:, g * LANES:(g + 1) * LANES],
                                            gain[:, g * LANES:(g + 1) * LANES])
                            for g in range(y.shape[1] // LANES)], axis=1)


def _long_conv_kernel(zx0_ref, zx1_ref, kf_ref, g_ref, h_ref, wf_ref, wi_ref, bias_ref, gain_ref, o_ref,
                      s0_ref, s1_ref, *, n_a):
    s = pl.program_id(1)
    half = n_a // 2
    kb = kf_ref.shape[1]
    nkb = n_a // kb

    def load_zx(b):
        return jnp.concatenate([zx0_ref[pl.ds(b, half, stride=DFT_B), :],
                                zx1_ref[pl.ds(b, half, stride=DFT_B), :]], axis=1)

    def load_stage(rows):
        return _unpack_pair(jnp.concatenate([s0_ref[rows, :], s1_ref[rows, :]], axis=1))

    def store_stage(rows, re, im):
        w = _pack_pair(re, im)
        s0_ref[rows, :] = w[:, :LANES]
        s1_ref[rows, :] = w[:, LANES:]

    @pl.when(s == 0)
    def _():
        def body(b, carry):
            z, _ = _unpack_pair(load_zx(b))
            r = jnp.dot(g_ref[b], z, preferred_element_type=F32)
            store_stage(pl.ds(b, n_a, stride=_PITCH), r[:n_a], r[n_a:])
            return carry

        lax.fori_loop(0, DFT_B, body, 0, unroll=_B_UNROLL)

    @pl.when((s > 0) & (s <= nkb))
    def _():
        k0 = (s - 1) * kb
        for i in range(kb):
            rows = _stage_rows(k0 + i)
            re, im = load_stage(rows)
            zz = jnp.dot(wf_ref[...], jnp.concatenate([re, im], axis=0), preferred_element_type=F32)
            zr, zi = zz[:DFT_B], zz[DFT_B:]
            kr, ki = kf_ref[0, i], kf_ref[1, i]
            y = jnp.concatenate([zr * kr - zi * ki, zr * ki + zi * kr], axis=0).astype(BF16)
            cc = jnp.dot(wi_ref[...], y, preferred_element_type=F32)
            store_stage(rows, cc[:DFT_B], cc[DFT_B:])

    @pl.when(s == nkb + 1)
    def _():
        def body(b, carry):
            re, im = load_stage(pl.ds(b, n_a, stride=_PITCH))
            y = jnp.dot(h_ref[b], jnp.concatenate([re, im], axis=0), preferred_element_type=F32)
            o_ref[pl.ds(b, half, stride=DFT_B), :] = _pack_groups(y, load_zx(b), bias_ref[...], gain_ref[...])
            return carry

        lax.fori_loop(0, DFT_B, body, 0, unroll=_B_UNROLL)


def _long_conv(zx, kf, g_z, h_tab, wf, wi, bias, gain):
    l, c = zx.shape
    n_a = 2 * l // DFT_B
    kb = min(n_a, _K1_BLOCK)
    nkb = n_a // kb
    cw = 2 * LANES
    return pl.pallas_call(
        functools.partial(_long_conv_kernel, n_a=n_a),
        grid=(c // cw, nkb + 2),
        in_specs=[_single((l, LANES), lambda ci, s: (0, 2 * ci)), _single((l, LANES), lambda ci, s: (0, 2 * ci + 1)),
                  pl.BlockSpec((2, kb, DFT_B, cw), lambda ci, s: (0, jnp.clip(s - 1, 0, nkb - 1), 0, ci)),
                  _single(g_z.shape, lambda ci, s: (0, 0, 0)), _single(h_tab.shape, lambda ci, s: (0, 0, 0)),
                  _single(wf.shape, lambda ci, s: (0, 0)), _single(wi.shape, lambda ci, s: (0, 0)),
                  pl.BlockSpec((1, cw), lambda ci, s: (0, ci)), pl.BlockSpec((1, cw), lambda ci, s: (0, ci))],
        out_specs=pl.BlockSpec((l, LANES), lambda ci, s: (0, ci)),
        out_shape=jax.ShapeDtypeStruct((l, c // 2), jnp.uint32),
        scratch_shapes=[pltpu.VMEM((n_a * _PITCH, LANES), jnp.uint32), pltpu.VMEM((n_a * _PITCH, LANES), jnp.uint32)],
        compiler_params=_params("parallel", "arbitrary"),
        name="long_conv",
    )(zx, zx, kf, g_z, h_tab, wf, wi, bias.reshape(1, c), gain.reshape(1, c))


def _hyena_long(zx, filt, bias, gain):
    l, c = zx.shape
    n_a = 2 * l // DFT_B
    wf, wi = _dft128_embed()
    wf, wi = wf.astype(BF16), wi.astype(BF16)
    g_z = _dft_tables(n_a, n_a // 2).astype(BF16)
    g_k = _dft_tables(n_a, n_a).astype(BF16)
    h_tab = _idft_tables(n_a, n_a // 2).astype(BF16)
    hperm = _filter_hidden_permuted(l, filt['hy_w1'], filt['hy_b1'], filt['hy_freq'], filt['hy_w2'], filt['hy_b2'])
    kf = _filter_spectrum(hperm, filt['hy_w3'], l, c, g_k, wf)
    return _long_conv(zx, kf, g_z, h_tab, wf, wi, bias, gain)


def _ctx_conv_kernel(zx_ref, k_ref, fz_ref, fk_ref, hi_ref, bias_ref, gain_ref, o_ref):
    n = k_ref.shape[0]
    zx = zx_ref[...]
    z, _ = _unpack_pair(zx)
    zs = jnp.dot(fz_ref[...], z, preferred_element_type=F32)
    ks = jnp.dot(fk_ref[...], k_ref[...].astype(BF16), preferred_element_type=F32)
    zr, zi, kr, ki = zs[:n], zs[n:], ks[:n], ks[n:]
    y = jnp.concatenate([zr * kr - zi * ki, zr * ki + zi * kr], axis=0).astype(BF16)
    yt = jnp.dot(hi_ref[...], y, preferred_element_type=F32)
    o_ref[...] = _pack_groups(yt, zx, bias_ref[...], gain_ref[...])


def _hyena_short(zx, k2, bias, gain):
    l, c = zx.shape
    n = 2 * l
    k = np.arange(n)[:, None]
    t = np.arange(n)[None, :]
    e = np.exp(-2j * np.pi * ((k * t) % n) / n)
    fk = jnp.asarray(np.concatenate([e.real, e.imag], axis=0), F32).astype(BF16)
    fz = fk[:, :l]
    ei = np.conj(e[:, :l]).T / n
    hi = jnp.asarray(np.concatenate([ei.real, -ei.imag], axis=1), F32).astype(BF16)
    cw = 2 * LANES
    col = lambda rows: pl.BlockSpec((rows, cw), lambda ci: (0, ci))
    full = lambda a: pl.BlockSpec(a.shape, lambda ci: (0, 0))
    return pl.pallas_call(
        _ctx_conv_kernel,
        grid=(c // cw,),
        in_specs=[col(l), col(n), full(fz), full(fk), full(hi), col(1), col(1)],
        out_specs=pl.BlockSpec((l, LANES), lambda ci: (0, ci)),
        out_shape=jax.ShapeDtypeStruct((l, c // 2), jnp.uint32),
        compiler_params=_params("parallel"),
        name="hyena_short",
    )(zx, k2, fz, fk, hi, bias.reshape(1, c), gain.reshape(1, c))


def _split3(x):
    hi = x.astype(BF16)
    r1 = x - hi.astype(F32)
    mid = r1.astype(BF16)
    lo = (r1 - mid.astype(F32)).astype(BF16)
    return hi, mid, lo


def _gla_kernel(*refs, reverse, combine):
    if combine:
        (k_ref, q_ref, vlo_ref, vhi_ref, r_ref, wg_ref, bg_ref, s0_ref, of_ref, gate_ref, gout_ref,
         o_ref, sfin_ref, st_ref, la_ref, qe_ref, upd_ref, dec_ref, ke_ref, kl_ref, q2_ref, sc_ref, ob_ref) = refs
    else:
        (k_ref, q_ref, vlo_ref, vhi_ref, r_ref, wg_ref, bg_ref, s0_ref,
         o_ref, sfin_ref, st_ref, la_ref, qe_ref, upd_ref, dec_ref, ke_ref, kl_ref, q2_ref, sc_ref) = refs
        ob_ref = o_ref
    v_half = (vlo_ref, vhi_ref)
    rows_blk = k_ref.shape[0]
    nc = rows_blk // GLA_CHUNK
    qk = k_ref.shape[1]

    @pl.when(pl.program_id(0) == 0)
    def _():
        st_ref[...] = s0_ref[...]

    ri = lax.broadcasted_iota(jnp.int32, (2 * GLA_CHUNK, GLA_CHUNK), 0) % GLA_CHUNK
    ci = lax.broadcasted_iota(jnp.int32, (2 * GLA_CHUNK, GLA_CHUNK), 1)
    r3 = lax.broadcasted_iota(jnp.int32, (GLA_CHUNK, 3 * GLA_CHUNK), 0)
    c3 = lax.broadcasted_iota(jnp.int32, (GLA_CHUNK, 3 * GLA_CHUNK), 1) % GLA_CHUNK
    if reverse:
        tri3 = (c3 >= r3).astype(BF16)
        keep = ci > ri
    else:
        tri3 = (c3 <= r3).astype(BF16)
        keep = ci <= ri
    lane = lax.broadcasted_iota(jnp.int32, (1, LANES), 1)
    first = (lane < GLA_DK).astype(F32)
    srow = lax.broadcasted_iota(jnp.int32, (2 * GLA_DV, LANES), 0)
    scol = lax.broadcasted_iota(jnp.int32, (2 * GLA_DV, LANES), 1)
    own = (srow < GLA_DV) == (scol < GLA_DK)
    nt = (((1,), (1,)), ((), ()))
    tn = (((0,), (0,)), ((), ()))
    npair = GLA_HEADS // 2

    rh, rm, _ = _split3(r_ref[...])
    wh, wm, _ = _split3(wg_ref[...])
    pre = jnp.dot(jnp.concatenate([rh, rh, rm], axis=1), jnp.concatenate([wh, wm, wh], axis=0),
                  preferred_element_type=F32) + bg_ref[...]
    la_ref[...] = jax.nn.log_sigmoid(pre) / GLA_GATE_TAU

    def chunk_rows(c):
        return pl.ds(pl.multiple_of(c * GLA_CHUNK, GLA_CHUNK), GLA_CHUNK)

    def cum_pass(c, carry):
        rows = chunk_rows(c)
        hi, mid, lo = _split3(la_ref[rows, :])
        la_ref[rows, :] = jnp.dot(tri3, jnp.concatenate([hi, mid, lo], axis=0), preferred_element_type=F32)
        return carry

    def decay_pass(c, carry):
        rows = chunk_rows(c)
        cum = la_ref[rows, :]
        tot = cum[0:1] if reverse else cum[GLA_CHUNK - 1:GLA_CHUNK]
        k = k_ref[rows, :]
        qe = q_ref[rows, :] * (GLA_DK ** -0.5) * jnp.exp(cum)
        ke_ref[rows, :] = (k * jnp.exp(-cum)).astype(BF16)
        kl_ref[rows, :] = (k * jnp.exp(tot - cum)).astype(BF16)
        dec_ref[pl.ds(c, 1), :] = jnp.exp(tot)
        qe_ref[rows, :] = qe.astype(BF16)
        for p in range(npair):
            pr = slice(p * LANES, (p + 1) * LANES)
            q2_ref[c * npair + p] = jnp.concatenate([qe[:, pr] * first, qe[:, pr] * (1.0 - first)],
                                                    axis=0).astype(BF16)
        return carry

    def score_pass(c, carry):
        rows = chunk_rows(c)
        for p in range(npair):
            pr = slice(p * LANES, (p + 1) * LANES)
            sc = lax.dot_general(q2_ref[c * npair + p], ke_ref[rows, pr], nt, preferred_element_type=F32)
            sc_ref[c * npair + p] = jnp.where(keep, sc, 0.0).astype(BF16)
        return carry

    def value_pass(c, carry):
        rows = chunk_rows(c)
        for p in range(npair):
            pr = slice(p * LANES, (p + 1) * LANES)
            pv = slice(2 * p * GLA_DV, 2 * (p + 1) * GLA_DV)
            v_p = v_half[p // 2][rows, pl.ds((p % 2) * 2 * GLA_DV, 2 * GLA_DV)].astype(BF16)
            o2 = jnp.dot(sc_ref[c * npair + p], v_p, preferred_element_type=F32)
            ob_ref[rows, pl.ds(2 * p * GLA_DV, GLA_DV)] = o2[:GLA_CHUNK, :GLA_DV]
            ob_ref[rows, pl.ds((2 * p + 1) * GLA_DV, GLA_DV)] = o2[GLA_CHUNK:, GLA_DV:]
            u2 = lax.dot_general(v_p, kl_ref[rows, pr], tn, preferred_element_type=F32)
            upd_ref[c * npair + p] = jnp.where(own, u2, 0.0)
        return carry

    lax.fori_loop(0, nc, cum_pass, 0, unroll=4)
    lax.fori_loop(0, nc, decay_pass, 0)
    lax.fori_loop(0, nc, score_pass, 0, unroll=2)
    lax.fori_loop(0, nc, value_pass, 0, unroll=2)

    def carried(idx, carry):
        c = (nc - 1 - idx) if reverse else idx
        rows = pl.ds(pl.multiple_of(c * GLA_CHUNK, GLA_CHUNK), GLA_CHUNK)
        dec = dec_ref[pl.ds(c, 1), :]
        for p in range(npair):
            pr = slice(p * LANES, (p + 1) * LANES)
            pv = slice(2 * p * GLA_DV, 2 * (p + 1) * GLA_DV)
            st = st_ref[p]
            ob_ref[rows, pv] += lax.dot_general(qe_ref[rows, pr], st.astype(BF16), nt, preferred_element_type=F32)
            st_ref[p] = st * dec[:, pr] + upd_ref[c * npair + p]
        return carry

    lax.fori_loop(0, nc, carried, 0)

    if combine:
        for h in range(GLA_HEADS):
            hv = slice(h * GLA_DV, (h + 1) * GLA_DV)
            o = of_ref[:, hv] + ob_ref[:, hv]
            ms = jnp.mean(o * o, axis=-1, keepdims=True)
            o = o * lax.rsqrt(ms + NORM_EPS) * gout_ref[...]
            o_ref[:, hv] = (o * _silu(gate_ref[:, hv])).astype(o_ref.dtype)

    @pl.when(pl.program_id(0) == pl.num_programs(0) - 1)
    def _():
        sfin_ref[...] = st_ref[...]


_GLA_STATE = (GLA_HEADS // 2, 2 * GLA_DV, LANES)

_COL_K, _COL_V, _COL_Q, _COL_G = 24, 28, 36, 40


def _gla_pass(p, p_r, wg_pad, bg, s0, reverse, of=None, gout=None):
    l = p.shape[0]
    rb = min(l, 512)
    nblk = l // rb
    qk = GLA_HEADS * GLA_DK
    vw = GLA_HEADS * GLA_DV
    combine = of is not None
    rmap = (lambda i: nblk - 1 - i) if reverse else (lambda i: i)

    def cols(width, off128):
        return pl.BlockSpec((rb, width), lambda i: (rmap(i), off128 * LANES // width))

    state = pl.BlockSpec(_GLA_STATE, lambda i: (0, 0, 0))
    in_specs = [cols(qk, _COL_K), cols(qk, _COL_Q), cols(vw // 2, _COL_V), cols(vw // 2, _COL_V + vw // 2 // LANES),
                pl.BlockSpec((rb, LANES), lambda i: (rmap(i), 0)),
                pl.BlockSpec((LANES, qk), lambda i: (0, 0)), pl.BlockSpec((1, qk), lambda i: (0, 0)), state]
    args = [p, p, p, p, p_r, wg_pad, bg.reshape(1, qk), s0]
    nc = rb // GLA_CHUNK
    npair = _GLA_STATE[0]
    scratch = [pltpu.VMEM(_GLA_STATE, F32),
               pltpu.VMEM((rb, qk), F32),
               pltpu.VMEM((rb, qk), BF16),
               pltpu.VMEM((nc * npair,) + _GLA_STATE[1:], F32),
               pltpu.VMEM((nc, qk), F32),
               pltpu.VMEM((rb, qk), BF16),
               pltpu.VMEM((rb, qk), BF16),
               pltpu.VMEM((nc * npair, 2 * GLA_CHUNK, LANES), BF16),
               pltpu.VMEM((nc * npair, 2 * GLA_CHUNK, GLA_CHUNK), BF16)]
    if combine:
        in_specs += [pl.BlockSpec((rb, vw), lambda i: (rmap(i), 0)), cols(vw, _COL_G),
                     pl.BlockSpec((1, GLA_DV), lambda i: (0, 0))]
        args += [of, p, gout.reshape(1, GLA_DV)]
        scratch.append(pltpu.VMEM((rb, vw), F32))
    out_dtype = BF16 if combine else F32
    return pl.pallas_call(
        functools.partial(_gla_kernel, reverse=reverse, combine=combine),
        grid=(nblk,),
        in_specs=in_specs,
        out_specs=[pl.BlockSpec((rb, vw), lambda i: (rmap(i), 0)), state],
        out_shape=[jax.ShapeDtypeStruct((l, vw), out_dtype),
                   jax.ShapeDtypeStruct(_GLA_STATE, F32)],
        scratch_shapes=scratch,
        compiler_params=_params("arbitrary"),
        name="gla_bwd" if reverse else "gla_fwd",
    )(*args)


def _out_proj_kernel(hy_ref, o_ref, w_ref, x_ref, ga_ref, g_ref, sh_ref, sc_ref, xo_ref, h_ref):
    q = hy_ref.shape[1]
    hy_a, hy_b = _unpack_pair(hy_ref[...])
    mix = (jnp.dot(hy_a, w_ref[:q, :], preferred_element_type=F32)
           + jnp.dot(hy_b, w_ref[q:2 * q, :], preferred_element_type=F32)
           + jnp.dot(o_ref[...], w_ref[2 * q:, :], preferred_element_type=F32))
    xn = x_ref[...] + ga_ref[...] * mix
    xo_ref[...] = xn
    h_ref[...] = _norm_mod(xn, g_ref[...], sh_ref[...], sc_ref[...]).astype(BF16)


def _out_proj(hy, o, w, x, ga, g, shift, scale, layer):
    m, d = x.shape
    tm = min(m, 256)
    vec = pl.BlockSpec((1, d), lambda i: (0, 0))
    row = lambda width: pl.BlockSpec((tm, width), lambda i: (i, 0))
    return pl.pallas_call(
        _out_proj_kernel,
        grid=(m // tm,),
        in_specs=[row(hy.shape[1]), row(o.shape[1]), pl.BlockSpec((None, w.shape[1], d), lambda i: (layer, 0, 0)),
                  row(d), vec, vec, vec, vec],
        out_specs=[row(d), row(d)],
        out_shape=[jax.ShapeDtypeStruct((m, d), F32), jax.ShapeDtypeStruct((m, d), BF16)],
        compiler_params=_params("parallel"),
        name="out_proj",
    )(hy, o, w, x, ga, g, shift, scale)


_HALO = GRID_W + SUBLANES


def _ffn_up_conv_kernel(x_ref, wa_ref, wu_ref, cw_ref, o_ref, a_ref, u_ref, *, vertical, cols, nm):
    j = pl.program_id(0)
    i = pl.program_id(1)
    tt = x_ref.shape[0]
    cur = i % 2
    prv = 1 - cur

    @pl.when((j == 0) & (i == 0))
    def _():
        a_ref[...] = jnp.zeros_like(a_ref)
        u_ref[...] = jnp.zeros_like(u_ref)

    def conv_previous_tile():
        col = lax.broadcasted_iota(jnp.int32, o_ref.shape, 0) & (cols - 1)
        acc = None
        for dc in (-1, 0, 1):
            part = None
            for dr in ((-1, 0, 1) if vertical else (0,)):
                tap = cw_ref[3 * (dr + 1) + (dc + 1):3 * (dr + 1) + (dc + 1) + 1, :]
                term = tap * a_ref[prv, pl.ds(_HALO + GRID_W * dr + dc, tt), :]
                part = term if part is None else part + term
            if dc == -1:
                part = jnp.where(col == 0, 0.0, part)
            elif dc == 1:
                part = jnp.where(col == cols - 1, 0.0, part)
            acc = part if acc is None else acc + part
        o_ref[...] = (_silu(acc) * u_ref[prv].astype(F32)).astype(o_ref.dtype)

    @pl.when(i < nm)
    def _():
        x = x_ref[...]
        a_new = jnp.dot(x, wa_ref[...].astype(BF16), preferred_element_type=F32)
        u_ref[cur] = jnp.dot(x, wu_ref[...].astype(BF16), preferred_element_type=F32).astype(BF16)
        upper = a_ref[prv, pl.ds(_HALO + tt - GRID_W, GRID_W), :]
        a_ref[cur, pl.ds(SUBLANES, GRID_W), :] = jnp.where(i == 0, 0.0, upper)
        a_ref[cur, pl.ds(_HALO, tt), :] = a_new
        a_ref[prv, pl.ds(_HALO + tt, GRID_W), :] = a_new[:GRID_W]
        conv_previous_tile()

    @pl.when(i == nm)
    def _():
        a_ref[prv, pl.ds(_HALO + tt, GRID_W), :] = jnp.zeros((GRID_W, a_ref.shape[2]), F32)
        conv_previous_tile()


def _ffn_up_conv(h, w_up, conv_w, cols, layer):
    m, d = h.shape
    dff = conv_w.shape[-1]
    vertical = m > cols
    assert cols & (cols - 1) == 0 and (cols == GRID_W or not vertical)
    tt = min(m, 1024)
    assert tt % cols == 0
    cb = 512
    ncb = dff // cb
    nm = m // tt
    return pl.pallas_call(
        functools.partial(_ffn_up_conv_kernel, vertical=vertical, cols=cols, nm=nm),
        grid=(ncb, nm + 1),
        in_specs=[pl.BlockSpec((tt, d), lambda j, i: (jnp.minimum(i, nm - 1), 0)),
                  pl.BlockSpec((None, d, cb), lambda j, i: (layer, 0, j)),
                  pl.BlockSpec((None, d, cb), lambda j, i: (layer, 0, ncb + j)),
                  pl.BlockSpec((9, cb), lambda j, i: (0, j))],
        out_specs=pl.BlockSpec((tt, cb), lambda j, i: (jnp.maximum(i - 1, 0), j)),
        out_shape=jax.ShapeDtypeStruct((m, dff), BF16),
        scratch_shapes=[pltpu.VMEM((2, tt + 2 * _HALO, cb), F32), pltpu.VMEM((2, tt, cb), BF16)],
        compiler_params=_params("arbitrary", "arbitrary"),
        name="ffn_up_conv",
    )(h, w_up, w_up, conv_w.reshape(9, dff))


def _ffn_down_kernel(g_ref, w_ref, x_ref, gate_ref, ng_ref, nsh_ref, nsc_ref, o_ref, *h_ref, post):
    xn = x_ref[...] + gate_ref[...] * jnp.dot(g_ref[...], w_ref[...], preferred_element_type=F32)
    if post == 'final':
        ms = jnp.mean(xn * xn, axis=-1, keepdims=True)
        xn = xn * lax.rsqrt(ms + NORM_EPS) * ng_ref[...]
    o_ref[...] = xn
    if post == 'next':
        h_ref[0][...] = _norm_mod(xn, ng_ref[...], nsh_ref[...], nsc_ref[...]).astype(BF16)


def _ffn_down(g, w, x, gate, norm, post, layer):
    m, d = x.shape
    dff = g.shape[1]
    tm = min(m, 256)
    vec = pl.BlockSpec((1, d), lambda i: (0, 0))
    row = pl.BlockSpec((tm, d), lambda i: (i, 0))
    out_specs, out_shape = [row], [jax.ShapeDtypeStruct((m, d), F32)]
    if post == 'next':
        out_specs.append(row)
        out_shape.append(jax.ShapeDtypeStruct((m, d), BF16))
    return pl.pallas_call(
        functools.partial(_ffn_down_kernel, post=post),
        grid=(m // tm,),
        in_specs=[pl.BlockSpec((tm, dff), lambda i: (i, 0)),
                  _single((None, dff, d), lambda i: (layer, 0, 0)),
                  row, vec, vec, vec, vec],
        out_specs=out_specs,
        out_shape=out_shape,
        compiler_params=_params("parallel"),
        name="ffn_down",
    )(g, w, x, gate, *norm)


def _regroup_w_in(w_in_t, hy_proj):
    qk = GLA_HEADS * GLA_DK
    vw = GLA_HEADS * GLA_DV
    head = hy_proj + qk + vw
    r = w_in_t[:, head:head + 2 * GLA_RANK]
    tail = w_in_t[:, head + 2 * GLA_RANK:]
    pad = jnp.zeros((w_in_t.shape[0], LANES - 2 * GLA_RANK, w_in_t.shape[2]), w_in_t.dtype)
    return head, tail, jnp.concatenate([r, pad], axis=1)


def _pad_gate_w(wg, lane0):
    return jnp.zeros((LANES, wg.shape[1]), F32).at[lane0:lane0 + GLA_RANK].set(wg)


def _mixer(p, p_r, lp, s0_f, s0_b, want_output):
    l = p.shape[0]
    width = lp['hy_bias'].shape[0]
    o_f, s_f = _gla_pass(p, p_r, lp['wg_f_pad'], lp['gla_bg_f'], s0_f, reverse=False)
    o, s_b = _gla_pass(p, p_r, lp['wg_b_pad'], lp['gla_bg_b'], s0_b, reverse=True, of=o_f, gout=lp['gla_out_g'])
    if not want_output:
        return None, None, s_f, s_b
    zx = _hy_pre(p, lp['hy_conv'], width)
    if 2 * l // DFT_B >= 2 * SUBLANES:
        hy = _hyena_long(zx, lp, lp['hy_bias'], lp['hy_out_g'])
    else:
        k2 = _hyena_filter2(l, lp['hy_w1'], lp['hy_b1'], lp['hy_freq'], lp['hy_w2'], lp['hy_b2'], lp['hy_w3'], width)
        hy = _hyena_short(zx, k2, lp['hy_bias'], lp['hy_out_g'])
    return hy, o, s_f, s_b


def _order_w_out(w_out, hy_width):
    depth, _, d = w_out.shape
    hy_rows = w_out[:, :hy_width].reshape(depth, hy_width // (2 * LANES), 2, LANES, d)
    hy_rows = jnp.swapaxes(hy_rows, 1, 2).reshape(depth, hy_width, d)
    return jnp.concatenate([hy_rows, w_out[:, hy_width:]], axis=1).astype(BF16)


def _ffn(h2, x, gate, lp, cols, norm, post, layer):
    gact = _ffn_up_conv(h2, lp['ffn_w_up'], lp['ffn_conv'], cols, layer)
    return _ffn_down(gact, lp['ffn_w_down'], x, gate, norm, post, layer)


def kernel(x, c, ctx, c_ctx, w_mod, b_mod, g_attn, w_in, hy_conv, hy_w1, hy_b1, hy_freq, hy_w2, hy_b2, hy_w3,
           hy_bias, hy_out_g, gla_wg_f, gla_bg_f, gla_wg_b, gla_bg_b, gla_out_g, w_out, g_ffn, ffn_w_up,
           ffn_conv, ffn_w_down, g_final):
    depth = w_mod.shape[0]
    batch, seq, d = x.shape
    assert batch == 1 and c.shape[0] == 1 and ctx.shape[0] == 1
    hy_proj = hy_conv.shape[-1]
    xs = x[0]
    cs = ctx[0]
    cc = jnp.zeros((SUBLANES, d), F32).at[0].set(c[0]).at[1].set(c_ctx)
    zero_state = jnp.zeros(_GLA_STATE, F32)
    gfin = g_final.reshape(1, d)
    w_in = jnp.swapaxes(w_in, 1, 2)
    n_head_cols, w_tail, w_rank = _regroup_w_in(w_in, hy_proj)
    w_out_b = _order_w_out(w_out, hy_bias.shape[-1])
    w_down_b = ffn_w_down.astype(BF16)
    mods =[_adaln(cc, w_mod, b_mod, l) for l in range(depth)]

    def attn_norm(l, row):
        return (g_attn[l].reshape(1, d), mods[l][row:row + 1, 0:d], mods[l][row:row + 1, d:2 * d])

    hx = _norm_mod_call(xs, *attn_norm(0, 0))
    hc = _norm_mod_call(cs, *attn_norm(0, 1))
    for l in range(depth):
        last = l == depth - 1
        lp = {
            'hy_conv': hy_conv[l], 'hy_w1': hy_w1[l], 'hy_b1': hy_b1[l], 'hy_freq': hy_freq[l],
            'hy_w2': hy_w2[l], 'hy_b2': hy_b2[l], 'hy_w3': hy_w3[l], 'hy_bias': hy_bias[l],
            'hy_out_g': hy_out_g[l], 'gla_bg_f': gla_bg_f[l], 'gla_bg_b': gla_bg_b[l],
            'gla_out_g': gla_out_g[l],
            'wg_f_pad': _pad_gate_w(gla_wg_f[l], 0), 'wg_b_pad': _pad_gate_w(gla_wg_b[l], GLA_RANK),
            'ffn_w_up': ffn_w_up, 'ffn_conv': ffn_conv[l], 'ffn_w_down': w_down_b,
        }
        mod = mods[l]
        _, _, ga, sf, scf, gf = [mod[0:1, i * d:(i + 1) * d] for i in range(6)]
        _, _, cga, csf, cscf, cgf = [mod[1:2, i * d:(i + 1) * d] for i in range(6)]
        gf_row = g_ffn[l].reshape(1, d)

        pc = _in_proj(hc, w_in, w_tail, n_head_cols, l)
        hy_c, o_c, s_f, s_b = _mixer(pc, _gate_rank_proj(hc, w_rank, l), lp, zero_state, zero_state,
                                     want_output=not last)
        if not last:
            cs, h2c = _out_proj(hy_c, o_c, w_out_b, cs, cga, gf_row, csf, cscf, l)
            cs, hc = _ffn(h2c, cs, cgf, lp, cs.shape[0], attn_norm(l + 1, 1), 'next', l)

        px = _in_proj(hx, w_in, w_tail, n_head_cols, l)
        hy_x, o_x, _, _ = _mixer(px, _gate_rank_proj(hx, w_rank, l), lp, s_f, s_b, want_output=True)
        xs, h2 = _out_proj(hy_x, o_x, w_out_b, xs, ga, gf_row, sf, scf, l)
        if last:
            (xs,) = _ffn(h2, xs, gf, lp, GRID_W, (gfin, gfin, gfin), 'final', l)
        else:
            xs, hx = _ffn(h2, xs, gf, lp, GRID_W, attn_norm(l + 1, 0), 'next', l)
    return xs[None]
```

```python
import functools
import math

import numpy as np
import jax
import jax.numpy as jnp
from jax import lax
from jax.experimental import pallas as pl
from jax.experimental.pallas import tpu as pltpu

F32 = jnp.float32
BF16 = jnp.bfloat16
HIGHEST = lax.Precision.HIGHEST

LANES = 128
SUBLANES = 8
VMEM_BYTES_V7X = 64 * 1024 * 1024
VMEM_LIMIT = VMEM_BYTES_V7X - 12 * 1024 * 1024

NORM_EPS = 1e-6
HYENA_GROUP = 128
FILTER_BANDS = 16
FILTER_HIDDEN = 64
FILTER_TARGET = 1e-2
FILTER_FAST_PCT = 0.3
FILTER_SLOW_PCT = 1.5
GLA_HEADS = 8
GLA_DK = 64
GLA_DV = 128
GLA_RANK = 16
GLA_GATE_TAU = 16.0
GLA_CHUNK = 64
GRID_W = 64
DFT_B = 128


def _params(*sem):
    return pltpu.CompilerParams(dimension_semantics=sem, vmem_limit_bytes=VMEM_LIMIT)


def _silu(x):
    return x * jax.nn.sigmoid(x)


def _adaln_kernel(c_ref, w_ref, b_ref, o_ref):
    s = _silu(c_ref[...]).astype(BF16)
    o_ref[...] = jnp.dot(s, w_ref[...].astype(BF16), preferred_element_type=F32) + b_ref[...]


def _adaln(cc, w_mod, b_mod, layer):
    depth, d, n = w_mod.shape
    tn = 1024
    return pl.pallas_call(
        _adaln_kernel,
        grid=(n // tn,),
        in_specs=[pl.BlockSpec((SUBLANES, d), lambda j: (0, 0)),
                  pl.BlockSpec((None, d, tn), lambda j: (layer, 0, j)),
                  pl.BlockSpec((None, 1, tn), lambda j: (layer, 0, j))],
        out_specs=pl.BlockSpec((SUBLANES, tn), lambda j: (0, j)),
        out_shape=jax.ShapeDtypeStruct((SUBLANES, n), F32),
        compiler_params=_params("arbitrary"),
        name="adaln",
    )(cc, w_mod, b_mod.reshape(depth, 1, n))


def _norm_mod(x, g, shift, scale):
    ms = jnp.mean(x * x, axis=-1, keepdims=True)
    return (x * lax.rsqrt(ms + NORM_EPS) * g) * (1.0 + scale) + shift


def _norm_mod_kernel(x_ref, g_ref, sh_ref, sc_ref, o_ref):
    o_ref[...] = _norm_mod(x_ref[...], g_ref[...], sh_ref[...], sc_ref[...]).astype(o_ref.dtype)


def _norm_mod_call(x, g, shift, scale):
    m, d = x.shape
    tm = min(m, 512)
    vec = pl.BlockSpec((1, d), lambda i: (0, 0))
    return pl.pallas_call(
        _norm_mod_kernel,
        grid=(m // tm,),
        in_specs=[pl.BlockSpec((tm, d), lambda i: (i, 0)), vec, vec, vec],
        out_specs=pl.BlockSpec((tm, d), lambda i: (i, 0)),
        out_shape=jax.ShapeDtypeStruct((m, d), BF16),
        compiler_params=_params("parallel"),
        name="norm_mod",
    )(x, g, shift, scale)


_IN_TILE = 768


_NT = (((1,), (1,)), ((), ()))


def _in_proj_kernel(h_ref, w_ref, wt_ref, o_ref, wb_ref, *, n_head):
    j = pl.program_id(0)

    @pl.when((pl.program_id(1) == 0) & (j < n_head))
    def _():
        wb_ref[...] = w_ref[...].T.astype(BF16)

    @pl.when((pl.program_id(1) == 0) & (j >= n_head))
    def _():
        wb_ref[...] = wt_ref[...].T.astype(BF16)

    o_ref[...] = jnp.dot(h_ref[...], wb_ref[...], preferred_element_type=F32)


def _in_proj(h, w_in_t, w_tail_t, n_head_cols, layer):
    m, d = h.shape
    tn = _IN_TILE
    n_head = n_head_cols // tn
    n_tail = w_tail_t.shape[1] // tn
    assert n_head * tn == n_head_cols and n_tail * tn == w_tail_t.shape[1]
    tm = min(m, 1024)
    return pl.pallas_call(
        functools.partial(_in_proj_kernel, n_head=n_head),
        grid=(n_head + n_tail, m // tm),
        in_specs=[pl.BlockSpec((tm, d), lambda j, i: (i, 0)),
                  _single((None, tn, d), lambda j, i: (layer, jnp.minimum(j, n_head - 1), 0)),
                  _single((None, tn, d), lambda j, i: (layer, jnp.maximum(j - n_head, 0), 0))],
        out_specs=pl.BlockSpec((tm, tn), lambda j, i: (i, j)),
        out_shape=jax.ShapeDtypeStruct((m, n_head_cols + w_tail_t.shape[1]), F32),
        scratch_shapes=[pltpu.VMEM((d, tn), BF16)],
        compiler_params=_params("arbitrary", "arbitrary"),
        name="in_proj",
    )(h, w_in_t, w_tail_t)


def _gate_rank_kernel(h_ref, w_ref, o_ref):
    o_ref[...] = lax.dot_general(h_ref[...], w_ref[...].astype(BF16), _NT, preferred_element_type=F32)


def _gate_rank_proj(h, w_r_t, layer):
    m, d = h.shape
    tm = min(m, 2048)
    return pl.pallas_call(
        _gate_rank_kernel,
        grid=(m // tm,),
        in_specs=[pl.BlockSpec((tm, d), lambda i: (i, 0)), pl.BlockSpec((None, LANES, d), lambda i: (layer, 0, 0))],
        out_specs=pl.BlockSpec((tm, LANES), lambda i: (i, 0)),
        out_shape=jax.ShapeDtypeStruct((m, LANES), F32),
        compiler_params=_params("parallel"),
        name="gate_rank_proj",
    )(h, w_r_t)


def _hy_pre_kernel(x0_ref, x1_ref, v_ref, x0p_ref, x1p_ref, vp_ref, x0n_ref, x1n_ref, vn_ref,
                   w0_ref, w1_ref, wv_ref, zx_ref):
    i = pl.program_id(0)
    first = i == 0
    last = i == pl.num_programs(0) - 1
    tt = x0_ref.shape[0]
    row = lax.broadcasted_iota(jnp.int32, x0_ref.shape, 0)

    def conv(ref, prev_ref, next_ref, w_ref):
        x = ref[...]
        prev_row = jnp.where(first, 0.0, prev_ref[SUBLANES - 1:SUBLANES, :])
        next_row = jnp.where(last, 0.0, next_ref[0:1, :])
        below = jnp.where(row == 0, prev_row, pltpu.roll(x, 1, 0))
        above = jnp.where(row == tt - 1, next_row, pltpu.roll(x, tt - 1, 0))
        return w_ref[0:1, :] * below + w_ref[1:2, :] * x + w_ref[2:3, :] * above

    z = conv(v_ref, vp_ref, vn_ref, wv_ref) * conv(x1_ref, x1p_ref, x1n_ref, w1_ref)
    zx_ref[...] = _pack_pair(z, conv(x0_ref, x0p_ref, x0n_ref, w0_ref))


def _hy_pre(p, hy_conv, width):
    l = p.shape[0]
    tt = min(l, 1024)
    cb = 512
    ncb = width // cb
    nb8 = l // SUBLANES
    t8 = tt // SUBLANES

    def main(g):
        return pl.BlockSpec((tt, cb), lambda i, j: (i, g * ncb + j))

    def prev(g):
        return pl.BlockSpec((SUBLANES, cb), lambda i, j: (jnp.maximum(i * t8 - 1, 0), g * ncb + j))

    def nxt(g):
        return pl.BlockSpec((SUBLANES, cb), lambda i, j: (jnp.minimum((i + 1) * t8, nb8 - 1), g * ncb + j))

    def wspec(g):
        return pl.BlockSpec((3, cb), lambda i, j: (0, g * ncb + j))

    out = pl.BlockSpec((tt, cb), lambda i, j: (i, j))
    return pl.pallas_call(
        _hy_pre_kernel,
        grid=(l // tt, ncb),
        in_specs=[main(0), main(1), main(2), prev(0), prev(1), prev(2), nxt(0), nxt(1), nxt(2),
                  wspec(0), wspec(1), wspec(2)],
        out_specs=out,
        out_shape=jax.ShapeDtypeStruct((l, width), jnp.uint32),
        compiler_params=_params("parallel", "parallel"),
        name="hy_pre",
    )(p, p, p, p, p, p, p, p, p, hy_conv, hy_conv, hy_conv)


def _filter_hidden(j, seq_len, w1t_ref, w1c_ref, w1s_ref, b1_ref, fr_ref, w2_ref, b2_ref):
    pos_i = jnp.where(j < seq_len, j, 2 * seq_len - j)
    pos = pos_i.astype(F32)
    t = pos / float(max(seq_len - 1, 1))
    lane = lax.broadcasted_iota(jnp.int32, j.shape, 1)
    band = jnp.where(lane < FILTER_BANDS, lane + 1, 0).astype(F32)
    ang = (2.0 * math.pi / seq_len) * pos * band
    fr = fr_ref[...]
    pre = (t[:, 0:1] * w1t_ref[...]
           + jnp.dot(jnp.cos(ang), w1c_ref[...], precision=HIGHEST, preferred_element_type=F32)
           + jnp.dot(jnp.sin(ang), w1s_ref[...], precision=HIGHEST, preferred_element_type=F32)
           + b1_ref[...])
    h = jnp.sin(fr * pre)
    h = jnp.sin(fr * (jnp.dot(h, w2_ref[...], precision=HIGHEST, preferred_element_type=F32) + b2_ref[...]))
    return h, pos_i, t


def _filter_kernel(w1t_ref, w1c_ref, w1s_ref, b1_ref, fr_ref, w2_ref, b2_ref, w3_ref, delta_ref, o_ref, *, seq_len):
    tr = o_ref.shape[0]
    j = pl.program_id(0) * tr + lax.broadcasted_iota(jnp.int32, (tr, LANES), 0)
    h, pos_i, t = _filter_hidden(j, seq_len, w1t_ref, w1c_ref, w1s_ref, b1_ref, fr_ref, w2_ref, b2_ref)
    hh = jnp.dot(h.astype(BF16), w3_ref[...].astype(BF16), preferred_element_type=F32)
    decay = jnp.exp(-t[:, 0:1] * delta_ref[...])
    valid = pos_i[:, 0:1] != seq_len
    o_ref[...] = jnp.where(valid, hh * decay, 0.0)


_FILTER_PACK = LANES // FILTER_BANDS


def _filter_hidden_kernel(w1t_ref, w1c_ref, w1s_ref, b1_ref, fr_ref, w2_ref, b2_ref, o_ref, *, seq_len, n_a):
    tp = o_ref.shape[0]
    shift = n_a.bit_length() - 1

    def position(shape, lanes_per_pos):
        prow = pl.program_id(0) * tp + lax.broadcasted_iota(jnp.int32, shape, 0)
        r = prow * _FILTER_PACK + lax.broadcasted_iota(jnp.int32, shape, 1) // lanes_per_pos
        j = ((r & (n_a - 1)) * DFT_B) + (r >> shift)
        return jnp.where(j < seq_len, j, 2 * seq_len - j).astype(F32)

    pos = position((tp, LANES), FILTER_BANDS)
    band = (lax.broadcasted_iota(jnp.int32, (tp, LANES), 1) % FILTER_BANDS + 1).astype(F32)
    ang = (2.0 * math.pi / seq_len) * pos * band
    t = position(o_ref.shape, FILTER_HIDDEN) / float(max(seq_len - 1, 1))
    fr = fr_ref[...]
    pre = (t * w1t_ref[...]
           + jnp.dot(jnp.cos(ang), w1c_ref[...], precision=HIGHEST, preferred_element_type=F32)
           + jnp.dot(jnp.sin(ang), w1s_ref[...], precision=HIGHEST, preferred_element_type=F32)
           + b1_ref[...])
    h = jnp.sin(fr * pre)
    o_ref[...] = jnp.sin(fr * (jnp.dot(h, w2_ref[...], precision=HIGHEST, preferred_element_type=F32) + b2_ref[...]))


def _filter_mlp_args(w1, b1, freq, w2, b2):
    hid = FILTER_HIDDEN
    w1c = jnp.zeros((LANES, hid), F32).at[:FILTER_BANDS].set(w1[1:1 + FILTER_BANDS])
    w1s = jnp.zeros((LANES, hid), F32).at[:FILTER_BANDS].set(w1[1 + FILTER_BANDS:1 + 2 * FILTER_BANDS])
    full = lambda shape: pl.BlockSpec(shape, lambda i: (0, 0))
    specs = [full((1, hid)), full((LANES, hid)), full((LANES, hid)), full((1, hid)), full((1, hid)),
             full((hid, hid)), full((1, hid))]
    args = (w1[0:1], w1c, w1s, b1.reshape(1, hid), freq.reshape(1, hid), w2, b2.reshape(1, hid))
    return specs, args


def _filter_delta(width):
    return jnp.abs(jnp.linspace(math.log(FILTER_TARGET) / FILTER_SLOW_PCT,
                                math.log(FILTER_TARGET) / FILTER_FAST_PCT, width, dtype=F32)).reshape(1, width)


def _hyena_filter2(seq_len, w1, b1, freq, w2, b2, w3, width):
    n2 = 2 * seq_len
    tr = min(seq_len, 1024)
    nfwd = seq_len // tr
    specs, args = _filter_mlp_args(w1, b1, freq, w2, b2)
    return pl.pallas_call(
        functools.partial(_filter_kernel, seq_len=seq_len),
        grid=(n2 // tr,),
        in_specs=specs + [pl.BlockSpec((FILTER_HIDDEN, width), lambda i: (0, jnp.where(i < nfwd, 0, 1))),
                          pl.BlockSpec((1, width), lambda i: (0, 0))],
        out_specs=pl.BlockSpec((tr, width), lambda i: (i, 0)),
        out_shape=jax.ShapeDtypeStruct((n2, width), F32),
        compiler_params=_params("parallel"),
        name="hyena_filter",
    )(*args, w3, _filter_delta(width))


def _filter_hidden_permuted(seq_len, w1, b1, freq, w2, b2):
    n2 = 2 * seq_len
    n_a = n2 // DFT_B
    hid = FILTER_HIDDEN
    wide = _FILTER_PACK * hid
    rows = n2 // _FILTER_PACK
    tp = min(rows, 256)
    eye = jnp.eye(_FILTER_PACK, dtype=F32)
    tile = lambda v: jnp.tile(v.reshape(1, hid), (1, _FILTER_PACK))
    args = (tile(w1[0]), jnp.kron(eye, w1[1:1 + FILTER_BANDS]),
            jnp.kron(eye, w1[1 + FILTER_BANDS:1 + 2 * FILTER_BANDS]), tile(b1), tile(freq),
            jnp.kron(eye, w2), tile(b2))
    packed = pl.pallas_call(
        functools.partial(_filter_hidden_kernel, seq_len=seq_len, n_a=n_a),
        grid=(rows // tp,),
        in_specs=[pl.BlockSpec(a.shape, lambda i: (0, 0)) for a in args],
        out_specs=pl.BlockSpec((tp, wide), lambda i: (i, 0)),
        out_shape=jax.ShapeDtypeStruct((rows, wide), F32),
        compiler_params=_params("parallel"),
        name="hyena_filter_hidden",
    )(*args)
    return packed.reshape(n2, hid)


def _dft_tables(n_a, a_in):
    n = n_a * DFT_B
    b = np.arange(DFT_B)[:, None, None]
    k1 = np.arange(n_a)[None, :, None]
    a = np.arange(a_in)[None, None, :]
    e = np.exp(-2j * np.pi * ((k1 * (DFT_B * a + b)) % n) / n)
    g = np.concatenate([e.real, e.imag], axis=1)
    return jnp.asarray(g, F32)


def _idft_tables(n_a, a_out):
    n = n_a * DFT_B
    b = np.arange(DFT_B)[:, None, None]
    a = np.arange(a_out)[None, :, None]
    k1 = np.arange(n_a)[None, None, :]
    e = np.exp(2j * np.pi * ((k1 * (DFT_B * a + b)) % n) / n) / n
    h = np.concatenate([e.real, -e.imag], axis=2)
    return jnp.asarray(h, F32)


def _dft128_embed():
    i = np.arange(DFT_B)
    f = np.exp(-2j * np.pi * ((i[:, None] * i[None, :]) % DFT_B) / DFT_B)
    fwd = np.block([[f.real, -f.imag], [f.imag, f.real]])
    inv = np.block([[f.real, f.imag], [-f.imag, f.real]])
    return jnp.asarray(fwd, F32), jnp.asarray(inv, F32)


_PITCH = DFT_B + SUBLANES
_K1_BLOCK = 8
_B_UNROLL = 8


def _stage_rows(k1):
    return pl.ds(pl.multiple_of(k1 * _PITCH, SUBLANES), DFT_B)


def _single(shape, index_map):
    return pl.BlockSpec(shape, index_map, pipeline_mode=pl.Buffered(1))


def _pack_pair(re, im):
    hi = lax.bitcast_convert_type(re.astype(BF16).astype(F32), jnp.uint32)
    lo = lax.bitcast_convert_type(im.astype(BF16).astype(F32), jnp.uint32)
    return hi | (lo >> 16)


def _unpack_pair(w):
    re = lax.bitcast_convert_type(w & jnp.uint32(0xFFFF0000), F32)
    im = lax.bitcast_convert_type(w << 16, F32)
    return re.astype(BF16), im.astype(BF16)


def _spectrum_kernel(h_ref, w3f_ref, w3b_ref, delta_ref, g_ref, wf_ref, o_ref, s0_ref, s1_ref, *, seq_len, n_a):
    s = pl.program_id(1)
    half = n_a // 2
    cw = o_ref.shape[3]

    @pl.when(s == 0)
    def _():
        a_i = lax.broadcasted_iota(jnp.int32, (n_a, cw), 0)

        def body(b, carry):
            base = pl.multiple_of(b * n_a, SUBLANES)
            hf = jnp.dot(h_ref[pl.ds(base, half), :].astype(BF16), w3f_ref[...].astype(BF16),
                         preferred_element_type=F32)
            hb = jnp.dot(h_ref[pl.ds(base + half, half), :].astype(BF16), w3b_ref[...].astype(BF16),
                         preferred_element_type=F32)
            hh = jnp.concatenate([hf, hb], axis=0)
            j = a_i * DFT_B + b
            pos_i = jnp.where(j < seq_len, j, 2 * seq_len - j)
            t = pos_i.astype(F32) / float(max(seq_len - 1, 1))
            k2 = jnp.where(pos_i != seq_len, hh * jnp.exp(-t * delta_ref[...]), 0.0)
            r = jnp.dot(g_ref[b], k2.astype(BF16), preferred_element_type=F32)
            w = _pack_pair(r[:n_a], r[n_a:])
            s0_ref[pl.ds(b, n_a, stride=_PITCH), :] = w[:, :LANES]
            s1_ref[pl.ds(b, n_a, stride=_PITCH), :] = w[:, LANES:]
            return carry

        lax.fori_loop(0, DFT_B, body, 0, unroll=_B_UNROLL)

    @pl.when(s > 0)
    def _():
        k0 = (s - 1) * o_ref.shape[1]
        for i in range(o_ref.shape[1]):
            rows = _stage_rows(k0 + i)
            re, im = _unpack_pair(jnp.concatenate([s0_ref[rows, :], s1_ref[rows, :]], axis=1))
            z = jnp.dot(wf_ref[...], jnp.concatenate([re, im], axis=0), preferred_element_type=F32)
            o_ref[0, i] = z[:DFT_B]
            o_ref[1, i] = z[DFT_B:]


def _filter_spectrum(hperm, w3, seq_len, width, g_k, wf):
    n_a = 2 * seq_len // DFT_B
    kb = min(n_a, _K1_BLOCK)
    cw = 2 * LANES
    ncb = width // cw
    hid = FILTER_HIDDEN
    return pl.pallas_call(
        functools.partial(_spectrum_kernel, seq_len=seq_len, n_a=n_a),
        grid=(ncb, 1 + n_a // kb),
        in_specs=[_single(hperm.shape, lambda ci, s: (0, 0)),
                  pl.BlockSpec((hid, cw), lambda ci, s: (0, ci)),
                  pl.BlockSpec((hid, cw), lambda ci, s: (0, ncb + ci)),
                  pl.BlockSpec((1, cw), lambda ci, s: (0, ci)),
                  _single(g_k.shape, lambda ci, s: (0, 0, 0)),
                  _single(wf.shape, lambda ci, s: (0, 0))],
        out_specs=pl.BlockSpec((2, kb, DFT_B, cw), lambda ci, s: (0, jnp.maximum(s - 1, 0), 0, ci)),
        out_shape=jax.ShapeDtypeStruct((2, n_a, DFT_B, width), F32),
        scratch_shapes=[pltpu.VMEM((n_a * _PITCH, LANES), jnp.uint32), pltpu.VMEM((n_a * _PITCH, LANES), jnp.uint32)],
        compiler_params=_params("parallel", "arbitrary"),
        name="filter_spectrum",
    )(hperm, w3, w3, _filter_delta(width), g_k, wf)


def _hyena_epilogue(y, z, x0, bias, gain):
    yv = (y + z * bias) * x0
    ms = jnp.mean(yv * yv, axis=-1, keepdims=True)
    return yv * lax.rsqrt(ms + NORM_EPS) * gain


def _pack_groups(y, zx, bias, gain):
    z, x0 = _unpack_pair(zx)
    out = _hyena_epilogue_groups(y, z.astype(F32), x0.astype(F32), bias, gain)
    return _pack_pair(out[:, :LANES], out[:, LANES:])


def _hyena_epilogue_groups(y, z, x0, bias, gain):
    return jnp.concatenate([_hyena_epilogue(y[:, g * LANES:(g + 1) * LANES], z[:, g * LANES:(g + 1) * LANES],
                                            x0[:, g * LANES:(g + 1) * LANES], bias[:, g * LANES:(g + 1) * LANES],
                                            gain[:, g * LANES:(g + 1) * LANES])
                            for g in range(y.shape[1] // LANES)], axis=1)


def _long_conv_kernel(zx0_ref, zx1_ref, kf_ref, g_ref, h_ref, wf_ref, wi_ref, bias_ref, gain_ref, o_ref,
                      s0_ref, s1_ref, *, n_a):
    s = pl.program_id(1)
    half = n_a // 2
    kb = kf_ref.shape[1]
    nkb = n_a // kb

    def load_zx(b):
        return jnp.concatenate([zx0_ref[pl.ds(b, half, stride=DFT_B), :],
                                zx1_ref[pl.ds(b, half, stride=DFT_B), :]], axis=1)

    def load_stage(rows):
        return _unpack_pair(jnp.concatenate([s0_ref[rows, :], s1_ref[rows, :]], axis=1))

    def store_stage(rows, re, im):
        w = _pack_pair(re, im)
        s0_ref[rows, :] = w[:, :LANES]
        s1_ref[rows, :] = w[:, LANES:]

    @pl.when(s == 0)
    def _():
        def body(b, carry):
            z, _ = _unpack_pair(load_zx(b))
            r = jnp.dot(g_ref[b], z, preferred_element_type=F32)
            store_stage(pl.ds(b, n_a, stride=_PITCH), r[:n_a], r[n_a:])
            return carry

        lax.fori_loop(0, DFT_B, body, 0, unroll=_B_UNROLL)

    @pl.when((s > 0) & (s <= nkb))
    def _():
        k0 = (s - 1) * kb
        for i in range(kb):
            rows = _stage_rows(k0 + i)
            re, im = load_stage(rows)
            zz = jnp.dot(wf_ref[...], jnp.concatenate([re, im], axis=0), preferred_element_type=F32)
            zr, zi = zz[:DFT_B], zz[DFT_B:]
            kr, ki = kf_ref[0, i], kf_ref[1, i]
            y = jnp.concatenate([zr * kr - zi * ki, zr * ki + zi * kr], axis=0).astype(BF16)
            cc = jnp.dot(wi_ref[...], y, preferred_element_type=F32)
            store_stage(rows, cc[:DFT_B], cc[DFT_B:])

    @pl.when(s == nkb + 1)
    def _():
        def body(b, carry):
            re, im = load_stage(pl.ds(b, n_a, stride=_PITCH))
            y = jnp.dot(h_ref[b], jnp.concatenate([re, im], axis=0), preferred_element_type=F32)
            o_ref[pl.ds(b, half, stride=DFT_B), :] = _pack_groups(y, load_zx(b), bias_ref[...], gain_ref[...])
            return carry

        lax.fori_loop(0, DFT_B, body, 0, unroll=_B_UNROLL)


def _long_conv(zx, kf, g_z, h_tab, wf, wi, bias, gain):
    l, c = zx.shape
    n_a = 2 * l // DFT_B
    kb = min(n_a, _K1_BLOCK)
    nkb = n_a // kb
    cw = 2 * LANES
    return pl.pallas_call(
        functools.partial(_long_conv_kernel, n_a=n_a),
        grid=(c // cw, nkb + 2),
        in_specs=[_single((l, LANES), lambda ci, s: (0, 2 * ci)), _single((l, LANES), lambda ci, s: (0, 2 * ci + 1)),
                  pl.BlockSpec((2, kb, DFT_B, cw), lambda ci, s: (0, jnp.clip(s - 1, 0, nkb - 1), 0, ci)),
                  _single(g_z.shape, lambda ci, s: (0, 0, 0)), _single(h_tab.shape, lambda ci, s: (0, 0, 0)),
                  _single(wf.shape, lambda ci, s: (0, 0)), _single(wi.shape, lambda ci, s: (0, 0)),
                  pl.BlockSpec((1, cw), lambda ci, s: (0, ci)), pl.BlockSpec((1, cw), lambda ci, s: (0, ci))],
        out_specs=pl.BlockSpec((l, LANES), lambda ci, s: (0, ci)),
        out_shape=jax.ShapeDtypeStruct((l, c // 2), jnp.uint32),
        scratch_shapes=[pltpu.VMEM((n_a * _PITCH, LANES), jnp.uint32), pltpu.VMEM((n_a * _PITCH, LANES), jnp.uint32)],
        compiler_params=_params("parallel", "arbitrary"),
        name="long_conv",
    )(zx, zx, kf, g_z, h_tab, wf, wi, bias.reshape(1, c), gain.reshape(1, c))


def _hyena_long(zx, filt, bias, gain):
    l, c = zx.shape
    n_a = 2 * l // DFT_B
    wf, wi = _dft128_embed()
    wf, wi = wf.astype(BF16), wi.astype(BF16)
    g_z = _dft_tables(n_a, n_a // 2).astype(BF16)
    g_k = _dft_tables(n_a, n_a).astype(BF16)
    h_tab = _idft_tables(n_a, n_a // 2).astype(BF16)
    hperm = _filter_hidden_permuted(l, filt['hy_w1'], filt['hy_b1'], filt['hy_freq'], filt['hy_w2'], filt['hy_b2'])
    kf = _filter_spectrum(hperm, filt['hy_w3'], l, c, g_k, wf)
    return _long_conv(zx, kf, g_z, h_tab, wf, wi, bias, gain)


def _ctx_conv_kernel(zx_ref, k_ref, fz_ref, fk_ref, hi_ref, bias_ref, gain_ref, o_ref):
    n = k_ref.shape[0]
    zx = zx_ref[...]
    z, _ = _unpack_pair(zx)
    zs = jnp.dot(fz_ref[...], z, preferred_element_type=F32)
    ks = jnp.dot(fk_ref[...], k_ref[...].astype(BF16), preferred_element_type=F32)
    zr, zi, kr, ki = zs[:n], zs[n:], ks[:n], ks[n:]
    y = jnp.concatenate([zr * kr - zi * ki, zr * ki + zi * kr], axis=0).astype(BF16)
    yt = jnp.dot(hi_ref[...], y, preferred_element_type=F32)
    o_ref[...] = _pack_groups(yt, zx, bias_ref[...], gain_ref[...])


def _hyena_short(zx, k2, bias, gain):
    l, c = zx.shape
    n = 2 * l
    k = np.arange(n)[:, None]
    t = np.arange(n)[None, :]
    e = np.exp(-2j * np.pi * ((k * t) % n) / n)
    fk = jnp.asarray(np.concatenate([e.real, e.imag], axis=0), F32).astype(BF16)
    fz = fk[:, :l]
    ei = np.conj(e[:, :l]).T / n
    hi = jnp.asarray(np.concatenate([ei.real, -ei.imag], axis=1), F32).astype(BF16)
    cw = 2 * LANES
    col = lambda rows: pl.BlockSpec((rows, cw), lambda ci: (0, ci))
    full = lambda a: pl.BlockSpec(a.shape, lambda ci: (0, 0))
    return pl.pallas_call(
        _ctx_conv_kernel,
        grid=(c // cw,),
        in_specs=[col(l), col(n), full(fz), full(fk), full(hi), col(1), col(1)],
        out_specs=pl.BlockSpec((l, LANES), lambda ci: (0, ci)),
        out_shape=jax.ShapeDtypeStruct((l, c // 2), jnp.uint32),
        compiler_params=_params("parallel"),
        name="hyena_short",
    )(zx, k2, fz, fk, hi, bias.reshape(1, c), gain.reshape(1, c))


def _split3(x):
    hi = x.astype(BF16)
    r1 = x - hi.astype(F32)
    mid = r1.astype(BF16)
    lo = (r1 - mid.astype(F32)).astype(BF16)
    return hi, mid, lo


def _gla_kernel(*refs, reverse, combine):
    if combine:
        (k_ref, q_ref, vlo_ref, vhi_ref, r_ref, wg_ref, bg_ref, s0_ref, of_ref, gate_ref, gout_ref,
         o_ref, sfin_ref, st_ref, la_ref, qe_ref, upd_ref, dec_ref, ke_ref, kl_ref, q2_ref, sc_ref, ob_ref) = refs
    else:
        (k_ref, q_ref, vlo_ref, vhi_ref, r_ref, wg_ref, bg_ref, s0_ref,
         o_ref, sfin_ref, st_ref, la_ref, qe_ref, upd_ref, dec_ref, ke_ref, kl_ref, q2_ref, sc_ref) = refs
        ob_ref = o_ref
    v_half = (vlo_ref, vhi_ref)
    rows_blk = k_ref.shape[0]
    nc = rows_blk // GLA_CHUNK
    qk = k_ref.shape[1]

    @pl.when(pl.program_id(0) == 0)
    def _():
        st_ref[...] = s0_ref[...]

    ri = lax.broadcasted_iota(jnp.int32, (2 * GLA_CHUNK, GLA_CHUNK), 0) % GLA_CHUNK
    ci = lax.broadcasted_iota(jnp.int32, (2 * GLA_CHUNK, GLA_CHUNK), 1)
    r3 = lax.broadcasted_iota(jnp.int32, (GLA_CHUNK, 3 * GLA_CHUNK), 0)
    c3 = lax.broadcasted_iota(jnp.int32, (GLA_CHUNK, 3 * GLA_CHUNK), 1) % GLA_CHUNK
    if reverse:
        tri3 = (c3 >= r3).astype(BF16)
        keep = ci > ri
    else:
        tri3 = (c3 <= r3).astype(BF16)
        keep = ci <= ri
    lane = lax.broadcasted_iota(jnp.int32, (1, LANES), 1)
    first = (lane < GLA_DK).astype(F32)
    srow = lax.broadcasted_iota(jnp.int32, (2 * GLA_DV, LANES), 0)
    scol = lax.broadcasted_iota(jnp.int32, (2 * GLA_DV, LANES), 1)
    own = (srow < GLA_DV) == (scol < GLA_DK)
    nt = (((1,), (1,)), ((), ()))
    tn = (((0,), (0,)), ((), ()))
    npair = GLA_HEADS // 2

    rh, rm, _ = _split3(r_ref[...])
    wh, wm, _ = _split3(wg_ref[...])
    pre = jnp.dot(jnp.concatenate([rh, rh, rm], axis=1), jnp.concatenate([wh, wm, wh], axis=0),
                  preferred_element_type=F32) + bg_ref[...]
    la_ref[...] = jax.nn.log_sigmoid(pre) / GLA_GATE_TAU

    def chunk_rows(c):
        return pl.ds(pl.multiple_of(c * GLA_CHUNK, GLA_CHUNK), GLA_CHUNK)

    def cum_pass(c, carry):
        rows = chunk_rows(c)
        hi, mid, lo = _split3(la_ref[rows, :])
        la_ref[rows, :] = jnp.dot(tri3, jnp.concatenate([hi, mid, lo], axis=0), preferred_element_type=F32)
        return carry

    def decay_pass(c, carry):
        rows = chunk_rows(c)
        cum = la_ref[rows, :]
        tot = cum[0:1] if reverse else cum[GLA_CHUNK - 1:GLA_CHUNK]
        k = k_ref[rows, :]
        qe = q_ref[rows, :] * (GLA_DK ** -0.5) * jnp.exp(cum)
        ke_ref[rows, :] = (k * jnp.exp(-cum)).astype(BF16)
        kl_ref[rows, :] = (k * jnp.exp(tot - cum)).astype(BF16)
        dec_ref[pl.ds(c, 1), :] = jnp.exp(tot)
        qe_ref[rows, :] = qe.astype(BF16)
        for p in range(npair):
            pr = slice(p * LANES, (p + 1) * LANES)
            q2_ref[c * npair + p] = jnp.concatenate([qe[:, pr] * first, qe[:, pr] * (1.0 - first)],
                                                    axis=0).astype(BF16)
        return carry

    def score_pass(c, carry):
        rows = chunk_rows(c)
        for p in range(npair):
            pr = slice(p * LANES, (p + 1) * LANES)
            sc = lax.dot_general(q2_ref[c * npair + p], ke_ref[rows, pr], nt, preferred_element_type=F32)
            sc_ref[c * npair + p] = jnp.where(keep, sc, 0.0).astype(BF16)
        return carry

    def value_pass(c, carry):
        rows = chunk_rows(c)
        for p in range(npair):
            pr = slice(p * LANES, (p + 1) * LANES)
            pv = slice(2 * p * GLA_DV, 2 * (p + 1) * GLA_DV)
            v_p = v_half[p // 2][rows, pl.ds((p % 2) * 2 * GLA_DV, 2 * GLA_DV)].astype(BF16)
            o2 = jnp.dot(sc_ref[c * npair + p], v_p, preferred_element_type=F32)
            ob_ref[rows, pl.ds(2 * p * GLA_DV, GLA_DV)] = o2[:GLA_CHUNK, :GLA_DV]
            ob_ref[rows, pl.ds((2 * p + 1) * GLA_DV, GLA_DV)] = o2[GLA_CHUNK:, GLA_DV:]
            u2 = lax.dot_general(v_p, kl_ref[rows, pr], tn, preferred_element_type=F32)
            upd_ref[c * npair + p] = jnp.where(own, u2, 0.0)
        return carry

    lax.fori_loop(0, nc, cum_pass, 0, unroll=4)
    lax.fori_loop(0, nc, decay_pass, 0)
    lax.fori_loop(0, nc, score_pass, 0, unroll=2)
    lax.fori_loop(0, nc, value_pass, 0, unroll=2)

    def carried(idx, carry):
        c = (nc - 1 - idx) if reverse else idx
        rows = pl.ds(pl.multiple_of(c * GLA_CHUNK, GLA_CHUNK), GLA_CHUNK)
        dec = dec_ref[pl.ds(c, 1), :]
        for p in range(npair):
            pr = slice(p * LANES, (p + 1) * LANES)
            pv = slice(2 * p * GLA_DV, 2 * (p + 1) * GLA_DV)
            st = st_ref[p]
            ob_ref[rows, pv] += lax.dot_general(qe_ref[rows, pr], st.astype(BF16), nt, preferred_element_type=F32)
            st_ref[p] = st * dec[:, pr] + upd_ref[c * npair + p]
        return carry

    lax.fori_loop(0, nc, carried, 0)

    if combine:
        for h in range(GLA_HEADS):
            hv = slice(h * GLA_DV, (h + 1) * GLA_DV)
            o = of_ref[:, hv] + ob_ref[:, hv]
            ms = jnp.mean(o * o, axis=-1, keepdims=True)
            o = o * lax.rsqrt(ms + NORM_EPS) * gout_ref[...]
            o_ref[:, hv] = (o * _silu(gate_ref[:, hv])).astype(o_ref.dtype)

    @pl.when(pl.program_id(0) == pl.num_programs(0) - 1)
    def _():
        sfin_ref[...] = st_ref[...]


_GLA_STATE = (GLA_HEADS // 2, 2 * GLA_DV, LANES)

_COL_K, _COL_V, _COL_Q, _COL_G = 24, 28, 36, 40


def _gla_pass(p, p_r, wg_pad, bg, s0, reverse, of=None, gout=None):
    l = p.shape[0]
    rb = min(l, 512)
    nblk = l // rb
    qk = GLA_HEADS * GLA_DK
    vw = GLA_HEADS * GLA_DV
    combine = of is not None
    rmap = (lambda i: nblk - 1 - i) if reverse else (lambda i: i)

    def cols(width, off128):
        return pl.BlockSpec((rb, width), lambda i: (rmap(i), off128 * LANES // width))

    state = pl.BlockSpec(_GLA_STATE, lambda i: (0, 0, 0))
    in_specs = [cols(qk, _COL_K), cols(qk, _COL_Q), cols(vw // 2, _COL_V), cols(vw // 2, _COL_V + vw // 2 // LANES),
                pl.BlockSpec((rb, LANES), lambda i: (rmap(i), 0)),
                pl.BlockSpec((LANES, qk), lambda i: (0, 0)), pl.BlockSpec((1, qk), lambda i: (0, 0)), state]
    args = [p, p, p, p, p_r, wg_pad, bg.reshape(1, qk), s0]
    nc = rb // GLA_CHUNK
    npair = _GLA_STATE[0]
    scratch = [pltpu.VMEM(_GLA_STATE, F32),
               pltpu.VMEM((rb, qk), F32),
               pltpu.VMEM((rb, qk), BF16),
               pltpu.VMEM((nc * npair,) + _GLA_STATE[1:], F32),
               pltpu.VMEM((nc, qk), F32),
               pltpu.VMEM((rb, qk), BF16),
               pltpu.VMEM((rb, qk), BF16),
               pltpu.VMEM((nc * npair, 2 * GLA_CHUNK, LANES), BF16),
               pltpu.VMEM((nc * npair, 2 * GLA_CHUNK, GLA_CHUNK), BF16)]
    if combine:
        in_specs += [pl.BlockSpec((rb, vw), lambda i: (rmap(i), 0)), cols(vw, _COL_G),
                     pl.BlockSpec((1, GLA_DV), lambda i: (0, 0))]
        args += [of, p, gout.reshape(1, GLA_DV)]
        scratch.append(pltpu.VMEM((rb, vw), F32))
    out_dtype = BF16 if combine else F32
    return pl.pallas_call(
        functools.partial(_gla_kernel, reverse=reverse, combine=combine),
        grid=(nblk,),
        in_specs=in_specs,
        out_specs=[pl.BlockSpec((rb, vw), lambda i: (rmap(i), 0)), state],
        out_shape=[jax.ShapeDtypeStruct((l, vw), out_dtype),
                   jax.ShapeDtypeStruct(_GLA_STATE, F32)],
        scratch_shapes=scratch,
        compiler_params=_params("arbitrary"),
        name="gla_bwd" if reverse else "gla_fwd",
    )(*args)


def _out_proj_kernel(hy_ref, o_ref, w_ref, x_ref, ga_ref, g_ref, sh_ref, sc_ref, xo_ref, h_ref):
    q = hy_ref.shape[1]
    hy_a, hy_b = _unpack_pair(hy_ref[...])
    mix = (jnp.dot(hy_a, w_ref[:q, :], preferred_element_type=F32)
           + jnp.dot(hy_b, w_ref[q:2 * q, :], preferred_element_type=F32)
           + jnp.dot(o_ref[...], w_ref[2 * q:, :], preferred_element_type=F32))
    xn = x_ref[...] + ga_ref[...] * mix
    xo_ref[...] = xn
    h_ref[...] = _norm_mod(xn, g_ref[...], sh_ref[...], sc_ref[...]).astype(BF16)


def _out_proj(hy, o, w, x, ga, g, shift, scale, layer):
    m, d = x.shape
    tm = min(m, 256)
    vec = pl.BlockSpec((1, d), lambda i: (0, 0))
    row = lambda width: pl.BlockSpec((tm, width), lambda i: (i, 0))
    return pl.pallas_call(
        _out_proj_kernel,
        grid=(m // tm,),
        in_specs=[row(hy.shape[1]), row(o.shape[1]), pl.BlockSpec((None, w.shape[1], d), lambda i: (layer, 0, 0)),
                  row(d), vec, vec, vec, vec],
        out_specs=[row(d), row(d)],
        out_shape=[jax.ShapeDtypeStruct((m, d), F32), jax.ShapeDtypeStruct((m, d), BF16)],
        compiler_params=_params("parallel"),
        name="out_proj",
    )(hy, o, w, x, ga, g, shift, scale)


_HALO = GRID_W + SUBLANES


def _ffn_up_conv_kernel(x_ref, wa_ref, wu_ref, cw_ref, o_ref, a_ref, u_ref, wb_ref, *, vertical, cols, nm):
    j = pl.program_id(0)
    i = pl.program_id(1)
    tt = x_ref.shape[0]
    cur = i % 2
    prv = 1 - cur

    @pl.when((j == 0) & (i == 0))
    def _():
        a_ref[...] = jnp.zeros_like(a_ref)
        u_ref[...] = jnp.zeros_like(u_ref)

    def conv_previous_tile():
        col = lax.broadcasted_iota(jnp.int32, o_ref.shape, 0) & (cols - 1)
        acc = None
        for dc in (-1, 0, 1):
            part = None
            for dr in ((-1, 0, 1) if vertical else (0,)):
                tap = cw_ref[3 * (dr + 1) + (dc + 1):3 * (dr + 1) + (dc + 1) + 1, :]
                term = tap * a_ref[prv, pl.ds(_HALO + GRID_W * dr + dc, tt), :]
                part = term if part is None else part + term
            if dc == -1:
                part = jnp.where(col == 0, 0.0, part)
            elif dc == 1:
                part = jnp.where(col == cols - 1, 0.0, part)
            acc = part if acc is None else acc + part
        o_ref[...] = (_silu(acc) * u_ref[prv].astype(F32)).astype(o_ref.dtype)

    @pl.when(i == 0)
    def _():
        wb_ref[0] = wa_ref[...].astype(BF16)
        wb_ref[1] = wu_ref[...].astype(BF16)

    @pl.when(i < nm)
    def _():
        x = x_ref[...]
        a_new = jnp.dot(x, wb_ref[0], preferred_element_type=F32)
        u_ref[cur] = jnp.dot(x, wb_ref[1], preferred_element_type=F32).astype(BF16)
        upper = a_ref[prv, pl.ds(_HALO + tt - GRID_W, GRID_W), :]
        a_ref[cur, pl.ds(SUBLANES, GRID_W), :] = jnp.where(i == 0, 0.0, upper)
        a_ref[cur, pl.ds(_HALO, tt), :] = a_new
        a_ref[prv, pl.ds(_HALO + tt, GRID_W), :] = a_new[:GRID_W]
        conv_previous_tile()

    @pl.when(i == nm)
    def _():
        a_ref[prv, pl.ds(_HALO + tt, GRID_W), :] = jnp.zeros((GRID_W, a_ref.shape[2]), F32)
        conv_previous_tile()


def _ffn_up_conv(h, w_up, conv_w, cols, layer):
    m, d = h.shape
    dff = conv_w.shape[-1]
    vertical = m > cols
    assert cols & (cols - 1) == 0 and (cols == GRID_W or not vertical)
    tt = min(m, 1024)
    assert tt % cols == 0
    cb = 512
    ncb = dff // cb
    nm = m // tt
    return pl.pallas_call(
        functools.partial(_ffn_up_conv_kernel, vertical=vertical, cols=cols, nm=nm),
        grid=(ncb, nm + 1),
        in_specs=[pl.BlockSpec((tt, d), lambda j, i: (jnp.minimum(i, nm - 1), 0)),
                  pl.BlockSpec((None, d, cb), lambda j, i: (layer, 0, j)),
                  pl.BlockSpec((None, d, cb), lambda j, i: (layer, 0, ncb + j)),
                  pl.BlockSpec((9, cb), lambda j, i: (0, j))],
        out_specs=pl.BlockSpec((tt, cb), lambda j, i: (jnp.maximum(i - 1, 0), j)),
        out_shape=jax.ShapeDtypeStruct((m, dff), BF16),
        scratch_shapes=[pltpu.VMEM((2, tt + 2 * _HALO, cb), F32), pltpu.VMEM((2, tt, cb), BF16),
                        pltpu.VMEM((2, d, cb), BF16)],
        compiler_params=_params("arbitrary", "arbitrary"),
        name="ffn_up_conv",
    )(h, w_up, w_up, conv_w.reshape(9, dff))


def _ffn_down_kernel(g_ref, w_ref, x_ref, gate_ref, ng_ref, nsh_ref, nsc_ref, o_ref, *h_ref, post):
    xn = x_ref[...] + gate_ref[...] * jnp.dot(g_ref[...], w_ref[...], preferred_element_type=F32)
    if post == 'final':
        ms = jnp.mean(xn * xn, axis=-1, keepdims=True)
        xn = xn * lax.rsqrt(ms + NORM_EPS) * ng_ref[...]
    o_ref[...] = xn
    if post == 'next':
        h_ref[0][...] = _norm_mod(xn, ng_ref[...], nsh_ref[...], nsc_ref[...]).astype(BF16)


def _ffn_down(g, w, x, gate, norm, post, layer):
    m, d = x.shape
    dff = g.shape[1]
    tm = min(m, 256)
    vec = pl.BlockSpec((1, d), lambda i: (0, 0))
    row = pl.BlockSpec((tm, d), lambda i: (i, 0))
    out_specs, out_shape = [row], [jax.ShapeDtypeStruct((m, d), F32)]
    if post == 'next':
        out_specs.append(row)
        out_shape.append(jax.ShapeDtypeStruct((m, d), BF16))
    return pl.pallas_call(
        functools.partial(_ffn_down_kernel, post=post),
        grid=(m // tm,),
        in_specs=[pl.BlockSpec((tm, dff), lambda i: (i, 0)),
                  _single((None, dff, d), lambda i: (layer, 0, 0)),
                  row, vec, vec, vec, vec],
        out_specs=out_specs,
        out_shape=out_shape,
        compiler_params=_params("parallel"),
        name="ffn_down",
    )(g, w, x, gate, *norm)


def _regroup_w_in(w_in_t, hy_proj):
    qk = GLA_HEADS * GLA_DK
    vw = GLA_HEADS * GLA_DV
    head = hy_proj + qk + vw
    r = w_in_t[:, head:head + 2 * GLA_RANK]
    tail = w_in_t[:, head + 2 * GLA_RANK:]
    pad = jnp.zeros((w_in_t.shape[0], LANES - 2 * GLA_RANK, w_in_t.shape[2]), w_in_t.dtype)
    return head, tail, jnp.concatenate([r, pad], axis=1)


def _pad_gate_w(wg, lane0):
    return jnp.zeros((LANES, wg.shape[1]), F32).at[lane0:lane0 + GLA_RANK].set(wg)


def _mixer(p, p_r, lp, s0_f, s0_b, want_output):
    l = p.shape[0]
    width = lp['hy_bias'].shape[0]
    o_f, s_f = _gla_pass(p, p_r, lp['wg_f_pad'], lp['gla_bg_f'], s0_f, reverse=False)
    o, s_b = _gla_pass(p, p_r, lp['wg_b_pad'], lp['gla_bg_b'], s0_b, reverse=True, of=o_f, gout=lp['gla_out_g'])
    if not want_output:
        return None, None, s_f, s_b
    zx = _hy_pre(p, lp['hy_conv'], width)
    if 2 * l // DFT_B >= 2 * SUBLANES:
        hy = _hyena_long(zx, lp, lp['hy_bias'], lp['hy_out_g'])
    else:
        k2 = _hyena_filter2(l, lp['hy_w1'], lp['hy_b1'], lp['hy_freq'], lp['hy_w2'], lp['hy_b2'], lp['hy_w3'], width)
        hy = _hyena_short(zx, k2, lp['hy_bias'], lp['hy_out_g'])
    return hy, o, s_f, s_b


def _order_w_out(w_out, hy_width):
    depth, _, d = w_out.shape
    hy_rows = w_out[:, :hy_width].reshape(depth, hy_width // (2 * LANES), 2, LANES, d)
    hy_rows = jnp.swapaxes(hy_rows, 1, 2).reshape(depth, hy_width, d)
    return jnp.concatenate([hy_rows, w_out[:, hy_width:]], axis=1).astype(BF16)


def _ffn(h2, x, gate, lp, cols, norm, post, layer):
    gact = _ffn_up_conv(h2, lp['ffn_w_up'], lp['ffn_conv'], cols, layer)
    return _ffn_down(gact, lp['ffn_w_down'], x, gate, norm, post, layer)


def kernel(x, c, ctx, c_ctx, w_mod, b_mod, g_attn, w_in, hy_conv, hy_w1, hy_b1, hy_freq, hy_w2, hy_b2, hy_w3,
           hy_bias, hy_out_g, gla_wg_f, gla_bg_f, gla_wg_b, gla_bg_b, gla_out_g, w_out, g_ffn, ffn_w_up,
           ffn_conv, ffn_w_down, g_final):
    depth = w_mod.shape[0]
    batch, seq, d = x.shape
    assert batch == 1 and c.shape[0] == 1 and ctx.shape[0] == 1
    hy_proj = hy_conv.shape[-1]
    xs = x[0]
    cs = ctx[0]
    cc = jnp.zeros((SUBLANES, d), F32).at[0].set(c[0]).at[1].set(c_ctx)
    zero_state = jnp.zeros(_GLA_STATE, F32)
    gfin = g_final.reshape(1, d)
    w_in = jnp.swapaxes(w_in, 1, 2)
    n_head_cols, w_tail, w_rank = _regroup_w_in(w_in, hy_proj)
    w_out_b = _order_w_out(w_out, hy_bias.shape[-1])
    w_down_b = ffn_w_down.astype(BF16)
    mods =[_adaln(cc, w_mod, b_mod, l) for l in range(depth)]

    def attn_norm(l, row):
        return (g_attn[l].reshape(1, d), mods[l][row:row + 1, 0:d], mods[l][row:row + 1, d:2 * d])

    hx = _norm_mod_call(xs, *attn_norm(0, 0))
    hc = _norm_mod_call(cs, *attn_norm(0, 1))
    for l in range(depth):
        last = l == depth - 1
        lp = {
            'hy_conv': hy_conv[l], 'hy_w1': hy_w1[l], 'hy_b1': hy_b1[l], 'hy_freq': hy_freq[l],
            'hy_w2': hy_w2[l], 'hy_b2': hy_b2[l], 'hy_w3': hy_w3[l], 'hy_bias': hy_bias[l],
            'hy_out_g': hy_out_g[l], 'gla_bg_f': gla_bg_f[l], 'gla_bg_b': gla_bg_b[l],
            'gla_out_g': gla_out_g[l],
            'wg_f_pad': _pad_gate_w(gla_wg_f[l], 0), 'wg_b_pad': _pad_gate_w(gla_wg_b[l], GLA_RANK),
            'ffn_w_up': ffn_w_up, 'ffn_conv': ffn_conv[l], 'ffn_w_down': w_down_b,
        }
        mod = mods[l]
        _, _, ga, sf, scf, gf = [mod[0:1, i * d:(i + 1) * d] for i in range(6)]
        _, _, cga, csf, cscf, cgf = [mod[1:2, i * d:(i + 1) * d] for i in range(6)]
        gf_row = g_ffn[l].reshape(1, d)

        pc = _in_proj(hc, w_in, w_tail, n_head_cols, l)
        hy_c, o_c, s_f, s_b = _mixer(pc, _gate_rank_proj(hc, w_rank, l), lp, zero_state, zero_state,
                                     want_output=not last)
        if not last:
            cs, h2c = _out_proj(hy_c, o_c, w_out_b, cs, cga, gf_row, csf, cscf, l)
            cs, hc = _ffn(h2c, cs, cgf, lp, cs.shape[0], attn_norm(l + 1, 1), 'next', l)

        px = _in_proj(hx, w_in, w_tail, n_head_cols, l)
        hy_x, o_x, _, _ = _mixer(px, _gate_rank_proj(hx, w_rank, l), lp, s_f, s_b, want_output=True)
        xs, h2 = _out_proj(hy_x, o_x, w_out_b, xs, ga, gf_row, sf, scf, l)
        if last:
            (xs,) = _ffn(h2, xs, gf, lp, GRID_W, (gfin, gfin, gfin), 'final', l)
        else:
            xs, hx = _ffn(h2, xs, gf, lp, GRID_W, attn_norm(l + 1, 0), 'next', l)
    return xs[None]
```

```python
import functools
import math

import numpy as np
import jax
import jax.numpy as jnp
from jax import lax
from jax.experimental import pallas as pl
from jax.experimental.pallas import tpu as pltpu

F32 = jnp.float32
BF16 = jnp.bfloat16
HIGHEST = lax.Precision.HIGHEST

LANES = 128
SUBLANES = 8
VMEM_BYTES_V7X = 64 * 1024 * 1024
VMEM_LIMIT = VMEM_BYTES_V7X - 12 * 1024 * 1024

NORM_EPS = 1e-6
HYENA_GROUP = 128
FILTER_BANDS = 16
FILTER_HIDDEN = 64
FILTER_TARGET = 1e-2
FILTER_FAST_PCT = 0.3
FILTER_SLOW_PCT = 1.5
GLA_HEADS = 8
GLA_DK = 64
GLA_DV = 128
GLA_RANK = 16
GLA_GATE_TAU = 16.0
GLA_CHUNK = 64
GRID_W = 64
DFT_B = 128


def _params(*sem):
    return pltpu.CompilerParams(dimension_semantics=sem, vmem_limit_bytes=VMEM_LIMIT)


def _silu(x):
    return x * jax.nn.sigmoid(x)


def _adaln_kernel(c_ref, w_ref, b_ref, o_ref):
    s = _silu(c_ref[...]).astype(BF16)
    o_ref[...] = jnp.dot(s, w_ref[...].astype(BF16), preferred_element_type=F32) + b_ref[...]


def _adaln(cc, w_mod, b_mod, layer):
    depth, d, n = w_mod.shape
    tn = 1024
    return pl.pallas_call(
        _adaln_kernel,
        grid=(n // tn,),
        in_specs=[pl.BlockSpec((SUBLANES, d), lambda j: (0, 0)),
                  pl.BlockSpec((None, d, tn), lambda j: (layer, 0, j)),
                  pl.BlockSpec((None, 1, tn), lambda j: (layer, 0, j))],
        out_specs=pl.BlockSpec((SUBLANES, tn), lambda j: (0, j)),
        out_shape=jax.ShapeDtypeStruct((SUBLANES, n), F32),
        compiler_params=_params("arbitrary"),
        name="adaln",
    )(cc, w_mod, b_mod.reshape(depth, 1, n))


def _norm_mod(x, g, shift, scale):
    ms = jnp.mean(x * x, axis=-1, keepdims=True)
    return (x * lax.rsqrt(ms + NORM_EPS) * g) * (1.0 + scale) + shift


def _norm_mod_kernel(x_ref, g_ref, sh_ref, sc_ref, o_ref):
    o_ref[...] = _norm_mod(x_ref[...], g_ref[...], sh_ref[...], sc_ref[...]).astype(o_ref.dtype)


def _norm_mod_call(x, g, shift, scale):
    m, d = x.shape
    tm = min(m, 512)
    vec = pl.BlockSpec((1, d), lambda i: (0, 0))
    return pl.pallas_call(
        _norm_mod_kernel,
        grid=(m // tm,),
        in_specs=[pl.BlockSpec((tm, d), lambda i: (i, 0)), vec, vec, vec],
        out_specs=pl.BlockSpec((tm, d), lambda i: (i, 0)),
        out_shape=jax.ShapeDtypeStruct((m, d), BF16),
        compiler_params=_params("parallel"),
        name="norm_mod",
    )(x, g, shift, scale)


_IN_TILE = 768


_NT = (((1,), (1,)), ((), ()))


def _in_proj_kernel(h_ref, w_ref, wt_ref, o_ref, wb_ref, *, n_head):
    j = pl.program_id(0)

    @pl.when((pl.program_id(1) == 0) & (j < n_head))
    def _():
        wb_ref[...] = w_ref[...].T.astype(BF16)

    @pl.when((pl.program_id(1) == 0) & (j >= n_head))
    def _():
        wb_ref[...] = wt_ref[...].T.astype(BF16)

    o_ref[...] = jnp.dot(h_ref[...], wb_ref[...], preferred_element_type=F32)


def _in_proj(h, w_in_t, w_tail_t, n_head_cols, layer):
    m, d = h.shape
    tn = _IN_TILE
    n_head = n_head_cols // tn
    n_tail = w_tail_t.shape[1] // tn
    assert n_head * tn == n_head_cols and n_tail * tn == w_tail_t.shape[1]
    tm = min(m, 1024)
    return pl.pallas_call(
        functools.partial(_in_proj_kernel, n_head=n_head),
        grid=(n_head + n_tail, m // tm),
        in_specs=[pl.BlockSpec((tm, d), lambda j, i: (i, 0)),
                  _single((None, tn, d), lambda j, i: (layer, jnp.minimum(j, n_head - 1), 0)),
                  _single((None, tn, d), lambda j, i: (layer, jnp.maximum(j - n_head, 0), 0))],
        out_specs=pl.BlockSpec((tm, tn), lambda j, i: (i, j)),
        out_shape=jax.ShapeDtypeStruct((m, n_head_cols + w_tail_t.shape[1]), F32),
        scratch_shapes=[pltpu.VMEM((d, tn), BF16)],
        compiler_params=_params("arbitrary", "arbitrary"),
        name="in_proj",
    )(h, w_in_t, w_tail_t)


def _gate_rank_kernel(h_ref, w_ref, o_ref):
    o_ref[...] = lax.dot_general(h_ref[...], w_ref[...].astype(BF16), _NT, preferred_element_type=F32)


def _gate_rank_proj(h, w_r_t, layer):
    m, d = h.shape
    tm = min(m, 2048)
    return pl.pallas_call(
        _gate_rank_kernel,
        grid=(m // tm,),
        in_specs=[pl.BlockSpec((tm, d), lambda i: (i, 0)), pl.BlockSpec((None, LANES, d), lambda i: (layer, 0, 0))],
        out_specs=pl.BlockSpec((tm, LANES), lambda i: (i, 0)),
        out_shape=jax.ShapeDtypeStruct((m, LANES), F32),
        compiler_params=_params("parallel"),
        name="gate_rank_proj",
    )(h, w_r_t)


def _hy_pre_kernel(x0_ref, x1_ref, v_ref, x0p_ref, x1p_ref, vp_ref, x0n_ref, x1n_ref, vn_ref,
                   w0_ref, w1_ref, wv_ref, zx_ref):
    i = pl.program_id(0)
    first = i == 0
    last = i == pl.num_programs(0) - 1
    tt = x0_ref.shape[0]
    row = lax.broadcasted_iota(jnp.int32, x0_ref.shape, 0)

    def conv(ref, prev_ref, next_ref, w_ref):
        x = ref[...]
        prev_row = jnp.where(first, 0.0, prev_ref[SUBLANES - 1:SUBLANES, :])
        next_row = jnp.where(last, 0.0, next_ref[0:1, :])
        below = jnp.where(row == 0, prev_row, pltpu.roll(x, 1, 0))
        above = jnp.where(row == tt - 1, next_row, pltpu.roll(x, tt - 1, 0))
        return w_ref[0:1, :] * below + w_ref[1:2, :] * x + w_ref[2:3, :] * above

    z = conv(v_ref, vp_ref, vn_ref, wv_ref) * conv(x1_ref, x1p_ref, x1n_ref, w1_ref)
    zx_ref[...] = _pack_pair(z, conv(x0_ref, x0p_ref, x0n_ref, w0_ref))


def _hy_pre(p, hy_conv, width):
    l = p.shape[0]
    tt = min(l, 1024)
    cb = 512
    ncb = width // cb
    nb8 = l // SUBLANES
    t8 = tt // SUBLANES

    def main(g):
        return pl.BlockSpec((tt, cb), lambda i, j: (i, g * ncb + j))

    def prev(g):
        return pl.BlockSpec((SUBLANES, cb), lambda i, j: (jnp.maximum(i * t8 - 1, 0), g * ncb + j))

    def nxt(g):
        return pl.BlockSpec((SUBLANES, cb), lambda i, j: (jnp.minimum((i + 1) * t8, nb8 - 1), g * ncb + j))

    def wspec(g):
        return pl.BlockSpec((3, cb), lambda i, j: (0, g * ncb + j))

    out = pl.BlockSpec((tt, cb), lambda i, j: (i, j))
    return pl.pallas_call(
        _hy_pre_kernel,
        grid=(l // tt, ncb),
        in_specs=[main(0), main(1), main(2), prev(0), prev(1), prev(2), nxt(0), nxt(1), nxt(2),
                  wspec(0), wspec(1), wspec(2)],
        out_specs=out,
        out_shape=jax.ShapeDtypeStruct((l, width), jnp.uint32),
        compiler_params=_params("parallel", "parallel"),
        name="hy_pre",
    )(p, p, p, p, p, p, p, p, p, hy_conv, hy_conv, hy_conv)


def _filter_hidden(j, seq_len, w1t_ref, w1c_ref, w1s_ref, b1_ref, fr_ref, w2_ref, b2_ref):
    pos_i = jnp.where(j < seq_len, j, 2 * seq_len - j)
    pos = pos_i.astype(F32)
    t = pos / float(max(seq_len - 1, 1))
    lane = lax.broadcasted_iota(jnp.int32, j.shape, 1)
    band = jnp.where(lane < FILTER_BANDS, lane + 1, 0).astype(F32)
    ang = (2.0 * math.pi / seq_len) * pos * band
    fr = fr_ref[...]
    pre = (t[:, 0:1] * w1t_ref[...]
           + jnp.dot(jnp.cos(ang), w1c_ref[...], precision=HIGHEST, preferred_element_type=F32)
           + jnp.dot(jnp.sin(ang), w1s_ref[...], precision=HIGHEST, preferred_element_type=F32)
           + b1_ref[...])
    h = jnp.sin(fr * pre)
    h = jnp.sin(fr * (jnp.dot(h, w2_ref[...], precision=HIGHEST, preferred_element_type=F32) + b2_ref[...]))
    return h, pos_i, t


def _filter_kernel(w1t_ref, w1c_ref, w1s_ref, b1_ref, fr_ref, w2_ref, b2_ref, w3_ref, delta_ref, o_ref, *, seq_len):
    tr = o_ref.shape[0]
    j = pl.program_id(0) * tr + lax.broadcasted_iota(jnp.int32, (tr, LANES), 0)
    h, pos_i, t = _filter_hidden(j, seq_len, w1t_ref, w1c_ref, w1s_ref, b1_ref, fr_ref, w2_ref, b2_ref)
    hh = jnp.dot(h.astype(BF16), w3_ref[...].astype(BF16), preferred_element_type=F32)
    decay = jnp.exp(-t[:, 0:1] * delta_ref[...])
    valid = pos_i[:, 0:1] != seq_len
    o_ref[...] = jnp.where(valid, hh * decay, 0.0)


_FILTER_PACK = LANES // FILTER_BANDS


def _filter_hidden_kernel(w1t_ref, w1c_ref, w1s_ref, b1_ref, fr_ref, w2_ref, b2_ref, o_ref, *, seq_len, n_a):
    tp = o_ref.shape[0]
    shift = n_a.bit_length() - 1

    def position(shape, lanes_per_pos):
        prow = pl.program_id(0) * tp + lax.broadcasted_iota(jnp.int32, shape, 0)
        r = prow * _FILTER_PACK + lax.broadcasted_iota(jnp.int32, shape, 1) // lanes_per_pos
        j = ((r & (n_a - 1)) * DFT_B) + (r >> shift)
        return jnp.where(j < seq_len, j, 2 * seq_len - j).astype(F32)

    pos = position((tp, LANES), FILTER_BANDS)
    band = (lax.broadcasted_iota(jnp.int32, (tp, LANES), 1) % FILTER_BANDS + 1).astype(F32)
    ang = (2.0 * math.pi / seq_len) * pos * band
    t = position(o_ref.shape, FILTER_HIDDEN) / float(max(seq_len - 1, 1))
    fr = fr_ref[...]
    pre = (t * w1t_ref[...]
           + jnp.dot(jnp.cos(ang), w1c_ref[...], precision=HIGHEST, preferred_element_type=F32)
           + jnp.dot(jnp.sin(ang), w1s_ref[...], precision=HIGHEST, preferred_element_type=F32)
           + b1_ref[...])
    h = jnp.sin(fr * pre)
    o_ref[...] = jnp.sin(fr * (jnp.dot(h, w2_ref[...], precision=HIGHEST, preferred_element_type=F32) + b2_ref[...]))


def _filter_mlp_args(w1, b1, freq, w2, b2):
    hid = FILTER_HIDDEN
    w1c = jnp.zeros((LANES, hid), F32).at[:FILTER_BANDS].set(w1[1:1 + FILTER_BANDS])
    w1s = jnp.zeros((LANES, hid), F32).at[:FILTER_BANDS].set(w1[1 + FILTER_BANDS:1 + 2 * FILTER_BANDS])
    full = lambda shape: pl.BlockSpec(shape, lambda i: (0, 0))
    specs = [full((1, hid)), full((LANES, hid)), full((LANES, hid)), full((1, hid)), full((1, hid)),
             full((hid, hid)), full((1, hid))]
    args = (w1[0:1], w1c, w1s, b1.reshape(1, hid), freq.reshape(1, hid), w2, b2.reshape(1, hid))
    return specs, args


def _filter_delta(width):
    return jnp.abs(jnp.linspace(math.log(FILTER_TARGET) / FILTER_SLOW_PCT,
                                math.log(FILTER_TARGET) / FILTER_FAST_PCT, width, dtype=F32)).reshape(1, width)


def _hyena_filter2(seq_len, w1, b1, freq, w2, b2, w3, width):
    n2 = 2 * seq_len
    tr = min(seq_len, 1024)
    nfwd = seq_len // tr
    specs, args = _filter_mlp_args(w1, b1, freq, w2, b2)
    return pl.pallas_call(
        functools.partial(_filter_kernel, seq_len=seq_len),
        grid=(n2 // tr,),
        in_specs=specs + [pl.BlockSpec((FILTER_HIDDEN, width), lambda i: (0, jnp.where(i < nfwd, 0, 1))),
                          pl.BlockSpec((1, width), lambda i: (0, 0))],
        out_specs=pl.BlockSpec((tr, width), lambda i: (i, 0)),
        out_shape=jax.ShapeDtypeStruct((n2, width), F32),
        compiler_params=_params("parallel"),
        name="hyena_filter",
    )(*args, w3, _filter_delta(width))


def _filter_hidden_permuted(seq_len, w1, b1, freq, w2, b2):
    n2 = 2 * seq_len
    n_a = n2 // DFT_B
    hid = FILTER_HIDDEN
    wide = _FILTER_PACK * hid
    rows = n2 // _FILTER_PACK
    tp = min(rows, 256)
    eye = jnp.eye(_FILTER_PACK, dtype=F32)
    tile = lambda v: jnp.tile(v.reshape(1, hid), (1, _FILTER_PACK))
    args = (tile(w1[0]), jnp.kron(eye, w1[1:1 + FILTER_BANDS]),
            jnp.kron(eye, w1[1 + FILTER_BANDS:1 + 2 * FILTER_BANDS]), tile(b1), tile(freq),
            jnp.kron(eye, w2), tile(b2))
    packed = pl.pallas_call(
        functools.partial(_filter_hidden_kernel, seq_len=seq_len, n_a=n_a),
        grid=(rows // tp,),
        in_specs=[pl.BlockSpec(a.shape, lambda i: (0, 0)) for a in args],
        out_specs=pl.BlockSpec((tp, wide), lambda i: (i, 0)),
        out_shape=jax.ShapeDtypeStruct((rows, wide), F32),
        compiler_params=_params("parallel"),
        name="hyena_filter_hidden",
    )(*args)
    return packed.reshape(n2, hid)


def _kept_k1(n_a):
    kept = n_a // 2 + 1
    return kept, -(-kept // SUBLANES) * SUBLANES


def _dft_tables(n_a, a_in):
    n = n_a * DFT_B
    kept, padded = _kept_k1(n_a)
    b = np.arange(DFT_B)[:, None, None]
    k1 = np.arange(padded)[None, :, None]
    a = np.arange(a_in)[None, None, :]
    e = np.exp(-2j * np.pi * ((k1 * (DFT_B * a + b)) % n) / n) * (k1 < kept)
    g = np.concatenate([e.real, e.imag], axis=1)
    return jnp.asarray(g, F32)


def _idft_tables(n_a, a_out):
    n = n_a * DFT_B
    kept, padded = _kept_k1(n_a)
    b = np.arange(DFT_B)[:, None, None]
    a = np.arange(a_out)[None, :, None]
    k1 = np.arange(padded)[None, None, :]
    weight = np.where((k1 == 0) | (k1 == n_a // 2), 1.0, 2.0) * (k1 < kept)
    e = np.exp(2j * np.pi * ((k1 * (DFT_B * a + b)) % n) / n) * weight / n
    h = np.concatenate([e.real, -e.imag], axis=2)
    return jnp.asarray(h, F32)


def _dft128_embed():
    i = np.arange(DFT_B)
    f = np.exp(-2j * np.pi * ((i[:, None] * i[None, :]) % DFT_B) / DFT_B)
    fwd = np.block([[f.real, -f.imag], [f.imag, f.real]])
    inv = np.block([[f.real, f.imag], [-f.imag, f.real]])
    return jnp.asarray(fwd, F32), jnp.asarray(inv, F32)


_PITCH = DFT_B + SUBLANES
_K1_BLOCK = 8
_B_UNROLL = 8


def _stage_rows(k1):
    return pl.ds(pl.multiple_of(k1 * _PITCH, SUBLANES), DFT_B)


def _single(shape, index_map):
    return pl.BlockSpec(shape, index_map, pipeline_mode=pl.Buffered(1))


def _pack_pair(re, im):
    hi = lax.bitcast_convert_type(re.astype(BF16).astype(F32), jnp.uint32)
    lo = lax.bitcast_convert_type(im.astype(BF16).astype(F32), jnp.uint32)
    return hi | (lo >> 16)


def _unpack_pair(w):
    re = lax.bitcast_convert_type(w & jnp.uint32(0xFFFF0000), F32)
    im = lax.bitcast_convert_type(w << 16, F32)
    return re.astype(BF16), im.astype(BF16)


def _spectrum_kernel(h_ref, w3f_ref, w3b_ref, delta_ref, g_ref, wf_ref, o_ref, s0_ref, s1_ref, *, seq_len, n_a):
    s = pl.program_id(1)
    half = n_a // 2
    cw = o_ref.shape[3]

    @pl.when(s == 0)
    def _():
        a_i = lax.broadcasted_iota(jnp.int32, (n_a, cw), 0)

        def body(b, carry):
            base = pl.multiple_of(b * n_a, SUBLANES)
            hf = jnp.dot(h_ref[pl.ds(base, half), :].astype(BF16), w3f_ref[...].astype(BF16),
                         preferred_element_type=F32)
            hb = jnp.dot(h_ref[pl.ds(base + half, half), :].astype(BF16), w3b_ref[...].astype(BF16),
                         preferred_element_type=F32)
            hh = jnp.concatenate([hf, hb], axis=0)
            j = a_i * DFT_B + b
            pos_i = jnp.where(j < seq_len, j, 2 * seq_len - j)
            t = pos_i.astype(F32) / float(max(seq_len - 1, 1))
            k2 = jnp.where(pos_i != seq_len, hh * jnp.exp(-t * delta_ref[...]), 0.0)
            r = jnp.dot(g_ref[b], k2.astype(BF16), preferred_element_type=F32)
            nk = r.shape[0] // 2
            w = _pack_pair(r[:nk], r[nk:])
            s0_ref[pl.ds(b, nk, stride=_PITCH), :] = w[:, :LANES]
            s1_ref[pl.ds(b, nk, stride=_PITCH), :] = w[:, LANES:]
            return carry

        lax.fori_loop(0, DFT_B, body, 0, unroll=_B_UNROLL)

    @pl.when(s > 0)
    def _():
        k0 = (s - 1) * o_ref.shape[1]
        for i in range(o_ref.shape[1]):
            rows = _stage_rows(k0 + i)
            re, im = _unpack_pair(jnp.concatenate([s0_ref[rows, :], s1_ref[rows, :]], axis=1))
            z = jnp.dot(wf_ref[...], jnp.concatenate([re, im], axis=0), preferred_element_type=F32)
            o_ref[0, i] = z[:DFT_B]
            o_ref[1, i] = z[DFT_B:]


def _filter_spectrum(hperm, w3, seq_len, width, g_k, wf):
    n_a = 2 * seq_len // DFT_B
    nk = g_k.shape[1] // 2
    kb = min(nk, _K1_BLOCK)
    cw = 2 * LANES
    ncb = width // cw
    hid = FILTER_HIDDEN
    return pl.pallas_call(
        functools.partial(_spectrum_kernel, seq_len=seq_len, n_a=n_a),
        grid=(ncb, 1 + nk // kb),
        in_specs=[_single(hperm.shape, lambda ci, s: (0, 0)),
                  pl.BlockSpec((hid, cw), lambda ci, s: (0, ci)),
                  pl.BlockSpec((hid, cw), lambda ci, s: (0, ncb + ci)),
                  pl.BlockSpec((1, cw), lambda ci, s: (0, ci)),
                  _single(g_k.shape, lambda ci, s: (0, 0, 0)),
                  _single(wf.shape, lambda ci, s: (0, 0))],
        out_specs=pl.BlockSpec((2, kb, DFT_B, cw), lambda ci, s: (0, jnp.maximum(s - 1, 0), 0, ci)),
        out_shape=jax.ShapeDtypeStruct((2, nk, DFT_B, width), F32),
        scratch_shapes=[pltpu.VMEM((nk * _PITCH, LANES), jnp.uint32), pltpu.VMEM((nk * _PITCH, LANES), jnp.uint32)],
        compiler_params=_params("parallel", "arbitrary"),
        name="filter_spectrum",
    )(hperm, w3, w3, _filter_delta(width), g_k, wf)


def _hyena_epilogue(y, z, x0, bias, gain):
    yv = (y + z * bias) * x0
    ms = jnp.mean(yv * yv, axis=-1, keepdims=True)
    return yv * lax.rsqrt(ms + NORM_EPS) * gain


def _pack_groups(y, zx, bias, gain):
    z, x0 = _unpack_pair(zx)
    out = _hyena_epilogue_groups(y, z.astype(F32), x0.astype(F32), bias, gain)
    return _pack_pair(out[:, :LANES], out[:, LANES:])


def _hyena_epilogue_groups(y, z, x0, bias, gain):
    return jnp.concatenate([_hyena_epilogue(y[:, g * LANES:(g + 1) * LANES], z[:, g * LANES:(g + 1) * LANES],
                                            x0[:, g * LANES:(g + 1) * LANES], bias[:, g * LANES:(g + 1) * LANES],
                                            gain[:, g * LANES:(g + 1) * LANES])
                            for g in range(y.shape[1] // LANES)], axis=1)


def _long_conv_kernel(zx0_ref, zx1_ref, kf_ref, g_ref, h_ref, wf_ref, wi_ref, bias_ref, gain_ref, o_ref,
                      s0_ref, s1_ref, *, n_a):
    s = pl.program_id(1)
    half = n_a // 2
    nk = g_ref.shape[1] // 2
    kb = kf_ref.shape[1]
    nkb = nk // kb

    def load_zx(b):
        return jnp.concatenate([zx0_ref[pl.ds(b, half, stride=DFT_B), :],
                                zx1_ref[pl.ds(b, half, stride=DFT_B), :]], axis=1)

    def load_stage(rows):
        return _unpack_pair(jnp.concatenate([s0_ref[rows, :], s1_ref[rows, :]], axis=1))

    def store_stage(rows, re, im):
        w = _pack_pair(re, im)
        s0_ref[rows, :] = w[:, :LANES]
        s1_ref[rows, :] = w[:, LANES:]

    @pl.when(s == 0)
    def _():
        def body(b, carry):
            z, _ = _unpack_pair(load_zx(b))
            r = jnp.dot(g_ref[b], z, preferred_element_type=F32)
            store_stage(pl.ds(b, nk, stride=_PITCH), r[:nk], r[nk:])
            return carry

        lax.fori_loop(0, DFT_B, body, 0, unroll=_B_UNROLL)

    @pl.when((s > 0) & (s <= nkb))
    def _():
        k0 = (s - 1) * kb
        for i in range(kb):
            rows = _stage_rows(k0 + i)
            re, im = load_stage(rows)
            zz = jnp.dot(wf_ref[...], jnp.concatenate([re, im], axis=0), preferred_element_type=F32)
            zr, zi = zz[:DFT_B], zz[DFT_B:]
            kr, ki = kf_ref[0, i], kf_ref[1, i]
            y = jnp.concatenate([zr * kr - zi * ki, zr * ki + zi * kr], axis=0).astype(BF16)
            cc = jnp.dot(wi_ref[...], y, preferred_element_type=F32)
            store_stage(rows, cc[:DFT_B], cc[DFT_B:])

    @pl.when(s == nkb + 1)
    def _():
        def body(b, carry):
            re, im = load_stage(pl.ds(b, nk, stride=_PITCH))
            y = jnp.dot(h_ref[b], jnp.concatenate([re, im], axis=0), preferred_element_type=F32)
            o_ref[pl.ds(b, half, stride=DFT_B), :] = _pack_groups(y, load_zx(b), bias_ref[...], gain_ref[...])
            return carry

        lax.fori_loop(0, DFT_B, body, 0, unroll=_B_UNROLL)


def _long_conv(zx, kf, g_z, h_tab, wf, wi, bias, gain):
    l, c = zx.shape
    n_a = 2 * l // DFT_B
    nk = g_z.shape[1] // 2
    kb = min(nk, _K1_BLOCK)
    nkb = nk // kb
    cw = 2 * LANES
    return pl.pallas_call(
        functools.partial(_long_conv_kernel, n_a=n_a),
        grid=(c // cw, nkb + 2),
        in_specs=[_single((l, LANES), lambda ci, s: (0, 2 * ci)), _single((l, LANES), lambda ci, s: (0, 2 * ci + 1)),
                  pl.BlockSpec((2, kb, DFT_B, cw), lambda ci, s: (0, jnp.clip(s - 1, 0, nkb - 1), 0, ci)),
                  _single(g_z.shape, lambda ci, s: (0, 0, 0)), _single(h_tab.shape, lambda ci, s: (0, 0, 0)),
                  _single(wf.shape, lambda ci, s: (0, 0)), _single(wi.shape, lambda ci, s: (0, 0)),
                  pl.BlockSpec((1, cw), lambda ci, s: (0, ci)), pl.BlockSpec((1, cw), lambda ci, s: (0, ci))],
        out_specs=pl.BlockSpec((l, LANES), lambda ci, s: (0, ci)),
        out_shape=jax.ShapeDtypeStruct((l, c // 2), jnp.uint32),
        scratch_shapes=[pltpu.VMEM((nk * _PITCH, LANES), jnp.uint32), pltpu.VMEM((nk * _PITCH, LANES), jnp.uint32)],
        compiler_params=_params("parallel", "arbitrary"),
        name="long_conv",
    )(zx, zx, kf, g_z, h_tab, wf, wi, bias.reshape(1, c), gain.reshape(1, c))


def _hyena_long(zx, filt, bias, gain):
    l, c = zx.shape
    n_a = 2 * l // DFT_B
    wf, wi = _dft128_embed()
    wf, wi = wf.astype(BF16), wi.astype(BF16)
    g_z = _dft_tables(n_a, n_a // 2).astype(BF16)
    g_k = _dft_tables(n_a, n_a).astype(BF16)
    h_tab = _idft_tables(n_a, n_a // 2).astype(BF16)
    hperm = _filter_hidden_permuted(l, filt['hy_w1'], filt['hy_b1'], filt['hy_freq'], filt['hy_w2'], filt['hy_b2'])
    kf = _filter_spectrum(hperm, filt['hy_w3'], l, c, g_k, wf)
    return _long_conv(zx, kf, g_z, h_tab, wf, wi, bias, gain)


def _ctx_conv_kernel(zx_ref, k_ref, fz_ref, fk_ref, hi_ref, bias_ref, gain_ref, o_ref):
    n = k_ref.shape[0]
    zx = zx_ref[...]
    z, _ = _unpack_pair(zx)
    zs = jnp.dot(fz_ref[...], z, preferred_element_type=F32)
    ks = jnp.dot(fk_ref[...], k_ref[...].astype(BF16), preferred_element_type=F32)
    zr, zi, kr, ki = zs[:n], zs[n:], ks[:n], ks[n:]
    y = jnp.concatenate([zr * kr - zi * ki, zr * ki + zi * kr], axis=0).astype(BF16)
    yt = jnp.dot(hi_ref[...], y, preferred_element_type=F32)
    o_ref[...] = _pack_groups(yt, zx, bias_ref[...], gain_ref[...])


def _hyena_short(zx, k2, bias, gain):
    l, c = zx.shape
    n = 2 * l
    k = np.arange(n)[:, None]
    t = np.arange(n)[None, :]
    e = np.exp(-2j * np.pi * ((k * t) % n) / n)
    fk = jnp.asarray(np.concatenate([e.real, e.imag], axis=0), F32).astype(BF16)
    fz = fk[:, :l]
    ei = np.conj(e[:, :l]).T / n
    hi = jnp.asarray(np.concatenate([ei.real, -ei.imag], axis=1), F32).astype(BF16)
    cw = 2 * LANES
    col = lambda rows: pl.BlockSpec((rows, cw), lambda ci: (0, ci))
    full = lambda a: pl.BlockSpec(a.shape, lambda ci: (0, 0))
    return pl.pallas_call(
        _ctx_conv_kernel,
        grid=(c // cw,),
        in_specs=[col(l), col(n), full(fz), full(fk), full(hi), col(1), col(1)],
        out_specs=pl.BlockSpec((l, LANES), lambda ci: (0, ci)),
        out_shape=jax.ShapeDtypeStruct((l, c // 2), jnp.uint32),
        compiler_params=_params("parallel"),
        name="hyena_short",
    )(zx, k2, fz, fk, hi, bias.reshape(1, c), gain.reshape(1, c))


def _split3(x):
    hi = x.astype(BF16)
    r1 = x - hi.astype(F32)
    mid = r1.astype(BF16)
    lo = (r1 - mid.astype(F32)).astype(BF16)
    return hi, mid, lo


def _gla_kernel(*refs, reverse, combine):
    if combine:
        (k_ref, q_ref, vlo_ref, vhi_ref, r_ref, wg_ref, bg_ref, s0_ref, of_ref, gate_ref, gout_ref,
         o_ref, sfin_ref, st_ref, la_ref, qe_ref, upd_ref, dec_ref, ke_ref, kl_ref, q2_ref, sc_ref, ob_ref) = refs
    else:
        (k_ref, q_ref, vlo_ref, vhi_ref, r_ref, wg_ref, bg_ref, s0_ref,
         o_ref, sfin_ref, st_ref, la_ref, qe_ref, upd_ref, dec_ref, ke_ref, kl_ref, q2_ref, sc_ref) = refs
        ob_ref = o_ref
    v_half = (vlo_ref, vhi_ref)
    rows_blk = k_ref.shape[0]
    nc = rows_blk // GLA_CHUNK
    qk = k_ref.shape[1]

    @pl.when(pl.program_id(0) == 0)
    def _():
        st_ref[...] = s0_ref[...]

    ri = lax.broadcasted_iota(jnp.int32, (2 * GLA_CHUNK, GLA_CHUNK), 0) % GLA_CHUNK
    ci = lax.broadcasted_iota(jnp.int32, (2 * GLA_CHUNK, GLA_CHUNK), 1)
    r3 = lax.broadcasted_iota(jnp.int32, (GLA_CHUNK, 3 * GLA_CHUNK), 0)
    c3 = lax.broadcasted_iota(jnp.int32, (GLA_CHUNK, 3 * GLA_CHUNK), 1) % GLA_CHUNK
    if reverse:
        tri3 = (c3 >= r3).astype(BF16)
        keep = ci > ri
    else:
        tri3 = (c3 <= r3).astype(BF16)
        keep = ci <= ri
    lane = lax.broadcasted_iota(jnp.int32, (1, LANES), 1)
    first = (lane < GLA_DK).astype(F32)
    srow = lax.broadcasted_iota(jnp.int32, (2 * GLA_DV, LANES), 0)
    scol = lax.broadcasted_iota(jnp.int32, (2 * GLA_DV, LANES), 1)
    own = (srow < GLA_DV) == (scol < GLA_DK)
    nt = (((1,), (1,)), ((), ()))
    tn = (((0,), (0,)), ((), ()))
    npair = GLA_HEADS // 2

    rh, rm, _ = _split3(r_ref[...])
    wh, wm, _ = _split3(wg_ref[...])
    pre = jnp.dot(jnp.concatenate([rh, rh, rm], axis=1), jnp.concatenate([wh, wm, wh], axis=0),
                  preferred_element_type=F32) + bg_ref[...]
    la_ref[...] = jax.nn.log_sigmoid(pre) / GLA_GATE_TAU

    def chunk_rows(c):
        return pl.ds(pl.multiple_of(c * GLA_CHUNK, GLA_CHUNK), GLA_CHUNK)

    def cum_pass(c, carry):
        rows = chunk_rows(c)
        hi, mid, lo = _split3(la_ref[rows, :])
        la_ref[rows, :] = jnp.dot(tri3, jnp.concatenate([hi, mid, lo], axis=0), preferred_element_type=F32)
        return carry

    def decay_pass(c, carry):
        rows = chunk_rows(c)
        cum = la_ref[rows, :]
        tot = cum[0:1] if reverse else cum[GLA_CHUNK - 1:GLA_CHUNK]
        k = k_ref[rows, :]
        qe = q_ref[rows, :] * (GLA_DK ** -0.5) * jnp.exp(cum)
        ke_ref[rows, :] = (k * jnp.exp(-cum)).astype(BF16)
        kl_ref[rows, :] = (k * jnp.exp(tot - cum)).astype(BF16)
        dec_ref[pl.ds(c, 1), :] = jnp.exp(tot)
        qe_ref[rows, :] = qe.astype(BF16)
        for p in range(npair):
            pr = slice(p * LANES, (p + 1) * LANES)
            q2_ref[c * npair + p] = jnp.concatenate([qe[:, pr] * first, qe[:, pr] * (1.0 - first)],
                                                    axis=0).astype(BF16)
        return carry

    def score_pass(c, carry):
        rows = chunk_rows(c)
        for p in range(npair):
            pr = slice(p * LANES, (p + 1) * LANES)
            sc = lax.dot_general(q2_ref[c * npair + p], ke_ref[rows, pr], nt, preferred_element_type=F32)
            sc_ref[c * npair + p] = jnp.where(keep, sc, 0.0).astype(BF16)
        return carry

    def value_pass(c, carry):
        rows = chunk_rows(c)
        for p in range(npair):
            pr = slice(p * LANES, (p + 1) * LANES)
            pv = slice(2 * p * GLA_DV, 2 * (p + 1) * GLA_DV)
            v_p = v_half[p // 2][rows, pl.ds((p % 2) * 2 * GLA_DV, 2 * GLA_DV)].astype(BF16)
            o2 = jnp.dot(sc_ref[c * npair + p], v_p, preferred_element_type=F32)
            ob_ref[rows, pl.ds(2 * p * GLA_DV, GLA_DV)] = o2[:GLA_CHUNK, :GLA_DV]
            ob_ref[rows, pl.ds((2 * p + 1) * GLA_DV, GLA_DV)] = o2[GLA_CHUNK:, GLA_DV:]
            u2 = lax.dot_general(v_p, kl_ref[rows, pr], tn, preferred_element_type=F32)
            upd_ref[c * npair + p] = jnp.where(own, u2, 0.0)
        return carry

    lax.fori_loop(0, nc, cum_pass, 0, unroll=4)
    lax.fori_loop(0, nc, decay_pass, 0)
    lax.fori_loop(0, nc, score_pass, 0, unroll=2)
    lax.fori_loop(0, nc, value_pass, 0, unroll=2)

    def carried(idx, carry):
        c = (nc - 1 - idx) if reverse else idx
        rows = pl.ds(pl.multiple_of(c * GLA_CHUNK, GLA_CHUNK), GLA_CHUNK)
        dec = dec_ref[pl.ds(c, 1), :]
        for p in range(npair):
            pr = slice(p * LANES, (p + 1) * LANES)
            pv = slice(2 * p * GLA_DV, 2 * (p + 1) * GLA_DV)
            st = st_ref[p]
            ob_ref[rows, pv] += lax.dot_general(qe_ref[rows, pr], st.astype(BF16), nt, preferred_element_type=F32)
            st_ref[p] = st * dec[:, pr] + upd_ref[c * npair + p]
        return carry

    lax.fori_loop(0, nc, carried, 0)

    if combine:
        for h in range(GLA_HEADS):
            hv = slice(h * GLA_DV, (h + 1) * GLA_DV)
            o = of_ref[:, hv] + ob_ref[:, hv]
            ms = jnp.mean(o * o, axis=-1, keepdims=True)
            o = o * lax.rsqrt(ms + NORM_EPS) * gout_ref[...]
            o_ref[:, hv] = (o * _silu(gate_ref[:, hv])).astype(o_ref.dtype)

    @pl.when(pl.program_id(0) == pl.num_programs(0) - 1)
    def _():
        sfin_ref[...] = st_ref[...]


_GLA_STATE = (GLA_HEADS // 2, 2 * GLA_DV, LANES)

_COL_K, _COL_V, _COL_Q, _COL_G = 24, 28, 36, 40


def _gla_pass(p, p_r, wg_pad, bg, s0, reverse, of=None, gout=None):
    l = p.shape[0]
    rb = min(l, 512)
    nblk = l // rb
    qk = GLA_HEADS * GLA_DK
    vw = GLA_HEADS * GLA_DV
    combine = of is not None
    rmap = (lambda i: nblk - 1 - i) if reverse else (lambda i: i)

    def cols(width, off128):
        return pl.BlockSpec((rb, width), lambda i: (rmap(i), off128 * LANES // width))

    state = pl.BlockSpec(_GLA_STATE, lambda i: (0, 0, 0))
    in_specs = [cols(qk, _COL_K), cols(qk, _COL_Q), cols(vw // 2, _COL_V), cols(vw // 2, _COL_V + vw // 2 // LANES),
                pl.BlockSpec((rb, LANES), lambda i: (rmap(i), 0)),
                pl.BlockSpec((LANES, qk), lambda i: (0, 0)), pl.BlockSpec((1, qk), lambda i: (0, 0)), state]
    args = [p, p, p, p, p_r, wg_pad, bg.reshape(1, qk), s0]
    nc = rb // GLA_CHUNK
    npair = _GLA_STATE[0]
    scratch = [pltpu.VMEM(_GLA_STATE, F32),
               pltpu.VMEM((rb, qk), F32),
               pltpu.VMEM((rb, qk), BF16),
               pltpu.VMEM((nc * npair,) + _GLA_STATE[1:], F32),
               pltpu.VMEM((nc, qk), F32),
               pltpu.VMEM((rb, qk), BF16),
               pltpu.VMEM((rb, qk), BF16),
               pltpu.VMEM((nc * npair, 2 * GLA_CHUNK, LANES), BF16),
               pltpu.VMEM((nc * npair, 2 * GLA_CHUNK, GLA_CHUNK), BF16)]
    if combine:
        in_specs += [pl.BlockSpec((rb, vw), lambda i: (rmap(i), 0)), cols(vw, _COL_G),
                     pl.BlockSpec((1, GLA_DV), lambda i: (0, 0))]
        args += [of, p, gout.reshape(1, GLA_DV)]
        scratch.append(pltpu.VMEM((rb, vw), F32))
    out_dtype = BF16 if combine else F32
    return pl.pallas_call(
        functools.partial(_gla_kernel, reverse=reverse, combine=combine),
        grid=(nblk,),
        in_specs=in_specs,
        out_specs=[pl.BlockSpec((rb, vw), lambda i: (rmap(i), 0)), state],
        out_shape=[jax.ShapeDtypeStruct((l, vw), out_dtype),
                   jax.ShapeDtypeStruct(_GLA_STATE, F32)],
        scratch_shapes=scratch,
        compiler_params=_params("arbitrary"),
        name="gla_bwd" if reverse else "gla_fwd",
    )(*args)


def _out_proj_kernel(hy_ref, o_ref, w_ref, x_ref, ga_ref, g_ref, sh_ref, sc_ref, xo_ref, h_ref):
    q = hy_ref.shape[1]
    hy_a, hy_b = _unpack_pair(hy_ref[...])
    mix = (jnp.dot(hy_a, w_ref[:q, :], preferred_element_type=F32)
           + jnp.dot(hy_b, w_ref[q:2 * q, :], preferred_element_type=F32)
           + jnp.dot(o_ref[...], w_ref[2 * q:, :], preferred_element_type=F32))
    xn = x_ref[...] + ga_ref[...] * mix
    xo_ref[...] = xn
    h_ref[...] = _norm_mod(xn, g_ref[...], sh_ref[...], sc_ref[...]).astype(BF16)


def _out_proj(hy, o, w, x, ga, g, shift, scale, layer):
    m, d = x.shape
    tm = min(m, 256)
    vec = pl.BlockSpec((1, d), lambda i: (0, 0))
    row = lambda width: pl.BlockSpec((tm, width), lambda i: (i, 0))
    return pl.pallas_call(
        _out_proj_kernel,
        grid=(m // tm,),
        in_specs=[row(hy.shape[1]), row(o.shape[1]), pl.BlockSpec((None, w.shape[1], d), lambda i: (layer, 0, 0)),
                  row(d), vec, vec, vec, vec],
        out_specs=[row(d), row(d)],
        out_shape=[jax.ShapeDtypeStruct((m, d), F32), jax.ShapeDtypeStruct((m, d), BF16)],
        compiler_params=_params("parallel"),
        name="out_proj",
    )(hy, o, w, x, ga, g, shift, scale)


_HALO = GRID_W + SUBLANES


def _ffn_up_conv_kernel(x_ref, wa_ref, wu_ref, cw_ref, o_ref, a_ref, u_ref, wb_ref, *, vertical, cols, nm):
    j = pl.program_id(0)
    i = pl.program_id(1)
    tt = x_ref.shape[0]
    cur = i % 2
    prv = 1 - cur

    @pl.when((j == 0) & (i == 0))
    def _():
        a_ref[...] = jnp.zeros_like(a_ref)
        u_ref[...] = jnp.zeros_like(u_ref)

    def conv_previous_tile():
        col = lax.broadcasted_iota(jnp.int32, o_ref.shape, 0) & (cols - 1)
        acc = None
        for dc in (-1, 0, 1):
            part = None
            for dr in ((-1, 0, 1) if vertical else (0,)):
                tap = cw_ref[3 * (dr + 1) + (dc + 1):3 * (dr + 1) + (dc + 1) + 1, :]
                term = tap * a_ref[prv, pl.ds(_HALO + GRID_W * dr + dc, tt), :]
                part = term if part is None else part + term
            if dc == -1:
                part = jnp.where(col == 0, 0.0, part)
            elif dc == 1:
                part = jnp.where(col == cols - 1, 0.0, part)
            acc = part if acc is None else acc + part
        o_ref[...] = (_silu(acc) * u_ref[prv].astype(F32)).astype(o_ref.dtype)

    @pl.when(i == 0)
    def _():
        wb_ref[0] = wa_ref[...].astype(BF16)
        wb_ref[1] = wu_ref[...].astype(BF16)

    @pl.when(i < nm)
    def _():
        x = x_ref[...]
        a_new = jnp.dot(x, wb_ref[0], preferred_element_type=F32)
        u_ref[cur] = jnp.dot(x, wb_ref[1], preferred_element_type=F32).astype(BF16)
        upper = a_ref[prv, pl.ds(_HALO + tt - GRID_W, GRID_W), :]
        a_ref[cur, pl.ds(SUBLANES, GRID_W), :] = jnp.where(i == 0, 0.0, upper)
        a_ref[cur, pl.ds(_HALO, tt), :] = a_new
        a_ref[prv, pl.ds(_HALO + tt, GRID_W), :] = a_new[:GRID_W]
        conv_previous_tile()

    @pl.when(i == nm)
    def _():
        a_ref[prv, pl.ds(_HALO + tt, GRID_W), :] = jnp.zeros((GRID_W, a_ref.shape[2]), F32)
        conv_previous_tile()


def _ffn_up_conv(h, w_up, conv_w, cols, layer):
    m, d = h.shape
    dff = conv_w.shape[-1]
    vertical = m > cols
    assert cols & (cols - 1) == 0 and (cols == GRID_W or not vertical)
    tt = min(m, 1024)
    assert tt % cols == 0
    cb = 512
    ncb = dff // cb
    nm = m // tt
    return pl.pallas_call(
        functools.partial(_ffn_up_conv_kernel, vertical=vertical, cols=cols, nm=nm),
        grid=(ncb, nm + 1),
        in_specs=[pl.BlockSpec((tt, d), lambda j, i: (jnp.minimum(i, nm - 1), 0)),
                  pl.BlockSpec((None, d, cb), lambda j, i: (layer, 0, j)),
                  pl.BlockSpec((None, d, cb), lambda j, i: (layer, 0, ncb + j)),
                  pl.BlockSpec((9, cb), lambda j, i: (0, j))],
        out_specs=pl.BlockSpec((tt, cb), lambda j, i: (jnp.maximum(i - 1, 0), j)),
        out_shape=jax.ShapeDtypeStruct((m, dff), BF16),
        scratch_shapes=[pltpu.VMEM((2, tt + 2 * _HALO, cb), F32), pltpu.VMEM((2, tt, cb), BF16),
                        pltpu.VMEM((2, d, cb), BF16)],
        compiler_params=_params("arbitrary", "arbitrary"),
        name="ffn_up_conv",
    )(h, w_up, w_up, conv_w.reshape(9, dff))


def _ffn_down_kernel(g_ref, w_ref, x_ref, gate_ref, ng_ref, nsh_ref, nsc_ref, o_ref, *h_ref, post):
    xn = x_ref[...] + gate_ref[...] * jnp.dot(g_ref[...], w_ref[...], preferred_element_type=F32)
    if post == 'final':
        ms = jnp.mean(xn * xn, axis=-1, keepdims=True)
        xn = xn * lax.rsqrt(ms + NORM_EPS) * ng_ref[...]
    o_ref[...] = xn
    if post == 'next':
        h_ref[0][...] = _norm_mod(xn, ng_ref[...], nsh_ref[...], nsc_ref[...]).astype(BF16)


def _ffn_down(g, w, x, gate, norm, post, layer):
    m, d = x.shape
    dff = g.shape[1]
    tm = min(m, 256)
    vec = pl.BlockSpec((1, d), lambda i: (0, 0))
    row = pl.BlockSpec((tm, d), lambda i: (i, 0))
    out_specs, out_shape = [row], [jax.ShapeDtypeStruct((m, d), F32)]
    if post == 'next':
        out_specs.append(row)
        out_shape.append(jax.ShapeDtypeStruct((m, d), BF16))
    return pl.pallas_call(
        functools.partial(_ffn_down_kernel, post=post),
        grid=(m // tm,),
        in_specs=[pl.BlockSpec((tm, dff), lambda i: (i, 0)),
                  _single((None, dff, d), lambda i: (layer, 0, 0)),
                  row, vec, vec, vec, vec],
        out_specs=out_specs,
        out_shape=out_shape,
        compiler_params=_params("parallel"),
        name="ffn_down",
    )(g, w, x, gate, *norm)


def _regroup_w_in(w_in_t, hy_proj):
    qk = GLA_HEADS * GLA_DK
    vw = GLA_HEADS * GLA_DV
    head = hy_proj + qk + vw
    r = w_in_t[:, head:head + 2 * GLA_RANK]
    tail = w_in_t[:, head + 2 * GLA_RANK:]
    pad = jnp.zeros((w_in_t.shape[0], LANES - 2 * GLA_RANK, w_in_t.shape[2]), w_in_t.dtype)
    return head, tail, jnp.concatenate([r, pad], axis=1)


def _pad_gate_w(wg, lane0):
    return jnp.zeros((LANES, wg.shape[1]), F32).at[lane0:lane0 + GLA_RANK].set(wg)


def _mixer(p, p_r, lp, s0_f, s0_b, want_output):
    l = p.shape[0]
    width = lp['hy_bias'].shape[0]
    o_f, s_f = _gla_pass(p, p_r, lp['wg_f_pad'], lp['gla_bg_f'], s0_f, reverse=False)
    o, s_b = _gla_pass(p, p_r, lp['wg_b_pad'], lp['gla_bg_b'], s0_b, reverse=True, of=o_f, gout=lp['gla_out_g'])
    if not want_output:
        return None, None, s_f, s_b
    zx = _hy_pre(p, lp['hy_conv'], width)
    if 2 * l // DFT_B >= 2 * SUBLANES:
        hy = _hyena_long(zx, lp, lp['hy_bias'], lp['hy_out_g'])
    else:
        k2 = _hyena_filter2(l, lp['hy_w1'], lp['hy_b1'], lp['hy_freq'], lp['hy_w2'], lp['hy_b2'], lp['hy_w3'], width)
        hy = _hyena_short(zx, k2, lp['hy_bias'], lp['hy_out_g'])
    return hy, o, s_f, s_b


def _order_w_out(w_out, hy_width):
    depth, _, d = w_out.shape
    hy_rows = w_out[:, :hy_width].reshape(depth, hy_width // (2 * LANES), 2, LANES, d)
    hy_rows = jnp.swapaxes(hy_rows, 1, 2).reshape(depth, hy_width, d)
    return jnp.concatenate([hy_rows, w_out[:, hy_width:]], axis=1).astype(BF16)


def _ffn(h2, x, gate, lp, cols, norm, post, layer):
    gact = _ffn_up_conv(h2, lp['ffn_w_up'], lp['ffn_conv'], cols, layer)
    return _ffn_down(gact, lp['ffn_w_down'], x, gate, norm, post, layer)


def kernel(x, c, ctx, c_ctx, w_mod, b_mod, g_attn, w_in, hy_conv, hy_w1, hy_b1, hy_freq, hy_w2, hy_b2, hy_w3,
           hy_bias, hy_out_g, gla_wg_f, gla_bg_f, gla_wg_b, gla_bg_b, gla_out_g, w_out, g_ffn, ffn_w_up,
           ffn_conv, ffn_w_down, g_final):
    depth = w_mod.shape[0]
    batch, seq, d = x.shape
    assert batch == 1 and c.shape[0] == 1 and ctx.shape[0] == 1
    hy_proj = hy_conv.shape[-1]
    xs = x[0]
    cs = ctx[0]
    cc = jnp.zeros((SUBLANES, d), F32).at[0].set(c[0]).at[1].set(c_ctx)
    zero_state = jnp.zeros(_GLA_STATE, F32)
    gfin = g_final.reshape(1, d)
    w_in = jnp.swapaxes(w_in, 1, 2)
    n_head_cols, w_tail, w_rank = _regroup_w_in(w_in, hy_proj)
    w_out_b = _order_w_out(w_out, hy_bias.shape[-1])
    w_down_b = ffn_w_down.astype(BF16)
    mods =[_adaln(cc, w_mod, b_mod, l) for l in range(depth)]

    def attn_norm(l, row):
        return (g_attn[l].reshape(1, d), mods[l][row:row + 1, 0:d], mods[l][row:row + 1, d:2 * d])

    hx = _norm_mod_call(xs, *attn_norm(0, 0))
    hc = _norm_mod_call(cs, *attn_norm(0, 1))
    for l in range(depth):
        last = l == depth - 1
        lp = {
            'hy_conv': hy_conv[l], 'hy_w1': hy_w1[l], 'hy_b1': hy_b1[l], 'hy_freq': hy_freq[l],
            'hy_w2': hy_w2[l], 'hy_b2': hy_b2[l], 'hy_w3': hy_w3[l], 'hy_bias': hy_bias[l],
            'hy_out_g': hy_out_g[l], 'gla_bg_f': gla_bg_f[l], 'gla_bg_b': gla_bg_b[l],
            'gla_out_g': gla_out_g[l],
            'wg_f_pad': _pad_gate_w(gla_wg_f[l], 0), 'wg_b_pad': _pad_gate_w(gla_wg_b[l], GLA_RANK),
            'ffn_w_up': ffn_w_up, 'ffn_conv': ffn_conv[l], 'ffn_w_down': w_down_b,
        }
        mod = mods[l]
        _, _, ga, sf, scf, gf = [mod[0:1, i * d:(i + 1) * d] for i in range(6)]
        _, _, cga, csf, cscf, cgf = [mod[1:2, i * d:(i + 1) * d] for i in range(6)]
        gf_row = g_ffn[l].reshape(1, d)

        pc = _in_proj(hc, w_in, w_tail, n_head_cols, l)
        hy_c, o_c, s_f, s_b = _mixer(pc, _gate_rank_proj(hc, w_rank, l), lp, zero_state, zero_state,
                                     want_output=not last)
        if not last:
            cs, h2c = _out_proj(hy_c, o_c, w_out_b, cs, cga, gf_row, csf, cscf, l)
            cs, hc = _ffn(h2c, cs, cgf, lp, cs.shape[0], attn_norm(l + 1, 1), 'next', l)

        px = _in_proj(hx, w_in, w_tail, n_head_cols, l)
        hy_x, o_x, _, _ = _mixer(px, _gate_rank_proj(hx, w_rank, l), lp, s_f, s_b, want_output=True)
        xs, h2 = _out_proj(hy_x, o_x, w_out_b, xs, ga, gf_row, sf, scf, l)
        if last:
            (xs,) = _ffn(h2, xs, gf, lp, GRID_W, (gfin, gfin, gfin), 'final', l)
        else:
            xs, hx = _ffn(h2, xs, gf, lp, GRID_W, attn_norm(l + 1, 0), 'next', l)
    return xs[None]
```

```python
import functools
import math

import numpy as np
import jax
import jax.numpy as jnp
from jax import lax
from jax.experimental import pallas as pl
from jax.experimental.pallas import tpu as pltpu

F32 = jnp.float32
BF16 = jnp.bfloat16
HIGHEST = lax.Precision.HIGHEST

LANES = 128
SUBLANES = 8
VMEM_BYTES_V7X = 64 * 1024 * 1024
VMEM_LIMIT = VMEM_BYTES_V7X - 12 * 1024 * 1024

NORM_EPS = 1e-6
HYENA_GROUP = 128
FILTER_BANDS = 16
FILTER_HIDDEN = 64
FILTER_TARGET = 1e-2
FILTER_FAST_PCT = 0.3
FILTER_SLOW_PCT = 1.5
GLA_HEADS = 8
GLA_DK = 64
GLA_DV = 128
GLA_RANK = 16
GLA_GATE_TAU = 16.0
GLA_CHUNK = 64
GRID_W = 64
DFT_B = 128


def _params(*sem):
    return pltpu.CompilerParams(dimension_semantics=sem, vmem_limit_bytes=VMEM_LIMIT)


def _silu(x):
    return x * jax.nn.sigmoid(x)


def _adaln_kernel(c_ref, w_ref, b_ref, o_ref):
    s = _silu(c_ref[...]).astype(BF16)
    o_ref[...] = jnp.dot(s, w_ref[...].astype(BF16), preferred_element_type=F32) + b_ref[...]


def _adaln(cc, w_mod, b_mod, layer):
    depth, d, n = w_mod.shape
    tn = 1024
    return pl.pallas_call(
        _adaln_kernel,
        grid=(n // tn,),
        in_specs=[pl.BlockSpec((SUBLANES, d), lambda j: (0, 0)),
                  pl.BlockSpec((None, d, tn), lambda j: (layer, 0, j)),
                  pl.BlockSpec((None, 1, tn), lambda j: (layer, 0, j))],
        out_specs=pl.BlockSpec((SUBLANES, tn), lambda j: (0, j)),
        out_shape=jax.ShapeDtypeStruct((SUBLANES, n), F32),
        compiler_params=_params("arbitrary"),
        name="adaln",
    )(cc, w_mod, b_mod.reshape(depth, 1, n))


def _norm_mod(x, g, shift, scale):
    ms = jnp.mean(x * x, axis=-1, keepdims=True)
    return (x * lax.rsqrt(ms + NORM_EPS) * g) * (1.0 + scale) + shift


def _norm_mod_kernel(x_ref, g_ref, sh_ref, sc_ref, o_ref):
    o_ref[...] = _norm_mod(x_ref[...], g_ref[...], sh_ref[...], sc_ref[...]).astype(o_ref.dtype)


def _norm_mod_call(x, g, shift, scale):
    m, d = x.shape
    tm = min(m, 512)
    vec = pl.BlockSpec((1, d), lambda i: (0, 0))
    return pl.pallas_call(
        _norm_mod_kernel,
        grid=(m // tm,),
        in_specs=[pl.BlockSpec((tm, d), lambda i: (i, 0)), vec, vec, vec],
        out_specs=pl.BlockSpec((tm, d), lambda i: (i, 0)),
        out_shape=jax.ShapeDtypeStruct((m, d), BF16),
        compiler_params=_params("parallel"),
        name="norm_mod",
    )(x, g, shift, scale)


_IN_TILE = 768


_NT = (((1,), (1,)), ((), ()))


def _in_proj_kernel(h_ref, w_ref, wt_ref, o_ref, wb_ref, *, n_head):
    j = pl.program_id(0)

    @pl.when((pl.program_id(1) == 0) & (j < n_head))
    def _():
        wb_ref[...] = w_ref[...].T.astype(BF16)

    @pl.when((pl.program_id(1) == 0) & (j >= n_head))
    def _():
        wb_ref[...] = wt_ref[...].T.astype(BF16)

    o_ref[...] = jnp.dot(h_ref[...], wb_ref[...], preferred_element_type=F32)


def _in_proj(h, w_in_t, w_tail_t, n_head_cols, layer):
    m, d = h.shape
    tn = _IN_TILE
    n_head = n_head_cols // tn
    n_tail = w_tail_t.shape[1] // tn
    assert n_head * tn == n_head_cols and n_tail * tn == w_tail_t.shape[1]
    tm = min(m, 1024)
    return pl.pallas_call(
        functools.partial(_in_proj_kernel, n_head=n_head),
        grid=(n_head + n_tail, m // tm),
        in_specs=[pl.BlockSpec((tm, d), lambda j, i: (i, 0)),
                  _single((None, tn, d), lambda j, i: (layer, jnp.minimum(j, n_head - 1), 0)),
                  _single((None, tn, d), lambda j, i: (layer, jnp.maximum(j - n_head, 0), 0))],
        out_specs=pl.BlockSpec((tm, tn), lambda j, i: (i, j)),
        out_shape=jax.ShapeDtypeStruct((m, n_head_cols + w_tail_t.shape[1]), F32),
        scratch_shapes=[pltpu.VMEM((d, tn), BF16)],
        compiler_params=_params("arbitrary", "arbitrary"),
        name="in_proj",
    )(h, w_in_t, w_tail_t)


def _gate_rank_kernel(h_ref, w_ref, o_ref):
    o_ref[...] = lax.dot_general(h_ref[...], w_ref[...].astype(BF16), _NT, preferred_element_type=F32)


def _gate_rank_proj(h, w_r_t, layer):
    m, d = h.shape
    tm = min(m, 2048)
    return pl.pallas_call(
        _gate_rank_kernel,
        grid=(m // tm,),
        in_specs=[pl.BlockSpec((tm, d), lambda i: (i, 0)), pl.BlockSpec((None, LANES, d), lambda i: (layer, 0, 0))],
        out_specs=pl.BlockSpec((tm, LANES), lambda i: (i, 0)),
        out_shape=jax.ShapeDtypeStruct((m, LANES), F32),
        compiler_params=_params("parallel"),
        name="gate_rank_proj",
    )(h, w_r_t)


def _hy_pre_kernel(x0_ref, x1_ref, v_ref, x0p_ref, x1p_ref, vp_ref, x0n_ref, x1n_ref, vn_ref,
                   w0_ref, w1_ref, wv_ref, zx_ref):
    i = pl.program_id(0)
    first = i == 0
    last = i == pl.num_programs(0) - 1
    tt = x0_ref.shape[0]
    row = lax.broadcasted_iota(jnp.int32, x0_ref.shape, 0)

    def conv(ref, prev_ref, next_ref, w_ref):
        x = ref[...]
        prev_row = jnp.where(first, 0.0, prev_ref[SUBLANES - 1:SUBLANES, :])
        next_row = jnp.where(last, 0.0, next_ref[0:1, :])
        below = jnp.where(row == 0, prev_row, pltpu.roll(x, 1, 0))
        above = jnp.where(row == tt - 1, next_row, pltpu.roll(x, tt - 1, 0))
        return w_ref[0:1, :] * below + w_ref[1:2, :] * x + w_ref[2:3, :] * above

    z = conv(v_ref, vp_ref, vn_ref, wv_ref) * conv(x1_ref, x1p_ref, x1n_ref, w1_ref)
    zx_ref[...] = _pack_pair(z, conv(x0_ref, x0p_ref, x0n_ref, w0_ref))


def _hy_pre(p, hy_conv, width):
    l = p.shape[0]
    tt = min(l, 1024)
    cb = 512
    ncb = width // cb
    nb8 = l // SUBLANES
    t8 = tt // SUBLANES

    def main(g):
        return pl.BlockSpec((tt, cb), lambda i, j: (i, g * ncb + j))

    def prev(g):
        return pl.BlockSpec((SUBLANES, cb), lambda i, j: (jnp.maximum(i * t8 - 1, 0), g * ncb + j))

    def nxt(g):
        return pl.BlockSpec((SUBLANES, cb), lambda i, j: (jnp.minimum((i + 1) * t8, nb8 - 1), g * ncb + j))

    def wspec(g):
        return pl.BlockSpec((3, cb), lambda i, j: (0, g * ncb + j))

    out = pl.BlockSpec((tt, cb), lambda i, j: (i, j))
    return pl.pallas_call(
        _hy_pre_kernel,
        grid=(l // tt, ncb),
        in_specs=[main(0), main(1), main(2), prev(0), prev(1), prev(2), nxt(0), nxt(1), nxt(2),
                  wspec(0), wspec(1), wspec(2)],
        out_specs=out,
        out_shape=jax.ShapeDtypeStruct((l, width), jnp.uint32),
        compiler_params=_params("parallel", "parallel"),
        name="hy_pre",
    )(p, p, p, p, p, p, p, p, p, hy_conv, hy_conv, hy_conv)


def _filter_hidden(j, seq_len, w1t_ref, w1c_ref, w1s_ref, b1_ref, fr_ref, w2_ref, b2_ref):
    pos_i = jnp.where(j < seq_len, j, 2 * seq_len - j)
    pos = pos_i.astype(F32)
    t = pos / float(max(seq_len - 1, 1))
    lane = lax.broadcasted_iota(jnp.int32, j.shape, 1)
    band = jnp.where(lane < FILTER_BANDS, lane + 1, 0).astype(F32)
    ang = (2.0 * math.pi / seq_len) * pos * band
    fr = fr_ref[...]
    pre = (t[:, 0:1] * w1t_ref[...]
           + jnp.dot(jnp.cos(ang), w1c_ref[...], precision=HIGHEST, preferred_element_type=F32)
           + jnp.dot(jnp.sin(ang), w1s_ref[...], precision=HIGHEST, preferred_element_type=F32)
           + b1_ref[...])
    h = jnp.sin(fr * pre)
    h = jnp.sin(fr * (jnp.dot(h, w2_ref[...], precision=HIGHEST, preferred_element_type=F32) + b2_ref[...]))
    return h, pos_i, t


def _filter_kernel(w1t_ref, w1c_ref, w1s_ref, b1_ref, fr_ref, w2_ref, b2_ref, w3_ref, delta_ref, o_ref, *, seq_len):
    tr = o_ref.shape[0]
    j = pl.program_id(0) * tr + lax.broadcasted_iota(jnp.int32, (tr, LANES), 0)
    h, pos_i, t = _filter_hidden(j, seq_len, w1t_ref, w1c_ref, w1s_ref, b1_ref, fr_ref, w2_ref, b2_ref)
    hh = jnp.dot(h.astype(BF16), w3_ref[...].astype(BF16), preferred_element_type=F32)
    decay = jnp.exp(-t[:, 0:1] * delta_ref[...])
    valid = pos_i[:, 0:1] != seq_len
    o_ref[...] = jnp.where(valid, hh * decay, 0.0)


_FILTER_PACK = LANES // FILTER_BANDS


def _filter_hidden_kernel(w1t_ref, w1c_ref, w1s_ref, b1_ref, fr_ref, w2_ref, b2_ref, o_ref, *, seq_len, n_a):
    tp = o_ref.shape[0]
    shift = n_a.bit_length() - 1

    def position(shape, lanes_per_pos):
        prow = pl.program_id(0) * tp + lax.broadcasted_iota(jnp.int32, shape, 0)
        r = prow * _FILTER_PACK + lax.broadcasted_iota(jnp.int32, shape, 1) // lanes_per_pos
        j = ((r & (n_a - 1)) * DFT_B) + (r >> shift)
        return jnp.where(j < seq_len, j, 2 * seq_len - j).astype(F32)

    pos = position((tp, LANES), FILTER_BANDS)
    band = (lax.broadcasted_iota(jnp.int32, (tp, LANES), 1) % FILTER_BANDS + 1).astype(F32)
    ang = (2.0 * math.pi / seq_len) * pos * band
    t = position(o_ref.shape, FILTER_HIDDEN) / float(max(seq_len - 1, 1))
    fr = fr_ref[...]
    pre = (t * w1t_ref[...]
           + jnp.dot(jnp.cos(ang), w1c_ref[...], precision=HIGHEST, preferred_element_type=F32)
           + jnp.dot(jnp.sin(ang), w1s_ref[...], precision=HIGHEST, preferred_element_type=F32)
           + b1_ref[...])
    h = jnp.sin(fr * pre)
    o_ref[...] = jnp.sin(fr * (jnp.dot(h, w2_ref[...], precision=HIGHEST, preferred_element_type=F32) + b2_ref[...]))


def _filter_mlp_args(w1, b1, freq, w2, b2):
    hid = FILTER_HIDDEN
    w1c = jnp.zeros((LANES, hid), F32).at[:FILTER_BANDS].set(w1[1:1 + FILTER_BANDS])
    w1s = jnp.zeros((LANES, hid), F32).at[:FILTER_BANDS].set(w1[1 + FILTER_BANDS:1 + 2 * FILTER_BANDS])
    full = lambda shape: pl.BlockSpec(shape, lambda i: (0, 0))
    specs = [full((1, hid)), full((LANES, hid)), full((LANES, hid)), full((1, hid)), full((1, hid)),
             full((hid, hid)), full((1, hid))]
    args = (w1[0:1], w1c, w1s, b1.reshape(1, hid), freq.reshape(1, hid), w2, b2.reshape(1, hid))
    return specs, args


def _filter_delta(width):
    return jnp.abs(jnp.linspace(math.log(FILTER_TARGET) / FILTER_SLOW_PCT,
                                math.log(FILTER_TARGET) / FILTER_FAST_PCT, width, dtype=F32)).reshape(1, width)


def _hyena_filter2(seq_len, w1, b1, freq, w2, b2, w3, width):
    n2 = 2 * seq_len
    tr = min(seq_len, 1024)
    nfwd = seq_len // tr
    specs, args = _filter_mlp_args(w1, b1, freq, w2, b2)
    return pl.pallas_call(
        functools.partial(_filter_kernel, seq_len=seq_len),
        grid=(n2 // tr,),
        in_specs=specs + [pl.BlockSpec((FILTER_HIDDEN, width), lambda i: (0, jnp.where(i < nfwd, 0, 1))),
                          pl.BlockSpec((1, width), lambda i: (0, 0))],
        out_specs=pl.BlockSpec((tr, width), lambda i: (i, 0)),
        out_shape=jax.ShapeDtypeStruct((n2, width), F32),
        compiler_params=_params("parallel"),
        name="hyena_filter",
    )(*args, w3, _filter_delta(width))


def _filter_hidden_permuted(seq_len, w1, b1, freq, w2, b2):
    n2 = 2 * seq_len
    n_a = n2 // DFT_B
    hid = FILTER_HIDDEN
    wide = _FILTER_PACK * hid
    rows = n2 // _FILTER_PACK
    tp = min(rows, 256)
    eye = jnp.eye(_FILTER_PACK, dtype=F32)
    tile = lambda v: jnp.tile(v.reshape(1, hid), (1, _FILTER_PACK))
    args = (tile(w1[0]), jnp.kron(eye, w1[1:1 + FILTER_BANDS]),
            jnp.kron(eye, w1[1 + FILTER_BANDS:1 + 2 * FILTER_BANDS]), tile(b1), tile(freq),
            jnp.kron(eye, w2), tile(b2))
    packed = pl.pallas_call(
        functools.partial(_filter_hidden_kernel, seq_len=seq_len, n_a=n_a),
        grid=(rows // tp,),
        in_specs=[pl.BlockSpec(a.shape, lambda i: (0, 0)) for a in args],
        out_specs=pl.BlockSpec((tp, wide), lambda i: (i, 0)),
        out_shape=jax.ShapeDtypeStruct((rows, wide), F32),
        compiler_params=_params("parallel"),
        name="hyena_filter_hidden",
    )(*args)
    return packed.reshape(n2, hid)


def _kept_k1(n_a):
    kept = n_a // 2 + 1
    return kept, -(-kept // SUBLANES) * SUBLANES


def _dft_tables(n_a, a_in):
    n = n_a * DFT_B
    kept, padded = _kept_k1(n_a)
    b = np.arange(DFT_B)[:, None, None]
    k1 = np.arange(padded)[None, :, None]
    a = np.arange(a_in)[None, None, :]
    e = np.exp(-2j * np.pi * ((k1 * (DFT_B * a + b)) % n) / n) * (k1 < kept)
    g = np.concatenate([e.real, e.imag], axis=1)
    return jnp.asarray(g, F32)


def _idft_tables(n_a, a_out):
    n = n_a * DFT_B
    kept, padded = _kept_k1(n_a)
    b = np.arange(DFT_B)[:, None, None]
    a = np.arange(a_out)[None, :, None]
    k1 = np.arange(padded)[None, None, :]
    weight = np.where((k1 == 0) | (k1 == n_a // 2), 1.0, 2.0) * (k1 < kept)
    e = np.exp(2j * np.pi * ((k1 * (DFT_B * a + b)) % n) / n) * weight / n
    h = np.concatenate([e.real, -e.imag], axis=2)
    return jnp.asarray(h, F32)


def _dft128_embed():
    i = np.arange(DFT_B)
    f = np.exp(-2j * np.pi * ((i[:, None] * i[None, :]) % DFT_B) / DFT_B)
    fwd = np.block([[f.real, -f.imag], [f.imag, f.real]])
    inv = np.block([[f.real, f.imag], [-f.imag, f.real]])
    return jnp.asarray(fwd, F32), jnp.asarray(inv, F32)


_PITCH = DFT_B + SUBLANES
_K1_BLOCK = 8
_B_UNROLL = 8


def _stage_rows(k1):
    return pl.ds(pl.multiple_of(k1 * _PITCH, SUBLANES), DFT_B)


def _single(shape, index_map):
    return pl.BlockSpec(shape, index_map, pipeline_mode=pl.Buffered(1))


def _pack_pair(re, im):
    hi = lax.bitcast_convert_type(re.astype(BF16).astype(F32), jnp.uint32)
    lo = lax.bitcast_convert_type(im.astype(BF16).astype(F32), jnp.uint32)
    return hi | (lo >> 16)


def _unpack_pair(w):
    re = lax.bitcast_convert_type(w & jnp.uint32(0xFFFF0000), F32)
    im = lax.bitcast_convert_type(w << 16, F32)
    return re.astype(BF16), im.astype(BF16)


def _spectrum_kernel(h_ref, w3f_ref, w3b_ref, delta_ref, g_ref, wf_ref, o_ref, s0_ref, s1_ref, *, seq_len, n_a):
    s = pl.program_id(1)
    half = n_a // 2
    cw = o_ref.shape[3]

    @pl.when(s == 0)
    def _():
        a_i = lax.broadcasted_iota(jnp.int32, (n_a, cw), 0)

        def body(b, carry):
            base = pl.multiple_of(b * n_a, SUBLANES)
            hf = jnp.dot(h_ref[pl.ds(base, half), :].astype(BF16), w3f_ref[...].astype(BF16),
                         preferred_element_type=F32)
            hb = jnp.dot(h_ref[pl.ds(base + half, half), :].astype(BF16), w3b_ref[...].astype(BF16),
                         preferred_element_type=F32)
            hh = jnp.concatenate([hf, hb], axis=0)
            j = a_i * DFT_B + b
            pos_i = jnp.where(j < seq_len, j, 2 * seq_len - j)
            t = pos_i.astype(F32) / float(max(seq_len - 1, 1))
            k2 = jnp.where(pos_i != seq_len, hh * jnp.exp(-t * delta_ref[...]), 0.0)
            r = jnp.dot(g_ref[b], k2.astype(BF16), preferred_element_type=F32)
            nk = r.shape[0] // 2
            w = _pack_pair(r[:nk], r[nk:])
            s0_ref[pl.ds(b, nk, stride=_PITCH), :] = w[:, :LANES]
            s1_ref[pl.ds(b, nk, stride=_PITCH), :] = w[:, LANES:]
            return carry

        lax.fori_loop(0, DFT_B, body, 0, unroll=_B_UNROLL)

    @pl.when(s > 0)
    def _():
        k0 = (s - 1) * o_ref.shape[1]
        for i in range(o_ref.shape[1]):
            rows = _stage_rows(k0 + i)
            re, im = _unpack_pair(jnp.concatenate([s0_ref[rows, :], s1_ref[rows, :]], axis=1))
            z = jnp.dot(wf_ref[...], jnp.concatenate([re, im], axis=0), preferred_element_type=F32)
            o_ref[0, i] = z[:DFT_B]
            o_ref[1, i] = z[DFT_B:]


def _filter_spectrum(hperm, w3, seq_len, width, g_k, wf):
    n_a = 2 * seq_len // DFT_B
    nk = g_k.shape[1] // 2
    kb = min(nk, _K1_BLOCK)
    cw = 2 * LANES
    ncb = width // cw
    hid = FILTER_HIDDEN
    return pl.pallas_call(
        functools.partial(_spectrum_kernel, seq_len=seq_len, n_a=n_a),
        grid=(ncb, 1 + nk // kb),
        in_specs=[_single(hperm.shape, lambda ci, s: (0, 0)),
                  pl.BlockSpec((hid, cw), lambda ci, s: (0, ci)),
                  pl.BlockSpec((hid, cw), lambda ci, s: (0, ncb + ci)),
                  pl.BlockSpec((1, cw), lambda ci, s: (0, ci)),
                  _single(g_k.shape, lambda ci, s: (0, 0, 0)),
                  _single(wf.shape, lambda ci, s: (0, 0))],
        out_specs=pl.BlockSpec((2, kb, DFT_B, cw), lambda ci, s: (0, jnp.maximum(s - 1, 0), 0, ci)),
        out_shape=jax.ShapeDtypeStruct((2, nk, DFT_B, width), F32),
        scratch_shapes=[pltpu.VMEM((nk * _PITCH, LANES), jnp.uint32), pltpu.VMEM((nk * _PITCH, LANES), jnp.uint32)],
        compiler_params=_params("parallel", "arbitrary"),
        name="filter_spectrum",
    )(hperm, w3, w3, _filter_delta(width), g_k, wf)


def _hyena_epilogue(y, z, x0, bias, gain):
    yv = (y + z * bias) * x0
    ms = jnp.mean(yv * yv, axis=-1, keepdims=True)
    return yv * lax.rsqrt(ms + NORM_EPS) * gain


def _pack_groups(y, zx, bias, gain):
    z, x0 = _unpack_pair(zx)
    out = _hyena_epilogue_groups(y, z.astype(F32), x0.astype(F32), bias, gain)
    return _pack_pair(out[:, :LANES], out[:, LANES:])


def _hyena_epilogue_groups(y, z, x0, bias, gain):
    return jnp.concatenate([_hyena_epilogue(y[:, g * LANES:(g + 1) * LANES], z[:, g * LANES:(g + 1) * LANES],
                                            x0[:, g * LANES:(g + 1) * LANES], bias[:, g * LANES:(g + 1) * LANES],
                                            gain[:, g * LANES:(g + 1) * LANES])
                            for g in range(y.shape[1] // LANES)], axis=1)


def _long_conv_kernel(zx0_ref, zx1_ref, kf_ref, g_ref, h_ref, wf_ref, wi_ref, bias_ref, gain_ref, o_ref,
                      s0_ref, s1_ref, *, n_a):
    s = pl.program_id(1)
    half = n_a // 2
    nk = g_ref.shape[1] // 2
    kb = kf_ref.shape[1]
    nkb = nk // kb

    def load_zx(b):
        return jnp.concatenate([zx0_ref[pl.ds(b, half, stride=DFT_B), :],
                                zx1_ref[pl.ds(b, half, stride=DFT_B), :]], axis=1)

    def load_stage(rows):
        return _unpack_pair(jnp.concatenate([s0_ref[rows, :], s1_ref[rows, :]], axis=1))

    def store_stage(rows, re, im):
        w = _pack_pair(re, im)
        s0_ref[rows, :] = w[:, :LANES]
        s1_ref[rows, :] = w[:, LANES:]

    @pl.when(s == 0)
    def _():
        def body(b, carry):
            z, _ = _unpack_pair(load_zx(b))
            r = jnp.dot(g_ref[b], z, preferred_element_type=F32)
            store_stage(pl.ds(b, nk, stride=_PITCH), r[:nk], r[nk:])
            return carry

        lax.fori_loop(0, DFT_B, body, 0, unroll=_B_UNROLL)

    @pl.when((s > 0) & (s <= nkb))
    def _():
        k0 = (s - 1) * kb
        for i in range(kb):
            rows = _stage_rows(k0 + i)
            re, im = load_stage(rows)
            zz = jnp.dot(wf_ref[...], jnp.concatenate([re, im], axis=0), preferred_element_type=F32)
            zr, zi = zz[:DFT_B], zz[DFT_B:]
            kr, ki = kf_ref[0, i], kf_ref[1, i]
            y = jnp.concatenate([zr * kr - zi * ki, zr * ki + zi * kr], axis=0).astype(BF16)
            cc = jnp.dot(wi_ref[...], y, preferred_element_type=F32)
            store_stage(rows, cc[:DFT_B], cc[DFT_B:])

    @pl.when(s == nkb + 1)
    def _():
        def body(b, carry):
            re, im = load_stage(pl.ds(b, nk, stride=_PITCH))
            y = jnp.dot(h_ref[b], jnp.concatenate([re, im], axis=0), preferred_element_type=F32)
            o_ref[pl.ds(b, half, stride=DFT_B), :] = _pack_groups(y, load_zx(b), bias_ref[...], gain_ref[...])
            return carry

        lax.fori_loop(0, DFT_B, body, 0, unroll=_B_UNROLL)


def _long_conv(zx, kf, g_z, h_tab, wf, wi, bias, gain):
    l, c = zx.shape
    n_a = 2 * l // DFT_B
    nk = g_z.shape[1] // 2
    kb = min(nk, _K1_BLOCK)
    nkb = nk // kb
    cw = 2 * LANES
    return pl.pallas_call(
        functools.partial(_long_conv_kernel, n_a=n_a),
        grid=(c // cw, nkb + 2),
        in_specs=[_single((l, LANES), lambda ci, s: (0, 2 * ci)), _single((l, LANES), lambda ci, s: (0, 2 * ci + 1)),
                  pl.BlockSpec((2, kb, DFT_B, cw), lambda ci, s: (0, jnp.clip(s - 1, 0, nkb - 1), 0, ci)),
                  _single(g_z.shape, lambda ci, s: (0, 0, 0)), _single(h_tab.shape, lambda ci, s: (0, 0, 0)),
                  _single(wf.shape, lambda ci, s: (0, 0)), _single(wi.shape, lambda ci, s: (0, 0)),
                  pl.BlockSpec((1, cw), lambda ci, s: (0, ci)), pl.BlockSpec((1, cw), lambda ci, s: (0, ci))],
        out_specs=pl.BlockSpec((l, LANES), lambda ci, s: (0, ci)),
        out_shape=jax.ShapeDtypeStruct((l, c // 2), jnp.uint32),
        scratch_shapes=[pltpu.VMEM((nk * _PITCH, LANES), jnp.uint32), pltpu.VMEM((nk * _PITCH, LANES), jnp.uint32)],
        compiler_params=_params("parallel", "arbitrary"),
        name="long_conv",
    )(zx, zx, kf, g_z, h_tab, wf, wi, bias.reshape(1, c), gain.reshape(1, c))


def _hyena_long(zx, filt, bias, gain):
    l, c = zx.shape
    n_a = 2 * l // DFT_B
    wf, wi = _dft128_embed()
    wf, wi = wf.astype(BF16), wi.astype(BF16)
    g_z = _dft_tables(n_a, n_a // 2).astype(BF16)
    g_k = _dft_tables(n_a, n_a).astype(BF16)
    h_tab = _idft_tables(n_a, n_a // 2).astype(BF16)
    hperm = _filter_hidden_permuted(l, filt['hy_w1'], filt['hy_b1'], filt['hy_freq'], filt['hy_w2'], filt['hy_b2'])
    kf = _filter_spectrum(hperm, filt['hy_w3'], l, c, g_k, wf)
    return _long_conv(zx, kf, g_z, h_tab, wf, wi, bias, gain)


def _ctx_conv_kernel(zx_ref, k_ref, fz_ref, fk_ref, hi_ref, bias_ref, gain_ref, o_ref):
    n = k_ref.shape[0]
    zx = zx_ref[...]
    z, _ = _unpack_pair(zx)
    zs = jnp.dot(fz_ref[...], z, preferred_element_type=F32)
    ks = jnp.dot(fk_ref[...], k_ref[...].astype(BF16), preferred_element_type=F32)
    zr, zi, kr, ki = zs[:n], zs[n:], ks[:n], ks[n:]
    y = jnp.concatenate([zr * kr - zi * ki, zr * ki + zi * kr], axis=0).astype(BF16)
    yt = jnp.dot(hi_ref[...], y, preferred_element_type=F32)
    o_ref[...] = _pack_groups(yt, zx, bias_ref[...], gain_ref[...])


def _hyena_short(zx, k2, bias, gain):
    l, c = zx.shape
    n = 2 * l
    k = np.arange(n)[:, None]
    t = np.arange(n)[None, :]
    e = np.exp(-2j * np.pi * ((k * t) % n) / n)
    fk = jnp.asarray(np.concatenate([e.real, e.imag], axis=0), F32).astype(BF16)
    fz = fk[:, :l]
    ei = np.conj(e[:, :l]).T / n
    hi = jnp.asarray(np.concatenate([ei.real, -ei.imag], axis=1), F32).astype(BF16)
    cw = 2 * LANES
    col = lambda rows: pl.BlockSpec((rows, cw), lambda ci: (0, ci))
    full = lambda a: pl.BlockSpec(a.shape, lambda ci: (0, 0))
    return pl.pallas_call(
        _ctx_conv_kernel,
        grid=(c // cw,),
        in_specs=[col(l), col(n), full(fz), full(fk), full(hi), col(1), col(1)],
        out_specs=pl.BlockSpec((l, LANES), lambda ci: (0, ci)),
        out_shape=jax.ShapeDtypeStruct((l, c // 2), jnp.uint32),
        compiler_params=_params("parallel"),
        name="hyena_short",
    )(zx, k2, fz, fk, hi, bias.reshape(1, c), gain.reshape(1, c))


def _split3(x):
    hi = x.astype(BF16)
    r1 = x - hi.astype(F32)
    mid = r1.astype(BF16)
    lo = (r1 - mid.astype(F32)).astype(BF16)
    return hi, mid, lo


def _gla_kernel(*refs, reverse, combine):
    if combine:
        (k_ref, q_ref, vlo_ref, vhi_ref, r_ref, wg_ref, bg_ref, s0_ref, of_ref, gate_ref, gout_ref,
         o_ref, sfin_ref, st_ref, la_ref, qe_ref, upd_ref, dec_ref, ke_ref, kl_ref, q2_ref, sc_ref, ob_ref) = refs
    else:
        (k_ref, q_ref, vlo_ref, vhi_ref, r_ref, wg_ref, bg_ref, s0_ref,
         o_ref, sfin_ref, st_ref, la_ref, qe_ref, upd_ref, dec_ref, ke_ref, kl_ref, q2_ref, sc_ref) = refs
        ob_ref = o_ref
    v_half = (vlo_ref, vhi_ref)
    rows_blk = k_ref.shape[0]
    nc = rows_blk // GLA_CHUNK
    qk = k_ref.shape[1]

    @pl.when(pl.program_id(0) == 0)
    def _():
        st_ref[...] = s0_ref[...]

    ri = lax.broadcasted_iota(jnp.int32, (2 * GLA_CHUNK, GLA_CHUNK), 0) % GLA_CHUNK
    ci = lax.broadcasted_iota(jnp.int32, (2 * GLA_CHUNK, GLA_CHUNK), 1)
    r3 = lax.broadcasted_iota(jnp.int32, (GLA_CHUNK, 3 * GLA_CHUNK), 0)
    c3 = lax.broadcasted_iota(jnp.int32, (GLA_CHUNK, 3 * GLA_CHUNK), 1) % GLA_CHUNK
    if reverse:
        tri3 = (c3 >= r3).astype(BF16)
        keep = ci > ri
    else:
        tri3 = (c3 <= r3).astype(BF16)
        keep = ci <= ri
    lane = lax.broadcasted_iota(jnp.int32, (1, LANES), 1)
    first = (lane < GLA_DK).astype(F32)
    srow = lax.broadcasted_iota(jnp.int32, (2 * GLA_DV, LANES), 0)
    scol = lax.broadcasted_iota(jnp.int32, (2 * GLA_DV, LANES), 1)
    own = (srow < GLA_DV) == (scol < GLA_DK)
    nt = (((1,), (1,)), ((), ()))
    tn = (((0,), (0,)), ((), ()))
    npair = GLA_HEADS // 2

    rh, rm, _ = _split3(r_ref[...])
    wh, wm, _ = _split3(wg_ref[...])
    pre = jnp.dot(jnp.concatenate([rh, rh, rm], axis=1), jnp.concatenate([wh, wm, wh], axis=0),
                  preferred_element_type=F32) + bg_ref[...]
    la_ref[...] = jax.nn.log_sigmoid(pre) / GLA_GATE_TAU

    def chunk_rows(c):
        return pl.ds(pl.multiple_of(c * GLA_CHUNK, GLA_CHUNK), GLA_CHUNK)

    def cum_pass(c, carry):
        rows = chunk_rows(c)
        hi, mid, lo = _split3(la_ref[rows, :])
        la_ref[rows, :] = jnp.dot(tri3, jnp.concatenate([hi, mid, lo], axis=0), preferred_element_type=F32)
        return carry

    def decay_pass(c, carry):
        rows = chunk_rows(c)
        cum = la_ref[rows, :]
        tot = cum[0:1] if reverse else cum[GLA_CHUNK - 1:GLA_CHUNK]
        k = k_ref[rows, :]
        qe = q_ref[rows, :] * (GLA_DK ** -0.5) * jnp.exp(cum)
        ke_ref[rows, :] = (k * jnp.exp(-cum)).astype(BF16)
        kl_ref[rows, :] = (k * jnp.exp(tot - cum)).astype(BF16)
        dec_ref[pl.ds(c, 1), :] = jnp.exp(tot)
        qe_ref[rows, :] = qe.astype(BF16)
        for p in range(npair):
            pr = slice(p * LANES, (p + 1) * LANES)
            q2_ref[c * npair + p] = jnp.concatenate([qe[:, pr] * first, qe[:, pr] * (1.0 - first)],
                                                    axis=0).astype(BF16)
        return carry

    def score_pass(c, carry):
        rows = chunk_rows(c)
        for p in range(npair):
            pr = slice(p * LANES, (p + 1) * LANES)
            sc = lax.dot_general(q2_ref[c * npair + p], ke_ref[rows, pr], nt, preferred_element_type=F32)
            sc_ref[c * npair + p] = jnp.where(keep, sc, 0.0).astype(BF16)
        return carry

    def value_pass(c, carry):
        rows = chunk_rows(c)
        for p in range(npair):
            pr = slice(p * LANES, (p + 1) * LANES)
            pv = slice(2 * p * GLA_DV, 2 * (p + 1) * GLA_DV)
            v_p = v_half[p // 2][rows, pl.ds((p % 2) * 2 * GLA_DV, 2 * GLA_DV)].astype(BF16)
            o2 = jnp.dot(sc_ref[c * npair + p], v_p, preferred_element_type=F32)
            ob_ref[rows, pl.ds(2 * p * GLA_DV, GLA_DV)] = o2[:GLA_CHUNK, :GLA_DV]
            ob_ref[rows, pl.ds((2 * p + 1) * GLA_DV, GLA_DV)] = o2[GLA_CHUNK:, GLA_DV:]
            u2 = lax.dot_general(v_p, kl_ref[rows, pr], tn, preferred_element_type=F32)
            upd_ref[c * npair + p] = jnp.where(own, u2, 0.0)
        return carry

    lax.fori_loop(0, nc, cum_pass, 0, unroll=4)
    lax.fori_loop(0, nc, decay_pass, 0)
    lax.fori_loop(0, nc, score_pass, 0, unroll=2)
    lax.fori_loop(0, nc, value_pass, 0, unroll=2)

    def carried(idx, carry):
        c = (nc - 1 - idx) if reverse else idx
        rows = pl.ds(pl.multiple_of(c * GLA_CHUNK, GLA_CHUNK), GLA_CHUNK)
        dec = dec_ref[pl.ds(c, 1), :]
        for p in range(npair):
            pr = slice(p * LANES, (p + 1) * LANES)
            pv = slice(2 * p * GLA_DV, 2 * (p + 1) * GLA_DV)
            st = st_ref[p]
            ob_ref[rows, pv] += lax.dot_general(qe_ref[rows, pr], st.astype(BF16), nt, preferred_element_type=F32)
            st_ref[p] = st * dec[:, pr] + upd_ref[c * npair + p]
        return carry

    lax.fori_loop(0, nc, carried, 0)

    if combine:
        for h in range(GLA_HEADS):
            hv = slice(h * GLA_DV, (h + 1) * GLA_DV)
            o = of_ref[:, hv] + ob_ref[:, hv]
            ms = jnp.mean(o * o, axis=-1, keepdims=True)
            o = o * lax.rsqrt(ms + NORM_EPS) * gout_ref[...]
            o_ref[:, hv] = (o * _silu(gate_ref[:, hv])).astype(o_ref.dtype)

    @pl.when(pl.program_id(0) == pl.num_programs(0) - 1)
    def _():
        sfin_ref[...] = st_ref[...]


_GLA_STATE = (GLA_HEADS // 2, 2 * GLA_DV, LANES)

_COL_K, _COL_V, _COL_Q, _COL_G = 24, 28, 36, 40


def _gla_pass(p, p_r, wg_pad, bg, s0, reverse, of=None, gout=None):
    l = p.shape[0]
    rb = min(l, 512)
    nblk = l // rb
    qk = GLA_HEADS * GLA_DK
    vw = GLA_HEADS * GLA_DV
    combine = of is not None
    rmap = (lambda i: nblk - 1 - i) if reverse else (lambda i: i)

    def cols(width, off128):
        return pl.BlockSpec((rb, width), lambda i: (rmap(i), off128 * LANES // width))

    state = pl.BlockSpec(_GLA_STATE, lambda i: (0, 0, 0))
    in_specs = [cols(qk, _COL_K), cols(qk, _COL_Q), cols(vw // 2, _COL_V), cols(vw // 2, _COL_V + vw // 2 // LANES),
                pl.BlockSpec((rb, LANES), lambda i: (rmap(i), 0)),
                pl.BlockSpec((LANES, qk), lambda i: (0, 0)), pl.BlockSpec((1, qk), lambda i: (0, 0)), state]
    args = [p, p, p, p, p_r, wg_pad, bg.reshape(1, qk), s0]
    nc = rb // GLA_CHUNK
    npair = _GLA_STATE[0]
    scratch = [pltpu.VMEM(_GLA_STATE, F32),
               pltpu.VMEM((rb, qk), F32),
               pltpu.VMEM((rb, qk), BF16),
               pltpu.VMEM((nc * npair,) + _GLA_STATE[1:], F32),
               pltpu.VMEM((nc, qk), F32),
               pltpu.VMEM((rb, qk), BF16),
               pltpu.VMEM((rb, qk), BF16),
               pltpu.VMEM((nc * npair, 2 * GLA_CHUNK, LANES), BF16),
               pltpu.VMEM((nc * npair, 2 * GLA_CHUNK, GLA_CHUNK), BF16)]
    if combine:
        in_specs += [pl.BlockSpec((rb, vw), lambda i: (rmap(i), 0)), cols(vw, _COL_G),
                     pl.BlockSpec((1, GLA_DV), lambda i: (0, 0))]
        args += [of, p, gout.reshape(1, GLA_DV)]
        scratch.append(pltpu.VMEM((rb, vw), F32))
    out_dtype = BF16 if combine else F32
    return pl.pallas_call(
        functools.partial(_gla_kernel, reverse=reverse, combine=combine),
        grid=(nblk,),
        in_specs=in_specs,
        out_specs=[pl.BlockSpec((rb, vw), lambda i: (rmap(i), 0)), state],
        out_shape=[jax.ShapeDtypeStruct((l, vw), out_dtype),
                   jax.ShapeDtypeStruct(_GLA_STATE, F32)],
        scratch_shapes=scratch,
        compiler_params=_params("arbitrary"),
        name="gla_bwd" if reverse else "gla_fwd",
    )(*args)


def _out_proj_kernel(hy_ref, o_ref, w_ref, x_ref, ga_ref, g_ref, sh_ref, sc_ref, xo_ref, h_ref):
    q = hy_ref.shape[1]
    hy_a, hy_b = _unpack_pair(hy_ref[...])
    mix = (jnp.dot(hy_a, w_ref[:q, :], preferred_element_type=F32)
           + jnp.dot(hy_b, w_ref[q:2 * q, :], preferred_element_type=F32)
           + jnp.dot(o_ref[...], w_ref[2 * q:, :], preferred_element_type=F32))
    xn = x_ref[...] + ga_ref[...] * mix
    xo_ref[...] = xn
    h_ref[...] = _norm_mod(xn, g_ref[...], sh_ref[...], sc_ref[...]).astype(BF16)


def _out_proj(hy, o, w, x, ga, g, shift, scale, layer):
    m, d = x.shape
    tm = min(m, 256)
    vec = pl.BlockSpec((1, d), lambda i: (0, 0))
    row = lambda width: pl.BlockSpec((tm, width), lambda i: (i, 0))
    return pl.pallas_call(
        _out_proj_kernel,
        grid=(m // tm,),
        in_specs=[row(hy.shape[1]), row(o.shape[1]), pl.BlockSpec((None, w.shape[1], d), lambda i: (layer, 0, 0)),
                  row(d), vec, vec, vec, vec],
        out_specs=[row(d), row(d)],
        out_shape=[jax.ShapeDtypeStruct((m, d), F32), jax.ShapeDtypeStruct((m, d), BF16)],
        compiler_params=_params("parallel"),
        name="out_proj",
    )(hy, o, w, x, ga, g, shift, scale)


_HALO = GRID_W + SUBLANES


def _ffn_up_conv_kernel(x_ref, wa_ref, wu_ref, cw_ref, o_ref, a_ref, u_ref, wb_ref, *, vertical, cols, nm):
    t = pl.program_id(0)
    total = pl.num_programs(0) - 1
    i = lax.rem(t, nm)
    tt = x_ref.shape[0]
    cur = lax.rem(t, 2)
    prv = 1 - cur

    @pl.when(t == 0)
    def _():
        a_ref[...] = jnp.zeros_like(a_ref)
        u_ref[...] = jnp.zeros_like(u_ref)

    def conv_previous_tile():
        col = lax.broadcasted_iota(jnp.int32, o_ref.shape, 0) & (cols - 1)
        acc = None
        for dc in (-1, 0, 1):
            part = None
            for dr in ((-1, 0, 1) if vertical else (0,)):
                tap = cw_ref[3 * (dr + 1) + (dc + 1):3 * (dr + 1) + (dc + 1) + 1, :]
                term = tap * a_ref[prv, pl.ds(_HALO + GRID_W * dr + dc, tt), :]
                part = term if part is None else part + term
            if dc == -1:
                part = jnp.where(col == 0, 0.0, part)
            elif dc == 1:
                part = jnp.where(col == cols - 1, 0.0, part)
            acc = part if acc is None else acc + part
        o_ref[...] = (_silu(acc) * u_ref[prv].astype(F32)).astype(o_ref.dtype)

    @pl.when((i == 0) & (t < total))
    def _():
        wb_ref[0] = wa_ref[...].astype(BF16)
        wb_ref[1] = wu_ref[...].astype(BF16)

    @pl.when(t < total)
    def _():
        x = x_ref[...]
        a_new = jnp.dot(x, wb_ref[0], preferred_element_type=F32)
        u_ref[cur] = jnp.dot(x, wb_ref[1], preferred_element_type=F32).astype(BF16)
        upper = a_ref[prv, pl.ds(_HALO + tt - GRID_W, GRID_W), :]
        a_ref[cur, pl.ds(SUBLANES, GRID_W), :] = jnp.where(i == 0, 0.0, upper)
        a_ref[cur, pl.ds(_HALO, tt), :] = a_new
        a_ref[prv, pl.ds(_HALO + tt, GRID_W), :] = jnp.where(i == 0, 0.0, a_new[:GRID_W])
        conv_previous_tile()

    @pl.when(t == total)
    def _():
        a_ref[prv, pl.ds(_HALO + tt, GRID_W), :] = jnp.zeros((GRID_W, a_ref.shape[2]), F32)
        conv_previous_tile()


def _ffn_up_conv(h, w_up, conv_w, cols, layer):
    m, d = h.shape
    dff = conv_w.shape[-1]
    vertical = m > cols
    assert cols & (cols - 1) == 0 and (cols == GRID_W or not vertical)
    tt = min(m, 1024)
    assert tt % cols == 0
    cb = 512
    ncb = dff // cb
    nm = m // tt
    total = ncb * nm
    cur_t = lambda t: jnp.minimum(t, total - 1)
    prev_t = lambda t: jnp.maximum(t - 1, 0)
    return pl.pallas_call(
        functools.partial(_ffn_up_conv_kernel, vertical=vertical, cols=cols, nm=nm),
        grid=(total + 1,),
        in_specs=[pl.BlockSpec((tt, d), lambda t: (cur_t(t) % nm, 0)),
                  pl.BlockSpec((None, d, cb), lambda t: (layer, 0, cur_t(t) // nm)),
                  pl.BlockSpec((None, d, cb), lambda t: (layer, 0, ncb + cur_t(t) // nm)),
                  pl.BlockSpec((9, cb), lambda t: (0, prev_t(t) // nm))],
        out_specs=pl.BlockSpec((tt, cb), lambda t: (prev_t(t) % nm, prev_t(t) // nm)),
        out_shape=jax.ShapeDtypeStruct((m, dff), BF16),
        scratch_shapes=[pltpu.VMEM((2, tt + 2 * _HALO, cb), F32), pltpu.VMEM((2, tt, cb), BF16),
                        pltpu.VMEM((2, d, cb), BF16)],
        compiler_params=_params("arbitrary"),
        name="ffn_up_conv",
    )(h, w_up, w_up, conv_w.reshape(9, dff))


def _ffn_down_kernel(g_ref, w_ref, x_ref, gate_ref, ng_ref, nsh_ref, nsc_ref, o_ref, *h_ref, post):
    xn = x_ref[...] + gate_ref[...] * jnp.dot(g_ref[...], w_ref[...], preferred_element_type=F32)
    if post == 'final':
        ms = jnp.mean(xn * xn, axis=-1, keepdims=True)
        xn = xn * lax.rsqrt(ms + NORM_EPS) * ng_ref[...]
    o_ref[...] = xn
    if post == 'next':
        h_ref[0][...] = _norm_mod(xn, ng_ref[...], nsh_ref[...], nsc_ref[...]).astype(BF16)


def _ffn_down(g, w, x, gate, norm, post, layer):
    m, d = x.shape
    dff = g.shape[1]
    tm = min(m, 256)
    vec = pl.BlockSpec((1, d), lambda i: (0, 0))
    row = pl.BlockSpec((tm, d), lambda i: (i, 0))
    out_specs, out_shape = [row], [jax.ShapeDtypeStruct((m, d), F32)]
    if post == 'next':
        out_specs.append(row)
        out_shape.append(jax.ShapeDtypeStruct((m, d), BF16))
    return pl.pallas_call(
        functools.partial(_ffn_down_kernel, post=post),
        grid=(m // tm,),
        in_specs=[pl.BlockSpec((tm, dff), lambda i: (i, 0)),
                  _single((None, dff, d), lambda i: (layer, 0, 0)),
                  row, vec, vec, vec, vec],
        out_specs=out_specs,
        out_shape=out_shape,
        compiler_params=_params("parallel"),
        name="ffn_down",
    )(g, w, x, gate, *norm)


def _regroup_w_in(w_in_t, hy_proj):
    qk = GLA_HEADS * GLA_DK
    vw = GLA_HEADS * GLA_DV
    head = hy_proj + qk + vw
    r = w_in_t[:, head:head + 2 * GLA_RANK]
    tail = w_in_t[:, head + 2 * GLA_RANK:]
    pad = jnp.zeros((w_in_t.shape[0], LANES - 2 * GLA_RANK, w_in_t.shape[2]), w_in_t.dtype)
    return head, tail, jnp.concatenate([r, pad], axis=1)


def _pad_gate_w(wg, lane0):
    return jnp.zeros((LANES, wg.shape[1]), F32).at[lane0:lane0 + GLA_RANK].set(wg)


def _mixer(p, p_r, lp, s0_f, s0_b, want_output):
    l = p.shape[0]
    width = lp['hy_bias'].shape[0]
    o_f, s_f = _gla_pass(p, p_r, lp['wg_f_pad'], lp['gla_bg_f'], s0_f, reverse=False)
    o, s_b = _gla_pass(p, p_r, lp['wg_b_pad'], lp['gla_bg_b'], s0_b, reverse=True, of=o_f, gout=lp['gla_out_g'])
    if not want_output:
        return None, None, s_f, s_b
    zx = _hy_pre(p, lp['hy_conv'], width)
    if 2 * l // DFT_B >= 2 * SUBLANES:
        hy = _hyena_long(zx, lp, lp['hy_bias'], lp['hy_out_g'])
    else:
        k2 = _hyena_filter2(l, lp['hy_w1'], lp['hy_b1'], lp['hy_freq'], lp['hy_w2'], lp['hy_b2'], lp['hy_w3'], width)
        hy = _hyena_short(zx, k2, lp['hy_bias'], lp['hy_out_g'])
    return hy, o, s_f, s_b


def _order_w_out(w_out, hy_width):
    depth, _, d = w_out.shape
    hy_rows = w_out[:, :hy_width].reshape(depth, hy_width // (2 * LANES), 2, LANES, d)
    hy_rows = jnp.swapaxes(hy_rows, 1, 2).reshape(depth, hy_width, d)
    return jnp.concatenate([hy_rows, w_out[:, hy_width:]], axis=1).astype(BF16)


def _ffn(h2, x, gate, lp, cols, norm, post, layer):
    gact = _ffn_up_conv(h2, lp['ffn_w_up'], lp['ffn_conv'], cols, layer)
    return _ffn_down(gact, lp['ffn_w_down'], x, gate, norm, post, layer)


def kernel(x, c, ctx, c_ctx, w_mod, b_mod, g_attn, w_in, hy_conv, hy_w1, hy_b1, hy_freq, hy_w2, hy_b2, hy_w3,
           hy_bias, hy_out_g, gla_wg_f, gla_bg_f, gla_wg_b, gla_bg_b, gla_out_g, w_out, g_ffn, ffn_w_up,
           ffn_conv, ffn_w_down, g_final):
    depth = w_mod.shape[0]
    batch, seq, d = x.shape
    assert batch == 1 and c.shape[0] == 1 and ctx.shape[0] == 1
    hy_proj = hy_conv.shape[-1]
    xs = x[0]
    cs = ctx[0]
    cc = jnp.zeros((SUBLANES, d), F32).at[0].set(c[0]).at[1].set(c_ctx)
    zero_state = jnp.zeros(_GLA_STATE, F32)
    gfin = g_final.reshape(1, d)
    w_in = jnp.swapaxes(w_in, 1, 2)
    n_head_cols, w_tail, w_rank = _regroup_w_in(w_in, hy_proj)
    w_out_b = _order_w_out(w_out, hy_bias.shape[-1])
    w_down_b = ffn_w_down.astype(BF16)
    mods =[_adaln(cc, w_mod, b_mod, l) for l in range(depth)]

    def attn_norm(l, row):
        return (g_attn[l].reshape(1, d), mods[l][row:row + 1, 0:d], mods[l][row:row + 1, d:2 * d])

    hx = _norm_mod_call(xs, *attn_norm(0, 0))
    hc = _norm_mod_call(cs, *attn_norm(0, 1))
    for l in range(depth):
        last = l == depth - 1
        lp = {
            'hy_conv': hy_conv[l], 'hy_w1': hy_w1[l], 'hy_b1': hy_b1[l], 'hy_freq': hy_freq[l],
            'hy_w2': hy_w2[l], 'hy_b2': hy_b2[l], 'hy_w3': hy_w3[l], 'hy_bias': hy_bias[l],
            'hy_out_g': hy_out_g[l], 'gla_bg_f': gla_bg_f[l], 'gla_bg_b': gla_bg_b[l],
            'gla_out_g': gla_out_g[l],
            'wg_f_pad': _pad_gate_w(gla_wg_f[l], 0), 'wg_b_pad': _pad_gate_w(gla_wg_b[l], GLA_RANK),
            'ffn_w_up': ffn_w_up, 'ffn_conv': ffn_conv[l], 'ffn_w_down': w_down_b,
        }
        mod = mods[l]
        _, _, ga, sf, scf, gf = [mod[0:1, i * d:(i + 1) * d] for i in range(6)]
        _, _, cga, csf, cscf, cgf = [mod[1:2, i * d:(i + 1) * d] for i in range(6)]
        gf_row = g_ffn[l].reshape(1, d)

        pc = _in_proj(hc, w_in, w_tail, n_head_cols, l)
        hy_c, o_c, s_f, s_b = _mixer(pc, _gate_rank_proj(hc, w_rank, l), lp, zero_state, zero_state,
                                     want_output=not last)
        if not last:
            cs, h2c = _out_proj(hy_c, o_c, w_out_b, cs, cga, gf_row, csf, cscf, l)
            cs, hc = _ffn(h2c, cs, cgf, lp, cs.shape[0], attn_norm(l + 1, 1), 'next', l)

        px = _in_proj(hx, w_in, w_tail, n_head_cols, l)
        hy_x, o_x, _, _ = _mixer(px, _gate_rank_proj(hx, w_rank, l), lp, s_f, s_b, want_output=True)
        xs, h2 = _out_proj(hy_x, o_x, w_out_b, xs, ga, gf_row, sf, scf, l)
        if last:
            (xs,) = _ffn(h2, xs, gf, lp, GRID_W, (gfin, gfin, gfin), 'final', l)
        else:
            xs, hx = _ffn(h2, xs, gf, lp, GRID_W, attn_norm(l + 1, 0), 'next', l)
    return xs[None]
```

```python
import functools
import math

import numpy as np
import jax
import jax.numpy as jnp
from jax import lax
from jax.experimental import pallas as pl
from jax.experimental.pallas import tpu as pltpu

F32 = jnp.float32
BF16 = jnp.bfloat16
HIGHEST = lax.Precision.HIGHEST

LANES = 128
SUBLANES = 8
VMEM_BYTES_V7X = 64 * 1024 * 1024
VMEM_LIMIT = VMEM_BYTES_V7X - 12 * 1024 * 1024

NORM_EPS = 1e-6
HYENA_GROUP = 128
FILTER_BANDS = 16
FILTER_HIDDEN = 64
FILTER_TARGET = 1e-2
FILTER_FAST_PCT = 0.3
FILTER_SLOW_PCT = 1.5
GLA_HEADS = 8
GLA_DK = 64
GLA_DV = 128
GLA_RANK = 16
GLA_GATE_TAU = 16.0
GLA_CHUNK = 64
GRID_W = 64
DFT_B = 128


def _params(*sem):
    return pltpu.CompilerParams(dimension_semantics=sem, vmem_limit_bytes=VMEM_LIMIT)


def _silu(x):
    return x * jax.nn.sigmoid(x)


def _adaln_kernel(c_ref, w_ref, b_ref, o_ref):
    s = _silu(c_ref[...]).astype(BF16)
    o_ref[...] = jnp.dot(s, w_ref[...].astype(BF16), preferred_element_type=F32) + b_ref[...]


def _adaln(cc, w_mod, b_mod, layer):
    depth, d, n = w_mod.shape
    tn = 1024
    return pl.pallas_call(
        _adaln_kernel,
        grid=(n // tn,),
        in_specs=[pl.BlockSpec((SUBLANES, d), lambda j: (0, 0)),
                  pl.BlockSpec((None, d, tn), lambda j: (layer, 0, j)),
                  pl.BlockSpec((None, 1, tn), lambda j: (layer, 0, j))],
        out_specs=pl.BlockSpec((SUBLANES, tn), lambda j: (0, j)),
        out_shape=jax.ShapeDtypeStruct((SUBLANES, n), F32),
        compiler_params=_params("arbitrary"),
        name="adaln",
    )(cc, w_mod, b_mod.reshape(depth, 1, n))


def _norm_mod(x, g, shift, scale):
    ms = jnp.mean(x * x, axis=-1, keepdims=True)
    return (x * lax.rsqrt(ms + NORM_EPS) * g) * (1.0 + scale) + shift


def _norm_mod_kernel(x_ref, g_ref, sh_ref, sc_ref, o_ref):
    o_ref[...] = _norm_mod(x_ref[...], g_ref[...], sh_ref[...], sc_ref[...]).astype(o_ref.dtype)


def _norm_mod_call(x, g, shift, scale):
    m, d = x.shape
    tm = min(m, 512)
    vec = pl.BlockSpec((1, d), lambda i: (0, 0))
    return pl.pallas_call(
        _norm_mod_kernel,
        grid=(m // tm,),
        in_specs=[pl.BlockSpec((tm, d), lambda i: (i, 0)), vec, vec, vec],
        out_specs=pl.BlockSpec((tm, d), lambda i: (i, 0)),
        out_shape=jax.ShapeDtypeStruct((m, d), BF16),
        compiler_params=_params("parallel"),
        name="norm_mod",
    )(x, g, shift, scale)


_IN_TILE = 768


_NT = (((1,), (1,)), ((), ()))


def _in_proj_kernel(h_ref, w_ref, wt_ref, o_ref, wb_ref, *, n_head):
    j = pl.program_id(0)

    @pl.when((pl.program_id(1) == 0) & (j < n_head))
    def _():
        wb_ref[...] = w_ref[...].T.astype(BF16)

    @pl.when((pl.program_id(1) == 0) & (j >= n_head))
    def _():
        wb_ref[...] = wt_ref[...].T.astype(BF16)

    o_ref[...] = jnp.dot(h_ref[...], wb_ref[...], preferred_element_type=F32)


def _in_proj(h, w_in_t, w_tail_t, n_head_cols, layer):
    m, d = h.shape
    tn = _IN_TILE
    n_head = n_head_cols // tn
    n_tail = w_tail_t.shape[1] // tn
    assert n_head * tn == n_head_cols and n_tail * tn == w_tail_t.shape[1]
    tm = min(m, 1024)
    return pl.pallas_call(
        functools.partial(_in_proj_kernel, n_head=n_head),
        grid=(n_head + n_tail, m // tm),
        in_specs=[pl.BlockSpec((tm, d), lambda j, i: (i, 0)),
                  _single((None, tn, d), lambda j, i: (layer, jnp.minimum(j, n_head - 1), 0)),
                  _single((None, tn, d), lambda j, i: (layer, jnp.maximum(j - n_head, 0), 0))],
        out_specs=pl.BlockSpec((tm, tn), lambda j, i: (i, j)),
        out_shape=jax.ShapeDtypeStruct((m, n_head_cols + w_tail_t.shape[1]), F32),
        scratch_shapes=[pltpu.VMEM((d, tn), BF16)],
        compiler_params=_params("arbitrary", "arbitrary"),
        name="in_proj",
    )(h, w_in_t, w_tail_t)


def _gate_rank_kernel(h_ref, w_ref, o_ref):
    o_ref[...] = lax.dot_general(h_ref[...], w_ref[...].astype(BF16), _NT, preferred_element_type=F32)


def _gate_rank_proj(h, w_r_t, layer):
    m, d = h.shape
    tm = min(m, 2048)
    return pl.pallas_call(
        _gate_rank_kernel,
        grid=(m // tm,),
        in_specs=[pl.BlockSpec((tm, d), lambda i: (i, 0)), pl.BlockSpec((None, LANES, d), lambda i: (layer, 0, 0))],
        out_specs=pl.BlockSpec((tm, LANES), lambda i: (i, 0)),
        out_shape=jax.ShapeDtypeStruct((m, LANES), F32),
        compiler_params=_params("parallel"),
        name="gate_rank_proj",
    )(h, w_r_t)


def _hy_pre_kernel(x0_ref, x1_ref, v_ref, x0p_ref, x1p_ref, vp_ref, x0n_ref, x1n_ref, vn_ref,
                   w0_ref, w1_ref, wv_ref, zx_ref):
    i = pl.program_id(0)
    first = i == 0
    last = i == pl.num_programs(0) - 1
    tt = x0_ref.shape[0]
    row = lax.broadcasted_iota(jnp.int32, x0_ref.shape, 0)

    def conv(ref, prev_ref, next_ref, w_ref):
        x = ref[...]
        prev_row = jnp.where(first, 0.0, prev_ref[SUBLANES - 1:SUBLANES, :])
        next_row = jnp.where(last, 0.0, next_ref[0:1, :])
        below = jnp.where(row == 0, prev_row, pltpu.roll(x, 1, 0))
        above = jnp.where(row == tt - 1, next_row, pltpu.roll(x, tt - 1, 0))
        return w_ref[0:1, :] * below + w_ref[1:2, :] * x + w_ref[2:3, :] * above

    z = conv(v_ref, vp_ref, vn_ref, wv_ref) * conv(x1_ref, x1p_ref, x1n_ref, w1_ref)
    zx_ref[...] = _pack_pair(z, conv(x0_ref, x0p_ref, x0n_ref, w0_ref))


def _hy_pre(p, hy_conv, width):
    l = p.shape[0]
    tt = min(l, 1024)
    cb = 512
    ncb = width // cb
    nb8 = l // SUBLANES
    t8 = tt // SUBLANES

    def main(g):
        return pl.BlockSpec((tt, cb), lambda i, j: (i, g * ncb + j))

    def prev(g):
        return pl.BlockSpec((SUBLANES, cb), lambda i, j: (jnp.maximum(i * t8 - 1, 0), g * ncb + j))

    def nxt(g):
        return pl.BlockSpec((SUBLANES, cb), lambda i, j: (jnp.minimum((i + 1) * t8, nb8 - 1), g * ncb + j))

    def wspec(g):
        return pl.BlockSpec((3, cb), lambda i, j: (0, g * ncb + j))

    out = pl.BlockSpec((tt, cb), lambda i, j: (i, j))
    return pl.pallas_call(
        _hy_pre_kernel,
        grid=(l // tt, ncb),
        in_specs=[main(0), main(1), main(2), prev(0), prev(1), prev(2), nxt(0), nxt(1), nxt(2),
                  wspec(0), wspec(1), wspec(2)],
        out_specs=out,
        out_shape=jax.ShapeDtypeStruct((l, width), jnp.uint32),
        compiler_params=_params("parallel", "parallel"),
        name="hy_pre",
    )(p, p, p, p, p, p, p, p, p, hy_conv, hy_conv, hy_conv)


def _filter_hidden(j, seq_len, w1t_ref, w1c_ref, w1s_ref, b1_ref, fr_ref, w2_ref, b2_ref):
    pos_i = jnp.where(j < seq_len, j, 2 * seq_len - j)
    pos = pos_i.astype(F32)
    t = pos / float(max(seq_len - 1, 1))
    lane = lax.broadcasted_iota(jnp.int32, j.shape, 1)
    band = jnp.where(lane < FILTER_BANDS, lane + 1, 0).astype(F32)
    ang = (2.0 * math.pi / seq_len) * pos * band
    fr = fr_ref[...]
    pre = (t[:, 0:1] * w1t_ref[...]
           + jnp.dot(jnp.cos(ang), w1c_ref[...], precision=HIGHEST, preferred_element_type=F32)
           + jnp.dot(jnp.sin(ang), w1s_ref[...], precision=HIGHEST, preferred_element_type=F32)
           + b1_ref[...])
    h = jnp.sin(fr * pre)
    h = jnp.sin(fr * (jnp.dot(h, w2_ref[...], precision=HIGHEST, preferred_element_type=F32) + b2_ref[...]))
    return h, pos_i, t


def _filter_kernel(w1t_ref, w1c_ref, w1s_ref, b1_ref, fr_ref, w2_ref, b2_ref, w3_ref, delta_ref, o_ref, *, seq_len):
    tr = o_ref.shape[0]
    j = pl.program_id(0) * tr + lax.broadcasted_iota(jnp.int32, (tr, LANES), 0)
    h, pos_i, t = _filter_hidden(j, seq_len, w1t_ref, w1c_ref, w1s_ref, b1_ref, fr_ref, w2_ref, b2_ref)
    hh = jnp.dot(h.astype(BF16), w3_ref[...].astype(BF16), preferred_element_type=F32)
    decay = jnp.exp(-t[:, 0:1] * delta_ref[...])
    valid = pos_i[:, 0:1] != seq_len
    o_ref[...] = jnp.where(valid, hh * decay, 0.0)


_FILTER_PACK = LANES // FILTER_BANDS


def _filter_hidden_kernel(w1t_ref, w1c_ref, w1s_ref, b1_ref, fr_ref, w2_ref, b2_ref, o_ref, *, seq_len, n_a):
    tp = o_ref.shape[0]
    shift = n_a.bit_length() - 1

    def position(shape, lanes_per_pos):
        prow = pl.program_id(0) * tp + lax.broadcasted_iota(jnp.int32, shape, 0)
        r = prow * _FILTER_PACK + lax.broadcasted_iota(jnp.int32, shape, 1) // lanes_per_pos
        j = ((r & (n_a - 1)) * DFT_B) + (r >> shift)
        return jnp.where(j < seq_len, j, 2 * seq_len - j).astype(F32)

    pos = position((tp, LANES), FILTER_BANDS)
    band = (lax.broadcasted_iota(jnp.int32, (tp, LANES), 1) % FILTER_BANDS + 1).astype(F32)
    ang = (2.0 * math.pi / seq_len) * pos * band
    t = position(o_ref.shape, FILTER_HIDDEN) / float(max(seq_len - 1, 1))
    fr = fr_ref[...]
    pre = (t * w1t_ref[...]
           + jnp.dot(jnp.cos(ang), w1c_ref[...], precision=HIGHEST, preferred_element_type=F32)
           + jnp.dot(jnp.sin(ang), w1s_ref[...], precision=HIGHEST, preferred_element_type=F32)
           + b1_ref[...])
    h = jnp.sin(fr * pre)
    o_ref[...] = jnp.sin(fr * (jnp.dot(h, w2_ref[...], precision=HIGHEST, preferred_element_type=F32) + b2_ref[...]))


def _filter_mlp_args(w1, b1, freq, w2, b2):
    hid = FILTER_HIDDEN
    w1c = jnp.zeros((LANES, hid), F32).at[:FILTER_BANDS].set(w1[1:1 + FILTER_BANDS])
    w1s = jnp.zeros((LANES, hid), F32).at[:FILTER_BANDS].set(w1[1 + FILTER_BANDS:1 + 2 * FILTER_BANDS])
    full = lambda shape: pl.BlockSpec(shape, lambda i: (0, 0))
    specs = [full((1, hid)), full((LANES, hid)), full((LANES, hid)), full((1, hid)), full((1, hid)),
             full((hid, hid)), full((1, hid))]
    args = (w1[0:1], w1c, w1s, b1.reshape(1, hid), freq.reshape(1, hid), w2, b2.reshape(1, hid))
    return specs, args


def _filter_delta(width):
    return jnp.abs(jnp.linspace(math.log(FILTER_TARGET) / FILTER_SLOW_PCT,
                                math.log(FILTER_TARGET) / FILTER_FAST_PCT, width, dtype=F32)).reshape(1, width)


def _hyena_filter2(seq_len, w1, b1, freq, w2, b2, w3, width):
    n2 = 2 * seq_len
    tr = min(seq_len, 1024)
    nfwd = seq_len // tr
    specs, args = _filter_mlp_args(w1, b1, freq, w2, b2)
    return pl.pallas_call(
        functools.partial(_filter_kernel, seq_len=seq_len),
        grid=(n2 // tr,),
        in_specs=specs + [pl.BlockSpec((FILTER_HIDDEN, width), lambda i: (0, jnp.where(i < nfwd, 0, 1))),
                          pl.BlockSpec((1, width), lambda i: (0, 0))],
        out_specs=pl.BlockSpec((tr, width), lambda i: (i, 0)),
        out_shape=jax.ShapeDtypeStruct((n2, width), F32),
        compiler_params=_params("parallel"),
        name="hyena_filter",
    )(*args, w3, _filter_delta(width))


def _filter_hidden_permuted(seq_len, w1, b1, freq, w2, b2):
    n2 = 2 * seq_len
    n_a = n2 // DFT_B
    hid = FILTER_HIDDEN
    wide = _FILTER_PACK * hid
    rows = n2 // _FILTER_PACK
    tp = min(rows, 256)
    eye = jnp.eye(_FILTER_PACK, dtype=F32)
    tile = lambda v: jnp.tile(v.reshape(1, hid), (1, _FILTER_PACK))
    args = (tile(w1[0]), jnp.kron(eye, w1[1:1 + FILTER_BANDS]),
            jnp.kron(eye, w1[1 + FILTER_BANDS:1 + 2 * FILTER_BANDS]), tile(b1), tile(freq),
            jnp.kron(eye, w2), tile(b2))
    packed = pl.pallas_call(
        functools.partial(_filter_hidden_kernel, seq_len=seq_len, n_a=n_a),
        grid=(rows // tp,),
        in_specs=[pl.BlockSpec(a.shape, lambda i: (0, 0)) for a in args],
        out_specs=pl.BlockSpec((tp, wide), lambda i: (i, 0)),
        out_shape=jax.ShapeDtypeStruct((rows, wide), F32),
        compiler_params=_params("parallel"),
        name="hyena_filter_hidden",
    )(*args)
    return packed.reshape(n2, hid)


def _kept_k1(n_a):
    kept = n_a // 2 + 1
    return kept, -(-kept // SUBLANES) * SUBLANES


def _dft_tables(n_a, a_in):
    n = n_a * DFT_B
    kept, padded = _kept_k1(n_a)
    b = np.arange(DFT_B)[:, None, None]
    k1 = np.arange(padded)[None, :, None]
    a = np.arange(a_in)[None, None, :]
    e = np.exp(-2j * np.pi * ((k1 * (DFT_B * a + b)) % n) / n) * (k1 < kept)
    g = np.concatenate([e.real, e.imag], axis=1)
    return jnp.asarray(g, F32)


def _idft_tables(n_a, a_out):
    n = n_a * DFT_B
    kept, padded = _kept_k1(n_a)
    b = np.arange(DFT_B)[:, None, None]
    a = np.arange(a_out)[None, :, None]
    k1 = np.arange(padded)[None, None, :]
    weight = np.where((k1 == 0) | (k1 == n_a // 2), 1.0, 2.0) * (k1 < kept)
    e = np.exp(2j * np.pi * ((k1 * (DFT_B * a + b)) % n) / n) * weight / n
    h = np.concatenate([e.real, -e.imag], axis=2)
    return jnp.asarray(h, F32)


def _dft128_embed():
    i = np.arange(DFT_B)
    f = np.exp(-2j * np.pi * ((i[:, None] * i[None, :]) % DFT_B) / DFT_B)
    fwd = np.block([[f.real, -f.imag], [f.imag, f.real]])
    inv = np.block([[f.real, f.imag], [-f.imag, f.real]])
    return jnp.asarray(fwd, F32), jnp.asarray(inv, F32)


_PITCH = DFT_B + SUBLANES
_K1_BLOCK = 8
_B_UNROLL = 16


def _stage_rows(k1):
    return pl.ds(pl.multiple_of(k1 * _PITCH, SUBLANES), DFT_B)


def _single(shape, index_map):
    return pl.BlockSpec(shape, index_map, pipeline_mode=pl.Buffered(1))


def _pack_pair(re, im):
    hi = lax.bitcast_convert_type(re.astype(BF16).astype(F32), jnp.uint32)
    lo = lax.bitcast_convert_type(im.astype(BF16).astype(F32), jnp.uint32)
    return hi | (lo >> 16)


def _unpack_pair(w):
    re = lax.bitcast_convert_type(w & jnp.uint32(0xFFFF0000), F32)
    im = lax.bitcast_convert_type(w << 16, F32)
    return re.astype(BF16), im.astype(BF16)


def _spectrum_kernel(h_ref, w3f_ref, w3b_ref, delta_ref, g_ref, wf_ref, o_ref, s0_ref, s1_ref, *, seq_len, n_a):
    s = pl.program_id(1)
    half = n_a // 2
    cw = o_ref.shape[3]

    @pl.when(s == 0)
    def _():
        a_i = lax.broadcasted_iota(jnp.int32, (n_a, cw), 0)

        def body(b, carry):
            base = pl.multiple_of(b * n_a, SUBLANES)
            hf = jnp.dot(h_ref[pl.ds(base, half), :].astype(BF16), w3f_ref[...].astype(BF16),
                         preferred_element_type=F32)
            hb = jnp.dot(h_ref[pl.ds(base + half, half), :].astype(BF16), w3b_ref[...].astype(BF16),
                         preferred_element_type=F32)
            hh = jnp.concatenate([hf, hb], axis=0)
            j = a_i * DFT_B + b
            pos_i = jnp.where(j < seq_len, j, 2 * seq_len - j)
            t = pos_i.astype(F32) / float(max(seq_len - 1, 1))
            k2 = jnp.where(pos_i != seq_len, hh * jnp.exp(-t * delta_ref[...]), 0.0)
            r = jnp.dot(g_ref[b], k2.astype(BF16), preferred_element_type=F32)
            nk = r.shape[0] // 2
            w = _pack_pair(r[:nk], r[nk:])
            s0_ref[pl.ds(b, nk, stride=_PITCH), :] = w[:, :LANES]
            s1_ref[pl.ds(b, nk, stride=_PITCH), :] = w[:, LANES:]
            return carry

        lax.fori_loop(0, DFT_B, body, 0, unroll=_B_UNROLL)

    @pl.when(s > 0)
    def _():
        k0 = (s - 1) * o_ref.shape[1]
        for i in range(o_ref.shape[1]):
            rows = _stage_rows(k0 + i)
            re, im = _unpack_pair(jnp.concatenate([s0_ref[rows, :], s1_ref[rows, :]], axis=1))
            z = jnp.dot(wf_ref[...], jnp.concatenate([re, im], axis=0), preferred_element_type=F32)
            o_ref[0, i] = z[:DFT_B]
            o_ref[1, i] = z[DFT_B:]


def _filter_spectrum(hperm, w3, seq_len, width, g_k, wf):
    n_a = 2 * seq_len // DFT_B
    nk = g_k.shape[1] // 2
    kb = min(nk, _K1_BLOCK)
    cw = 2 * LANES
    ncb = width // cw
    hid = FILTER_HIDDEN
    return pl.pallas_call(
        functools.partial(_spectrum_kernel, seq_len=seq_len, n_a=n_a),
        grid=(ncb, 1 + nk // kb),
        in_specs=[_single(hperm.shape, lambda ci, s: (0, 0)),
                  pl.BlockSpec((hid, cw), lambda ci, s: (0, ci)),
                  pl.BlockSpec((hid, cw), lambda ci, s: (0, ncb + ci)),
                  pl.BlockSpec((1, cw), lambda ci, s: (0, ci)),
                  _single(g_k.shape, lambda ci, s: (0, 0, 0)),
                  _single(wf.shape, lambda ci, s: (0, 0))],
        out_specs=pl.BlockSpec((2, kb, DFT_B, cw), lambda ci, s: (0, jnp.maximum(s - 1, 0), 0, ci)),
        out_shape=jax.ShapeDtypeStruct((2, nk, DFT_B, width), F32),
        scratch_shapes=[pltpu.VMEM((nk * _PITCH, LANES), jnp.uint32), pltpu.VMEM((nk * _PITCH, LANES), jnp.uint32)],
        compiler_params=_params("parallel", "arbitrary"),
        name="filter_spectrum",
    )(hperm, w3, w3, _filter_delta(width), g_k, wf)


def _hyena_epilogue(y, z, x0, bias, gain):
    yv = (y + z * bias) * x0
    ms = jnp.mean(yv * yv, axis=-1, keepdims=True)
    return yv * lax.rsqrt(ms + NORM_EPS) * gain


def _pack_groups(y, zx, bias, gain):
    z, x0 = _unpack_pair(zx)
    out = _hyena_epilogue_groups(y, z.astype(F32), x0.astype(F32), bias, gain)
    return _pack_pair(out[:, :LANES], out[:, LANES:])


def _hyena_epilogue_groups(y, z, x0, bias, gain):
    return jnp.concatenate([_hyena_epilogue(y[:, g * LANES:(g + 1) * LANES], z[:, g * LANES:(g + 1) * LANES],
                                            x0[:, g * LANES:(g + 1) * LANES], bias[:, g * LANES:(g + 1) * LANES],
                                            gain[:, g * LANES:(g + 1) * LANES])
                            for g in range(y.shape[1] // LANES)], axis=1)


def _long_conv_kernel(zx0_ref, zx1_ref, kf_ref, g_ref, h_ref, wf_ref, wi_ref, bias_ref, gain_ref, o_ref,
                      s0_ref, s1_ref, *, n_a):
    s = pl.program_id(1)
    half = n_a // 2
    nk = g_ref.shape[1] // 2
    kb = kf_ref.shape[1]
    nkb = nk // kb

    def load_zx(b):
        return jnp.concatenate([zx0_ref[pl.ds(b, half, stride=DFT_B), :],
                                zx1_ref[pl.ds(b, half, stride=DFT_B), :]], axis=1)

    def load_stage(rows):
        return _unpack_pair(jnp.concatenate([s0_ref[rows, :], s1_ref[rows, :]], axis=1))

    def store_stage(rows, re, im):
        w = _pack_pair(re, im)
        s0_ref[rows, :] = w[:, :LANES]
        s1_ref[rows, :] = w[:, LANES:]

    @pl.when(s == 0)
    def _():
        def body(b, carry):
            z, _ = _unpack_pair(load_zx(b))
            r = jnp.dot(g_ref[b], z, preferred_element_type=F32)
            store_stage(pl.ds(b, nk, stride=_PITCH), r[:nk], r[nk:])
            return carry

        lax.fori_loop(0, DFT_B, body, 0, unroll=_B_UNROLL)

    @pl.when((s > 0) & (s <= nkb))
    def _():
        k0 = (s - 1) * kb
        for i in range(kb):
            rows = _stage_rows(k0 + i)
            re, im = load_stage(rows)
            zz = jnp.dot(wf_ref[...], jnp.concatenate([re, im], axis=0), preferred_element_type=F32)
            zr, zi = zz[:DFT_B], zz[DFT_B:]
            kr, ki = kf_ref[0, i], kf_ref[1, i]
            y = jnp.concatenate([zr * kr - zi * ki, zr * ki + zi * kr], axis=0).astype(BF16)
            cc = jnp.dot(wi_ref[...], y, preferred_element_type=F32)
            store_stage(rows, cc[:DFT_B], cc[DFT_B:])

    @pl.when(s == nkb + 1)
    def _():
        def body(b, carry):
            re, im = load_stage(pl.ds(b, nk, stride=_PITCH))
            y = jnp.dot(h_ref[b], jnp.concatenate([re, im], axis=0), preferred_element_type=F32)
            o_ref[pl.ds(b, half, stride=DFT_B), :] = _pack_groups(y, load_zx(b), bias_ref[...], gain_ref[...])
            return carry

        lax.fori_loop(0, DFT_B, body, 0, unroll=_B_UNROLL)


def _long_conv(zx, kf, g_z, h_tab, wf, wi, bias, gain):
    l, c = zx.shape
    n_a = 2 * l // DFT_B
    nk = g_z.shape[1] // 2
    kb = min(nk, _K1_BLOCK)
    nkb = nk // kb
    cw = 2 * LANES
    return pl.pallas_call(
        functools.partial(_long_conv_kernel, n_a=n_a),
        grid=(c // cw, nkb + 2),
        in_specs=[_single((l, LANES), lambda ci, s: (0, 2 * ci)), _single((l, LANES), lambda ci, s: (0, 2 * ci + 1)),
                  pl.BlockSpec((2, kb, DFT_B, cw), lambda ci, s: (0, jnp.clip(s - 1, 0, nkb - 1), 0, ci)),
                  _single(g_z.shape, lambda ci, s: (0, 0, 0)), _single(h_tab.shape, lambda ci, s: (0, 0, 0)),
                  _single(wf.shape, lambda ci, s: (0, 0)), _single(wi.shape, lambda ci, s: (0, 0)),
                  pl.BlockSpec((1, cw), lambda ci, s: (0, ci)), pl.BlockSpec((1, cw), lambda ci, s: (0, ci))],
        out_specs=pl.BlockSpec((l, LANES), lambda ci, s: (0, ci)),
        out_shape=jax.ShapeDtypeStruct((l, c // 2), jnp.uint32),
        scratch_shapes=[pltpu.VMEM((nk * _PITCH, LANES), jnp.uint32), pltpu.VMEM((nk * _PITCH, LANES), jnp.uint32)],
        compiler_params=_params("parallel", "arbitrary"),
        name="long_conv",
    )(zx, zx, kf, g_z, h_tab, wf, wi, bias.reshape(1, c), gain.reshape(1, c))


def _hyena_long(zx, filt, bias, gain):
    l, c = zx.shape
    n_a = 2 * l // DFT_B
    wf, wi = _dft128_embed()
    wf, wi = wf.astype(BF16), wi.astype(BF16)
    g_z = _dft_tables(n_a, n_a // 2).astype(BF16)
    g_k = _dft_tables(n_a, n_a).astype(BF16)
    h_tab = _idft_tables(n_a, n_a // 2).astype(BF16)
    hperm = _filter_hidden_permuted(l, filt['hy_w1'], filt['hy_b1'], filt['hy_freq'], filt['hy_w2'], filt['hy_b2'])
    kf = _filter_spectrum(hperm, filt['hy_w3'], l, c, g_k, wf)
    return _long_conv(zx, kf, g_z, h_tab, wf, wi, bias, gain)


def _ctx_conv_kernel(zx_ref, k_ref, fz_ref, fk_ref, hi_ref, bias_ref, gain_ref, o_ref):
    n = k_ref.shape[0]
    zx = zx_ref[...]
    z, _ = _unpack_pair(zx)
    zs = jnp.dot(fz_ref[...], z, preferred_element_type=F32)
    ks = jnp.dot(fk_ref[...], k_ref[...].astype(BF16), preferred_element_type=F32)
    zr, zi, kr, ki = zs[:n], zs[n:], ks[:n], ks[n:]
    y = jnp.concatenate([zr * kr - zi * ki, zr * ki + zi * kr], axis=0).astype(BF16)
    yt = jnp.dot(hi_ref[...], y, preferred_element_type=F32)
    o_ref[...] = _pack_groups(yt, zx, bias_ref[...], gain_ref[...])


def _hyena_short(zx, k2, bias, gain):
    l, c = zx.shape
    n = 2 * l
    k = np.arange(n)[:, None]
    t = np.arange(n)[None, :]
    e = np.exp(-2j * np.pi * ((k * t) % n) / n)
    fk = jnp.asarray(np.concatenate([e.real, e.imag], axis=0), F32).astype(BF16)
    fz = fk[:, :l]
    ei = np.conj(e[:, :l]).T / n
    hi = jnp.asarray(np.concatenate([ei.real, -ei.imag], axis=1), F32).astype(BF16)
    cw = 2 * LANES
    col = lambda rows: pl.BlockSpec((rows, cw), lambda ci: (0, ci))
    full = lambda a: pl.BlockSpec(a.shape, lambda ci: (0, 0))
    return pl.pallas_call(
        _ctx_conv_kernel,
        grid=(c // cw,),
        in_specs=[col(l), col(n), full(fz), full(fk), full(hi), col(1), col(1)],
        out_specs=pl.BlockSpec((l, LANES), lambda ci: (0, ci)),
        out_shape=jax.ShapeDtypeStruct((l, c // 2), jnp.uint32),
        compiler_params=_params("parallel"),
        name="hyena_short",
    )(zx, k2, fz, fk, hi, bias.reshape(1, c), gain.reshape(1, c))


def _split3(x):
    hi = x.astype(BF16)
    r1 = x - hi.astype(F32)
    mid = r1.astype(BF16)
    lo = (r1 - mid.astype(F32)).astype(BF16)
    return hi, mid, lo


def _gla_kernel(*refs, reverse, combine):
    if combine:
        (k_ref, q_ref, vlo_ref, vhi_ref, r_ref, wg_ref, bg_ref, s0_ref, of_ref, gate_ref, gout_ref,
         o_ref, sfin_ref, st_ref, la_ref, qe_ref, upd_ref, dec_ref, ke_ref, kl_ref, q2_ref, sc_ref, ob_ref) = refs
    else:
        (k_ref, q_ref, vlo_ref, vhi_ref, r_ref, wg_ref, bg_ref, s0_ref,
         o_ref, sfin_ref, st_ref, la_ref, qe_ref, upd_ref, dec_ref, ke_ref, kl_ref, q2_ref, sc_ref) = refs
        ob_ref = o_ref
    v_half = (vlo_ref, vhi_ref)
    rows_blk = k_ref.shape[0]
    nc = rows_blk // GLA_CHUNK
    qk = k_ref.shape[1]

    @pl.when(pl.program_id(0) == 0)
    def _():
        st_ref[...] = s0_ref[...]

    ri = lax.broadcasted_iota(jnp.int32, (2 * GLA_CHUNK, GLA_CHUNK), 0) % GLA_CHUNK
    ci = lax.broadcasted_iota(jnp.int32, (2 * GLA_CHUNK, GLA_CHUNK), 1)
    r3 = lax.broadcasted_iota(jnp.int32, (GLA_CHUNK, 3 * GLA_CHUNK), 0)
    c3 = lax.broadcasted_iota(jnp.int32, (GLA_CHUNK, 3 * GLA_CHUNK), 1) % GLA_CHUNK
    if reverse:
        tri3 = (c3 >= r3).astype(BF16)
        keep = ci > ri
    else:
        tri3 = (c3 <= r3).astype(BF16)
        keep = ci <= ri
    lane = lax.broadcasted_iota(jnp.int32, (1, LANES), 1)
    first = (lane < GLA_DK).astype(F32)
    srow = lax.broadcasted_iota(jnp.int32, (2 * GLA_DV, LANES), 0)
    scol = lax.broadcasted_iota(jnp.int32, (2 * GLA_DV, LANES), 1)
    own = (srow < GLA_DV) == (scol < GLA_DK)
    nt = (((1,), (1,)), ((), ()))
    tn = (((0,), (0,)), ((), ()))
    npair = GLA_HEADS // 2

    rh, rm, _ = _split3(r_ref[...])
    wh, wm, _ = _split3(wg_ref[...])
    pre = jnp.dot(jnp.concatenate([rh, rh, rm], axis=1), jnp.concatenate([wh, wm, wh], axis=0),
                  preferred_element_type=F32) + bg_ref[...]
    la_ref[...] = jax.nn.log_sigmoid(pre) / GLA_GATE_TAU

    def chunk_rows(c):
        return pl.ds(pl.multiple_of(c * GLA_CHUNK, GLA_CHUNK), GLA_CHUNK)

    def cum_pass(c, carry):
        rows = chunk_rows(c)
        hi, mid, lo = _split3(la_ref[rows, :])
        la_ref[rows, :] = jnp.dot(tri3, jnp.concatenate([hi, mid, lo], axis=0), preferred_element_type=F32)
        return carry

    def decay_pass(c, carry):
        rows = chunk_rows(c)
        cum = la_ref[rows, :]
        tot = cum[0:1] if reverse else cum[GLA_CHUNK - 1:GLA_CHUNK]
        k = k_ref[rows, :]
        qe = q_ref[rows, :] * (GLA_DK ** -0.5) * jnp.exp(cum)
        ke_ref[rows, :] = (k * jnp.exp(-cum)).astype(BF16)
        kl_ref[rows, :] = (k * jnp.exp(tot - cum)).astype(BF16)
        dec_ref[pl.ds(c, 1), :] = jnp.exp(tot)
        qe_ref[rows, :] = qe.astype(BF16)
        for p in range(npair):
            pr = slice(p * LANES, (p + 1) * LANES)
            q2_ref[c * npair + p] = jnp.concatenate([qe[:, pr] * first, qe[:, pr] * (1.0 - first)],
                                                    axis=0).astype(BF16)
        return carry

    def score_pass(c, carry):
        rows = chunk_rows(c)
        for p in range(npair):
            pr = slice(p * LANES, (p + 1) * LANES)
            sc = lax.dot_general(q2_ref[c * npair + p], ke_ref[rows, pr], nt, preferred_element_type=F32)
            sc_ref[c * npair + p] = jnp.where(keep, sc, 0.0).astype(BF16)
        return carry

    def value_pass(c, carry):
        rows = chunk_rows(c)
        for p in range(npair):
            pr = slice(p * LANES, (p + 1) * LANES)
            pv = slice(2 * p * GLA_DV, 2 * (p + 1) * GLA_DV)
            v_p = v_half[p // 2][rows, pl.ds((p % 2) * 2 * GLA_DV, 2 * GLA_DV)].astype(BF16)
            o2 = jnp.dot(sc_ref[c * npair + p], v_p, preferred_element_type=F32)
            ob_ref[rows, pl.ds(2 * p * GLA_DV, GLA_DV)] = o2[:GLA_CHUNK, :GLA_DV]
            ob_ref[rows, pl.ds((2 * p + 1) * GLA_DV, GLA_DV)] = o2[GLA_CHUNK:, GLA_DV:]
            u2 = lax.dot_general(v_p, kl_ref[rows, pr], tn, preferred_element_type=F32)
            upd_ref[c * npair + p] = jnp.where(own, u2, 0.0)
        return carry

    lax.fori_loop(0, nc, cum_pass, 0, unroll=4)
    lax.fori_loop(0, nc, decay_pass, 0)
    lax.fori_loop(0, nc, score_pass, 0, unroll=2)
    lax.fori_loop(0, nc, value_pass, 0, unroll=2)

    def carried(idx, carry):
        c = (nc - 1 - idx) if reverse else idx
        rows = pl.ds(pl.multiple_of(c * GLA_CHUNK, GLA_CHUNK), GLA_CHUNK)
        dec = dec_ref[pl.ds(c, 1), :]
        for p in range(npair):
            pr = slice(p * LANES, (p + 1) * LANES)
            pv = slice(2 * p * GLA_DV, 2 * (p + 1) * GLA_DV)
            st = st_ref[p]
            ob_ref[rows, pv] += lax.dot_general(qe_ref[rows, pr], st.astype(BF16), nt, preferred_element_type=F32)
            st_ref[p] = st * dec[:, pr] + upd_ref[c * npair + p]
        return carry

    lax.fori_loop(0, nc, carried, 0, unroll=2)

    if combine:
        for h in range(GLA_HEADS):
            hv = slice(h * GLA_DV, (h + 1) * GLA_DV)
            o = of_ref[:, hv] + ob_ref[:, hv]
            ms = jnp.mean(o * o, axis=-1, keepdims=True)
            o = o * lax.rsqrt(ms + NORM_EPS) * gout_ref[...]
            o_ref[:, hv] = (o * _silu(gate_ref[:, hv])).astype(o_ref.dtype)

    @pl.when(pl.program_id(0) == pl.num_programs(0) - 1)
    def _():
        sfin_ref[...] = st_ref[...]


_GLA_STATE = (GLA_HEADS // 2, 2 * GLA_DV, LANES)

_COL_K, _COL_V, _COL_Q, _COL_G = 24, 28, 36, 40


def _gla_pass(p, p_r, wg_pad, bg, s0, reverse, of=None, gout=None):
    l = p.shape[0]
    rb = min(l, 512)
    nblk = l // rb
    qk = GLA_HEADS * GLA_DK
    vw = GLA_HEADS * GLA_DV
    combine = of is not None
    rmap = (lambda i: nblk - 1 - i) if reverse else (lambda i: i)

    def cols(width, off128):
        return pl.BlockSpec((rb, width), lambda i: (rmap(i), off128 * LANES // width))

    state = pl.BlockSpec(_GLA_STATE, lambda i: (0, 0, 0))
    in_specs = [cols(qk, _COL_K), cols(qk, _COL_Q), cols(vw // 2, _COL_V), cols(vw // 2, _COL_V + vw // 2 // LANES),
                pl.BlockSpec((rb, LANES), lambda i: (rmap(i), 0)),
                pl.BlockSpec((LANES, qk), lambda i: (0, 0)), pl.BlockSpec((1, qk), lambda i: (0, 0)), state]
    args = [p, p, p, p, p_r, wg_pad, bg.reshape(1, qk), s0]
    nc = rb // GLA_CHUNK
    npair = _GLA_STATE[0]
    scratch = [pltpu.VMEM(_GLA_STATE, F32),
               pltpu.VMEM((rb, qk), F32),
               pltpu.VMEM((rb, qk), BF16),
               pltpu.VMEM((nc * npair,) + _GLA_STATE[1:], F32),
               pltpu.VMEM((nc, qk), F32),
               pltpu.VMEM((rb, qk), BF16),
               pltpu.VMEM((rb, qk), BF16),
               pltpu.VMEM((nc * npair, 2 * GLA_CHUNK, LANES), BF16),
               pltpu.VMEM((nc * npair, 2 * GLA_CHUNK, GLA_CHUNK), BF16)]
    if combine:
        in_specs += [pl.BlockSpec((rb, vw), lambda i: (rmap(i), 0)), cols(vw, _COL_G),
                     pl.BlockSpec((1, GLA_DV), lambda i: (0, 0))]
        args += [of, p, gout.reshape(1, GLA_DV)]
        scratch.append(pltpu.VMEM((rb, vw), F32))
    out_dtype = BF16 if combine else F32
    return pl.pallas_call(
        functools.partial(_gla_kernel, reverse=reverse, combine=combine),
        grid=(nblk,),
        in_specs=in_specs,
        out_specs=[pl.BlockSpec((rb, vw), lambda i: (rmap(i), 0)), state],
        out_shape=[jax.ShapeDtypeStruct((l, vw), out_dtype),
                   jax.ShapeDtypeStruct(_GLA_STATE, F32)],
        scratch_shapes=scratch,
        compiler_params=_params("arbitrary"),
        name="gla_bwd" if reverse else "gla_fwd",
    )(*args)


def _out_proj_kernel(hy_ref, o_ref, w_ref, x_ref, ga_ref, g_ref, sh_ref, sc_ref, xo_ref, h_ref):
    q = hy_ref.shape[1]
    hy_a, hy_b = _unpack_pair(hy_ref[...])
    mix = (jnp.dot(hy_a, w_ref[:q, :], preferred_element_type=F32)
           + jnp.dot(hy_b, w_ref[q:2 * q, :], preferred_element_type=F32)
           + jnp.dot(o_ref[...], w_ref[2 * q:, :], preferred_element_type=F32))
    xn = x_ref[...] + ga_ref[...] * mix
    xo_ref[...] = xn
    h_ref[...] = _norm_mod(xn, g_ref[...], sh_ref[...], sc_ref[...]).astype(BF16)


def _out_proj(hy, o, w, x, ga, g, shift, scale, layer):
    m, d = x.shape
    tm = min(m, 256)
    vec = pl.BlockSpec((1, d), lambda i: (0, 0))
    row = lambda width: pl.BlockSpec((tm, width), lambda i: (i, 0))
    return pl.pallas_call(
        _out_proj_kernel,
        grid=(m // tm,),
        in_specs=[row(hy.shape[1]), row(o.shape[1]), pl.BlockSpec((None, w.shape[1], d), lambda i: (layer, 0, 0)),
                  row(d), vec, vec, vec, vec],
        out_specs=[row(d), row(d)],
        out_shape=[jax.ShapeDtypeStruct((m, d), F32), jax.ShapeDtypeStruct((m, d), BF16)],
        compiler_params=_params("parallel"),
        name="out_proj",
    )(hy, o, w, x, ga, g, shift, scale)


_HALO = GRID_W + SUBLANES


def _ffn_up_conv_kernel(x_ref, wa_ref, wu_ref, cw_ref, o_ref, a_ref, u_ref, wb_ref, *, vertical, cols, nm):
    t = pl.program_id(0)
    total = pl.num_programs(0) - 1
    i = lax.rem(t, nm)
    tt = x_ref.shape[0]
    cur = lax.rem(t, 2)
    prv = 1 - cur

    @pl.when(t == 0)
    def _():
        a_ref[...] = jnp.zeros_like(a_ref)
        u_ref[...] = jnp.zeros_like(u_ref)

    def conv_previous_tile():
        col = lax.broadcasted_iota(jnp.int32, o_ref.shape, 0) & (cols - 1)
        acc = None
        for dc in (-1, 0, 1):
            part = None
            for dr in ((-1, 0, 1) if vertical else (0,)):
                tap = cw_ref[3 * (dr + 1) + (dc + 1):3 * (dr + 1) + (dc + 1) + 1, :]
                term = tap * a_ref[prv, pl.ds(_HALO + GRID_W * dr + dc, tt), :]
                part = term if part is None else part + term
            if dc == -1:
                part = jnp.where(col == 0, 0.0, part)
            elif dc == 1:
                part = jnp.where(col == cols - 1, 0.0, part)
            acc = part if acc is None else acc + part
        o_ref[...] = (_silu(acc) * u_ref[prv].astype(F32)).astype(o_ref.dtype)

    @pl.when((i == 0) & (t < total))
    def _():
        wb_ref[0] = wa_ref[...].astype(BF16)
        wb_ref[1] = wu_ref[...].astype(BF16)

    @pl.when(t < total)
    def _():
        x = x_ref[...]
        a_new = jnp.dot(x, wb_ref[0], preferred_element_type=F32)
        u_ref[cur] = jnp.dot(x, wb_ref[1], preferred_element_type=F32).astype(BF16)
        upper = a_ref[prv, pl.ds(_HALO + tt - GRID_W, GRID_W), :]
        a_ref[cur, pl.ds(SUBLANES, GRID_W), :] = jnp.where(i == 0, 0.0, upper)
        a_ref[cur, pl.ds(_HALO, tt), :] = a_new
        a_ref[prv, pl.ds(_HALO + tt, GRID_W), :] = jnp.where(i == 0, 0.0, a_new[:GRID_W])
        conv_previous_tile()

    @pl.when(t == total)
    def _():
        a_ref[prv, pl.ds(_HALO + tt, GRID_W), :] = jnp.zeros((GRID_W, a_ref.shape[2]), F32)
        conv_previous_tile()


def _ffn_up_conv(h, w_up, conv_w, cols, layer):
    m, d = h.shape
    dff = conv_w.shape[-1]
    vertical = m > cols
    assert cols & (cols - 1) == 0 and (cols == GRID_W or not vertical)
    tt = min(m, 1024)
    assert tt % cols == 0
    cb = 512
    ncb = dff // cb
    nm = m // tt
    total = ncb * nm
    cur_t = lambda t: jnp.minimum(t, total - 1)
    prev_t = lambda t: jnp.maximum(t - 1, 0)
    return pl.pallas_call(
        functools.partial(_ffn_up_conv_kernel, vertical=vertical, cols=cols, nm=nm),
        grid=(total + 1,),
        in_specs=[pl.BlockSpec((tt, d), lambda t: (cur_t(t) % nm, 0)),
                  pl.BlockSpec((None, d, cb), lambda t: (layer, 0, cur_t(t) // nm)),
                  pl.BlockSpec((None, d, cb), lambda t: (layer, 0, ncb + cur_t(t) // nm)),
                  pl.BlockSpec((9, cb), lambda t: (0, prev_t(t) // nm))],
        out_specs=pl.BlockSpec((tt, cb), lambda t: (prev_t(t) % nm, prev_t(t) // nm)),
        out_shape=jax.ShapeDtypeStruct((m, dff), BF16),
        scratch_shapes=[pltpu.VMEM((2, tt + 2 * _HALO, cb), F32), pltpu.VMEM((2, tt, cb), BF16),
                        pltpu.VMEM((2, d, cb), BF16)],
        compiler_params=_params("arbitrary"),
        name="ffn_up_conv",
    )(h, w_up, w_up, conv_w.reshape(9, dff))


def _ffn_down_kernel(g_ref, w_ref, x_ref, gate_ref, ng_ref, nsh_ref, nsc_ref, o_ref, *h_ref, post):
    xn = x_ref[...] + gate_ref[...] * jnp.dot(g_ref[...], w_ref[...], preferred_element_type=F32)
    if post == 'final':
        ms = jnp.mean(xn * xn, axis=-1, keepdims=True)
        xn = xn * lax.rsqrt(ms + NORM_EPS) * ng_ref[...]
    o_ref[...] = xn
    if post == 'next':
        h_ref[0][...] = _norm_mod(xn, ng_ref[...], nsh_ref[...], nsc_ref[...]).astype(BF16)


def _ffn_down(g, w, x, gate, norm, post, layer):
    m, d = x.shape
    dff = g.shape[1]
    tm = min(m, 256)
    vec = pl.BlockSpec((1, d), lambda i: (0, 0))
    row = pl.BlockSpec((tm, d), lambda i: (i, 0))
    out_specs, out_shape = [row], [jax.ShapeDtypeStruct((m, d), F32)]
    if post == 'next':
        out_specs.append(row)
        out_shape.append(jax.ShapeDtypeStruct((m, d), BF16))
    return pl.pallas_call(
        functools.partial(_ffn_down_kernel, post=post),
        grid=(m // tm,),
        in_specs=[pl.BlockSpec((tm, dff), lambda i: (i, 0)),
                  _single((None, dff, d), lambda i: (layer, 0, 0)),
                  row, vec, vec, vec, vec],
        out_specs=out_specs,
        out_shape=out_shape,
        compiler_params=_params("parallel"),
        name="ffn_down",
    )(g, w, x, gate, *norm)


def _regroup_w_in(w_in_t, hy_proj):
    qk = GLA_HEADS * GLA_DK
    vw = GLA_HEADS * GLA_DV
    head = hy_proj + qk + vw
    r = w_in_t[:, head:head + 2 * GLA_RANK]
    tail = w_in_t[:, head + 2 * GLA_RANK:]
    pad = jnp.zeros((w_in_t.shape[0], LANES - 2 * GLA_RANK, w_in_t.shape[2]), w_in_t.dtype)
    return head, tail, jnp.concatenate([r, pad], axis=1)


def _pad_gate_w(wg, lane0):
    return jnp.zeros((LANES, wg.shape[1]), F32).at[lane0:lane0 + GLA_RANK].set(wg)


def _mixer(p, p_r, lp, s0_f, s0_b, want_output):
    l = p.shape[0]
    width = lp['hy_bias'].shape[0]
    o_f, s_f = _gla_pass(p, p_r, lp['wg_f_pad'], lp['gla_bg_f'], s0_f, reverse=False)
    o, s_b = _gla_pass(p, p_r, lp['wg_b_pad'], lp['gla_bg_b'], s0_b, reverse=True, of=o_f, gout=lp['gla_out_g'])
    if not want_output:
        return None, None, s_f, s_b
    zx = _hy_pre(p, lp['hy_conv'], width)
    if 2 * l // DFT_B >= 2 * SUBLANES:
        hy = _hyena_long(zx, lp, lp['hy_bias'], lp['hy_out_g'])
    else:
        k2 = _hyena_filter2(l, lp['hy_w1'], lp['hy_b1'], lp['hy_freq'], lp['hy_w2'], lp['hy_b2'], lp['hy_w3'], width)
        hy = _hyena_short(zx, k2, lp['hy_bias'], lp['hy_out_g'])
    return hy, o, s_f, s_b


def _order_w_out(w_out, hy_width):
    depth, _, d = w_out.shape
    hy_rows = w_out[:, :hy_width].reshape(depth, hy_width // (2 * LANES), 2, LANES, d)
    hy_rows = jnp.swapaxes(hy_rows, 1, 2).reshape(depth, hy_width, d)
    return jnp.concatenate([hy_rows, w_out[:, hy_width:]], axis=1).astype(BF16)


def _ffn(h2, x, gate, lp, cols, norm, post, layer):
    gact = _ffn_up_conv(h2, lp['ffn_w_up'], lp['ffn_conv'], cols, layer)
    return _ffn_down(gact, lp['ffn_w_down'], x, gate, norm, post, layer)


def kernel(x, c, ctx, c_ctx, w_mod, b_mod, g_attn, w_in, hy_conv, hy_w1, hy_b1, hy_freq, hy_w2, hy_b2, hy_w3,
           hy_bias, hy_out_g, gla_wg_f, gla_bg_f, gla_wg_b, gla_bg_b, gla_out_g, w_out, g_ffn, ffn_w_up,
           ffn_conv, ffn_w_down, g_final):
    depth = w_mod.shape[0]
    batch, seq, d = x.shape
    assert batch == 1 and c.shape[0] == 1 and ctx.shape[0] == 1
    hy_proj = hy_conv.shape[-1]
    xs = x[0]
    cs = ctx[0]
    cc = jnp.zeros((SUBLANES, d), F32).at[0].set(c[0]).at[1].set(c_ctx)
    zero_state = jnp.zeros(_GLA_STATE, F32)
    gfin = g_final.reshape(1, d)
    w_in = jnp.swapaxes(w_in, 1, 2)
    n_head_cols, w_tail, w_rank = _regroup_w_in(w_in, hy_proj)
    w_out_b = _order_w_out(w_out, hy_bias.shape[-1])
    w_down_b = ffn_w_down.astype(BF16)
    mods =[_adaln(cc, w_mod, b_mod, l) for l in range(depth)]

    def attn_norm(l, row):
        return (g_attn[l].reshape(1, d), mods[l][row:row + 1, 0:d], mods[l][row:row + 1, d:2 * d])

    hx = _norm_mod_call(xs, *attn_norm(0, 0))
    hc = _norm_mod_call(cs, *attn_norm(0, 1))
    for l in range(depth):
        last = l == depth - 1
        lp = {
            'hy_conv': hy_conv[l], 'hy_w1': hy_w1[l], 'hy_b1': hy_b1[l], 'hy_freq': hy_freq[l],
            'hy_w2': hy_w2[l], 'hy_b2': hy_b2[l], 'hy_w3': hy_w3[l], 'hy_bias': hy_bias[l],
            'hy_out_g': hy_out_g[l], 'gla_bg_f': gla_bg_f[l], 'gla_bg_b': gla_bg_b[l],
            'gla_out_g': gla_out_g[l],
            'wg_f_pad': _pad_gate_w(gla_wg_f[l], 0), 'wg_b_pad': _pad_gate_w(gla_wg_b[l], GLA_RANK),
            'ffn_w_up': ffn_w_up, 'ffn_conv': ffn_conv[l], 'ffn_w_down': w_down_b,
        }
        mod = mods[l]
        _, _, ga, sf, scf, gf = [mod[0:1, i * d:(i + 1) * d] for i in range(6)]
        _, _, cga, csf, cscf, cgf = [mod[1:2, i * d:(i + 1) * d] for i in range(6)]
        gf_row = g_ffn[l].reshape(1, d)

        pc = _in_proj(hc, w_in, w_tail, n_head_cols, l)
        hy_c, o_c, s_f, s_b = _mixer(pc, _gate_rank_proj(hc, w_rank, l), lp, zero_state, zero_state,
                                     want_output=not last)
        if not last:
            cs, h2c = _out_proj(hy_c, o_c, w_out_b, cs, cga, gf_row, csf, cscf, l)
            cs, hc = _ffn(h2c, cs, cgf, lp, cs.shape[0], attn_norm(l + 1, 1), 'next', l)

        px = _in_proj(hx, w_in, w_tail, n_head_cols, l)
        hy_x, o_x, _, _ = _mixer(px, _gate_rank_proj(hx, w_rank, l), lp, s_f, s_b, want_output=True)
        xs, h2 = _out_proj(hy_x, o_x, w_out_b, xs, ga, gf_row, sf, scf, l)
        if last:
            (xs,) = _ffn(h2, xs, gf, lp, GRID_W, (gfin, gfin, gfin), 'final', l)
        else:
            xs, hx = _ffn(h2, xs, gf, lp, GRID_W, attn_norm(l + 1, 0), 'next', l)
    return xs[None]
```

```python
import functools
import math

import numpy as np
import jax
import jax.numpy as jnp
from jax import lax
from jax.experimental import pallas as pl
from jax.experimental.pallas import tpu as pltpu

F32 = jnp.float32
BF16 = jnp.bfloat16
HIGHEST = lax.Precision.HIGHEST

LANES = 128
SUBLANES = 8
VMEM_BYTES_V7X = 64 * 1024 * 1024
VMEM_LIMIT = VMEM_BYTES_V7X - 12 * 1024 * 1024

NORM_EPS = 1e-6
HYENA_GROUP = 128
FILTER_BANDS = 16
FILTER_HIDDEN = 64
FILTER_TARGET = 1e-2
FILTER_FAST_PCT = 0.3
FILTER_SLOW_PCT = 1.5
GLA_HEADS = 8
GLA_DK = 64
GLA_DV = 128
GLA_RANK = 16
GLA_GATE_TAU = 16.0
GLA_CHUNK = 64
GRID_W = 64
DFT_B = 128


def _params(*sem):
    return pltpu.CompilerParams(dimension_semantics=sem, vmem_limit_bytes=VMEM_LIMIT)


def _silu(x):
    return x * jax.nn.sigmoid(x)


def _adaln_kernel(c_ref, w_ref, b_ref, o_ref):
    s = _silu(c_ref[...]).astype(BF16)
    o_ref[...] = jnp.dot(s, w_ref[...].astype(BF16), preferred_element_type=F32) + b_ref[...]


def _adaln(cc, w_mod, b_mod, layer):
    depth, d, n = w_mod.shape
    tn = 1024
    return pl.pallas_call(
        _adaln_kernel,
        grid=(n // tn,),
        in_specs=[pl.BlockSpec((SUBLANES, d), lambda j: (0, 0)),
                  pl.BlockSpec((None, d, tn), lambda j: (layer, 0, j)),
                  pl.BlockSpec((None, 1, tn), lambda j: (layer, 0, j))],
        out_specs=pl.BlockSpec((SUBLANES, tn), lambda j: (0, j)),
        out_shape=jax.ShapeDtypeStruct((SUBLANES, n), F32),
        compiler_params=_params("arbitrary"),
        name="adaln",
    )(cc, w_mod, b_mod.reshape(depth, 1, n))


def _norm_mod(x, g, shift, scale):
    ms = jnp.mean(x * x, axis=-1, keepdims=True)
    return (x * lax.rsqrt(ms + NORM_EPS) * g) * (1.0 + scale) + shift


def _norm_mod_kernel(x_ref, g_ref, sh_ref, sc_ref, o_ref):
    o_ref[...] = _norm_mod(x_ref[...], g_ref[...], sh_ref[...], sc_ref[...]).astype(o_ref.dtype)


def _norm_mod_call(x, g, shift, scale):
    m, d = x.shape
    tm = min(m, 512)
    vec = pl.BlockSpec((1, d), lambda i: (0, 0))
    return pl.pallas_call(
        _norm_mod_kernel,
        grid=(m // tm,),
        in_specs=[pl.BlockSpec((tm, d), lambda i: (i, 0)), vec, vec, vec],
        out_specs=pl.BlockSpec((tm, d), lambda i: (i, 0)),
        out_shape=jax.ShapeDtypeStruct((m, d), BF16),
        compiler_params=_params("parallel"),
        name="norm_mod",
    )(x, g, shift, scale)


_IN_TILE = 768


_NT = (((1,), (1,)), ((), ()))


def _in_proj_kernel(h_ref, w_ref, wt_ref, o_ref, wb_ref, *, n_head):
    j = pl.program_id(0)

    @pl.when((pl.program_id(1) == 0) & (j < n_head))
    def _():
        wb_ref[...] = w_ref[...].T.astype(BF16)

    @pl.when((pl.program_id(1) == 0) & (j >= n_head))
    def _():
        wb_ref[...] = wt_ref[...].T.astype(BF16)

    o_ref[...] = jnp.dot(h_ref[...], wb_ref[...], preferred_element_type=F32)


def _in_proj(h, w_in_t, w_tail_t, n_head_cols, layer):
    m, d = h.shape
    tn = _IN_TILE
    n_head = n_head_cols // tn
    n_tail = w_tail_t.shape[1] // tn
    assert n_head * tn == n_head_cols and n_tail * tn == w_tail_t.shape[1]
    tm = min(m, 1024)
    return pl.pallas_call(
        functools.partial(_in_proj_kernel, n_head=n_head),
        grid=(n_head + n_tail, m // tm),
        in_specs=[pl.BlockSpec((tm, d), lambda j, i: (i, 0)),
                  _single((None, tn, d), lambda j, i: (layer, jnp.minimum(j, n_head - 1), 0)),
                  _single((None, tn, d), lambda j, i: (layer, jnp.maximum(j - n_head, 0), 0))],
        out_specs=pl.BlockSpec((tm, tn), lambda j, i: (i, j)),
        out_shape=jax.ShapeDtypeStruct((m, n_head_cols + w_tail_t.shape[1]), F32),
        scratch_shapes=[pltpu.VMEM((d, tn), BF16)],
        compiler_params=_params("arbitrary", "arbitrary"),
        name="in_proj",
    )(h, w_in_t, w_tail_t)


def _gate_rank_kernel(h_ref, w_ref, o_ref):
    o_ref[...] = lax.dot_general(h_ref[...], w_ref[...].astype(BF16), _NT, preferred_element_type=F32)


def _gate_rank_proj(h, w_r_t, layer):
    m, d = h.shape
    tm = min(m, 2048)
    return pl.pallas_call(
        _gate_rank_kernel,
        grid=(m // tm,),
        in_specs=[pl.BlockSpec((tm, d), lambda i: (i, 0)), pl.BlockSpec((None, LANES, d), lambda i: (layer, 0, 0))],
        out_specs=pl.BlockSpec((tm, LANES), lambda i: (i, 0)),
        out_shape=jax.ShapeDtypeStruct((m, LANES), F32),
        compiler_params=_params("parallel"),
        name="gate_rank_proj",
    )(h, w_r_t)


def _hy_pre_kernel(x0_ref, x1_ref, v_ref, x0p_ref, x1p_ref, vp_ref, x0n_ref, x1n_ref, vn_ref,
                   w0_ref, w1_ref, wv_ref, zx_ref):
    i = pl.program_id(0)
    first = i == 0
    last = i == pl.num_programs(0) - 1
    tt = x0_ref.shape[0]
    row = lax.broadcasted_iota(jnp.int32, x0_ref.shape, 0)

    def conv(ref, prev_ref, next_ref, w_ref):
        x = ref[...]
        prev_row = jnp.where(first, 0.0, prev_ref[SUBLANES - 1:SUBLANES, :])
        next_row = jnp.where(last, 0.0, next_ref[0:1, :])
        below = jnp.where(row == 0, prev_row, pltpu.roll(x, 1, 0))
        above = jnp.where(row == tt - 1, next_row, pltpu.roll(x, tt - 1, 0))
        return w_ref[0:1, :] * below + w_ref[1:2, :] * x + w_ref[2:3, :] * above

    z = conv(v_ref, vp_ref, vn_ref, wv_ref) * conv(x1_ref, x1p_ref, x1n_ref, w1_ref)
    zx_ref[...] = _pack_pair(z, conv(x0_ref, x0p_ref, x0n_ref, w0_ref))


def _hy_pre(p, hy_conv, width):
    l = p.shape[0]
    tt = min(l, 1024)
    cb = 512
    ncb = width // cb
    nb8 = l // SUBLANES
    t8 = tt // SUBLANES

    def main(g):
        return pl.BlockSpec((tt, cb), lambda i, j: (i, g * ncb + j))

    def prev(g):
        return pl.BlockSpec((SUBLANES, cb), lambda i, j: (jnp.maximum(i * t8 - 1, 0), g * ncb + j))

    def nxt(g):
        return pl.BlockSpec((SUBLANES, cb), lambda i, j: (jnp.minimum((i + 1) * t8, nb8 - 1), g * ncb + j))

    def wspec(g):
        return pl.BlockSpec((3, cb), lambda i, j: (0, g * ncb + j))

    out = pl.BlockSpec((tt, cb), lambda i, j: (i, j))
    return pl.pallas_call(
        _hy_pre_kernel,
        grid=(l // tt, ncb),
        in_specs=[main(0), main(1), main(2), prev(0), prev(1), prev(2), nxt(0), nxt(1), nxt(2),
                  wspec(0), wspec(1), wspec(2)],
        out_specs=out,
        out_shape=jax.ShapeDtypeStruct((l, width), jnp.uint32),
        compiler_params=_params("parallel", "parallel"),
        name="hy_pre",
    )(p, p, p, p, p, p, p, p, p, hy_conv, hy_conv, hy_conv)


def _filter_hidden(j, seq_len, w1t_ref, w1c_ref, w1s_ref, b1_ref, fr_ref, w2_ref, b2_ref):
    pos_i = jnp.where(j < seq_len, j, 2 * seq_len - j)
    pos = pos_i.astype(F32)
    t = pos / float(max(seq_len - 1, 1))
    lane = lax.broadcasted_iota(jnp.int32, j.shape, 1)
    band = jnp.where(lane < FILTER_BANDS, lane + 1, 0).astype(F32)
    ang = (2.0 * math.pi / seq_len) * pos * band
    fr = fr_ref[...]
    pre = (t[:, 0:1] * w1t_ref[...]
           + jnp.dot(jnp.cos(ang), w1c_ref[...], precision=HIGHEST, preferred_element_type=F32)
           + jnp.dot(jnp.sin(ang), w1s_ref[...], precision=HIGHEST, preferred_element_type=F32)
           + b1_ref[...])
    h = jnp.sin(fr * pre)
    h = jnp.sin(fr * (jnp.dot(h, w2_ref[...], precision=HIGHEST, preferred_element_type=F32) + b2_ref[...]))
    return h, pos_i, t


def _filter_kernel(w1t_ref, w1c_ref, w1s_ref, b1_ref, fr_ref, w2_ref, b2_ref, w3_ref, delta_ref, o_ref, *, seq_len):
    tr = o_ref.shape[0]
    j = pl.program_id(0) * tr + lax.broadcasted_iota(jnp.int32, (tr, LANES), 0)
    h, pos_i, t = _filter_hidden(j, seq_len, w1t_ref, w1c_ref, w1s_ref, b1_ref, fr_ref, w2_ref, b2_ref)
    hh = jnp.dot(h.astype(BF16), w3_ref[...].astype(BF16), preferred_element_type=F32)
    decay = jnp.exp(-t[:, 0:1] * delta_ref[...])
    valid = pos_i[:, 0:1] != seq_len
    o_ref[...] = jnp.where(valid, hh * decay, 0.0)


_FILTER_PACK = LANES // FILTER_BANDS


def _filter_hidden_kernel(w1t_ref, w1c_ref, w1s_ref, b1_ref, fr_ref, w2_ref, b2_ref, o_ref, *, seq_len, n_a):
    tp = o_ref.shape[0]
    shift = n_a.bit_length() - 1

    def position(shape, lanes_per_pos):
        prow = pl.program_id(0) * tp + lax.broadcasted_iota(jnp.int32, shape, 0)
        r = prow * _FILTER_PACK + lax.broadcasted_iota(jnp.int32, shape, 1) // lanes_per_pos
        j = ((r & (n_a - 1)) * DFT_B) + (r >> shift)
        return jnp.where(j < seq_len, j, 2 * seq_len - j).astype(F32)

    pos = position((tp, LANES), FILTER_BANDS)
    band = (lax.broadcasted_iota(jnp.int32, (tp, LANES), 1) % FILTER_BANDS + 1).astype(F32)
    ang = (2.0 * math.pi / seq_len) * pos * band
    t = position(o_ref.shape, FILTER_HIDDEN) / float(max(seq_len - 1, 1))
    fr = fr_ref[...]
    pre = (t * w1t_ref[...]
           + jnp.dot(jnp.cos(ang), w1c_ref[...], precision=HIGHEST, preferred_element_type=F32)
           + jnp.dot(jnp.sin(ang), w1s_ref[...], precision=HIGHEST, preferred_element_type=F32)
           + b1_ref[...])
    h = jnp.sin(fr * pre)
    o_ref[...] = jnp.sin(fr * (jnp.dot(h, w2_ref[...], precision=HIGHEST, preferred_element_type=F32) + b2_ref[...]))


def _filter_mlp_args(w1, b1, freq, w2, b2):
    hid = FILTER_HIDDEN
    w1c = jnp.zeros((LANES, hid), F32).at[:FILTER_BANDS].set(w1[1:1 + FILTER_BANDS])
    w1s = jnp.zeros((LANES, hid), F32).at[:FILTER_BANDS].set(w1[1 + FILTER_BANDS:1 + 2 * FILTER_BANDS])
    full = lambda shape: pl.BlockSpec(shape, lambda i: (0, 0))
    specs = [full((1, hid)), full((LANES, hid)), full((LANES, hid)), full((1, hid)), full((1, hid)),
             full((hid, hid)), full((1, hid))]
    args = (w1[0:1], w1c, w1s, b1.reshape(1, hid), freq.reshape(1, hid), w2, b2.reshape(1, hid))
    return specs, args


def _filter_delta(width):
    return jnp.abs(jnp.linspace(math.log(FILTER_TARGET) / FILTER_SLOW_PCT,
                                math.log(FILTER_TARGET) / FILTER_FAST_PCT, width, dtype=F32)).reshape(1, width)


def _hyena_filter2(seq_len, w1, b1, freq, w2, b2, w3, width):
    n2 = 2 * seq_len
    tr = min(seq_len, 1024)
    nfwd = seq_len // tr
    specs, args = _filter_mlp_args(w1, b1, freq, w2, b2)
    return pl.pallas_call(
        functools.partial(_filter_kernel, seq_len=seq_len),
        grid=(n2 // tr,),
        in_specs=specs + [pl.BlockSpec((FILTER_HIDDEN, width), lambda i: (0, jnp.where(i < nfwd, 0, 1))),
                          pl.BlockSpec((1, width), lambda i: (0, 0))],
        out_specs=pl.BlockSpec((tr, width), lambda i: (i, 0)),
        out_shape=jax.ShapeDtypeStruct((n2, width), F32),
        compiler_params=_params("parallel"),
        name="hyena_filter",
    )(*args, w3, _filter_delta(width))


def _filter_hidden_permuted(seq_len, w1, b1, freq, w2, b2):
    n2 = 2 * seq_len
    n_a = n2 // DFT_B
    hid = FILTER_HIDDEN
    wide = _FILTER_PACK * hid
    rows = n2 // _FILTER_PACK
    tp = min(rows, 256)
    eye = jnp.eye(_FILTER_PACK, dtype=F32)
    tile = lambda v: jnp.tile(v.reshape(1, hid), (1, _FILTER_PACK))
    args = (tile(w1[0]), jnp.kron(eye, w1[1:1 + FILTER_BANDS]),
            jnp.kron(eye, w1[1 + FILTER_BANDS:1 + 2 * FILTER_BANDS]), tile(b1), tile(freq),
            jnp.kron(eye, w2), tile(b2))
    packed = pl.pallas_call(
        functools.partial(_filter_hidden_kernel, seq_len=seq_len, n_a=n_a),
        grid=(rows // tp,),
        in_specs=[pl.BlockSpec(a.shape, lambda i: (0, 0)) for a in args],
        out_specs=pl.BlockSpec((tp, wide), lambda i: (i, 0)),
        out_shape=jax.ShapeDtypeStruct((rows, wide), F32),
        compiler_params=_params("parallel"),
        name="hyena_filter_hidden",
    )(*args)
    return packed.reshape(n2, hid)


def _kept_k1(n_a):
    kept = n_a // 2 + 1
    return kept, -(-kept // SUBLANES) * SUBLANES


def _dft_tables(n_a, a_in):
    n = n_a * DFT_B
    kept, padded = _kept_k1(n_a)
    b = np.arange(DFT_B)[:, None, None]
    k1 = np.arange(padded)[None, :, None]
    a = np.arange(a_in)[None, None, :]
    e = np.exp(-2j * np.pi * ((k1 * (DFT_B * a + b)) % n) / n) * (k1 < kept)
    g = np.concatenate([e.real, e.imag], axis=1)
    return jnp.asarray(g, F32)


def _idft_tables(n_a, a_out):
    n = n_a * DFT_B
    kept, padded = _kept_k1(n_a)
    b = np.arange(DFT_B)[:, None, None]
    a = np.arange(a_out)[None, :, None]
    k1 = np.arange(padded)[None, None, :]
    weight = np.where((k1 == 0) | (k1 == n_a // 2), 1.0, 2.0) * (k1 < kept)
    e = np.exp(2j * np.pi * ((k1 * (DFT_B * a + b)) % n) / n) * weight / n
    h = np.concatenate([e.real, -e.imag], axis=2)
    return jnp.asarray(h, F32)


def _dft128_embed():
    i = np.arange(DFT_B)
    f = np.exp(-2j * np.pi * ((i[:, None] * i[None, :]) % DFT_B) / DFT_B)
    fwd = np.block([[f.real, -f.imag], [f.imag, f.real]])
    inv = np.block([[f.real, f.imag], [-f.imag, f.real]])
    return jnp.asarray(fwd, F32), jnp.asarray(inv, F32)


_PITCH = DFT_B + SUBLANES
_K1_BLOCK = 8
_B_UNROLL = 16


def _stage_rows(k1):
    return pl.ds(pl.multiple_of(k1 * _PITCH, SUBLANES), DFT_B)


def _single(shape, index_map):
    return pl.BlockSpec(shape, index_map, pipeline_mode=pl.Buffered(1))


def _pack_pair(re, im):
    hi = lax.bitcast_convert_type(re.astype(BF16).astype(F32), jnp.uint32)
    lo = lax.bitcast_convert_type(im.astype(BF16).astype(F32), jnp.uint32)
    return hi | (lo >> 16)


def _unpack_pair(w):
    re = lax.bitcast_convert_type(w & jnp.uint32(0xFFFF0000), F32)
    im = lax.bitcast_convert_type(w << 16, F32)
    return re.astype(BF16), im.astype(BF16)


def _spectrum_kernel(h_ref, w3f_ref, w3b_ref, delta_ref, g_ref, wf_ref, o_ref, s0_ref, s1_ref, *, seq_len, n_a):
    s = pl.program_id(1)
    half = n_a // 2
    cw = o_ref.shape[3]

    @pl.when(s == 0)
    def _():
        a_i = lax.broadcasted_iota(jnp.int32, (n_a, cw), 0)

        def body(b, carry):
            base = pl.multiple_of(b * n_a, SUBLANES)
            hf = jnp.dot(h_ref[pl.ds(base, half), :].astype(BF16), w3f_ref[...].astype(BF16),
                         preferred_element_type=F32)
            hb = jnp.dot(h_ref[pl.ds(base + half, half), :].astype(BF16), w3b_ref[...].astype(BF16),
                         preferred_element_type=F32)
            hh = jnp.concatenate([hf, hb], axis=0)
            j = a_i * DFT_B + b
            pos_i = jnp.where(j < seq_len, j, 2 * seq_len - j)
            t = pos_i.astype(F32) / float(max(seq_len - 1, 1))
            k2 = jnp.where(pos_i != seq_len, hh * jnp.exp(-t * delta_ref[...]), 0.0)
            r = jnp.dot(g_ref[b], k2.astype(BF16), preferred_element_type=F32)
            nk = r.shape[0] // 2
            w = _pack_pair(r[:nk], r[nk:])
            s0_ref[pl.ds(b, nk, stride=_PITCH), :] = w[:, :LANES]
            s1_ref[pl.ds(b, nk, stride=_PITCH), :] = w[:, LANES:]
            return carry

        lax.fori_loop(0, DFT_B, body, 0, unroll=_B_UNROLL)

    @pl.when(s > 0)
    def _():
        k0 = (s - 1) * o_ref.shape[1]
        for i in range(o_ref.shape[1]):
            rows = _stage_rows(k0 + i)
            re, im = _unpack_pair(jnp.concatenate([s0_ref[rows, :], s1_ref[rows, :]], axis=1))
            z = jnp.dot(wf_ref[...], jnp.concatenate([re, im], axis=0), preferred_element_type=F32)
            o_ref[0, i] = z[:DFT_B]
            o_ref[1, i] = z[DFT_B:]


def _filter_spectrum(hperm, w3, seq_len, width, g_k, wf):
    n_a = 2 * seq_len // DFT_B
    nk = g_k.shape[1] // 2
    kb = min(nk, _K1_BLOCK)
    cw = 2 * LANES
    ncb = width // cw
    hid = FILTER_HIDDEN
    return pl.pallas_call(
        functools.partial(_spectrum_kernel, seq_len=seq_len, n_a=n_a),
        grid=(ncb, 1 + nk // kb),
        in_specs=[_single(hperm.shape, lambda ci, s: (0, 0)),
                  pl.BlockSpec((hid, cw), lambda ci, s: (0, ci)),
                  pl.BlockSpec((hid, cw), lambda ci, s: (0, ncb + ci)),
                  pl.BlockSpec((1, cw), lambda ci, s: (0, ci)),
                  _single(g_k.shape, lambda ci, s: (0, 0, 0)),
                  _single(wf.shape, lambda ci, s: (0, 0))],
        out_specs=pl.BlockSpec((2, kb, DFT_B, cw), lambda ci, s: (0, jnp.maximum(s - 1, 0), 0, ci)),
        out_shape=jax.ShapeDtypeStruct((2, nk, DFT_B, width), F32),
        scratch_shapes=[pltpu.VMEM((nk * _PITCH, LANES), jnp.uint32), pltpu.VMEM((nk * _PITCH, LANES), jnp.uint32)],
        compiler_params=_params("parallel", "arbitrary"),
        name="filter_spectrum",
    )(hperm, w3, w3, _filter_delta(width), g_k, wf)


def _hyena_epilogue(y, z, x0, bias, gain):
    yv = (y + z * bias) * x0
    ms = jnp.mean(yv * yv, axis=-1, keepdims=True)
    return yv * lax.rsqrt(ms + NORM_EPS) * gain


def _pack_groups(y, zx, bias, gain):
    z, x0 = _unpack_pair(zx)
    out = _hyena_epilogue_groups(y, z.astype(F32), x0.astype(F32), bias, gain)
    return _pack_pair(out[:, :LANES], out[:, LANES:])


def _hyena_epilogue_groups(y, z, x0, bias, gain):
    return jnp.concatenate([_hyena_epilogue(y[:, g * LANES:(g + 1) * LANES], z[:, g * LANES:(g + 1) * LANES],
                                            x0[:, g * LANES:(g + 1) * LANES], bias[:, g * LANES:(g + 1) * LANES],
                                            gain[:, g * LANES:(g + 1) * LANES])
                            for g in range(y.shape[1] // LANES)], axis=1)


def _long_conv_kernel(zx0_ref, zx1_ref, kf_ref, g_ref, h_ref, wf_ref, wi_ref, bias_ref, gain_ref, o_ref,
                      s0_ref, s1_ref, *, n_a):
    s = pl.program_id(1)
    half = n_a // 2
    nk = g_ref.shape[1] // 2
    kb = kf_ref.shape[1]
    nkb = nk // kb

    def load_zx(b):
        return jnp.concatenate([zx0_ref[pl.ds(b, half, stride=DFT_B), :],
                                zx1_ref[pl.ds(b, half, stride=DFT_B), :]], axis=1)

    def load_stage(rows):
        return _unpack_pair(jnp.concatenate([s0_ref[rows, :], s1_ref[rows, :]], axis=1))

    def store_stage(rows, re, im):
        w = _pack_pair(re, im)
        s0_ref[rows, :] = w[:, :LANES]
        s1_ref[rows, :] = w[:, LANES:]

    @pl.when(s == 0)
    def _():
        def body(b, carry):
            z, _ = _unpack_pair(load_zx(b))
            r = jnp.dot(g_ref[b], z, preferred_element_type=F32)
            store_stage(pl.ds(b, nk, stride=_PITCH), r[:nk], r[nk:])
            return carry

        lax.fori_loop(0, DFT_B, body, 0, unroll=_B_UNROLL)

    @pl.when((s > 0) & (s <= nkb))
    def _():
        k0 = (s - 1) * kb
        for i in range(kb):
            rows = _stage_rows(k0 + i)
            re, im = load_stage(rows)
            zz = jnp.dot(wf_ref[...], jnp.concatenate([re, im], axis=0), preferred_element_type=F32)
            zr, zi = zz[:DFT_B], zz[DFT_B:]
            kr, ki = kf_ref[0, i], kf_ref[1, i]
            y = jnp.concatenate([zr * kr - zi * ki, zr * ki + zi * kr], axis=0).astype(BF16)
            cc = jnp.dot(wi_ref[...], y, preferred_element_type=F32)
            store_stage(rows, cc[:DFT_B], cc[DFT_B:])

    @pl.when(s == nkb + 1)
    def _():
        def body(b, carry):
            re, im = load_stage(pl.ds(b, nk, stride=_PITCH))
            y = jnp.dot(h_ref[b], jnp.concatenate([re, im], axis=0), preferred_element_type=F32)
            o_ref[pl.ds(b, half, stride=DFT_B), :] = _pack_groups(y, load_zx(b), bias_ref[...], gain_ref[...])
            return carry

        lax.fori_loop(0, DFT_B, body, 0, unroll=_B_UNROLL)


def _long_conv(zx, kf, g_z, h_tab, wf, wi, bias, gain):
    l, c = zx.shape
    n_a = 2 * l // DFT_B
    nk = g_z.shape[1] // 2
    kb = min(nk, _K1_BLOCK)
    nkb = nk // kb
    cw = 2 * LANES
    return pl.pallas_call(
        functools.partial(_long_conv_kernel, n_a=n_a),
        grid=(c // cw, nkb + 2),
        in_specs=[_single((l, LANES), lambda ci, s: (0, 2 * ci)), _single((l, LANES), lambda ci, s: (0, 2 * ci + 1)),
                  pl.BlockSpec((2, kb, DFT_B, cw), lambda ci, s: (0, jnp.clip(s - 1, 0, nkb - 1), 0, ci)),
                  _single(g_z.shape, lambda ci, s: (0, 0, 0)), _single(h_tab.shape, lambda ci, s: (0, 0, 0)),
                  _single(wf.shape, lambda ci, s: (0, 0)), _single(wi.shape, lambda ci, s: (0, 0)),
                  pl.BlockSpec((1, cw), lambda ci, s: (0, ci)), pl.BlockSpec((1, cw), lambda ci, s: (0, ci))],
        out_specs=pl.BlockSpec((l, LANES), lambda ci, s: (0, ci)),
        out_shape=jax.ShapeDtypeStruct((l, c // 2), jnp.uint32),
        scratch_shapes=[pltpu.VMEM((nk * _PITCH, LANES), jnp.uint32), pltpu.VMEM((nk * _PITCH, LANES), jnp.uint32)],
        compiler_params=_params("parallel", "arbitrary"),
        name="long_conv",
    )(zx, zx, kf, g_z, h_tab, wf, wi, bias.reshape(1, c), gain.reshape(1, c))


def _hyena_long(zx, filt, bias, gain):
    l, c = zx.shape
    n_a = 2 * l // DFT_B
    wf, wi = _dft128_embed()
    wf, wi = wf.astype(BF16), wi.astype(BF16)
    g_z = _dft_tables(n_a, n_a // 2).astype(BF16)
    g_k = _dft_tables(n_a, n_a).astype(BF16)
    h_tab = _idft_tables(n_a, n_a // 2).astype(BF16)
    hperm = _filter_hidden_permuted(l, filt['hy_w1'], filt['hy_b1'], filt['hy_freq'], filt['hy_w2'], filt['hy_b2'])
    kf = _filter_spectrum(hperm, filt['hy_w3'], l, c, g_k, wf)
    return _long_conv(zx, kf, g_z, h_tab, wf, wi, bias, gain)


def _ctx_conv_kernel(zx_ref, k_ref, fz_ref, fk_ref, hi_ref, bias_ref, gain_ref, o_ref):
    n = k_ref.shape[0]
    zx = zx_ref[...]
    z, _ = _unpack_pair(zx)
    zs = jnp.dot(fz_ref[...], z, preferred_element_type=F32)
    ks = jnp.dot(fk_ref[...], k_ref[...].astype(BF16), preferred_element_type=F32)
    zr, zi, kr, ki = zs[:n], zs[n:], ks[:n], ks[n:]
    y = jnp.concatenate([zr * kr - zi * ki, zr * ki + zi * kr], axis=0).astype(BF16)
    yt = jnp.dot(hi_ref[...], y, preferred_element_type=F32)
    o_ref[...] = _pack_groups(yt, zx, bias_ref[...], gain_ref[...])


def _hyena_short(zx, k2, bias, gain):
    l, c = zx.shape
    n = 2 * l
    k = np.arange(n)[:, None]
    t = np.arange(n)[None, :]
    e = np.exp(-2j * np.pi * ((k * t) % n) / n)
    fk = jnp.asarray(np.concatenate([e.real, e.imag], axis=0), F32).astype(BF16)
    fz = fk[:, :l]
    ei = np.conj(e[:, :l]).T / n
    hi = jnp.asarray(np.concatenate([ei.real, -ei.imag], axis=1), F32).astype(BF16)
    cw = 2 * LANES
    col = lambda rows: pl.BlockSpec((rows, cw), lambda ci: (0, ci))
    full = lambda a: pl.BlockSpec(a.shape, lambda ci: (0, 0))
    return pl.pallas_call(
        _ctx_conv_kernel,
        grid=(c // cw,),
        in_specs=[col(l), col(n), full(fz), full(fk), full(hi), col(1), col(1)],
        out_specs=pl.BlockSpec((l, LANES), lambda ci: (0, ci)),
        out_shape=jax.ShapeDtypeStruct((l, c // 2), jnp.uint32),
        compiler_params=_params("parallel"),
        name="hyena_short",
    )(zx, k2, fz, fk, hi, bias.reshape(1, c), gain.reshape(1, c))


def _split3(x):
    hi = x.astype(BF16)
    r1 = x - hi.astype(F32)
    mid = r1.astype(BF16)
    lo = (r1 - mid.astype(F32)).astype(BF16)
    return hi, mid, lo


def _gla_kernel(*refs, reverse, combine):
    if combine:
        (k_ref, q_ref, vlo_ref, vhi_ref, r_ref, wg_ref, bg_ref, s0_ref, of_ref, gate_ref, gout_ref,
         o_ref, sfin_ref, st_ref, la_ref, qe_ref, upd_ref, dec_ref, ke_ref, kl_ref, q2_ref, sc_ref, ob_ref) = refs
    else:
        (k_ref, q_ref, vlo_ref, vhi_ref, r_ref, wg_ref, bg_ref, s0_ref,
         o_ref, sfin_ref, st_ref, la_ref, qe_ref, upd_ref, dec_ref, ke_ref, kl_ref, q2_ref, sc_ref) = refs
        ob_ref = o_ref
    v_half = (vlo_ref, vhi_ref)
    rows_blk = k_ref.shape[0]
    nc = rows_blk // GLA_CHUNK
    qk = k_ref.shape[1]

    @pl.when(pl.program_id(0) == 0)
    def _():
        st_ref[...] = s0_ref[...]

    ri = lax.broadcasted_iota(jnp.int32, (2 * GLA_CHUNK, GLA_CHUNK), 0) % GLA_CHUNK
    ci = lax.broadcasted_iota(jnp.int32, (2 * GLA_CHUNK, GLA_CHUNK), 1)
    r3 = lax.broadcasted_iota(jnp.int32, (GLA_CHUNK, 3 * GLA_CHUNK), 0)
    c3 = lax.broadcasted_iota(jnp.int32, (GLA_CHUNK, 3 * GLA_CHUNK), 1) % GLA_CHUNK
    if reverse:
        tri3 = (c3 >= r3).astype(BF16)
        keep = ci > ri
    else:
        tri3 = (c3 <= r3).astype(BF16)
        keep = ci <= ri
    lane = lax.broadcasted_iota(jnp.int32, (1, LANES), 1)
    first = (lane < GLA_DK).astype(F32)
    srow = lax.broadcasted_iota(jnp.int32, (2 * GLA_DV, LANES), 0)
    scol = lax.broadcasted_iota(jnp.int32, (2 * GLA_DV, LANES), 1)
    own = (srow < GLA_DV) == (scol < GLA_DK)
    nt = (((1,), (1,)), ((), ()))
    tn = (((0,), (0,)), ((), ()))
    npair = GLA_HEADS // 2

    rh, rm, _ = _split3(r_ref[...])
    wh, wm, _ = _split3(wg_ref[...])
    pre = jnp.dot(jnp.concatenate([rh, rh, rm], axis=1), jnp.concatenate([wh, wm, wh], axis=0),
                  preferred_element_type=F32) + bg_ref[...]
    la_ref[...] = jax.nn.log_sigmoid(pre) / GLA_GATE_TAU

    def chunk_rows(c):
        return pl.ds(pl.multiple_of(c * GLA_CHUNK, GLA_CHUNK), GLA_CHUNK)

    def cum_pass(c, carry):
        rows = chunk_rows(c)
        hi, mid, lo = _split3(la_ref[rows, :])
        la_ref[rows, :] = jnp.dot(tri3, jnp.concatenate([hi, mid, lo], axis=0), preferred_element_type=F32)
        return carry

    def decay_pass(c, carry):
        rows = chunk_rows(c)
        cum = la_ref[rows, :]
        tot = cum[0:1] if reverse else cum[GLA_CHUNK - 1:GLA_CHUNK]
        k = k_ref[rows, :]
        qe = q_ref[rows, :] * (GLA_DK ** -0.5) * jnp.exp(cum)
        ke_ref[rows, :] = (k * jnp.exp(-cum)).astype(BF16)
        kl_ref[rows, :] = (k * jnp.exp(tot - cum)).astype(BF16)
        dec_ref[pl.ds(c, 1), :] = jnp.exp(tot)
        qe_ref[rows, :] = qe.astype(BF16)
        for p in range(npair):
            pr = slice(p * LANES, (p + 1) * LANES)
            q2_ref[c * npair + p] = jnp.concatenate([qe[:, pr] * first, qe[:, pr] * (1.0 - first)],
                                                    axis=0).astype(BF16)
        return carry

    def score_pass(c, carry):
        rows = chunk_rows(c)
        for p in range(npair):
            pr = slice(p * LANES, (p + 1) * LANES)
            sc = lax.dot_general(q2_ref[c * npair + p], ke_ref[rows, pr], nt, preferred_element_type=F32)
            sc_ref[c * npair + p] = jnp.where(keep, sc, 0.0).astype(BF16)
        return carry

    def value_pass(c, carry):
        rows = chunk_rows(c)
        for p in range(npair):
            pr = slice(p * LANES, (p + 1) * LANES)
            pv = slice(2 * p * GLA_DV, 2 * (p + 1) * GLA_DV)
            v_p = v_half[p // 2][rows, pl.ds((p % 2) * 2 * GLA_DV, 2 * GLA_DV)].astype(BF16)
            o2 = jnp.dot(sc_ref[c * npair + p], v_p, preferred_element_type=F32)
            ob_ref[rows, pl.ds(2 * p * GLA_DV, GLA_DV)] = o2[:GLA_CHUNK, :GLA_DV]
            ob_ref[rows, pl.ds((2 * p + 1) * GLA_DV, GLA_DV)] = o2[GLA_CHUNK:, GLA_DV:]
            u2 = lax.dot_general(v_p, kl_ref[rows, pr], tn, preferred_element_type=F32)
            upd_ref[c * npair + p] = jnp.where(own, u2, 0.0)
        return carry

    lax.fori_loop(0, nc, cum_pass, 0, unroll=4)
    lax.fori_loop(0, nc, decay_pass, 0)
    lax.fori_loop(0, nc, score_pass, 0, unroll=4)
    lax.fori_loop(0, nc, value_pass, 0, unroll=4)

    def carried(idx, carry):
        c = (nc - 1 - idx) if reverse else idx
        rows = pl.ds(pl.multiple_of(c * GLA_CHUNK, GLA_CHUNK), GLA_CHUNK)
        dec = dec_ref[pl.ds(c, 1), :]
        for p in range(npair):
            pr = slice(p * LANES, (p + 1) * LANES)
            pv = slice(2 * p * GLA_DV, 2 * (p + 1) * GLA_DV)
            st = st_ref[p]
            ob_ref[rows, pv] += lax.dot_general(qe_ref[rows, pr], st.astype(BF16), nt, preferred_element_type=F32)
            st_ref[p] = st * dec[:, pr] + upd_ref[c * npair + p]
        return carry

    lax.fori_loop(0, nc, carried, 0, unroll=4)

    if combine:
        for h in range(GLA_HEADS):
            hv = slice(h * GLA_DV, (h + 1) * GLA_DV)
            o = of_ref[:, hv] + ob_ref[:, hv]
            ms = jnp.mean(o * o, axis=-1, keepdims=True)
            o = o * lax.rsqrt(ms + NORM_EPS) * gout_ref[...]
            o_ref[:, hv] = (o * _silu(gate_ref[:, hv])).astype(o_ref.dtype)

    @pl.when(pl.program_id(0) == pl.num_programs(0) - 1)
    def _():
        sfin_ref[...] = st_ref[...]


_GLA_STATE = (GLA_HEADS // 2, 2 * GLA_DV, LANES)

_COL_K, _COL_V, _COL_Q, _COL_G = 24, 28, 36, 40


def _gla_pass(p, p_r, wg_pad, bg, s0, reverse, of=None, gout=None):
    l = p.shape[0]
    rb = min(l, 512)
    nblk = l // rb
    qk = GLA_HEADS * GLA_DK
    vw = GLA_HEADS * GLA_DV
    combine = of is not None
    rmap = (lambda i: nblk - 1 - i) if reverse else (lambda i: i)

    def cols(width, off128):
        return pl.BlockSpec((rb, width), lambda i: (rmap(i), off128 * LANES // width))

    state = pl.BlockSpec(_GLA_STATE, lambda i: (0, 0, 0))
    in_specs = [cols(qk, _COL_K), cols(qk, _COL_Q), cols(vw // 2, _COL_V), cols(vw // 2, _COL_V + vw // 2 // LANES),
                pl.BlockSpec((rb, LANES), lambda i: (rmap(i), 0)),
                pl.BlockSpec((LANES, qk), lambda i: (0, 0)), pl.BlockSpec((1, qk), lambda i: (0, 0)), state]
    args = [p, p, p, p, p_r, wg_pad, bg.reshape(1, qk), s0]
    nc = rb // GLA_CHUNK
    npair = _GLA_STATE[0]
    scratch = [pltpu.VMEM(_GLA_STATE, F32),
               pltpu.VMEM((rb, qk), F32),
               pltpu.VMEM((rb, qk), BF16),
               pltpu.VMEM((nc * npair,) + _GLA_STATE[1:], F32),
               pltpu.VMEM((nc, qk), F32),
               pltpu.VMEM((rb, qk), BF16),
               pltpu.VMEM((rb, qk), BF16),
               pltpu.VMEM((nc * npair, 2 * GLA_CHUNK, LANES), BF16),
               pltpu.VMEM((nc * npair, 2 * GLA_CHUNK, GLA_CHUNK), BF16)]
    if combine:
        in_specs += [pl.BlockSpec((rb, vw), lambda i: (rmap(i), 0)), cols(vw, _COL_G),
                     pl.BlockSpec((1, GLA_DV), lambda i: (0, 0))]
        args += [of, p, gout.reshape(1, GLA_DV)]
        scratch.append(pltpu.VMEM((rb, vw), F32))
    out_dtype = BF16 if combine else F32
    return pl.pallas_call(
        functools.partial(_gla_kernel, reverse=reverse, combine=combine),
        grid=(nblk,),
        in_specs=in_specs,
        out_specs=[pl.BlockSpec((rb, vw), lambda i: (rmap(i), 0)), state],
        out_shape=[jax.ShapeDtypeStruct((l, vw), out_dtype),
                   jax.ShapeDtypeStruct(_GLA_STATE, F32)],
        scratch_shapes=scratch,
        compiler_params=_params("arbitrary"),
        name="gla_bwd" if reverse else "gla_fwd",
    )(*args)


def _out_proj_kernel(hy_ref, o_ref, w_ref, x_ref, ga_ref, g_ref, sh_ref, sc_ref, xo_ref, h_ref):
    q = hy_ref.shape[1]
    hy_a, hy_b = _unpack_pair(hy_ref[...])
    mix = (jnp.dot(hy_a, w_ref[:q, :], preferred_element_type=F32)
           + jnp.dot(hy_b, w_ref[q:2 * q, :], preferred_element_type=F32)
           + jnp.dot(o_ref[...], w_ref[2 * q:, :], preferred_element_type=F32))
    xn = x_ref[...] + ga_ref[...] * mix
    xo_ref[...] = xn
    h_ref[...] = _norm_mod(xn, g_ref[...], sh_ref[...], sc_ref[...]).astype(BF16)


def _out_proj(hy, o, w, x, ga, g, shift, scale, layer):
    m, d = x.shape
    tm = min(m, 256)
    vec = pl.BlockSpec((1, d), lambda i: (0, 0))
    row = lambda width: pl.BlockSpec((tm, width), lambda i: (i, 0))
    return pl.pallas_call(
        _out_proj_kernel,
        grid=(m // tm,),
        in_specs=[row(hy.shape[1]), row(o.shape[1]), pl.BlockSpec((None, w.shape[1], d), lambda i: (layer, 0, 0)),
                  row(d), vec, vec, vec, vec],
        out_specs=[row(d), row(d)],
        out_shape=[jax.ShapeDtypeStruct((m, d), F32), jax.ShapeDtypeStruct((m, d), BF16)],
        compiler_params=_params("parallel"),
        name="out_proj",
    )(hy, o, w, x, ga, g, shift, scale)


_HALO = GRID_W + SUBLANES


def _ffn_up_conv_kernel(x_ref, wa_ref, wu_ref, cw_ref, o_ref, a_ref, u_ref, wb_ref, *, vertical, cols, nm):
    t = pl.program_id(0)
    total = pl.num_programs(0) - 1
    i = lax.rem(t, nm)
    tt = x_ref.shape[0]
    cur = lax.rem(t, 2)
    prv = 1 - cur

    @pl.when(t == 0)
    def _():
        a_ref[...] = jnp.zeros_like(a_ref)
        u_ref[...] = jnp.zeros_like(u_ref)

    def conv_previous_tile():
        col = lax.broadcasted_iota(jnp.int32, o_ref.shape, 0) & (cols - 1)
        acc = None
        for dc in (-1, 0, 1):
            part = None
            for dr in ((-1, 0, 1) if vertical else (0,)):
                tap = cw_ref[3 * (dr + 1) + (dc + 1):3 * (dr + 1) + (dc + 1) + 1, :]
                term = tap * a_ref[prv, pl.ds(_HALO + GRID_W * dr + dc, tt), :]
                part = term if part is None else part + term
            if dc == -1:
                part = jnp.where(col == 0, 0.0, part)
            elif dc == 1:
                part = jnp.where(col == cols - 1, 0.0, part)
            acc = part if acc is None else acc + part
        o_ref[...] = (_silu(acc) * u_ref[prv].astype(F32)).astype(o_ref.dtype)

    @pl.when((i == 0) & (t < total))
    def _():
        wb_ref[0] = wa_ref[...].astype(BF16)
        wb_ref[1] = wu_ref[...].astype(BF16)

    @pl.when(t < total)
    def _():
        x = x_ref[...]
        a_new = jnp.dot(x, wb_ref[0], preferred_element_type=F32)
        u_ref[cur] = jnp.dot(x, wb_ref[1], preferred_element_type=F32).astype(BF16)
        upper = a_ref[prv, pl.ds(_HALO + tt - GRID_W, GRID_W), :]
        a_ref[cur, pl.ds(SUBLANES, GRID_W), :] = jnp.where(i == 0, 0.0, upper)
        a_ref[cur, pl.ds(_HALO, tt), :] = a_new
        a_ref[prv, pl.ds(_HALO + tt, GRID_W), :] = jnp.where(i == 0, 0.0, a_new[:GRID_W])
        conv_previous_tile()

    @pl.when(t == total)
    def _():
        a_ref[prv, pl.ds(_HALO + tt, GRID_W), :] = jnp.zeros((GRID_W, a_ref.shape[2]), F32)
        conv_previous_tile()


def _ffn_up_conv(h, w_up, conv_w, cols, layer):
    m, d = h.shape
    dff = conv_w.shape[-1]
    vertical = m > cols
    assert cols & (cols - 1) == 0 and (cols == GRID_W or not vertical)
    tt = min(m, 1024)
    assert tt % cols == 0
    cb = 512
    ncb = dff // cb
    nm = m // tt
    total = ncb * nm
    cur_t = lambda t: jnp.minimum(t, total - 1)
    prev_t = lambda t: jnp.maximum(t - 1, 0)
    return pl.pallas_call(
        functools.partial(_ffn_up_conv_kernel, vertical=vertical, cols=cols, nm=nm),
        grid=(total + 1,),
        in_specs=[pl.BlockSpec((tt, d), lambda t: (cur_t(t) % nm, 0)),
                  pl.BlockSpec((None, d, cb), lambda t: (layer, 0, cur_t(t) // nm)),
                  pl.BlockSpec((None, d, cb), lambda t: (layer, 0, ncb + cur_t(t) // nm)),
                  pl.BlockSpec((9, cb), lambda t: (0, prev_t(t) // nm))],
        out_specs=pl.BlockSpec((tt, cb), lambda t: (prev_t(t) % nm, prev_t(t) // nm)),
        out_shape=jax.ShapeDtypeStruct((m, dff), BF16),
        scratch_shapes=[pltpu.VMEM((2, tt + 2 * _HALO, cb), F32), pltpu.VMEM((2, tt, cb), BF16),
                        pltpu.VMEM((2, d, cb), BF16)],
        compiler_params=_params("arbitrary"),
        name="ffn_up_conv",
    )(h, w_up, w_up, conv_w.reshape(9, dff))


def _ffn_down_kernel(g_ref, w_ref, x_ref, gate_ref, ng_ref, nsh_ref, nsc_ref, o_ref, *h_ref, post):
    xn = x_ref[...] + gate_ref[...] * jnp.dot(g_ref[...], w_ref[...], preferred_element_type=F32)
    if post == 'final':
        ms = jnp.mean(xn * xn, axis=-1, keepdims=True)
        xn = xn * lax.rsqrt(ms + NORM_EPS) * ng_ref[...]
    o_ref[...] = xn
    if post == 'next':
        h_ref[0][...] = _norm_mod(xn, ng_ref[...], nsh_ref[...], nsc_ref[...]).astype(BF16)


def _ffn_down(g, w, x, gate, norm, post, layer):
    m, d = x.shape
    dff = g.shape[1]
    tm = min(m, 256)
    vec = pl.BlockSpec((1, d), lambda i: (0, 0))
    row = pl.BlockSpec((tm, d), lambda i: (i, 0))
    out_specs, out_shape = [row], [jax.ShapeDtypeStruct((m, d), F32)]
    if post == 'next':
        out_specs.append(row)
        out_shape.append(jax.ShapeDtypeStruct((m, d), BF16))
    return pl.pallas_call(
        functools.partial(_ffn_down_kernel, post=post),
        grid=(m // tm,),
        in_specs=[pl.BlockSpec((tm, dff), lambda i: (i, 0)),
                  _single((None, dff, d), lambda i: (layer, 0, 0)),
                  row, vec, vec, vec, vec],
        out_specs=out_specs,
        out_shape=out_shape,
        compiler_params=_params("parallel"),
        name="ffn_down",
    )(g, w, x, gate, *norm)


def _regroup_w_in(w_in_t, hy_proj):
    qk = GLA_HEADS * GLA_DK
    vw = GLA_HEADS * GLA_DV
    head = hy_proj + qk + vw
    r = w_in_t[:, head:head + 2 * GLA_RANK]
    tail = w_in_t[:, head + 2 * GLA_RANK:]
    pad = jnp.zeros((w_in_t.shape[0], LANES - 2 * GLA_RANK, w_in_t.shape[2]), w_in_t.dtype)
    return head, tail, jnp.concatenate([r, pad], axis=1)


def _pad_gate_w(wg, lane0):
    return jnp.zeros((LANES, wg.shape[1]), F32).at[lane0:lane0 + GLA_RANK].set(wg)


def _mixer(p, p_r, lp, s0_f, s0_b, want_output):
    l = p.shape[0]
    width = lp['hy_bias'].shape[0]
    o_f, s_f = _gla_pass(p, p_r, lp['wg_f_pad'], lp['gla_bg_f'], s0_f, reverse=False)
    o, s_b = _gla_pass(p, p_r, lp['wg_b_pad'], lp['gla_bg_b'], s0_b, reverse=True, of=o_f, gout=lp['gla_out_g'])
    if not want_output:
        return None, None, s_f, s_b
    zx = _hy_pre(p, lp['hy_conv'], width)
    if 2 * l // DFT_B >= 2 * SUBLANES:
        hy = _hyena_long(zx, lp, lp['hy_bias'], lp['hy_out_g'])
    else:
        k2 = _hyena_filter2(l, lp['hy_w1'], lp['hy_b1'], lp['hy_freq'], lp['hy_w2'], lp['hy_b2'], lp['hy_w3'], width)
        hy = _hyena_short(zx, k2, lp['hy_bias'], lp['hy_out_g'])
    return hy, o, s_f, s_b


def _order_w_out(w_out, hy_width):
    depth, _, d = w_out.shape
    hy_rows = w_out[:, :hy_width].reshape(depth, hy_width // (2 * LANES), 2, LANES, d)
    hy_rows = jnp.swapaxes(hy_rows, 1, 2).reshape(depth, hy_width, d)
    return jnp.concatenate([hy_rows, w_out[:, hy_width:]], axis=1).astype(BF16)


def _ffn(h2, x, gate, lp, cols, norm, post, layer):
    gact = _ffn_up_conv(h2, lp['ffn_w_up'], lp['ffn_conv'], cols, layer)
    return _ffn_down(gact, lp['ffn_w_down'], x, gate, norm, post, layer)


def kernel(x, c, ctx, c_ctx, w_mod, b_mod, g_attn, w_in, hy_conv, hy_w1, hy_b1, hy_freq, hy_w2, hy_b2, hy_w3,
           hy_bias, hy_out_g, gla_wg_f, gla_bg_f, gla_wg_b, gla_bg_b, gla_out_g, w_out, g_ffn, ffn_w_up,
           ffn_conv, ffn_w_down, g_final):
    depth = w_mod.shape[0]
    batch, seq, d = x.shape
    assert batch == 1 and c.shape[0] == 1 and ctx.shape[0] == 1
    hy_proj = hy_conv.shape[-1]
    xs = x[0]
    cs = ctx[0]
    cc = jnp.zeros((SUBLANES, d), F32).at[0].set(c[0]).at[1].set(c_ctx)
    zero_state = jnp.zeros(_GLA_STATE, F32)
    gfin = g_final.reshape(1, d)
    w_in = jnp.swapaxes(w_in, 1, 2)
    n_head_cols, w_tail, w_rank = _regroup_w_in(w_in, hy_proj)
    w_out_b = _order_w_out(w_out, hy_bias.shape[-1])
    w_down_b = ffn_w_down.astype(BF16)
    mods =[_adaln(cc, w_mod, b_mod, l) for l in range(depth)]

    def attn_norm(l, row):
        return (g_attn[l].reshape(1, d), mods[l][row:row + 1, 0:d], mods[l][row:row + 1, d:2 * d])

    hx = _norm_mod_call(xs, *attn_norm(0, 0))
    hc = _norm_mod_call(cs, *attn_norm(0, 1))
    for l in range(depth):
        last = l == depth - 1
        lp = {
            'hy_conv': hy_conv[l], 'hy_w1': hy_w1[l], 'hy_b1': hy_b1[l], 'hy_freq': hy_freq[l],
            'hy_w2': hy_w2[l], 'hy_b2': hy_b2[l], 'hy_w3': hy_w3[l], 'hy_bias': hy_bias[l],
            'hy_out_g': hy_out_g[l], 'gla_bg_f': gla_bg_f[l], 'gla_bg_b': gla_bg_b[l],
            'gla_out_g': gla_out_g[l],
            'wg_f_pad': _pad_gate_w(gla_wg_f[l], 0), 'wg_b_pad': _pad_gate_w(gla_wg_b[l], GLA_RANK),
            'ffn_w_up': ffn_w_up, 'ffn_conv': ffn_conv[l], 'ffn_w_down': w_down_b,
        }
        mod = mods[l]
        _, _, ga, sf, scf, gf = [mod[0:1, i * d:(i + 1) * d] for i in range(6)]
        _, _, cga, csf, cscf, cgf = [mod[1:2, i * d:(i + 1) * d] for i in range(6)]
        gf_row = g_ffn[l].reshape(1, d)

        pc = _in_proj(hc, w_in, w_tail, n_head_cols, l)
        hy_c, o_c, s_f, s_b = _mixer(pc, _gate_rank_proj(hc, w_rank, l), lp, zero_state, zero_state,
                                     want_output=not last)
        if not last:
            cs, h2c = _out_proj(hy_c, o_c, w_out_b, cs, cga, gf_row, csf, cscf, l)
            cs, hc = _ffn(h2c, cs, cgf, lp, cs.shape[0], attn_norm(l + 1, 1), 'next', l)

        px = _in_proj(hx, w_in, w_tail, n_head_cols, l)
        hy_x, o_x, _, _ = _mixer(px, _gate_rank_proj(hx, w_rank, l), lp, s_f, s_b, want_output=True)
        xs, h2 = _out_proj(hy_x, o_x, w_out_b, xs, ga, gf_row, sf, scf, l)
        if last:
            (xs,) = _ffn(h2, xs, gf, lp, GRID_W, (gfin, gfin, gfin), 'final', l)
        else:
            xs, hx = _ffn(h2, xs, gf, lp, GRID_W, attn_norm(l + 1, 0), 'next', l)
    return xs[None]
```

```python
import functools
import math

import numpy as np
import jax
import jax.numpy as jnp
from jax import lax
from jax.experimental import pallas as pl
from jax.experimental.pallas import tpu as pltpu

F32 = jnp.float32
BF16 = jnp.bfloat16
HIGHEST = lax.Precision.HIGHEST

LANES = 128
SUBLANES = 8
VMEM_BYTES_V7X = 64 * 1024 * 1024
VMEM_LIMIT = VMEM_BYTES_V7X - 12 * 1024 * 1024

NORM_EPS = 1e-6
HYENA_GROUP = 128
FILTER_BANDS = 16
FILTER_HIDDEN = 64
FILTER_TARGET = 1e-2
FILTER_FAST_PCT = 0.3
FILTER_SLOW_PCT = 1.5
GLA_HEADS = 8
GLA_DK = 64
GLA_DV = 128
GLA_RANK = 16
GLA_GATE_TAU = 16.0
GLA_CHUNK = 64
GRID_W = 64
DFT_B = 128


def _params(*sem):
    return pltpu.CompilerParams(dimension_semantics=sem, vmem_limit_bytes=VMEM_LIMIT)


def _silu(x):
    return x * jax.nn.sigmoid(x)


def _adaln_kernel(c_ref, w_ref, b_ref, o_ref):
    s = _silu(c_ref[...]).astype(BF16)
    o_ref[...] = jnp.dot(s, w_ref[...].astype(BF16), preferred_element_type=F32) + b_ref[...]


def _adaln(cc, w_mod, b_mod, layer):
    depth, d, n = w_mod.shape
    tn = 1024
    return pl.pallas_call(
        _adaln_kernel,
        grid=(n // tn,),
        in_specs=[pl.BlockSpec((SUBLANES, d), lambda j: (0, 0)),
                  pl.BlockSpec((None, d, tn), lambda j: (layer, 0, j)),
                  pl.BlockSpec((None, 1, tn), lambda j: (layer, 0, j))],
        out_specs=pl.BlockSpec((SUBLANES, tn), lambda j: (0, j)),
        out_shape=jax.ShapeDtypeStruct((SUBLANES, n), F32),
        compiler_params=_params("arbitrary"),
        name="adaln",
    )(cc, w_mod, b_mod.reshape(depth, 1, n))


def _norm_mod(x, g, shift, scale):
    ms = jnp.mean(x * x, axis=-1, keepdims=True)
    return (x * lax.rsqrt(ms + NORM_EPS) * g) * (1.0 + scale) + shift


def _norm_mod_kernel(x_ref, g_ref, sh_ref, sc_ref, o_ref):
    o_ref[...] = _norm_mod(x_ref[...], g_ref[...], sh_ref[...], sc_ref[...]).astype(o_ref.dtype)


def _norm_mod_call(x, g, shift, scale):
    m, d = x.shape
    tm = min(m, 512)
    vec = pl.BlockSpec((1, d), lambda i: (0, 0))
    return pl.pallas_call(
        _norm_mod_kernel,
        grid=(m // tm,),
        in_specs=[pl.BlockSpec((tm, d), lambda i: (i, 0)), vec, vec, vec],
        out_specs=pl.BlockSpec((tm, d), lambda i: (i, 0)),
        out_shape=jax.ShapeDtypeStruct((m, d), BF16),
        compiler_params=_params("parallel"),
        name="norm_mod",
    )(x, g, shift, scale)


_IN_TILE = 768


_NT = (((1,), (1,)), ((), ()))


def _in_proj_kernel(h_ref, w_ref, wt_ref, o_ref, wb_ref, *, n_head):
    j = pl.program_id(0)

    @pl.when((pl.program_id(1) == 0) & (j < n_head))
    def _():
        wb_ref[...] = w_ref[...].T.astype(BF16)

    @pl.when((pl.program_id(1) == 0) & (j >= n_head))
    def _():
        wb_ref[...] = wt_ref[...].T.astype(BF16)

    o_ref[...] = jnp.dot(h_ref[...], wb_ref[...], preferred_element_type=F32)


def _in_proj(h, w_in_t, w_tail_t, n_head_cols, layer):
    m, d = h.shape
    tn = _IN_TILE
    n_head = n_head_cols // tn
    n_tail = w_tail_t.shape[1] // tn
    assert n_head * tn == n_head_cols and n_tail * tn == w_tail_t.shape[1]
    tm = min(m, 1024)
    return pl.pallas_call(
        functools.partial(_in_proj_kernel, n_head=n_head),
        grid=(n_head + n_tail, m // tm),
        in_specs=[pl.BlockSpec((tm, d), lambda j, i: (i, 0)),
                  _single((None, tn, d), lambda j, i: (layer, jnp.minimum(j, n_head - 1), 0)),
                  _single((None, tn, d), lambda j, i: (layer, jnp.maximum(j - n_head, 0), 0))],
        out_specs=pl.BlockSpec((tm, tn), lambda j, i: (i, j)),
        out_shape=jax.ShapeDtypeStruct((m, n_head_cols + w_tail_t.shape[1]), F32),
        scratch_shapes=[pltpu.VMEM((d, tn), BF16)],
        compiler_params=_params("arbitrary", "arbitrary"),
        name="in_proj",
    )(h, w_in_t, w_tail_t)


def _gate_rank_kernel(h_ref, w_ref, o_ref):
    o_ref[...] = lax.dot_general(h_ref[...], w_ref[...].astype(BF16), _NT, preferred_element_type=F32)


def _gate_rank_proj(h, w_r_t, layer):
    m, d = h.shape
    tm = min(m, 2048)
    return pl.pallas_call(
        _gate_rank_kernel,
        grid=(m // tm,),
        in_specs=[pl.BlockSpec((tm, d), lambda i: (i, 0)), pl.BlockSpec((None, LANES, d), lambda i: (layer, 0, 0))],
        out_specs=pl.BlockSpec((tm, LANES), lambda i: (i, 0)),
        out_shape=jax.ShapeDtypeStruct((m, LANES), F32),
        compiler_params=_params("parallel"),
        name="gate_rank_proj",
    )(h, w_r_t)


def _hy_pre_kernel(x0_ref, x1_ref, v_ref, x0p_ref, x1p_ref, vp_ref, x0n_ref, x1n_ref, vn_ref,
                   w0_ref, w1_ref, wv_ref, zx_ref):
    i = pl.program_id(0)
    first = i == 0
    last = i == pl.num_programs(0) - 1
    tt = x0_ref.shape[0]
    row = lax.broadcasted_iota(jnp.int32, x0_ref.shape, 0)

    def conv(ref, prev_ref, next_ref, w_ref):
        x = ref[...]
        prev_row = jnp.where(first, 0.0, prev_ref[SUBLANES - 1:SUBLANES, :])
        next_row = jnp.where(last, 0.0, next_ref[0:1, :])
        below = jnp.where(row == 0, prev_row, pltpu.roll(x, 1, 0))
        above = jnp.where(row == tt - 1, next_row, pltpu.roll(x, tt - 1, 0))
        return w_ref[0:1, :] * below + w_ref[1:2, :] * x + w_ref[2:3, :] * above

    z = conv(v_ref, vp_ref, vn_ref, wv_ref) * conv(x1_ref, x1p_ref, x1n_ref, w1_ref)
    zx_ref[...] = _pack_pair(z, conv(x0_ref, x0p_ref, x0n_ref, w0_ref))


def _hy_pre(p, hy_conv, width):
    l = p.shape[0]
    tt = min(l, 1024)
    cb = 512
    ncb = width // cb
    nb8 = l // SUBLANES
    t8 = tt // SUBLANES

    def main(g):
        return pl.BlockSpec((tt, cb), lambda i, j: (i, g * ncb + j))

    def prev(g):
        return pl.BlockSpec((SUBLANES, cb), lambda i, j: (jnp.maximum(i * t8 - 1, 0), g * ncb + j))

    def nxt(g):
        return pl.BlockSpec((SUBLANES, cb), lambda i, j: (jnp.minimum((i + 1) * t8, nb8 - 1), g * ncb + j))

    def wspec(g):
        return pl.BlockSpec((3, cb), lambda i, j: (0, g * ncb + j))

    out = pl.BlockSpec((tt, cb), lambda i, j: (i, j))
    return pl.pallas_call(
        _hy_pre_kernel,
        grid=(l // tt, ncb),
        in_specs=[main(0), main(1), main(2), prev(0), prev(1), prev(2), nxt(0), nxt(1), nxt(2),
                  wspec(0), wspec(1), wspec(2)],
        out_specs=out,
        out_shape=jax.ShapeDtypeStruct((l, width), jnp.uint32),
        compiler_params=_params("parallel", "parallel"),
        name="hy_pre",
    )(p, p, p, p, p, p, p, p, p, hy_conv, hy_conv, hy_conv)


def _filter_hidden(j, seq_len, w1t_ref, w1c_ref, w1s_ref, b1_ref, fr_ref, w2_ref, b2_ref):
    pos_i = jnp.where(j < seq_len, j, 2 * seq_len - j)
    pos = pos_i.astype(F32)
    t = pos / float(max(seq_len - 1, 1))
    lane = lax.broadcasted_iota(jnp.int32, j.shape, 1)
    band = jnp.where(lane < FILTER_BANDS, lane + 1, 0).astype(F32)
    ang = (2.0 * math.pi / seq_len) * pos * band
    fr = fr_ref[...]
    pre = (t[:, 0:1] * w1t_ref[...]
           + jnp.dot(jnp.cos(ang), w1c_ref[...], precision=HIGHEST, preferred_element_type=F32)
           + jnp.dot(jnp.sin(ang), w1s_ref[...], precision=HIGHEST, preferred_element_type=F32)
           + b1_ref[...])
    h = jnp.sin(fr * pre)
    h = jnp.sin(fr * (jnp.dot(h, w2_ref[...], precision=HIGHEST, preferred_element_type=F32) + b2_ref[...]))
    return h, pos_i, t


def _filter_kernel(w1t_ref, w1c_ref, w1s_ref, b1_ref, fr_ref, w2_ref, b2_ref, w3_ref, delta_ref, o_ref, *, seq_len):
    tr = o_ref.shape[0]
    j = pl.program_id(0) * tr + lax.broadcasted_iota(jnp.int32, (tr, LANES), 0)
    h, pos_i, t = _filter_hidden(j, seq_len, w1t_ref, w1c_ref, w1s_ref, b1_ref, fr_ref, w2_ref, b2_ref)
    hh = jnp.dot(h.astype(BF16), w3_ref[...].astype(BF16), preferred_element_type=F32)
    decay = jnp.exp(-t[:, 0:1] * delta_ref[...])
    valid = pos_i[:, 0:1] != seq_len
    o_ref[...] = jnp.where(valid, hh * decay, 0.0)


_FILTER_PACK = LANES // FILTER_BANDS


def _filter_hidden_kernel(w1t_ref, w1c_ref, w1s_ref, b1_ref, fr_ref, w2_ref, b2_ref, o_ref, *, seq_len, n_a):
    tp = o_ref.shape[0]
    shift = n_a.bit_length() - 1

    def position(shape, lanes_per_pos):
        prow = pl.program_id(0) * tp + lax.broadcasted_iota(jnp.int32, shape, 0)
        r = prow * _FILTER_PACK + lax.broadcasted_iota(jnp.int32, shape, 1) // lanes_per_pos
        j = ((r & (n_a - 1)) * DFT_B) + (r >> shift)
        return jnp.where(j < seq_len, j, 2 * seq_len - j).astype(F32)

    pos = position((tp, LANES), FILTER_BANDS)
    band = (lax.broadcasted_iota(jnp.int32, (tp, LANES), 1) % FILTER_BANDS + 1).astype(F32)
    ang = (2.0 * math.pi / seq_len) * pos * band
    t = position(o_ref.shape, FILTER_HIDDEN) / float(max(seq_len - 1, 1))
    fr = fr_ref[...]
    pre = (t * w1t_ref[...]
           + jnp.dot(jnp.cos(ang), w1c_ref[...], precision=HIGHEST, preferred_element_type=F32)
           + jnp.dot(jnp.sin(ang), w1s_ref[...], precision=HIGHEST, preferred_element_type=F32)
           + b1_ref[...])
    h = jnp.sin(fr * pre)
    o_ref[...] = jnp.sin(fr * (jnp.dot(h, w2_ref[...], precision=HIGHEST, preferred_element_type=F32) + b2_ref[...]))


def _filter_mlp_args(w1, b1, freq, w2, b2):
    hid = FILTER_HIDDEN
    w1c = jnp.zeros((LANES, hid), F32).at[:FILTER_BANDS].set(w1[1:1 + FILTER_BANDS])
    w1s = jnp.zeros((LANES, hid), F32).at[:FILTER_BANDS].set(w1[1 + FILTER_BANDS:1 + 2 * FILTER_BANDS])
    full = lambda shape: pl.BlockSpec(shape, lambda i: (0, 0))
    specs = [full((1, hid)), full((LANES, hid)), full((LANES, hid)), full((1, hid)), full((1, hid)),
             full((hid, hid)), full((1, hid))]
    args = (w1[0:1], w1c, w1s, b1.reshape(1, hid), freq.reshape(1, hid), w2, b2.reshape(1, hid))
    return specs, args


def _filter_delta(width):
    return jnp.abs(jnp.linspace(math.log(FILTER_TARGET) / FILTER_SLOW_PCT,
                                math.log(FILTER_TARGET) / FILTER_FAST_PCT, width, dtype=F32)).reshape(1, width)


def _hyena_filter2(seq_len, w1, b1, freq, w2, b2, w3, width):
    n2 = 2 * seq_len
    tr = min(seq_len, 1024)
    nfwd = seq_len // tr
    specs, args = _filter_mlp_args(w1, b1, freq, w2, b2)
    return pl.pallas_call(
        functools.partial(_filter_kernel, seq_len=seq_len),
        grid=(n2 // tr,),
        in_specs=specs + [pl.BlockSpec((FILTER_HIDDEN, width), lambda i: (0, jnp.where(i < nfwd, 0, 1))),
                          pl.BlockSpec((1, width), lambda i: (0, 0))],
        out_specs=pl.BlockSpec((tr, width), lambda i: (i, 0)),
        out_shape=jax.ShapeDtypeStruct((n2, width), F32),
        compiler_params=_params("parallel"),
        name="hyena_filter",
    )(*args, w3, _filter_delta(width))


def _filter_hidden_permuted(seq_len, w1, b1, freq, w2, b2):
    n2 = 2 * seq_len
    n_a = n2 // DFT_B
    hid = FILTER_HIDDEN
    wide = _FILTER_PACK * hid
    rows = n2 // _FILTER_PACK
    tp = min(rows, 256)
    eye = jnp.eye(_FILTER_PACK, dtype=F32)
    tile = lambda v: jnp.tile(v.reshape(1, hid), (1, _FILTER_PACK))
    args = (tile(w1[0]), jnp.kron(eye, w1[1:1 + FILTER_BANDS]),
            jnp.kron(eye, w1[1 + FILTER_BANDS:1 + 2 * FILTER_BANDS]), tile(b1), tile(freq),
            jnp.kron(eye, w2), tile(b2))
    packed = pl.pallas_call(
        functools.partial(_filter_hidden_kernel, seq_len=seq_len, n_a=n_a),
        grid=(rows // tp,),
        in_specs=[pl.BlockSpec(a.shape, lambda i: (0, 0)) for a in args],
        out_specs=pl.BlockSpec((tp, wide), lambda i: (i, 0)),
        out_shape=jax.ShapeDtypeStruct((rows, wide), F32),
        compiler_params=_params("parallel"),
        name="hyena_filter_hidden",
    )(*args)
    return packed.reshape(n2, hid)


def _kept_k1(n_a):
    kept = n_a // 2 + 1
    return kept, -(-kept // SUBLANES) * SUBLANES


def _dft_tables(n_a, a_in):
    n = n_a * DFT_B
    kept, padded = _kept_k1(n_a)
    b = np.arange(DFT_B)[:, None, None]
    k1 = np.arange(padded)[None, :, None]
    a = np.arange(a_in)[None, None, :]
    e = np.exp(-2j * np.pi * ((k1 * (DFT_B * a + b)) % n) / n) * (k1 < kept)
    g = np.concatenate([e.real, e.imag], axis=1)
    return jnp.asarray(g, F32)


def _idft_tables(n_a, a_out):
    n = n_a * DFT_B
    kept, padded = _kept_k1(n_a)
    b = np.arange(DFT_B)[:, None, None]
    a = np.arange(a_out)[None, :, None]
    k1 = np.arange(padded)[None, None, :]
    weight = np.where((k1 == 0) | (k1 == n_a // 2), 1.0, 2.0) * (k1 < kept)
    e = np.exp(2j * np.pi * ((k1 * (DFT_B * a + b)) % n) / n) * weight / n
    h = np.concatenate([e.real, -e.imag], axis=2)
    return jnp.asarray(h, F32)


def _dft128_embed():
    i = np.arange(DFT_B)
    f = np.exp(-2j * np.pi * ((i[:, None] * i[None, :]) % DFT_B) / DFT_B)
    fwd = np.block([[f.real, -f.imag], [f.imag, f.real]])
    inv = np.block([[f.real, f.imag], [-f.imag, f.real]])
    return jnp.asarray(fwd, F32), jnp.asarray(inv, F32)


_PITCH = DFT_B + SUBLANES
_K1_BLOCK = 8
_B_UNROLL = 16


def _stage_rows(k1):
    return pl.ds(pl.multiple_of(k1 * _PITCH, SUBLANES), DFT_B)


def _single(shape, index_map):
    return pl.BlockSpec(shape, index_map, pipeline_mode=pl.Buffered(1))


def _pack_pair(re, im):
    hi = lax.bitcast_convert_type(re.astype(BF16).astype(F32), jnp.uint32)
    lo = lax.bitcast_convert_type(im.astype(BF16).astype(F32), jnp.uint32)
    return hi | (lo >> 16)


def _unpack_pair(w):
    re = lax.bitcast_convert_type(w & jnp.uint32(0xFFFF0000), F32)
    im = lax.bitcast_convert_type(w << 16, F32)
    return re.astype(BF16), im.astype(BF16)


def _spectrum_kernel(h_ref, w3f_ref, w3b_ref, delta_ref, g_ref, wf_ref, o_ref, s0_ref, s1_ref, *, seq_len, n_a):
    s = pl.program_id(1)
    half = n_a // 2
    cw = o_ref.shape[3]

    @pl.when(s == 0)
    def _():
        a_i = lax.broadcasted_iota(jnp.int32, (n_a, cw), 0)

        def body(b, carry):
            base = pl.multiple_of(b * n_a, SUBLANES)
            hf = jnp.dot(h_ref[pl.ds(base, half), :].astype(BF16), w3f_ref[...].astype(BF16),
                         preferred_element_type=F32)
            hb = jnp.dot(h_ref[pl.ds(base + half, half), :].astype(BF16), w3b_ref[...].astype(BF16),
                         preferred_element_type=F32)
            hh = jnp.concatenate([hf, hb], axis=0)
            j = a_i * DFT_B + b
            pos_i = jnp.where(j < seq_len, j, 2 * seq_len - j)
            t = pos_i.astype(F32) / float(max(seq_len - 1, 1))
            k2 = jnp.where(pos_i != seq_len, hh * jnp.exp(-t * delta_ref[...]), 0.0)
            r = jnp.dot(g_ref[b], k2.astype(BF16), preferred_element_type=F32)
            nk = r.shape[0] // 2
            w = _pack_pair(r[:nk], r[nk:])
            s0_ref[pl.ds(b, nk, stride=_PITCH), :] = w[:, :LANES]
            s1_ref[pl.ds(b, nk, stride=_PITCH), :] = w[:, LANES:]
            return carry

        lax.fori_loop(0, DFT_B, body, 0, unroll=_B_UNROLL)

    @pl.when(s > 0)
    def _():
        k0 = (s - 1) * o_ref.shape[1]
        for i in range(o_ref.shape[1]):
            rows = _stage_rows(k0 + i)
            re, im = _unpack_pair(jnp.concatenate([s0_ref[rows, :], s1_ref[rows, :]], axis=1))
            z = jnp.dot(wf_ref[...], jnp.concatenate([re, im], axis=0), preferred_element_type=F32)
            o_ref[0, i] = z[:DFT_B]
            o_ref[1, i] = z[DFT_B:]


def _filter_spectrum(hperm, w3, seq_len, width, g_k, wf):
    n_a = 2 * seq_len // DFT_B
    nk = g_k.shape[1] // 2
    kb = min(nk, _K1_BLOCK)
    cw = 2 * LANES
    ncb = width // cw
    hid = FILTER_HIDDEN
    return pl.pallas_call(
        functools.partial(_spectrum_kernel, seq_len=seq_len, n_a=n_a),
        grid=(ncb, 1 + nk // kb),
        in_specs=[_single(hperm.shape, lambda ci, s: (0, 0)),
                  pl.BlockSpec((hid, cw), lambda ci, s: (0, ci)),
                  pl.BlockSpec((hid, cw), lambda ci, s: (0, ncb + ci)),
                  pl.BlockSpec((1, cw), lambda ci, s: (0, ci)),
                  _single(g_k.shape, lambda ci, s: (0, 0, 0)),
                  _single(wf.shape, lambda ci, s: (0, 0))],
        out_specs=pl.BlockSpec((2, kb, DFT_B, cw), lambda ci, s: (0, jnp.maximum(s - 1, 0), 0, ci)),
        out_shape=jax.ShapeDtypeStruct((2, nk, DFT_B, width), F32),
        scratch_shapes=[pltpu.VMEM((nk * _PITCH, LANES), jnp.uint32), pltpu.VMEM((nk * _PITCH, LANES), jnp.uint32)],
        compiler_params=_params("parallel", "arbitrary"),
        name="filter_spectrum",
    )(hperm, w3, w3, _filter_delta(width), g_k, wf)


def _hyena_epilogue(y, z, x0, bias, gain):
    yv = (y + z * bias) * x0
    ms = jnp.mean(yv * yv, axis=-1, keepdims=True)
    return yv * lax.rsqrt(ms + NORM_EPS) * gain


def _pack_groups(y, zx, bias, gain):
    z, x0 = _unpack_pair(zx)
    out = _hyena_epilogue_groups(y, z.astype(F32), x0.astype(F32), bias, gain)
    return _pack_pair(out[:, :LANES], out[:, LANES:])


def _hyena_epilogue_groups(y, z, x0, bias, gain):
    return jnp.concatenate([_hyena_epilogue(y[:, g * LANES:(g + 1) * LANES], z[:, g * LANES:(g + 1) * LANES],
                                            x0[:, g * LANES:(g + 1) * LANES], bias[:, g * LANES:(g + 1) * LANES],
                                            gain[:, g * LANES:(g + 1) * LANES])
                            for g in range(y.shape[1] // LANES)], axis=1)


def _long_conv_kernel(zx0_ref, zx1_ref, kf_ref, g_ref, h_ref, wf_ref, wi_ref, bias_ref, gain_ref, o_ref,
                      s0_ref, s1_ref, *, n_a):
    s = pl.program_id(1)
    half = n_a // 2
    nk = g_ref.shape[1] // 2
    kb = kf_ref.shape[1]
    nkb = nk // kb

    def load_zx(b):
        return jnp.concatenate([zx0_ref[pl.ds(b, half, stride=DFT_B), :],
                                zx1_ref[pl.ds(b, half, stride=DFT_B), :]], axis=1)

    def load_stage(rows):
        return _unpack_pair(jnp.concatenate([s0_ref[rows, :], s1_ref[rows, :]], axis=1))

    def store_stage(rows, re, im):
        w = _pack_pair(re, im)
        s0_ref[rows, :] = w[:, :LANES]
        s1_ref[rows, :] = w[:, LANES:]

    @pl.when(s == 0)
    def _():
        def body(b, carry):
            z, _ = _unpack_pair(load_zx(b))
            r = jnp.dot(g_ref[b], z, preferred_element_type=F32)
            store_stage(pl.ds(b, nk, stride=_PITCH), r[:nk], r[nk:])
            return carry

        lax.fori_loop(0, DFT_B, body, 0, unroll=_B_UNROLL)

    @pl.when((s > 0) & (s <= nkb))
    def _():
        k0 = (s - 1) * kb
        for i in range(kb):
            rows = _stage_rows(k0 + i)
            re, im = load_stage(rows)
            zz = jnp.dot(wf_ref[...], jnp.concatenate([re, im], axis=0), preferred_element_type=F32)
            zr, zi = zz[:DFT_B], zz[DFT_B:]
            kr, ki = kf_ref[0, i], kf_ref[1, i]
            y = jnp.concatenate([zr * kr - zi * ki, zr * ki + zi * kr], axis=0).astype(BF16)
            cc = jnp.dot(wi_ref[...], y, preferred_element_type=F32)
            store_stage(rows, cc[:DFT_B], cc[DFT_B:])

    @pl.when(s == nkb + 1)
    def _():
        def body(b, carry):
            re, im = load_stage(pl.ds(b, nk, stride=_PITCH))
            y = jnp.dot(h_ref[b], jnp.concatenate([re, im], axis=0), preferred_element_type=F32)
            o_ref[pl.ds(b, half, stride=DFT_B), :] = _pack_groups(y, load_zx(b), bias_ref[...], gain_ref[...])
            return carry

        lax.fori_loop(0, DFT_B, body, 0, unroll=_B_UNROLL)


def _long_conv(zx, kf, g_z, h_tab, wf, wi, bias, gain):
    l, c = zx.shape
    n_a = 2 * l // DFT_B
    nk = g_z.shape[1] // 2
    kb = min(nk, _K1_BLOCK)
    nkb = nk // kb
    cw = 2 * LANES
    return pl.pallas_call(
        functools.partial(_long_conv_kernel, n_a=n_a),
        grid=(c // cw, nkb + 2),
        in_specs=[_single((l, LANES), lambda ci, s: (0, 2 * ci)), _single((l, LANES), lambda ci, s: (0, 2 * ci + 1)),
                  pl.BlockSpec((2, kb, DFT_B, cw), lambda ci, s: (0, jnp.clip(s - 1, 0, nkb - 1), 0, ci)),
                  _single(g_z.shape, lambda ci, s: (0, 0, 0)), _single(h_tab.shape, lambda ci, s: (0, 0, 0)),
                  _single(wf.shape, lambda ci, s: (0, 0)), _single(wi.shape, lambda ci, s: (0, 0)),
                  pl.BlockSpec((1, cw), lambda ci, s: (0, ci)), pl.BlockSpec((1, cw), lambda ci, s: (0, ci))],
        out_specs=pl.BlockSpec((l, LANES), lambda ci, s: (0, ci)),
        out_shape=jax.ShapeDtypeStruct((l, c // 2), jnp.uint32),
        scratch_shapes=[pltpu.VMEM((nk * _PITCH, LANES), jnp.uint32), pltpu.VMEM((nk * _PITCH, LANES), jnp.uint32)],
        compiler_params=_params("parallel", "arbitrary"),
        name="long_conv",
    )(zx, zx, kf, g_z, h_tab, wf, wi, bias.reshape(1, c), gain.reshape(1, c))


def _hyena_long(zx, filt, bias, gain):
    l, c = zx.shape
    n_a = 2 * l // DFT_B
    wf, wi = _dft128_embed()
    wf, wi = wf.astype(BF16), wi.astype(BF16)
    g_z = _dft_tables(n_a, n_a // 2).astype(BF16)
    g_k = _dft_tables(n_a, n_a).astype(BF16)
    h_tab = _idft_tables(n_a, n_a // 2).astype(BF16)
    hperm = _filter_hidden_permuted(l, filt['hy_w1'], filt['hy_b1'], filt['hy_freq'], filt['hy_w2'], filt['hy_b2'])
    kf = _filter_spectrum(hperm, filt['hy_w3'], l, c, g_k, wf)
    return _long_conv(zx, kf, g_z, h_tab, wf, wi, bias, gain)


def _ctx_conv_kernel(zx_ref, k_ref, fz_ref, fk_ref, hi_ref, bias_ref, gain_ref, o_ref):
    n = k_ref.shape[0]
    zx = zx_ref[...]
    z, _ = _unpack_pair(zx)
    zs = jnp.dot(fz_ref[...], z, preferred_element_type=F32)
    ks = jnp.dot(fk_ref[...], k_ref[...].astype(BF16), preferred_element_type=F32)
    zr, zi, kr, ki = zs[:n], zs[n:], ks[:n], ks[n:]
    y = jnp.concatenate([zr * kr - zi * ki, zr * ki + zi * kr], axis=0).astype(BF16)
    yt = jnp.dot(hi_ref[...], y, preferred_element_type=F32)
    o_ref[...] = _pack_groups(yt, zx, bias_ref[...], gain_ref[...])


def _hyena_short(zx, k2, bias, gain):
    l, c = zx.shape
    n = 2 * l
    k = np.arange(n)[:, None]
    t = np.arange(n)[None, :]
    e = np.exp(-2j * np.pi * ((k * t) % n) / n)
    fk = jnp.asarray(np.concatenate([e.real, e.imag], axis=0), F32).astype(BF16)
    fz = fk[:, :l]
    ei = np.conj(e[:, :l]).T / n
    hi = jnp.asarray(np.concatenate([ei.real, -ei.imag], axis=1), F32).astype(BF16)
    cw = 2 * LANES
    col = lambda rows: pl.BlockSpec((rows, cw), lambda ci: (0, ci))
    full = lambda a: pl.BlockSpec(a.shape, lambda ci: (0, 0))
    return pl.pallas_call(
        _ctx_conv_kernel,
        grid=(c // cw,),
        in_specs=[col(l), col(n), full(fz), full(fk), full(hi), col(1), col(1)],
        out_specs=pl.BlockSpec((l, LANES), lambda ci: (0, ci)),
        out_shape=jax.ShapeDtypeStruct((l, c // 2), jnp.uint32),
        compiler_params=_params("parallel"),
        name="hyena_short",
    )(zx, k2, fz, fk, hi, bias.reshape(1, c), gain.reshape(1, c))


def _split3(x):
    hi = x.astype(BF16)
    r1 = x - hi.astype(F32)
    mid = r1.astype(BF16)
    lo = (r1 - mid.astype(F32)).astype(BF16)
    return hi, mid, lo


def _gla_kernel(*refs, reverse, combine):
    if combine:
        (k_ref, q_ref, vlo_ref, vhi_ref, r_ref, wg_ref, bg_ref, s0_ref, of_ref, gate_ref, gout_ref,
         o_ref, sfin_ref, st_ref, la_ref, qe_ref, upd_ref, dec_ref, ke_ref, kl_ref, q2_ref, sc_ref, ob_ref) = refs
    else:
        (k_ref, q_ref, vlo_ref, vhi_ref, r_ref, wg_ref, bg_ref, s0_ref,
         o_ref, sfin_ref, st_ref, la_ref, qe_ref, upd_ref, dec_ref, ke_ref, kl_ref, q2_ref, sc_ref) = refs
        ob_ref = o_ref
    v_half = (vlo_ref, vhi_ref)
    rows_blk = k_ref.shape[0]
    nc = rows_blk // GLA_CHUNK
    qk = k_ref.shape[1]

    @pl.when(pl.program_id(0) == 0)
    def _():
        st_ref[...] = s0_ref[...]

    ri = lax.broadcasted_iota(jnp.int32, (2 * GLA_CHUNK, GLA_CHUNK), 0) % GLA_CHUNK
    ci = lax.broadcasted_iota(jnp.int32, (2 * GLA_CHUNK, GLA_CHUNK), 1)
    r3 = lax.broadcasted_iota(jnp.int32, (GLA_CHUNK, 3 * GLA_CHUNK), 0)
    c3 = lax.broadcasted_iota(jnp.int32, (GLA_CHUNK, 3 * GLA_CHUNK), 1) % GLA_CHUNK
    if reverse:
        tri3 = (c3 >= r3).astype(BF16)
        keep = ci > ri
    else:
        tri3 = (c3 <= r3).astype(BF16)
        keep = ci <= ri
    lane = lax.broadcasted_iota(jnp.int32, (1, LANES), 1)
    first = (lane < GLA_DK).astype(F32)
    srow = lax.broadcasted_iota(jnp.int32, (2 * GLA_DV, LANES), 0)
    scol = lax.broadcasted_iota(jnp.int32, (2 * GLA_DV, LANES), 1)
    own = (srow < GLA_DV) == (scol < GLA_DK)
    nt = (((1,), (1,)), ((), ()))
    tn = (((0,), (0,)), ((), ()))
    npair = GLA_HEADS // 2

    rh, rm, _ = _split3(r_ref[...])
    wh, wm, _ = _split3(wg_ref[...])
    pre = jnp.dot(jnp.concatenate([rh, rh, rm], axis=1), jnp.concatenate([wh, wm, wh], axis=0),
                  preferred_element_type=F32) + bg_ref[...]
    la_ref[...] = jax.nn.log_sigmoid(pre) / GLA_GATE_TAU

    def chunk_rows(c):
        return pl.ds(pl.multiple_of(c * GLA_CHUNK, GLA_CHUNK), GLA_CHUNK)

    def cum_pass(c, carry):
        rows = chunk_rows(c)
        hi, mid, lo = _split3(la_ref[rows, :])
        la_ref[rows, :] = jnp.dot(tri3, jnp.concatenate([hi, mid, lo], axis=0), preferred_element_type=F32)
        return carry

    def decay_pass(c, carry):
        rows = chunk_rows(c)
        cum = la_ref[rows, :]
        tot = cum[0:1] if reverse else cum[GLA_CHUNK - 1:GLA_CHUNK]
        k = k_ref[rows, :]
        qe = q_ref[rows, :] * (GLA_DK ** -0.5) * jnp.exp(cum)
        ke_ref[rows, :] = (k * jnp.exp(-cum)).astype(BF16)
        kl_ref[rows, :] = (k * jnp.exp(tot - cum)).astype(BF16)
        dec_ref[pl.ds(c, 1), :] = jnp.exp(tot)
        qe_ref[rows, :] = qe.astype(BF16)
        for p in range(npair):
            pr = slice(p * LANES, (p + 1) * LANES)
            q2_ref[c * npair + p] = jnp.concatenate([qe[:, pr] * first, qe[:, pr] * (1.0 - first)],
                                                    axis=0).astype(BF16)
        return carry

    def score_pass(c, carry):
        rows = chunk_rows(c)
        for p in range(npair):
            pr = slice(p * LANES, (p + 1) * LANES)
            sc = lax.dot_general(q2_ref[c * npair + p], ke_ref[rows, pr], nt, preferred_element_type=F32)
            sc_ref[c * npair + p] = jnp.where(keep, sc, 0.0).astype(BF16)
        return carry

    def value_pass(c, carry):
        rows = chunk_rows(c)
        for p in range(npair):
            pr = slice(p * LANES, (p + 1) * LANES)
            pv = slice(2 * p * GLA_DV, 2 * (p + 1) * GLA_DV)
            v_p = v_half[p // 2][rows, pl.ds((p % 2) * 2 * GLA_DV, 2 * GLA_DV)].astype(BF16)
            o2 = jnp.dot(sc_ref[c * npair + p], v_p, preferred_element_type=F32)
            ob_ref[rows, pl.ds(2 * p * GLA_DV, GLA_DV)] = o2[:GLA_CHUNK, :GLA_DV]
            ob_ref[rows, pl.ds((2 * p + 1) * GLA_DV, GLA_DV)] = o2[GLA_CHUNK:, GLA_DV:]
            u2 = lax.dot_general(v_p, kl_ref[rows, pr], tn, preferred_element_type=F32)
            upd_ref[c * npair + p] = jnp.where(own, u2, 0.0)
        return carry

    lax.fori_loop(0, nc, cum_pass, 0, unroll=True)
    lax.fori_loop(0, nc, decay_pass, 0, unroll=2)
    lax.fori_loop(0, nc, score_pass, 0, unroll=True)
    lax.fori_loop(0, nc, value_pass, 0, unroll=4)

    def carried(idx, carry):
        c = (nc - 1 - idx) if reverse else idx
        rows = pl.ds(pl.multiple_of(c * GLA_CHUNK, GLA_CHUNK), GLA_CHUNK)
        dec = dec_ref[pl.ds(c, 1), :]
        for p in range(npair):
            pr = slice(p * LANES, (p + 1) * LANES)
            pv = slice(2 * p * GLA_DV, 2 * (p + 1) * GLA_DV)
            st = st_ref[p]
            ob_ref[rows, pv] += lax.dot_general(qe_ref[rows, pr], st.astype(BF16), nt, preferred_element_type=F32)
            st_ref[p] = st * dec[:, pr] + upd_ref[c * npair + p]
        return carry

    lax.fori_loop(0, nc, carried, 0, unroll=4)

    if combine:
        for h in range(GLA_HEADS):
            hv = slice(h * GLA_DV, (h + 1) * GLA_DV)
            o = of_ref[:, hv] + ob_ref[:, hv]
            ms = jnp.mean(o * o, axis=-1, keepdims=True)
            o = o * lax.rsqrt(ms + NORM_EPS) * gout_ref[...]
            o_ref[:, hv] = (o * _silu(gate_ref[:, hv])).astype(o_ref.dtype)

    @pl.when(pl.program_id(0) == pl.num_programs(0) - 1)
    def _():
        sfin_ref[...] = st_ref[...]


_GLA_STATE = (GLA_HEADS // 2, 2 * GLA_DV, LANES)

_COL_K, _COL_V, _COL_Q, _COL_G = 24, 28, 36, 40


def _gla_pass(p, p_r, wg_pad, bg, s0, reverse, of=None, gout=None):
    l = p.shape[0]
    rb = min(l, 512)
    nblk = l // rb
    qk = GLA_HEADS * GLA_DK
    vw = GLA_HEADS * GLA_DV
    combine = of is not None
    rmap = (lambda i: nblk - 1 - i) if reverse else (lambda i: i)

    def cols(width, off128):
        return pl.BlockSpec((rb, width), lambda i: (rmap(i), off128 * LANES // width))

    state = pl.BlockSpec(_GLA_STATE, lambda i: (0, 0, 0))
    in_specs = [cols(qk, _COL_K), cols(qk, _COL_Q), cols(vw // 2, _COL_V), cols(vw // 2, _COL_V + vw // 2 // LANES),
                pl.BlockSpec((rb, LANES), lambda i: (rmap(i), 0)),
                pl.BlockSpec((LANES, qk), lambda i: (0, 0)), pl.BlockSpec((1, qk), lambda i: (0, 0)), state]
    args = [p, p, p, p, p_r, wg_pad, bg.reshape(1, qk), s0]
    nc = rb // GLA_CHUNK
    npair = _GLA_STATE[0]
    scratch = [pltpu.VMEM(_GLA_STATE, F32),
               pltpu.VMEM((rb, qk), F32),
               pltpu.VMEM((rb, qk), BF16),
               pltpu.VMEM((nc * npair,) + _GLA_STATE[1:], F32),
               pltpu.VMEM((nc, qk), F32),
               pltpu.VMEM((rb, qk), BF16),
               pltpu.VMEM((rb, qk), BF16),
               pltpu.VMEM((nc * npair, 2 * GLA_CHUNK, LANES), BF16),
               pltpu.VMEM((nc * npair, 2 * GLA_CHUNK, GLA_CHUNK), BF16)]
    if combine:
        in_specs += [pl.BlockSpec((rb, vw), lambda i: (rmap(i), 0)), cols(vw, _COL_G),
                     pl.BlockSpec((1, GLA_DV), lambda i: (0, 0))]
        args += [of, p, gout.reshape(1, GLA_DV)]
        scratch.append(pltpu.VMEM((rb, vw), F32))
    out_dtype = BF16 if combine else F32
    return pl.pallas_call(
        functools.partial(_gla_kernel, reverse=reverse, combine=combine),
        grid=(nblk,),
        in_specs=in_specs,
        out_specs=[pl.BlockSpec((rb, vw), lambda i: (rmap(i), 0)), state],
        out_shape=[jax.ShapeDtypeStruct((l, vw), out_dtype),
                   jax.ShapeDtypeStruct(_GLA_STATE, F32)],
        scratch_shapes=scratch,
        compiler_params=_params("arbitrary"),
        name="gla_bwd" if reverse else "gla_fwd",
    )(*args)


def _out_proj_kernel(hy_ref, o_ref, w_ref, x_ref, ga_ref, g_ref, sh_ref, sc_ref, xo_ref, h_ref):
    q = hy_ref.shape[1]
    hy_a, hy_b = _unpack_pair(hy_ref[...])
    mix = (jnp.dot(hy_a, w_ref[:q, :], preferred_element_type=F32)
           + jnp.dot(hy_b, w_ref[q:2 * q, :], preferred_element_type=F32)
           + jnp.dot(o_ref[...], w_ref[2 * q:, :], preferred_element_type=F32))
    xn = x_ref[...] + ga_ref[...] * mix
    xo_ref[...] = xn
    h_ref[...] = _norm_mod(xn, g_ref[...], sh_ref[...], sc_ref[...]).astype(BF16)


def _out_proj(hy, o, w, x, ga, g, shift, scale, layer):
    m, d = x.shape
    tm = min(m, 256)
    vec = pl.BlockSpec((1, d), lambda i: (0, 0))
    row = lambda width: pl.BlockSpec((tm, width), lambda i: (i, 0))
    return pl.pallas_call(
        _out_proj_kernel,
        grid=(m // tm,),
        in_specs=[row(hy.shape[1]), row(o.shape[1]), pl.BlockSpec((None, w.shape[1], d), lambda i: (layer, 0, 0)),
                  row(d), vec, vec, vec, vec],
        out_specs=[row(d), row(d)],
        out_shape=[jax.ShapeDtypeStruct((m, d), F32), jax.ShapeDtypeStruct((m, d), BF16)],
        compiler_params=_params("parallel"),
        name="out_proj",
    )(hy, o, w, x, ga, g, shift, scale)


_HALO = GRID_W + SUBLANES


def _ffn_up_conv_kernel(x_ref, wa_ref, wu_ref, cw_ref, o_ref, a_ref, u_ref, wb_ref, *, vertical, cols, nm):
    t = pl.program_id(0)
    total = pl.num_programs(0) - 1
    i = lax.rem(t, nm)
    tt = x_ref.shape[0]
    cur = lax.rem(t, 2)
    prv = 1 - cur

    @pl.when(t == 0)
    def _():
        a_ref[...] = jnp.zeros_like(a_ref)
        u_ref[...] = jnp.zeros_like(u_ref)

    def conv_previous_tile():
        col = lax.broadcasted_iota(jnp.int32, o_ref.shape, 0) & (cols - 1)
        acc = None
        for dc in (-1, 0, 1):
            part = None
            for dr in ((-1, 0, 1) if vertical else (0,)):
                tap = cw_ref[3 * (dr + 1) + (dc + 1):3 * (dr + 1) + (dc + 1) + 1, :]
                term = tap * a_ref[prv, pl.ds(_HALO + GRID_W * dr + dc, tt), :]
                part = term if part is None else part + term
            if dc == -1:
                part = jnp.where(col == 0, 0.0, part)
            elif dc == 1:
                part = jnp.where(col == cols - 1, 0.0, part)
            acc = part if acc is None else acc + part
        o_ref[...] = (_silu(acc) * u_ref[prv].astype(F32)).astype(o_ref.dtype)

    @pl.when((i == 0) & (t < total))
    def _():
        wb_ref[0] = wa_ref[...].astype(BF16)
        wb_ref[1] = wu_ref[...].astype(BF16)

    @pl.when(t < total)
    def _():
        x = x_ref[...]
        a_new = jnp.dot(x, wb_ref[0], preferred_element_type=F32)
        u_ref[cur] = jnp.dot(x, wb_ref[1], preferred_element_type=F32).astype(BF16)
        upper = a_ref[prv, pl.ds(_HALO + tt - GRID_W, GRID_W), :]
        a_ref[cur, pl.ds(SUBLANES, GRID_W), :] = jnp.where(i == 0, 0.0, upper)
        a_ref[cur, pl.ds(_HALO, tt), :] = a_new
        a_ref[prv, pl.ds(_HALO + tt, GRID_W), :] = jnp.where(i == 0, 0.0, a_new[:GRID_W])
        conv_previous_tile()

    @pl.when(t == total)
    def _():
        a_ref[prv, pl.ds(_HALO + tt, GRID_W), :] = jnp.zeros((GRID_W, a_ref.shape[2]), F32)
        conv_previous_tile()


def _ffn_up_conv(h, w_up, conv_w, cols, layer):
    m, d = h.shape
    dff = conv_w.shape[-1]
    vertical = m > cols
    assert cols & (cols - 1) == 0 and (cols == GRID_W or not vertical)
    tt = min(m, 1024)
    assert tt % cols == 0
    cb = 512
    ncb = dff // cb
    nm = m // tt
    total = ncb * nm
    cur_t = lambda t: jnp.minimum(t, total - 1)
    prev_t = lambda t: jnp.maximum(t - 1, 0)
    return pl.pallas_call(
        functools.partial(_ffn_up_conv_kernel, vertical=vertical, cols=cols, nm=nm),
        grid=(total + 1,),
        in_specs=[pl.BlockSpec((tt, d), lambda t: (cur_t(t) % nm, 0)),
                  pl.BlockSpec((None, d, cb), lambda t: (layer, 0, cur_t(t) // nm)),
                  pl.BlockSpec((None, d, cb), lambda t: (layer, 0, ncb + cur_t(t) // nm)),
                  pl.BlockSpec((9, cb), lambda t: (0, prev_t(t) // nm))],
        out_specs=pl.BlockSpec((tt, cb), lambda t: (prev_t(t) % nm, prev_t(t) // nm)),
        out_shape=jax.ShapeDtypeStruct((m, dff), BF16),
        scratch_shapes=[pltpu.VMEM((2, tt + 2 * _HALO, cb), F32), pltpu.VMEM((2, tt, cb), BF16),
                        pltpu.VMEM((2, d, cb), BF16)],
        compiler_params=_params("arbitrary"),
        name="ffn_up_conv",
    )(h, w_up, w_up, conv_w.reshape(9, dff))


def _ffn_down_kernel(g_ref, w_ref, x_ref, gate_ref, ng_ref, nsh_ref, nsc_ref, o_ref, *h_ref, post):
    xn = x_ref[...] + gate_ref[...] * jnp.dot(g_ref[...], w_ref[...], preferred_element_type=F32)
    if post == 'final':
        ms = jnp.mean(xn * xn, axis=-1, keepdims=True)
        xn = xn * lax.rsqrt(ms + NORM_EPS) * ng_ref[...]
    o_ref[...] = xn
    if post == 'next':
        h_ref[0][...] = _norm_mod(xn, ng_ref[...], nsh_ref[...], nsc_ref[...]).astype(BF16)


def _ffn_down(g, w, x, gate, norm, post, layer):
    m, d = x.shape
    dff = g.shape[1]
    tm = min(m, 256)
    vec = pl.BlockSpec((1, d), lambda i: (0, 0))
    row = pl.BlockSpec((tm, d), lambda i: (i, 0))
    out_specs, out_shape = [row], [jax.ShapeDtypeStruct((m, d), F32)]
    if post == 'next':
        out_specs.append(row)
        out_shape.append(jax.ShapeDtypeStruct((m, d), BF16))
    return pl.pallas_call(
        functools.partial(_ffn_down_kernel, post=post),
        grid=(m // tm,),
        in_specs=[pl.BlockSpec((tm, dff), lambda i: (i, 0)),
                  _single((None, dff, d), lambda i: (layer, 0, 0)),
                  row, vec, vec, vec, vec],
        out_specs=out_specs,
        out_shape=out_shape,
        compiler_params=_params("parallel"),
        name="ffn_down",
    )(g, w, x, gate, *norm)


def _regroup_w_in(w_in_t, hy_proj):
    qk = GLA_HEADS * GLA_DK
    vw = GLA_HEADS * GLA_DV
    head = hy_proj + qk + vw
    r = w_in_t[:, head:head + 2 * GLA_RANK]
    tail = w_in_t[:, head + 2 * GLA_RANK:]
    pad = jnp.zeros((w_in_t.shape[0], LANES - 2 * GLA_RANK, w_in_t.shape[2]), w_in_t.dtype)
    return head, tail, jnp.concatenate([r, pad], axis=1)


def _pad_gate_w(wg, lane0):
    return jnp.zeros((LANES, wg.shape[1]), F32).at[lane0:lane0 + GLA_RANK].set(wg)


def _mixer(p, p_r, lp, s0_f, s0_b, want_output):
    l = p.shape[0]
    width = lp['hy_bias'].shape[0]
    o_f, s_f = _gla_pass(p, p_r, lp['wg_f_pad'], lp['gla_bg_f'], s0_f, reverse=False)
    o, s_b = _gla_pass(p, p_r, lp['wg_b_pad'], lp['gla_bg_b'], s0_b, reverse=True, of=o_f, gout=lp['gla_out_g'])
    if not want_output:
        return None, None, s_f, s_b
    zx = _hy_pre(p, lp['hy_conv'], width)
    if 2 * l // DFT_B >= 2 * SUBLANES:
        hy = _hyena_long(zx, lp, lp['hy_bias'], lp['hy_out_g'])
    else:
        k2 = _hyena_filter2(l, lp['hy_w1'], lp['hy_b1'], lp['hy_freq'], lp['hy_w2'], lp['hy_b2'], lp['hy_w3'], width)
        hy = _hyena_short(zx, k2, lp['hy_bias'], lp['hy_out_g'])
    return hy, o, s_f, s_b


def _order_w_out(w_out, hy_width):
    depth, _, d = w_out.shape
    hy_rows = w_out[:, :hy_width].reshape(depth, hy_width // (2 * LANES), 2, LANES, d)
    hy_rows = jnp.swapaxes(hy_rows, 1, 2).reshape(depth, hy_width, d)
    return jnp.concatenate([hy_rows, w_out[:, hy_width:]], axis=1).astype(BF16)


def _ffn(h2, x, gate, lp, cols, norm, post, layer):
    gact = _ffn_up_conv(h2, lp['ffn_w_up'], lp['ffn_conv'], cols, layer)
    return _ffn_down(gact, lp['ffn_w_down'], x, gate, norm, post, layer)


def kernel(x, c, ctx, c_ctx, w_mod, b_mod, g_attn, w_in, hy_conv, hy_w1, hy_b1, hy_freq, hy_w2, hy_b2, hy_w3,
           hy_bias, hy_out_g, gla_wg_f, gla_bg_f, gla_wg_b, gla_bg_b, gla_out_g, w_out, g_ffn, ffn_w_up,
           ffn_conv, ffn_w_down, g_final):
    depth = w_mod.shape[0]
    batch, seq, d = x.shape
    assert batch == 1 and c.shape[0] == 1 and ctx.shape[0] == 1
    hy_proj = hy_conv.shape[-1]
    xs = x[0]
    cs = ctx[0]
    cc = jnp.zeros((SUBLANES, d), F32).at[0].set(c[0]).at[1].set(c_ctx)
    zero_state = jnp.zeros(_GLA_STATE, F32)
    gfin = g_final.reshape(1, d)
    w_in = jnp.swapaxes(w_in, 1, 2)
    n_head_cols, w_tail, w_rank = _regroup_w_in(w_in, hy_proj)
    w_out_b = _order_w_out(w_out, hy_bias.shape[-1])
    w_down_b = ffn_w_down.astype(BF16)
    mods =[_adaln(cc, w_mod, b_mod, l) for l in range(depth)]

    def attn_norm(l, row):
        return (g_attn[l].reshape(1, d), mods[l][row:row + 1, 0:d], mods[l][row:row + 1, d:2 * d])

    hx = _norm_mod_call(xs, *attn_norm(0, 0))
    hc = _norm_mod_call(cs, *attn_norm(0, 1))
    for l in range(depth):
        last = l == depth - 1
        lp = {
            'hy_conv': hy_conv[l], 'hy_w1': hy_w1[l], 'hy_b1': hy_b1[l], 'hy_freq': hy_freq[l],
            'hy_w2': hy_w2[l], 'hy_b2': hy_b2[l], 'hy_w3': hy_w3[l], 'hy_bias': hy_bias[l],
            'hy_out_g': hy_out_g[l], 'gla_bg_f': gla_bg_f[l], 'gla_bg_b': gla_bg_b[l],
            'gla_out_g': gla_out_g[l],
            'wg_f_pad': _pad_gate_w(gla_wg_f[l], 0), 'wg_b_pad': _pad_gate_w(gla_wg_b[l], GLA_RANK),
            'ffn_w_up': ffn_w_up, 'ffn_conv': ffn_conv[l], 'ffn_w_down': w_down_b,
        }
        mod = mods[l]
        _, _, ga, sf, scf, gf = [mod[0:1, i * d:(i + 1) * d] for i in range(6)]
        _, _, cga, csf, cscf, cgf = [mod[1:2, i * d:(i + 1) * d] for i in range(6)]
        gf_row = g_ffn[l].reshape(1, d)

        pc = _in_proj(hc, w_in, w_tail, n_head_cols, l)
        hy_c, o_c, s_f, s_b = _mixer(pc, _gate_rank_proj(hc, w_rank, l), lp, zero_state, zero_state,
                                     want_output=not last)
        if not last:
            cs, h2c = _out_proj(hy_c, o_c, w_out_b, cs, cga, gf_row, csf, cscf, l)
            cs, hc = _ffn(h2c, cs, cgf, lp, cs.shape[0], attn_norm(l + 1, 1), 'next', l)

        px = _in_proj(hx, w_in, w_tail, n_head_cols, l)
        hy_x, o_x, _, _ = _mixer(px, _gate_rank_proj(hx, w_rank, l), lp, s_f, s_b, want_output=True)
        xs, h2 = _out_proj(hy_x, o_x, w_out_b, xs, ga, gf_row, sf, scf, l)
        if last:
            (xs,) = _ffn(h2, xs, gf, lp, GRID_W, (gfin, gfin, gfin), 'final', l)
        else:
            xs, hx = _ffn(h2, xs, gf, lp, GRID_W, attn_norm(l + 1, 0), 'next', l)
    return xs[None]
```

```python
import functools
import math

import numpy as np
import jax
import jax.numpy as jnp
from jax import lax
from jax.experimental import pallas as pl
from jax.experimental.pallas import tpu as pltpu

F32 = jnp.float32
BF16 = jnp.bfloat16
HIGHEST = lax.Precision.HIGHEST

LANES = 128
SUBLANES = 8
VMEM_BYTES_V7X = 64 * 1024 * 1024
VMEM_LIMIT = VMEM_BYTES_V7X - 12 * 1024 * 1024

NORM_EPS = 1e-6
HYENA_GROUP = 128
FILTER_BANDS = 16
FILTER_HIDDEN = 64
FILTER_TARGET = 1e-2
FILTER_FAST_PCT = 0.3
FILTER_SLOW_PCT = 1.5
GLA_HEADS = 8
GLA_DK = 64
GLA_DV = 128
GLA_RANK = 16
GLA_GATE_TAU = 16.0
GLA_CHUNK = 64
GRID_W = 64
DFT_B = 128


def _params(*sem):
    return pltpu.CompilerParams(dimension_semantics=sem, vmem_limit_bytes=VMEM_LIMIT)


def _silu(x):
    return x * jax.nn.sigmoid(x)


def _adaln_kernel(c_ref, w_ref, b_ref, o_ref):
    s = _silu(c_ref[...]).astype(BF16)
    o_ref[...] = jnp.dot(s, w_ref[...].astype(BF16), preferred_element_type=F32) + b_ref[...]


def _adaln(cc, w_mod, b_mod, layer):
    depth, d, n = w_mod.shape
    tn = 1024
    return pl.pallas_call(
        _adaln_kernel,
        grid=(n // tn,),
        in_specs=[pl.BlockSpec((SUBLANES, d), lambda j: (0, 0)),
                  pl.BlockSpec((None, d, tn), lambda j: (layer, 0, j)),
                  pl.BlockSpec((None, 1, tn), lambda j: (layer, 0, j))],
        out_specs=pl.BlockSpec((SUBLANES, tn), lambda j: (0, j)),
        out_shape=jax.ShapeDtypeStruct((SUBLANES, n), F32),
        compiler_params=_params("arbitrary"),
        name="adaln",
    )(cc, w_mod, b_mod.reshape(depth, 1, n))


def _norm_mod(x, g, shift, scale):
    ms = jnp.mean(x * x, axis=-1, keepdims=True)
    return (x * lax.rsqrt(ms + NORM_EPS) * g) * (1.0 + scale) + shift


def _norm_mod_kernel(x_ref, g_ref, sh_ref, sc_ref, o_ref):
    o_ref[...] = _norm_mod(x_ref[...], g_ref[...], sh_ref[...], sc_ref[...]).astype(o_ref.dtype)


def _norm_mod_call(x, g, shift, scale):
    m, d = x.shape
    tm = min(m, 512)
    vec = pl.BlockSpec((1, d), lambda i: (0, 0))
    return pl.pallas_call(
        _norm_mod_kernel,
        grid=(m // tm,),
        in_specs=[pl.BlockSpec((tm, d), lambda i: (i, 0)), vec, vec, vec],
        out_specs=pl.BlockSpec((tm, d), lambda i: (i, 0)),
        out_shape=jax.ShapeDtypeStruct((m, d), BF16),
        compiler_params=_params("parallel"),
        name="norm_mod",
    )(x, g, shift, scale)


_IN_TILE = 768


_NT = (((1,), (1,)), ((), ()))


def _in_proj_kernel(h_ref, w_ref, wt_ref, o_ref, wb_ref, *, n_head):
    j = pl.program_id(0)

    @pl.when((pl.program_id(1) == 0) & (j < n_head))
    def _():
        wb_ref[...] = w_ref[...].T.astype(BF16)

    @pl.when((pl.program_id(1) == 0) & (j >= n_head))
    def _():
        wb_ref[...] = wt_ref[...].T.astype(BF16)

    o_ref[...] = jnp.dot(h_ref[...], wb_ref[...], preferred_element_type=F32)


def _in_proj(h, w_in_t, w_tail_t, n_head_cols, layer):
    m, d = h.shape
    tn = _IN_TILE
    n_head = n_head_cols // tn
    n_tail = w_tail_t.shape[1] // tn
    assert n_head * tn == n_head_cols and n_tail * tn == w_tail_t.shape[1]
    tm = min(m, 1024)
    return pl.pallas_call(
        functools.partial(_in_proj_kernel, n_head=n_head),
        grid=(n_head + n_tail, m // tm),
        in_specs=[pl.BlockSpec((tm, d), lambda j, i: (i, 0)),
                  _single((None, tn, d), lambda j, i: (layer, jnp.minimum(j, n_head - 1), 0)),
                  _single((None, tn, d), lambda j, i: (layer, jnp.maximum(j - n_head, 0), 0))],
        out_specs=pl.BlockSpec((tm, tn), lambda j, i: (i, j)),
        out_shape=jax.ShapeDtypeStruct((m, n_head_cols + w_tail_t.shape[1]), F32),
        scratch_shapes=[pltpu.VMEM((d, tn), BF16)],
        compiler_params=_params("arbitrary", "arbitrary"),
        name="in_proj",
    )(h, w_in_t, w_tail_t)


def _gate_rank_kernel(h_ref, w_ref, o_ref):
    o_ref[...] = lax.dot_general(h_ref[...], w_ref[...].astype(BF16), _NT, preferred_element_type=F32)


def _gate_rank_proj(h, w_r_t, layer):
    m, d = h.shape
    tm = min(m, 2048)
    return pl.pallas_call(
        _gate_rank_kernel,
        grid=(m // tm,),
        in_specs=[pl.BlockSpec((tm, d), lambda i: (i, 0)), pl.BlockSpec((None, LANES, d), lambda i: (layer, 0, 0))],
        out_specs=pl.BlockSpec((tm, LANES), lambda i: (i, 0)),
        out_shape=jax.ShapeDtypeStruct((m, LANES), F32),
        compiler_params=_params("parallel"),
        name="gate_rank_proj",
    )(h, w_r_t)


def _hy_pre_kernel(x0_ref, x1_ref, v_ref, x0p_ref, x1p_ref, vp_ref, x0n_ref, x1n_ref, vn_ref,
                   w0_ref, w1_ref, wv_ref, zx_ref):
    i = pl.program_id(0)
    first = i == 0
    last = i == pl.num_programs(0) - 1
    tt = x0_ref.shape[0]
    row = lax.broadcasted_iota(jnp.int32, x0_ref.shape, 0)

    def conv(ref, prev_ref, next_ref, w_ref):
        x = ref[...]
        prev_row = jnp.where(first, 0.0, prev_ref[SUBLANES - 1:SUBLANES, :])
        next_row = jnp.where(last, 0.0, next_ref[0:1, :])
        below = jnp.where(row == 0, prev_row, pltpu.roll(x, 1, 0))
        above = jnp.where(row == tt - 1, next_row, pltpu.roll(x, tt - 1, 0))
        return w_ref[0:1, :] * below + w_ref[1:2, :] * x + w_ref[2:3, :] * above

    z = conv(v_ref, vp_ref, vn_ref, wv_ref) * conv(x1_ref, x1p_ref, x1n_ref, w1_ref)
    zx_ref[...] = _pack_pair(z, conv(x0_ref, x0p_ref, x0n_ref, w0_ref))


def _hy_pre(p, hy_conv, width):
    l = p.shape[0]
    tt = min(l, 1024)
    cb = 512
    ncb = width // cb
    nb8 = l // SUBLANES
    t8 = tt // SUBLANES

    def main(g):
        return pl.BlockSpec((tt, cb), lambda i, j: (i, g * ncb + j))

    def prev(g):
        return pl.BlockSpec((SUBLANES, cb), lambda i, j: (jnp.maximum(i * t8 - 1, 0), g * ncb + j))

    def nxt(g):
        return pl.BlockSpec((SUBLANES, cb), lambda i, j: (jnp.minimum((i + 1) * t8, nb8 - 1), g * ncb + j))

    def wspec(g):
        return pl.BlockSpec((3, cb), lambda i, j: (0, g * ncb + j))

    out = pl.BlockSpec((tt, cb), lambda i, j: (i, j))
    return pl.pallas_call(
        _hy_pre_kernel,
        grid=(l // tt, ncb),
        in_specs=[main(0), main(1), main(2), prev(0), prev(1), prev(2), nxt(0), nxt(1), nxt(2),
                  wspec(0), wspec(1), wspec(2)],
        out_specs=out,
        out_shape=jax.ShapeDtypeStruct((l, width), jnp.uint32),
        compiler_params=_params("parallel", "parallel"),
        name="hy_pre",
    )(p, p, p, p, p, p, p, p, p, hy_conv, hy_conv, hy_conv)


def _filter_hidden(j, seq_len, w1t_ref, w1c_ref, w1s_ref, b1_ref, fr_ref, w2_ref, b2_ref):
    pos_i = jnp.where(j < seq_len, j, 2 * seq_len - j)
    pos = pos_i.astype(F32)
    t = pos / float(max(seq_len - 1, 1))
    lane = lax.broadcasted_iota(jnp.int32, j.shape, 1)
    band = jnp.where(lane < FILTER_BANDS, lane + 1, 0).astype(F32)
    ang = (2.0 * math.pi / seq_len) * pos * band
    fr = fr_ref[...]
    pre = (t[:, 0:1] * w1t_ref[...]
           + jnp.dot(jnp.cos(ang), w1c_ref[...], precision=HIGHEST, preferred_element_type=F32)
           + jnp.dot(jnp.sin(ang), w1s_ref[...], precision=HIGHEST, preferred_element_type=F32)
           + b1_ref[...])
    h = jnp.sin(fr * pre)
    h = jnp.sin(fr * (jnp.dot(h, w2_ref[...], precision=HIGHEST, preferred_element_type=F32) + b2_ref[...]))
    return h, pos_i, t


def _filter_kernel(w1t_ref, w1c_ref, w1s_ref, b1_ref, fr_ref, w2_ref, b2_ref, w3_ref, delta_ref, o_ref, *, seq_len):
    tr = o_ref.shape[0]
    j = pl.program_id(0) * tr + lax.broadcasted_iota(jnp.int32, (tr, LANES), 0)
    h, pos_i, t = _filter_hidden(j, seq_len, w1t_ref, w1c_ref, w1s_ref, b1_ref, fr_ref, w2_ref, b2_ref)
    hh = jnp.dot(h.astype(BF16), w3_ref[...].astype(BF16), preferred_element_type=F32)
    decay = jnp.exp(-t[:, 0:1] * delta_ref[...])
    valid = pos_i[:, 0:1] != seq_len
    o_ref[...] = jnp.where(valid, hh * decay, 0.0)


_FILTER_PACK = LANES // FILTER_BANDS


def _filter_hidden_kernel(w1t_ref, w1c_ref, w1s_ref, b1_ref, fr_ref, w2_ref, b2_ref, o_ref, *, seq_len, n_a):
    tp = o_ref.shape[0]
    shift = n_a.bit_length() - 1

    def position(shape, lanes_per_pos):
        prow = pl.program_id(0) * tp + lax.broadcasted_iota(jnp.int32, shape, 0)
        r = prow * _FILTER_PACK + lax.broadcasted_iota(jnp.int32, shape, 1) // lanes_per_pos
        j = ((r & (n_a - 1)) * DFT_B) + (r >> shift)
        return jnp.where(j < seq_len, j, 2 * seq_len - j).astype(F32)

    pos = position((tp, LANES), FILTER_BANDS)
    band = (lax.broadcasted_iota(jnp.int32, (tp, LANES), 1) % FILTER_BANDS + 1).astype(F32)
    ang = (2.0 * math.pi / seq_len) * pos * band
    t = position(o_ref.shape, FILTER_HIDDEN) / float(max(seq_len - 1, 1))
    fr = fr_ref[...]
    pre = (t * w1t_ref[...]
           + jnp.dot(jnp.cos(ang), w1c_ref[...], precision=HIGHEST, preferred_element_type=F32)
           + jnp.dot(jnp.sin(ang), w1s_ref[...], precision=HIGHEST, preferred_element_type=F32)
           + b1_ref[...])
    h = jnp.sin(fr * pre)
    o_ref[...] = jnp.sin(fr * (jnp.dot(h, w2_ref[...], precision=HIGHEST, preferred_element_type=F32) + b2_ref[...]))


def _filter_mlp_args(w1, b1, freq, w2, b2):
    hid = FILTER_HIDDEN
    w1c = jnp.zeros((LANES, hid), F32).at[:FILTER_BANDS].set(w1[1:1 + FILTER_BANDS])
    w1s = jnp.zeros((LANES, hid), F32).at[:FILTER_BANDS].set(w1[1 + FILTER_BANDS:1 + 2 * FILTER_BANDS])
    full = lambda shape: pl.BlockSpec(shape, lambda i: (0, 0))
    specs = [full((1, hid)), full((LANES, hid)), full((LANES, hid)), full((1, hid)), full((1, hid)),
             full((hid, hid)), full((1, hid))]
    args = (w1[0:1], w1c, w1s, b1.reshape(1, hid), freq.reshape(1, hid), w2, b2.reshape(1, hid))
    return specs, args


def _filter_delta(width):
    return jnp.abs(jnp.linspace(math.log(FILTER_TARGET) / FILTER_SLOW_PCT,
                                math.log(FILTER_TARGET) / FILTER_FAST_PCT, width, dtype=F32)).reshape(1, width)


def _hyena_filter2(seq_len, w1, b1, freq, w2, b2, w3, width):
    n2 = 2 * seq_len
    tr = min(seq_len, 1024)
    nfwd = seq_len // tr
    specs, args = _filter_mlp_args(w1, b1, freq, w2, b2)
    return pl.pallas_call(
        functools.partial(_filter_kernel, seq_len=seq_len),
        grid=(n2 // tr,),
        in_specs=specs + [pl.BlockSpec((FILTER_HIDDEN, width), lambda i: (0, jnp.where(i < nfwd, 0, 1))),
                          pl.BlockSpec((1, width), lambda i: (0, 0))],
        out_specs=pl.BlockSpec((tr, width), lambda i: (i, 0)),
        out_shape=jax.ShapeDtypeStruct((n2, width), F32),
        compiler_params=_params("parallel"),
        name="hyena_filter",
    )(*args, w3, _filter_delta(width))


def _filter_hidden_permuted(seq_len, w1, b1, freq, w2, b2):
    n2 = 2 * seq_len
    n_a = n2 // DFT_B
    hid = FILTER_HIDDEN
    wide = _FILTER_PACK * hid
    rows = n2 // _FILTER_PACK
    tp = min(rows, 256)
    eye = jnp.eye(_FILTER_PACK, dtype=F32)
    tile = lambda v: jnp.tile(v.reshape(1, hid), (1, _FILTER_PACK))
    args = (tile(w1[0]), jnp.kron(eye, w1[1:1 + FILTER_BANDS]),
            jnp.kron(eye, w1[1 + FILTER_BANDS:1 + 2 * FILTER_BANDS]), tile(b1), tile(freq),
            jnp.kron(eye, w2), tile(b2))
    packed = pl.pallas_call(
        functools.partial(_filter_hidden_kernel, seq_len=seq_len, n_a=n_a),
        grid=(rows // tp,),
        in_specs=[pl.BlockSpec(a.shape, lambda i: (0, 0)) for a in args],
        out_specs=pl.BlockSpec((tp, wide), lambda i: (i, 0)),
        out_shape=jax.ShapeDtypeStruct((rows, wide), F32),
        compiler_params=_params("parallel"),
        name="hyena_filter_hidden",
    )(*args)
    return packed.reshape(n2, hid)


def _kept_k1(n_a):
    kept = n_a // 2 + 1
    return kept, -(-kept // SUBLANES) * SUBLANES


def _dft_tables(n_a, a_in):
    n = n_a * DFT_B
    kept, padded = _kept_k1(n_a)
    b = np.arange(DFT_B)[:, None, None]
    k1 = np.arange(padded)[None, :, None]
    a = np.arange(a_in)[None, None, :]
    e = np.exp(-2j * np.pi * ((k1 * (DFT_B * a + b)) % n) / n) * (k1 < kept)
    g = np.concatenate([e.real, e.imag], axis=1)
    return jnp.asarray(g, F32)


def _idft_tables(n_a, a_out):
    n = n_a * DFT_B
    kept, padded = _kept_k1(n_a)
    b = np.arange(DFT_B)[:, None, None]
    a = np.arange(a_out)[None, :, None]
    k1 = np.arange(padded)[None, None, :]
    weight = np.where((k1 == 0) | (k1 == n_a // 2), 1.0, 2.0) * (k1 < kept)
    e = np.exp(2j * np.pi * ((k1 * (DFT_B * a + b)) % n) / n) * weight / n
    h = np.concatenate([e.real, -e.imag], axis=2)
    return jnp.asarray(h, F32)


def _dft128_embed():
    i = np.arange(DFT_B)
    f = np.exp(-2j * np.pi * ((i[:, None] * i[None, :]) % DFT_B) / DFT_B)
    fwd = np.block([[f.real, -f.imag], [f.imag, f.real]])
    inv = np.block([[f.real, f.imag], [-f.imag, f.real]])
    return jnp.asarray(fwd, F32), jnp.asarray(inv, F32)


_PITCH = DFT_B + SUBLANES
_K1_BLOCK = 8
_B_UNROLL = 16


def _stage_rows(k1):
    return pl.ds(pl.multiple_of(k1 * _PITCH, SUBLANES), DFT_B)


def _single(shape, index_map):
    return pl.BlockSpec(shape, index_map, pipeline_mode=pl.Buffered(1))


def _pack_pair(re, im):
    hi = lax.bitcast_convert_type(re.astype(BF16).astype(F32), jnp.uint32)
    lo = lax.bitcast_convert_type(im.astype(BF16).astype(F32), jnp.uint32)
    return hi | (lo >> 16)


def _unpack_pair(w):
    re = lax.bitcast_convert_type(w & jnp.uint32(0xFFFF0000), F32)
    im = lax.bitcast_convert_type(w << 16, F32)
    return re.astype(BF16), im.astype(BF16)


def _spectrum_kernel(h_ref, w3f_ref, w3b_ref, delta_ref, g_ref, wf_ref, o_ref, s0_ref, s1_ref, *, seq_len, n_a):
    s = pl.program_id(1)
    half = n_a // 2
    cw = o_ref.shape[3]

    @pl.when(s == 0)
    def _():
        a_i = lax.broadcasted_iota(jnp.int32, (n_a, cw), 0)

        def body(b, carry):
            base = pl.multiple_of(b * n_a, SUBLANES)
            hf = jnp.dot(h_ref[pl.ds(base, half), :].astype(BF16), w3f_ref[...].astype(BF16),
                         preferred_element_type=F32)
            hb = jnp.dot(h_ref[pl.ds(base + half, half), :].astype(BF16), w3b_ref[...].astype(BF16),
                         preferred_element_type=F32)
            hh = jnp.concatenate([hf, hb], axis=0)
            j = a_i * DFT_B + b
            pos_i = jnp.where(j < seq_len, j, 2 * seq_len - j)
            t = pos_i.astype(F32) / float(max(seq_len - 1, 1))
            k2 = jnp.where(pos_i != seq_len, hh * jnp.exp(-t * delta_ref[...]), 0.0)
            r = jnp.dot(g_ref[b], k2.astype(BF16), preferred_element_type=F32)
            nk = r.shape[0] // 2
            w = _pack_pair(r[:nk], r[nk:])
            s0_ref[pl.ds(b, nk, stride=_PITCH), :] = w[:, :LANES]
            s1_ref[pl.ds(b, nk, stride=_PITCH), :] = w[:, LANES:]
            return carry

        lax.fori_loop(0, DFT_B, body, 0, unroll=_B_UNROLL)

    @pl.when(s > 0)
    def _():
        k0 = (s - 1) * o_ref.shape[1]
        for i in range(o_ref.shape[1]):
            rows = _stage_rows(k0 + i)
            re, im = _unpack_pair(jnp.concatenate([s0_ref[rows, :], s1_ref[rows, :]], axis=1))
            z = jnp.dot(wf_ref[...], jnp.concatenate([re, im], axis=0), preferred_element_type=F32)
            o_ref[0, i] = z[:DFT_B]
            o_ref[1, i] = z[DFT_B:]


def _filter_spectrum(hperm, w3, seq_len, width, g_k, wf):
    n_a = 2 * seq_len // DFT_B
    nk = g_k.shape[1] // 2
    kb = min(nk, _K1_BLOCK)
    cw = 2 * LANES
    ncb = width // cw
    hid = FILTER_HIDDEN
    return pl.pallas_call(
        functools.partial(_spectrum_kernel, seq_len=seq_len, n_a=n_a),
        grid=(ncb, 1 + nk // kb),
        in_specs=[_single(hperm.shape, lambda ci, s: (0, 0)),
                  pl.BlockSpec((hid, cw), lambda ci, s: (0, ci)),
                  pl.BlockSpec((hid, cw), lambda ci, s: (0, ncb + ci)),
                  pl.BlockSpec((1, cw), lambda ci, s: (0, ci)),
                  _single(g_k.shape, lambda ci, s: (0, 0, 0)),
                  _single(wf.shape, lambda ci, s: (0, 0))],
        out_specs=pl.BlockSpec((2, kb, DFT_B, cw), lambda ci, s: (0, jnp.maximum(s - 1, 0), 0, ci)),
        out_shape=jax.ShapeDtypeStruct((2, nk, DFT_B, width), F32),
        scratch_shapes=[pltpu.VMEM((nk * _PITCH, LANES), jnp.uint32), pltpu.VMEM((nk * _PITCH, LANES), jnp.uint32)],
        compiler_params=_params("parallel", "arbitrary"),
        name="filter_spectrum",
    )(hperm, w3, w3, _filter_delta(width), g_k, wf)


def _hyena_epilogue(y, z, x0, bias, gain):
    yv = (y + z * bias) * x0
    ms = jnp.mean(yv * yv, axis=-1, keepdims=True)
    return yv * lax.rsqrt(ms + NORM_EPS) * gain


def _pack_groups(y, zx, bias, gain):
    z, x0 = _unpack_pair(zx)
    out = _hyena_epilogue_groups(y, z.astype(F32), x0.astype(F32), bias, gain)
    return _pack_pair(out[:, :LANES], out[:, LANES:])


def _hyena_epilogue_groups(y, z, x0, bias, gain):
    return jnp.concatenate([_hyena_epilogue(y[:, g * LANES:(g + 1) * LANES], z[:, g * LANES:(g + 1) * LANES],
                                            x0[:, g * LANES:(g + 1) * LANES], bias[:, g * LANES:(g + 1) * LANES],
                                            gain[:, g * LANES:(g + 1) * LANES])
                            for g in range(y.shape[1] // LANES)], axis=1)


def _long_conv_kernel(zx0_ref, zx1_ref, kf_ref, g_ref, h_ref, wf_ref, wi_ref, bias_ref, gain_ref, o_ref,
                      s0_ref, s1_ref, *, n_a):
    s = pl.program_id(1)
    half = n_a // 2
    nk = g_ref.shape[1] // 2
    kb = kf_ref.shape[1]
    nkb = nk // kb

    def load_zx(b):
        return jnp.concatenate([zx0_ref[pl.ds(b, half, stride=DFT_B), :],
                                zx1_ref[pl.ds(b, half, stride=DFT_B), :]], axis=1)

    def load_stage(rows):
        return _unpack_pair(jnp.concatenate([s0_ref[rows, :], s1_ref[rows, :]], axis=1))

    def store_stage(rows, re, im):
        w = _pack_pair(re, im)
        s0_ref[rows, :] = w[:, :LANES]
        s1_ref[rows, :] = w[:, LANES:]

    @pl.when(s == 0)
    def _():
        def body(b, carry):
            z, _ = _unpack_pair(load_zx(b))
            r = jnp.dot(g_ref[b], z, preferred_element_type=F32)
            store_stage(pl.ds(b, nk, stride=_PITCH), r[:nk], r[nk:])
            return carry

        lax.fori_loop(0, DFT_B, body, 0, unroll=_B_UNROLL)

    @pl.when((s > 0) & (s <= nkb))
    def _():
        k0 = (s - 1) * kb
        for i in range(kb):
            rows = _stage_rows(k0 + i)
            re, im = load_stage(rows)
            zz = jnp.dot(wf_ref[...], jnp.concatenate([re, im], axis=0), preferred_element_type=F32)
            zr, zi = zz[:DFT_B], zz[DFT_B:]
            kr, ki = kf_ref[0, i], kf_ref[1, i]
            y = jnp.concatenate([zr * kr - zi * ki, zr * ki + zi * kr], axis=0).astype(BF16)
            cc = jnp.dot(wi_ref[...], y, preferred_element_type=F32)
            store_stage(rows, cc[:DFT_B], cc[DFT_B:])

    @pl.when(s == nkb + 1)
    def _():
        def body(b, carry):
            re, im = load_stage(pl.ds(b, nk, stride=_PITCH))
            y = jnp.dot(h_ref[b], jnp.concatenate([re, im], axis=0), preferred_element_type=F32)
            o_ref[pl.ds(b, half, stride=DFT_B), :] = _pack_groups(y, load_zx(b), bias_ref[...], gain_ref[...])
            return carry

        lax.fori_loop(0, DFT_B, body, 0, unroll=_B_UNROLL)


def _long_conv(zx, kf, g_z, h_tab, wf, wi, bias, gain):
    l, c = zx.shape
    n_a = 2 * l // DFT_B
    nk = g_z.shape[1] // 2
    kb = min(nk, _K1_BLOCK)
    nkb = nk // kb
    cw = 2 * LANES
    return pl.pallas_call(
        functools.partial(_long_conv_kernel, n_a=n_a),
        grid=(c // cw, nkb + 2),
        in_specs=[_single((l, LANES), lambda ci, s: (0, 2 * ci)), _single((l, LANES), lambda ci, s: (0, 2 * ci + 1)),
                  pl.BlockSpec((2, kb, DFT_B, cw), lambda ci, s: (0, jnp.clip(s - 1, 0, nkb - 1), 0, ci)),
                  _single(g_z.shape, lambda ci, s: (0, 0, 0)), _single(h_tab.shape, lambda ci, s: (0, 0, 0)),
                  _single(wf.shape, lambda ci, s: (0, 0)), _single(wi.shape, lambda ci, s: (0, 0)),
                  pl.BlockSpec((1, cw), lambda ci, s: (0, ci)), pl.BlockSpec((1, cw), lambda ci, s: (0, ci))],
        out_specs=pl.BlockSpec((l, LANES), lambda ci, s: (0, ci)),
        out_shape=jax.ShapeDtypeStruct((l, c // 2), jnp.uint32),
        scratch_shapes=[pltpu.VMEM((nk * _PITCH, LANES), jnp.uint32), pltpu.VMEM((nk * _PITCH, LANES), jnp.uint32)],
        compiler_params=_params("parallel", "arbitrary"),
        name="long_conv",
    )(zx, zx, kf, g_z, h_tab, wf, wi, bias.reshape(1, c), gain.reshape(1, c))


def _hyena_long(zx, filt, bias, gain):
    l, c = zx.shape
    n_a = 2 * l // DFT_B
    wf, wi = _dft128_embed()
    wf, wi = wf.astype(BF16), wi.astype(BF16)
    g_z = _dft_tables(n_a, n_a // 2).astype(BF16)
    g_k = _dft_tables(n_a, n_a).astype(BF16)
    h_tab = _idft_tables(n_a, n_a // 2).astype(BF16)
    hperm = _filter_hidden_permuted(l, filt['hy_w1'], filt['hy_b1'], filt['hy_freq'], filt['hy_w2'], filt['hy_b2'])
    kf = _filter_spectrum(hperm, filt['hy_w3'], l, c, g_k, wf)
    return _long_conv(zx, kf, g_z, h_tab, wf, wi, bias, gain)


def _ctx_conv_kernel(zx_ref, k_ref, fz_ref, fk_ref, hi_ref, bias_ref, gain_ref, o_ref):
    n = k_ref.shape[0]
    zx = zx_ref[...]
    z, _ = _unpack_pair(zx)
    zs = jnp.dot(fz_ref[...], z, preferred_element_type=F32)
    ks = jnp.dot(fk_ref[...], k_ref[...].astype(BF16), preferred_element_type=F32)
    zr, zi, kr, ki = zs[:n], zs[n:], ks[:n], ks[n:]
    y = jnp.concatenate([zr * kr - zi * ki, zr * ki + zi * kr], axis=0).astype(BF16)
    yt = jnp.dot(hi_ref[...], y, preferred_element_type=F32)
    o_ref[...] = _pack_groups(yt, zx, bias_ref[...], gain_ref[...])


def _hyena_short(zx, k2, bias, gain):
    l, c = zx.shape
    n = 2 * l
    k = np.arange(n)[:, None]
    t = np.arange(n)[None, :]
    e = np.exp(-2j * np.pi * ((k * t) % n) / n)
    fk = jnp.asarray(np.concatenate([e.real, e.imag], axis=0), F32).astype(BF16)
    fz = fk[:, :l]
    ei = np.conj(e[:, :l]).T / n
    hi = jnp.asarray(np.concatenate([ei.real, -ei.imag], axis=1), F32).astype(BF16)
    cw = 2 * LANES
    col = lambda rows: pl.BlockSpec((rows, cw), lambda ci: (0, ci))
    full = lambda a: pl.BlockSpec(a.shape, lambda ci: (0, 0))
    return pl.pallas_call(
        _ctx_conv_kernel,
        grid=(c // cw,),
        in_specs=[col(l), col(n), full(fz), full(fk), full(hi), col(1), col(1)],
        out_specs=pl.BlockSpec((l, LANES), lambda ci: (0, ci)),
        out_shape=jax.ShapeDtypeStruct((l, c // 2), jnp.uint32),
        compiler_params=_params("parallel"),
        name="hyena_short",
    )(zx, k2, fz, fk, hi, bias.reshape(1, c), gain.reshape(1, c))


def _split3(x):
    hi = x.astype(BF16)
    r1 = x - hi.astype(F32)
    mid = r1.astype(BF16)
    lo = (r1 - mid.astype(F32)).astype(BF16)
    return hi, mid, lo


def _gla_kernel(*refs, reverse, combine):
    if combine:
        (k_ref, q_ref, vlo_ref, vhi_ref, r_ref, wg_ref, bg_ref, s0_ref, of_ref, gate_ref, gout_ref,
         o_ref, sfin_ref, st_ref, la_ref, qe_ref, upd_ref, dec_ref, ke_ref, kl_ref, q2_ref, sc_ref, ob_ref) = refs
    else:
        (k_ref, q_ref, vlo_ref, vhi_ref, r_ref, wg_ref, bg_ref, s0_ref,
         o_ref, sfin_ref, st_ref, la_ref, qe_ref, upd_ref, dec_ref, ke_ref, kl_ref, q2_ref, sc_ref) = refs
        ob_ref = o_ref
    v_half = (vlo_ref, vhi_ref)
    rows_blk = k_ref.shape[0]
    nc = rows_blk // GLA_CHUNK
    qk = k_ref.shape[1]

    @pl.when(pl.program_id(0) == 0)
    def _():
        st_ref[...] = s0_ref[...]

    ri = lax.broadcasted_iota(jnp.int32, (2 * GLA_CHUNK, GLA_CHUNK), 0) % GLA_CHUNK
    ci = lax.broadcasted_iota(jnp.int32, (2 * GLA_CHUNK, GLA_CHUNK), 1)
    r3 = lax.broadcasted_iota(jnp.int32, (GLA_CHUNK, 3 * GLA_CHUNK), 0)
    c3 = lax.broadcasted_iota(jnp.int32, (GLA_CHUNK, 3 * GLA_CHUNK), 1) % GLA_CHUNK
    if reverse:
        tri3 = (c3 >= r3).astype(BF16)
        keep = ci > ri
    else:
        tri3 = (c3 <= r3).astype(BF16)
        keep = ci <= ri
    lane = lax.broadcasted_iota(jnp.int32, (1, LANES), 1)
    first = (lane < GLA_DK).astype(F32)
    srow = lax.broadcasted_iota(jnp.int32, (2 * GLA_DV, LANES), 0)
    scol = lax.broadcasted_iota(jnp.int32, (2 * GLA_DV, LANES), 1)
    own = (srow < GLA_DV) == (scol < GLA_DK)
    nt = (((1,), (1,)), ((), ()))
    tn = (((0,), (0,)), ((), ()))
    npair = GLA_HEADS // 2

    rh, rm, _ = _split3(r_ref[...])
    wh, wm, _ = _split3(wg_ref[...])
    pre = jnp.dot(jnp.concatenate([rh, rh, rm], axis=1), jnp.concatenate([wh, wm, wh], axis=0),
                  preferred_element_type=F32) + bg_ref[...]
    la_ref[...] = jax.nn.log_sigmoid(pre) / GLA_GATE_TAU

    def chunk_rows(c):
        return pl.ds(pl.multiple_of(c * GLA_CHUNK, GLA_CHUNK), GLA_CHUNK)

    def cum_pass(c, carry):
        rows = chunk_rows(c)
        hi, mid, lo = _split3(la_ref[rows, :])
        la_ref[rows, :] = jnp.dot(tri3, jnp.concatenate([hi, mid, lo], axis=0), preferred_element_type=F32)
        return carry

    def decay_pass(c, carry):
        rows = chunk_rows(c)
        cum = la_ref[rows, :]
        tot = cum[0:1] if reverse else cum[GLA_CHUNK - 1:GLA_CHUNK]
        k = k_ref[rows, :]
        qe = q_ref[rows, :] * (GLA_DK ** -0.5) * jnp.exp(cum)
        ke_ref[rows, :] = (k * jnp.exp(-cum)).astype(BF16)
        kl_ref[rows, :] = (k * jnp.exp(tot - cum)).astype(BF16)
        dec_ref[pl.ds(c, 1), :] = jnp.exp(tot)
        qe_ref[rows, :] = qe.astype(BF16)
        for p in range(npair):
            pr = slice(p * LANES, (p + 1) * LANES)
            q2_ref[c * npair + p] = jnp.concatenate([qe[:, pr] * first, qe[:, pr] * (1.0 - first)],
                                                    axis=0).astype(BF16)
        return carry

    def score_pass(c, carry):
        rows = chunk_rows(c)
        for p in range(npair):
            pr = slice(p * LANES, (p + 1) * LANES)
            sc = lax.dot_general(q2_ref[c * npair + p], ke_ref[rows, pr], nt, preferred_element_type=F32)
            sc_ref[c * npair + p] = jnp.where(keep, sc, 0.0).astype(BF16)
        return carry

    def value_pass(c, carry):
        rows = chunk_rows(c)
        for p in range(npair):
            pr = slice(p * LANES, (p + 1) * LANES)
            pv = slice(2 * p * GLA_DV, 2 * (p + 1) * GLA_DV)
            v_p = v_half[p // 2][rows, pl.ds((p % 2) * 2 * GLA_DV, 2 * GLA_DV)].astype(BF16)
            o2 = jnp.dot(sc_ref[c * npair + p], v_p, preferred_element_type=F32)
            ob_ref[rows, pl.ds(2 * p * GLA_DV, GLA_DV)] = o2[:GLA_CHUNK, :GLA_DV]
            ob_ref[rows, pl.ds((2 * p + 1) * GLA_DV, GLA_DV)] = o2[GLA_CHUNK:, GLA_DV:]
            u2 = lax.dot_general(v_p, kl_ref[rows, pr], tn, preferred_element_type=F32)
            upd_ref[c * npair + p] = jnp.where(own, u2, 0.0)
        return carry

    lax.fori_loop(0, nc, cum_pass, 0, unroll=True)
    lax.fori_loop(0, nc, decay_pass, 0, unroll=2)
    lax.fori_loop(0, nc, score_pass, 0, unroll=True)
    lax.fori_loop(0, nc, value_pass, 0, unroll=True)

    def carried(idx, carry):
        c = (nc - 1 - idx) if reverse else idx
        rows = pl.ds(pl.multiple_of(c * GLA_CHUNK, GLA_CHUNK), GLA_CHUNK)
        dec = dec_ref[pl.ds(c, 1), :]
        for p in range(npair):
            pr = slice(p * LANES, (p + 1) * LANES)
            pv = slice(2 * p * GLA_DV, 2 * (p + 1) * GLA_DV)
            st = st_ref[p]
            ob_ref[rows, pv] += lax.dot_general(qe_ref[rows, pr], st.astype(BF16), nt, preferred_element_type=F32)
            st_ref[p] = st * dec[:, pr] + upd_ref[c * npair + p]
        return carry

    lax.fori_loop(0, nc, carried, 0, unroll=4)

    if combine:
        for h in range(GLA_HEADS):
            hv = slice(h * GLA_DV, (h + 1) * GLA_DV)
            o = of_ref[:, hv] + ob_ref[:, hv]
            ms = jnp.mean(o * o, axis=-1, keepdims=True)
            o = o * lax.rsqrt(ms + NORM_EPS) * gout_ref[...]
            o_ref[:, hv] = (o * _silu(gate_ref[:, hv])).astype(o_ref.dtype)

    @pl.when(pl.program_id(0) == pl.num_programs(0) - 1)
    def _():
        sfin_ref[...] = st_ref[...]


_GLA_STATE = (GLA_HEADS // 2, 2 * GLA_DV, LANES)

_COL_K, _COL_V, _COL_Q, _COL_G = 24, 28, 36, 40


def _gla_pass(p, p_r, wg_pad, bg, s0, reverse, of=None, gout=None):
    l = p.shape[0]
    rb = min(l, 512)
    nblk = l // rb
    qk = GLA_HEADS * GLA_DK
    vw = GLA_HEADS * GLA_DV
    combine = of is not None
    rmap = (lambda i: nblk - 1 - i) if reverse else (lambda i: i)

    def cols(width, off128):
        return pl.BlockSpec((rb, width), lambda i: (rmap(i), off128 * LANES // width))

    state = pl.BlockSpec(_GLA_STATE, lambda i: (0, 0, 0))
    in_specs = [cols(qk, _COL_K), cols(qk, _COL_Q), cols(vw // 2, _COL_V), cols(vw // 2, _COL_V + vw // 2 // LANES),
                pl.BlockSpec((rb, LANES), lambda i: (rmap(i), 0)),
                pl.BlockSpec((LANES, qk), lambda i: (0, 0)), pl.BlockSpec((1, qk), lambda i: (0, 0)), state]
    args = [p, p, p, p, p_r, wg_pad, bg.reshape(1, qk), s0]
    nc = rb // GLA_CHUNK
    npair = _GLA_STATE[0]
    scratch = [pltpu.VMEM(_GLA_STATE, F32),
               pltpu.VMEM((rb, qk), F32),
               pltpu.VMEM((rb, qk), BF16),
               pltpu.VMEM((nc * npair,) + _GLA_STATE[1:], F32),
               pltpu.VMEM((nc, qk), F32),
               pltpu.VMEM((rb, qk), BF16),
               pltpu.VMEM((rb, qk), BF16),
               pltpu.VMEM((nc * npair, 2 * GLA_CHUNK, LANES), BF16),
               pltpu.VMEM((nc * npair, 2 * GLA_CHUNK, GLA_CHUNK), BF16)]
    if combine:
        in_specs += [pl.BlockSpec((rb, vw), lambda i: (rmap(i), 0)), cols(vw, _COL_G),
                     pl.BlockSpec((1, GLA_DV), lambda i: (0, 0))]
        args += [of, p, gout.reshape(1, GLA_DV)]
        scratch.append(pltpu.VMEM((rb, vw), F32))
    out_dtype = BF16 if combine else F32
    return pl.pallas_call(
        functools.partial(_gla_kernel, reverse=reverse, combine=combine),
        grid=(nblk,),
        in_specs=in_specs,
        out_specs=[pl.BlockSpec((rb, vw), lambda i: (rmap(i), 0)), state],
        out_shape=[jax.ShapeDtypeStruct((l, vw), out_dtype),
                   jax.ShapeDtypeStruct(_GLA_STATE, F32)],
        scratch_shapes=scratch,
        compiler_params=_params("arbitrary"),
        name="gla_bwd" if reverse else "gla_fwd",
    )(*args)


def _out_proj_kernel(hy_ref, o_ref, w_ref, x_ref, ga_ref, g_ref, sh_ref, sc_ref, xo_ref, h_ref):
    q = hy_ref.shape[1]
    hy_a, hy_b = _unpack_pair(hy_ref[...])
    mix = (jnp.dot(hy_a, w_ref[:q, :], preferred_element_type=F32)
           + jnp.dot(hy_b, w_ref[q:2 * q, :], preferred_element_type=F32)
           + jnp.dot(o_ref[...], w_ref[2 * q:, :], preferred_element_type=F32))
    xn = x_ref[...] + ga_ref[...] * mix
    xo_ref[...] = xn
    h_ref[...] = _norm_mod(xn, g_ref[...], sh_ref[...], sc_ref[...]).astype(BF16)


def _out_proj(hy, o, w, x, ga, g, shift, scale, layer):
    m, d = x.shape
    tm = min(m, 256)
    vec = pl.BlockSpec((1, d), lambda i: (0, 0))
    row = lambda width: pl.BlockSpec((tm, width), lambda i: (i, 0))
    return pl.pallas_call(
        _out_proj_kernel,
        grid=(m // tm,),
        in_specs=[row(hy.shape[1]), row(o.shape[1]), pl.BlockSpec((None, w.shape[1], d), lambda i: (layer, 0, 0)),
                  row(d), vec, vec, vec, vec],
        out_specs=[row(d), row(d)],
        out_shape=[jax.ShapeDtypeStruct((m, d), F32), jax.ShapeDtypeStruct((m, d), BF16)],
        compiler_params=_params("parallel"),
        name="out_proj",
    )(hy, o, w, x, ga, g, shift, scale)


_HALO = GRID_W + SUBLANES


def _ffn_up_conv_kernel(x_ref, wa_ref, wu_ref, cw_ref, o_ref, a_ref, u_ref, wb_ref, *, vertical, cols, nm):
    t = pl.program_id(0)
    total = pl.num_programs(0) - 1
    i = lax.rem(t, nm)
    tt = x_ref.shape[0]
    cur = lax.rem(t, 2)
    prv = 1 - cur

    @pl.when(t == 0)
    def _():
        a_ref[...] = jnp.zeros_like(a_ref)
        u_ref[...] = jnp.zeros_like(u_ref)

    def conv_previous_tile():
        col = lax.broadcasted_iota(jnp.int32, o_ref.shape, 0) & (cols - 1)
        acc = None
        for dc in (-1, 0, 1):
            part = None
            for dr in ((-1, 0, 1) if vertical else (0,)):
                tap = cw_ref[3 * (dr + 1) + (dc + 1):3 * (dr + 1) + (dc + 1) + 1, :]
                term = tap * a_ref[prv, pl.ds(_HALO + GRID_W * dr + dc, tt), :]
                part = term if part is None else part + term
            if dc == -1:
                part = jnp.where(col == 0, 0.0, part)
            elif dc == 1:
                part = jnp.where(col == cols - 1, 0.0, part)
            acc = part if acc is None else acc + part
        o_ref[...] = (_silu(acc) * u_ref[prv].astype(F32)).astype(o_ref.dtype)

    @pl.when((i == 0) & (t < total))
    def _():
        wb_ref[0] = wa_ref[...].astype(BF16)
        wb_ref[1] = wu_ref[...].astype(BF16)

    @pl.when(t < total)
    def _():
        x = x_ref[...]
        a_new = jnp.dot(x, wb_ref[0], preferred_element_type=F32)
        u_ref[cur] = jnp.dot(x, wb_ref[1], preferred_element_type=F32).astype(BF16)
        upper = a_ref[prv, pl.ds(_HALO + tt - GRID_W, GRID_W), :]
        a_ref[cur, pl.ds(SUBLANES, GRID_W), :] = jnp.where(i == 0, 0.0, upper)
        a_ref[cur, pl.ds(_HALO, tt), :] = a_new
        a_ref[prv, pl.ds(_HALO + tt, GRID_W), :] = jnp.where(i == 0, 0.0, a_new[:GRID_W])
        conv_previous_tile()

    @pl.when(t == total)
    def _():
        a_ref[prv, pl.ds(_HALO + tt, GRID_W), :] = jnp.zeros((GRID_W, a_ref.shape[2]), F32)
        conv_previous_tile()


def _ffn_up_conv(h, w_up, conv_w, cols, layer):
    m, d = h.shape
    dff = conv_w.shape[-1]
    vertical = m > cols
    assert cols & (cols - 1) == 0 and (cols == GRID_W or not vertical)
    tt = min(m, 1024)
    assert tt % cols == 0
    cb = 512
    ncb = dff // cb
    nm = m // tt
    total = ncb * nm
    cur_t = lambda t: jnp.minimum(t, total - 1)
    prev_t = lambda t: jnp.maximum(t - 1, 0)
    return pl.pallas_call(
        functools.partial(_ffn_up_conv_kernel, vertical=vertical, cols=cols, nm=nm),
        grid=(total + 1,),
        in_specs=[pl.BlockSpec((tt, d), lambda t: (cur_t(t) % nm, 0)),
                  pl.BlockSpec((None, d, cb), lambda t: (layer, 0, cur_t(t) // nm)),
                  pl.BlockSpec((None, d, cb), lambda t: (layer, 0, ncb + cur_t(t) // nm)),
                  pl.BlockSpec((9, cb), lambda t: (0, prev_t(t) // nm))],
        out_specs=pl.BlockSpec((tt, cb), lambda t: (prev_t(t) % nm, prev_t(t) // nm)),
        out_shape=jax.ShapeDtypeStruct((m, dff), BF16),
        scratch_shapes=[pltpu.VMEM((2, tt + 2 * _HALO, cb), F32), pltpu.VMEM((2, tt, cb), BF16),
                        pltpu.VMEM((2, d, cb), BF16)],
        compiler_params=_params("arbitrary"),
        name="ffn_up_conv",
    )(h, w_up, w_up, conv_w.reshape(9, dff))


def _ffn_down_kernel(g_ref, w_ref, x_ref, gate_ref, ng_ref, nsh_ref, nsc_ref, o_ref, *h_ref, post):
    xn = x_ref[...] + gate_ref[...] * jnp.dot(g_ref[...], w_ref[...], preferred_element_type=F32)
    if post == 'final':
        ms = jnp.mean(xn * xn, axis=-1, keepdims=True)
        xn = xn * lax.rsqrt(ms + NORM_EPS) * ng_ref[...]
    o_ref[...] = xn
    if post == 'next':
        h_ref[0][...] = _norm_mod(xn, ng_ref[...], nsh_ref[...], nsc_ref[...]).astype(BF16)


def _ffn_down(g, w, x, gate, norm, post, layer):
    m, d = x.shape
    dff = g.shape[1]
    tm = min(m, 256)
    vec = pl.BlockSpec((1, d), lambda i: (0, 0))
    row = pl.BlockSpec((tm, d), lambda i: (i, 0))
    out_specs, out_shape = [row], [jax.ShapeDtypeStruct((m, d), F32)]
    if post == 'next':
        out_specs.append(row)
        out_shape.append(jax.ShapeDtypeStruct((m, d), BF16))
    return pl.pallas_call(
        functools.partial(_ffn_down_kernel, post=post),
        grid=(m // tm,),
        in_specs=[pl.BlockSpec((tm, dff), lambda i: (i, 0)),
                  _single((None, dff, d), lambda i: (layer, 0, 0)),
                  row, vec, vec, vec, vec],
        out_specs=out_specs,
        out_shape=out_shape,
        compiler_params=_params("parallel"),
        name="ffn_down",
    )(g, w, x, gate, *norm)


def _regroup_w_in(w_in_t, hy_proj):
    qk = GLA_HEADS * GLA_DK
    vw = GLA_HEADS * GLA_DV
    head = hy_proj + qk + vw
    r = w_in_t[:, head:head + 2 * GLA_RANK]
    tail = w_in_t[:, head + 2 * GLA_RANK:]
    pad = jnp.zeros((w_in_t.shape[0], LANES - 2 * GLA_RANK, w_in_t.shape[2]), w_in_t.dtype)
    return head, tail, jnp.concatenate([r, pad], axis=1)


def _pad_gate_w(wg, lane0):
    return jnp.zeros((LANES, wg.shape[1]), F32).at[lane0:lane0 + GLA_RANK].set(wg)


def _mixer(p, p_r, lp, s0_f, s0_b, want_output):
    l = p.shape[0]
    width = lp['hy_bias'].shape[0]
    o_f, s_f = _gla_pass(p, p_r, lp['wg_f_pad'], lp['gla_bg_f'], s0_f, reverse=False)
    o, s_b = _gla_pass(p, p_r, lp['wg_b_pad'], lp['gla_bg_b'], s0_b, reverse=True, of=o_f, gout=lp['gla_out_g'])
    if not want_output:
        return None, None, s_f, s_b
    zx = _hy_pre(p, lp['hy_conv'], width)
    if 2 * l // DFT_B >= 2 * SUBLANES:
        hy = _hyena_long(zx, lp, lp['hy_bias'], lp['hy_out_g'])
    else:
        k2 = _hyena_filter2(l, lp['hy_w1'], lp['hy_b1'], lp['hy_freq'], lp['hy_w2'], lp['hy_b2'], lp['hy_w3'], width)
        hy = _hyena_short(zx, k2, lp['hy_bias'], lp['hy_out_g'])
    return hy, o, s_f, s_b


def _order_w_out(w_out, hy_width):
    depth, _, d = w_out.shape
    hy_rows = w_out[:, :hy_width].reshape(depth, hy_width // (2 * LANES), 2, LANES, d)
    hy_rows = jnp.swapaxes(hy_rows, 1, 2).reshape(depth, hy_width, d)
    return jnp.concatenate([hy_rows, w_out[:, hy_width:]], axis=1).astype(BF16)


def _ffn(h2, x, gate, lp, cols, norm, post, layer):
    gact = _ffn_up_conv(h2, lp['ffn_w_up'], lp['ffn_conv'], cols, layer)
    return _ffn_down(gact, lp['ffn_w_down'], x, gate, norm, post, layer)


def kernel(x, c, ctx, c_ctx, w_mod, b_mod, g_attn, w_in, hy_conv, hy_w1, hy_b1, hy_freq, hy_w2, hy_b2, hy_w3,
           hy_bias, hy_out_g, gla_wg_f, gla_bg_f, gla_wg_b, gla_bg_b, gla_out_g, w_out, g_ffn, ffn_w_up,
           ffn_conv, ffn_w_down, g_final):
    depth = w_mod.shape[0]
    batch, seq, d = x.shape
    assert batch == 1 and c.shape[0] == 1 and ctx.shape[0] == 1
    hy_proj = hy_conv.shape[-1]
    xs = x[0]
    cs = ctx[0]
    cc = jnp.zeros((SUBLANES, d), F32).at[0].set(c[0]).at[1].set(c_ctx)
    zero_state = jnp.zeros(_GLA_STATE, F32)
    gfin = g_final.reshape(1, d)
    w_in = jnp.swapaxes(w_in, 1, 2)
    n_head_cols, w_tail, w_rank = _regroup_w_in(w_in, hy_proj)
    w_out_b = _order_w_out(w_out, hy_bias.shape[-1])
    w_down_b = ffn_w_down.astype(BF16)
    mods =[_adaln(cc, w_mod, b_mod, l) for l in range(depth)]

    def attn_norm(l, row):
        return (g_attn[l].reshape(1, d), mods[l][row:row + 1, 0:d], mods[l][row:row + 1, d:2 * d])

    hx = _norm_mod_call(xs, *attn_norm(0, 0))
    hc = _norm_mod_call(cs, *attn_norm(0, 1))
    for l in range(depth):
        last = l == depth - 1
        lp = {
            'hy_conv': hy_conv[l], 'hy_w1': hy_w1[l], 'hy_b1': hy_b1[l], 'hy_freq': hy_freq[l],
            'hy_w2': hy_w2[l], 'hy_b2': hy_b2[l], 'hy_w3': hy_w3[l], 'hy_bias': hy_bias[l],
            'hy_out_g': hy_out_g[l], 'gla_bg_f': gla_bg_f[l], 'gla_bg_b': gla_bg_b[l],
            'gla_out_g': gla_out_g[l],
            'wg_f_pad': _pad_gate_w(gla_wg_f[l], 0), 'wg_b_pad': _pad_gate_w(gla_wg_b[l], GLA_RANK),
            'ffn_w_up': ffn_w_up, 'ffn_conv': ffn_conv[l], 'ffn_w_down': w_down_b,
        }
        mod = mods[l]
        _, _, ga, sf, scf, gf = [mod[0:1, i * d:(i + 1) * d] for i in range(6)]
        _, _, cga, csf, cscf, cgf = [mod[1:2, i * d:(i + 1) * d] for i in range(6)]
        gf_row = g_ffn[l].reshape(1, d)

        pc = _in_proj(hc, w_in, w_tail, n_head_cols, l)
        hy_c, o_c, s_f, s_b = _mixer(pc, _gate_rank_proj(hc, w_rank, l), lp, zero_state, zero_state,
                                     want_output=not last)
        if not last:
            cs, h2c = _out_proj(hy_c, o_c, w_out_b, cs, cga, gf_row, csf, cscf, l)
            cs, hc = _ffn(h2c, cs, cgf, lp, cs.shape[0], attn_norm(l + 1, 1), 'next', l)

        px = _in_proj(hx, w_in, w_tail, n_head_cols, l)
        hy_x, o_x, _, _ = _mixer(px, _gate_rank_proj(hx, w_rank, l), lp, s_f, s_b, want_output=True)
        xs, h2 = _out_proj(hy_x, o_x, w_out_b, xs, ga, gf_row, sf, scf, l)
        if last:
            (xs,) = _ffn(h2, xs, gf, lp, GRID_W, (gfin, gfin, gfin), 'final', l)
        else:
            xs, hx = _ffn(h2, xs, gf, lp, GRID_W, attn_norm(l + 1, 0), 'next', l)
    return xs[None]
```

```python
import functools
import math

import numpy as np
import jax
import jax.numpy as jnp
from jax import lax
from jax.experimental import pallas as pl
from jax.experimental.pallas import tpu as pltpu

F32 = jnp.float32
BF16 = jnp.bfloat16
HIGHEST = lax.Precision.HIGHEST

LANES = 128
SUBLANES = 8
VMEM_BYTES_V7X = 64 * 1024 * 1024
VMEM_LIMIT = VMEM_BYTES_V7X - 12 * 1024 * 1024

NORM_EPS = 1e-6
HYENA_GROUP = 128
FILTER_BANDS = 16
FILTER_HIDDEN = 64
FILTER_TARGET = 1e-2
FILTER_FAST_PCT = 0.3
FILTER_SLOW_PCT = 1.5
GLA_HEADS = 8
GLA_DK = 64
GLA_DV = 128
GLA_RANK = 16
GLA_GATE_TAU = 16.0
GLA_CHUNK = 64
GRID_W = 64
DFT_B = 128


def _params(*sem):
    return pltpu.CompilerParams(dimension_semantics=sem, vmem_limit_bytes=VMEM_LIMIT)


def _silu(x):
    return x * jax.nn.sigmoid(x)


def _adaln_kernel(c_ref, w_ref, b_ref, o_ref):
    s = _silu(c_ref[...]).astype(BF16)
    o_ref[...] = jnp.dot(s, w_ref[...].astype(BF16), preferred_element_type=F32) + b_ref[...]


def _adaln(cc, w_mod, b_mod, layer):
    depth, d, n = w_mod.shape
    tn = 1024
    return pl.pallas_call(
        _adaln_kernel,
        grid=(n // tn,),
        in_specs=[pl.BlockSpec((SUBLANES, d), lambda j: (0, 0)),
                  pl.BlockSpec((None, d, tn), lambda j: (layer, 0, j)),
                  pl.BlockSpec((None, 1, tn), lambda j: (layer, 0, j))],
        out_specs=pl.BlockSpec((SUBLANES, tn), lambda j: (0, j)),
        out_shape=jax.ShapeDtypeStruct((SUBLANES, n), F32),
        compiler_params=_params("arbitrary"),
        name="adaln",
    )(cc, w_mod, b_mod.reshape(depth, 1, n))


def _norm_mod(x, g, shift, scale):
    ms = jnp.mean(x * x, axis=-1, keepdims=True)
    return (x * lax.rsqrt(ms + NORM_EPS) * g) * (1.0 + scale) + shift


def _norm_mod_kernel(x_ref, g_ref, sh_ref, sc_ref, o_ref):
    o_ref[...] = _norm_mod(x_ref[...], g_ref[...], sh_ref[...], sc_ref[...]).astype(o_ref.dtype)


def _norm_mod_call(x, g, shift, scale):
    m, d = x.shape
    tm = min(m, 512)
    vec = pl.BlockSpec((1, d), lambda i: (0, 0))
    return pl.pallas_call(
        _norm_mod_kernel,
        grid=(m // tm,),
        in_specs=[pl.BlockSpec((tm, d), lambda i: (i, 0)), vec, vec, vec],
        out_specs=pl.BlockSpec((tm, d), lambda i: (i, 0)),
        out_shape=jax.ShapeDtypeStruct((m, d), BF16),
        compiler_params=_params("parallel"),
        name="norm_mod",
    )(x, g, shift, scale)


_IN_TILE = 768


_NT = (((1,), (1,)), ((), ()))


def _in_proj_kernel(h_ref, w_ref, wt_ref, o_ref, wb_ref, *, n_head):
    j = pl.program_id(0)

    @pl.when((pl.program_id(1) == 0) & (j < n_head))
    def _():
        wb_ref[...] = w_ref[...].T.astype(BF16)

    @pl.when((pl.program_id(1) == 0) & (j >= n_head))
    def _():
        wb_ref[...] = wt_ref[...].T.astype(BF16)

    o_ref[...] = jnp.dot(h_ref[...], wb_ref[...], preferred_element_type=F32)


def _in_proj(h, w_in_t, w_tail_t, n_head_cols, layer):
    m, d = h.shape
    tn = _IN_TILE
    n_head = n_head_cols // tn
    n_tail = w_tail_t.shape[1] // tn
    assert n_head * tn == n_head_cols and n_tail * tn == w_tail_t.shape[1]
    tm = min(m, 1024)
    return pl.pallas_call(
        functools.partial(_in_proj_kernel, n_head=n_head),
        grid=(n_head + n_tail, m // tm),
        in_specs=[pl.BlockSpec((tm, d), lambda j, i: (i, 0)),
                  _single((None, tn, d), lambda j, i: (layer, jnp.minimum(j, n_head - 1), 0)),
                  _single((None, tn, d), lambda j, i: (layer, jnp.maximum(j - n_head, 0), 0))],
        out_specs=pl.BlockSpec((tm, tn), lambda j, i: (i, j)),
        out_shape=jax.ShapeDtypeStruct((m, n_head_cols + w_tail_t.shape[1]), F32),
        scratch_shapes=[pltpu.VMEM((d, tn), BF16)],
        compiler_params=_params("arbitrary", "arbitrary"),
        name="in_proj",
    )(h, w_in_t, w_tail_t)


def _gate_rank_kernel(h_ref, w_ref, o_ref):
    o_ref[...] = lax.dot_general(h_ref[...], w_ref[...].astype(BF16), _NT, preferred_element_type=F32)


def _gate_rank_proj(h, w_r_t, layer):
    m, d = h.shape
    tm = min(m, 2048)
    return pl.pallas_call(
        _gate_rank_kernel,
        grid=(m // tm,),
        in_specs=[pl.BlockSpec((tm, d), lambda i: (i, 0)), pl.BlockSpec((None, LANES, d), lambda i: (layer, 0, 0))],
        out_specs=pl.BlockSpec((tm, LANES), lambda i: (i, 0)),
        out_shape=jax.ShapeDtypeStruct((m, LANES), F32),
        compiler_params=_params("parallel"),
        name="gate_rank_proj",
    )(h, w_r_t)


def _hy_pre_kernel(x0_ref, x1_ref, v_ref, x0p_ref, x1p_ref, vp_ref, x0n_ref, x1n_ref, vn_ref,
                   w0_ref, w1_ref, wv_ref, zx_ref):
    i = pl.program_id(0)
    first = i == 0
    last = i == pl.num_programs(0) - 1
    tt = x0_ref.shape[0]
    row = lax.broadcasted_iota(jnp.int32, x0_ref.shape, 0)

    def conv(ref, prev_ref, next_ref, w_ref):
        x = ref[...]
        prev_row = jnp.where(first, 0.0, prev_ref[SUBLANES - 1:SUBLANES, :])
        next_row = jnp.where(last, 0.0, next_ref[0:1, :])
        below = jnp.where(row == 0, prev_row, pltpu.roll(x, 1, 0))
        above = jnp.where(row == tt - 1, next_row, pltpu.roll(x, tt - 1, 0))
        return w_ref[0:1, :] * below + w_ref[1:2, :] * x + w_ref[2:3, :] * above

    z = conv(v_ref, vp_ref, vn_ref, wv_ref) * conv(x1_ref, x1p_ref, x1n_ref, w1_ref)
    zx_ref[...] = _pack_pair(z, conv(x0_ref, x0p_ref, x0n_ref, w0_ref))


def _hy_pre(p, hy_conv, width):
    l = p.shape[0]
    tt = min(l, 1024)
    cb = 512
    ncb = width // cb
    nb8 = l // SUBLANES
    t8 = tt // SUBLANES

    def main(g):
        return pl.BlockSpec((tt, cb), lambda i, j: (i, g * ncb + j))

    def prev(g):
        return pl.BlockSpec((SUBLANES, cb), lambda i, j: (jnp.maximum(i * t8 - 1, 0), g * ncb + j))

    def nxt(g):
        return pl.BlockSpec((SUBLANES, cb), lambda i, j: (jnp.minimum((i + 1) * t8, nb8 - 1), g * ncb + j))

    def wspec(g):
        return pl.BlockSpec((3, cb), lambda i, j: (0, g * ncb + j))

    out = pl.BlockSpec((tt, cb), lambda i, j: (i, j))
    return pl.pallas_call(
        _hy_pre_kernel,
        grid=(l // tt, ncb),
        in_specs=[main(0), main(1), main(2), prev(0), prev(1), prev(2), nxt(0), nxt(1), nxt(2),
                  wspec(0), wspec(1), wspec(2)],
        out_specs=out,
        out_shape=jax.ShapeDtypeStruct((l, width), jnp.uint32),
        compiler_params=_params("parallel", "parallel"),
        name="hy_pre",
    )(p, p, p, p, p, p, p, p, p, hy_conv, hy_conv, hy_conv)


def _filter_hidden(j, seq_len, w1t_ref, w1c_ref, w1s_ref, b1_ref, fr_ref, w2_ref, b2_ref):
    pos_i = jnp.where(j < seq_len, j, 2 * seq_len - j)
    pos = pos_i.astype(F32)
    t = pos / float(max(seq_len - 1, 1))
    lane = lax.broadcasted_iota(jnp.int32, j.shape, 1)
    band = jnp.where(lane < FILTER_BANDS, lane + 1, 0).astype(F32)
    ang = (2.0 * math.pi / seq_len) * pos * band
    fr = fr_ref[...]
    pre = (t[:, 0:1] * w1t_ref[...]
           + jnp.dot(jnp.cos(ang), w1c_ref[...], precision=HIGHEST, preferred_element_type=F32)
           + jnp.dot(jnp.sin(ang), w1s_ref[...], precision=HIGHEST, preferred_element_type=F32)
           + b1_ref[...])
    h = jnp.sin(fr * pre)
    h = jnp.sin(fr * (jnp.dot(h, w2_ref[...], precision=HIGHEST, preferred_element_type=F32) + b2_ref[...]))
    return h, pos_i, t


def _filter_kernel(w1t_ref, w1c_ref, w1s_ref, b1_ref, fr_ref, w2_ref, b2_ref, w3_ref, delta_ref, o_ref, *, seq_len):
    tr = o_ref.shape[0]
    j = pl.program_id(0) * tr + lax.broadcasted_iota(jnp.int32, (tr, LANES), 0)
    h, pos_i, t = _filter_hidden(j, seq_len, w1t_ref, w1c_ref, w1s_ref, b1_ref, fr_ref, w2_ref, b2_ref)
    hh = jnp.dot(h.astype(BF16), w3_ref[...].astype(BF16), preferred_element_type=F32)
    decay = jnp.exp(-t[:, 0:1] * delta_ref[...])
    valid = pos_i[:, 0:1] != seq_len
    o_ref[...] = jnp.where(valid, hh * decay, 0.0)


_FILTER_PACK = LANES // FILTER_BANDS


def _filter_hidden_kernel(w1t_ref, w1c_ref, w1s_ref, b1_ref, fr_ref, w2_ref, b2_ref, o_ref, *, seq_len, n_a):
    tp = o_ref.shape[0]
    shift = n_a.bit_length() - 1

    def position(shape, lanes_per_pos):
        prow = pl.program_id(0) * tp + lax.broadcasted_iota(jnp.int32, shape, 0)
        r = prow * _FILTER_PACK + lax.broadcasted_iota(jnp.int32, shape, 1) // lanes_per_pos
        j = ((r & (n_a - 1)) * DFT_B) + (r >> shift)
        return jnp.where(j < seq_len, j, 2 * seq_len - j).astype(F32)

    pos = position((tp, LANES), FILTER_BANDS)
    band = (lax.broadcasted_iota(jnp.int32, (tp, LANES), 1) % FILTER_BANDS + 1).astype(F32)
    ang = (2.0 * math.pi / seq_len) * pos * band
    t = position(o_ref.shape, FILTER_HIDDEN) / float(max(seq_len - 1, 1))
    fr = fr_ref[...]
    pre = (t * w1t_ref[...]
           + jnp.dot(jnp.cos(ang), w1c_ref[...], precision=HIGHEST, preferred_element_type=F32)
           + jnp.dot(jnp.sin(ang), w1s_ref[...], precision=HIGHEST, preferred_element_type=F32)
           + b1_ref[...])
    h = jnp.sin(fr * pre)
    o_ref[...] = jnp.sin(fr * (jnp.dot(h, w2_ref[...], precision=HIGHEST, preferred_element_type=F32) + b2_ref[...]))


def _filter_mlp_args(w1, b1, freq, w2, b2):
    hid = FILTER_HIDDEN
    w1c = jnp.zeros((LANES, hid), F32).at[:FILTER_BANDS].set(w1[1:1 + FILTER_BANDS])
    w1s = jnp.zeros((LANES, hid), F32).at[:FILTER_BANDS].set(w1[1 + FILTER_BANDS:1 + 2 * FILTER_BANDS])
    full = lambda shape: pl.BlockSpec(shape, lambda i: (0, 0))
    specs = [full((1, hid)), full((LANES, hid)), full((LANES, hid)), full((1, hid)), full((1, hid)),
             full((hid, hid)), full((1, hid))]
    args = (w1[0:1], w1c, w1s, b1.reshape(1, hid), freq.reshape(1, hid), w2, b2.reshape(1, hid))
    return specs, args


def _filter_delta(width):
    return jnp.abs(jnp.linspace(math.log(FILTER_TARGET) / FILTER_SLOW_PCT,
                                math.log(FILTER_TARGET) / FILTER_FAST_PCT, width, dtype=F32)).reshape(1, width)


def _hyena_filter2(seq_len, w1, b1, freq, w2, b2, w3, width):
    n2 = 2 * seq_len
    tr = min(seq_len, 1024)
    nfwd = seq_len // tr
    specs, args = _filter_mlp_args(w1, b1, freq, w2, b2)
    return pl.pallas_call(
        functools.partial(_filter_kernel, seq_len=seq_len),
        grid=(n2 // tr,),
        in_specs=specs + [pl.BlockSpec((FILTER_HIDDEN, width), lambda i: (0, jnp.where(i < nfwd, 0, 1))),
                          pl.BlockSpec((1, width), lambda i: (0, 0))],
        out_specs=pl.BlockSpec((tr, width), lambda i: (i, 0)),
        out_shape=jax.ShapeDtypeStruct((n2, width), F32),
        compiler_params=_params("parallel"),
        name="hyena_filter",
    )(*args, w3, _filter_delta(width))


def _filter_hidden_permuted(seq_len, w1, b1, freq, w2, b2):
    n2 = 2 * seq_len
    n_a = n2 // DFT_B
    hid = FILTER_HIDDEN
    wide = _FILTER_PACK * hid
    rows = n2 // _FILTER_PACK
    tp = min(rows, 256)
    eye = jnp.eye(_FILTER_PACK, dtype=F32)
    tile = lambda v: jnp.tile(v.reshape(1, hid), (1, _FILTER_PACK))
    args = (tile(w1[0]), jnp.kron(eye, w1[1:1 + FILTER_BANDS]),
            jnp.kron(eye, w1[1 + FILTER_BANDS:1 + 2 * FILTER_BANDS]), tile(b1), tile(freq),
            jnp.kron(eye, w2), tile(b2))
    packed = pl.pallas_call(
        functools.partial(_filter_hidden_kernel, seq_len=seq_len, n_a=n_a),
        grid=(rows // tp,),
        in_specs=[pl.BlockSpec(a.shape, lambda i: (0, 0)) for a in args],
        out_specs=pl.BlockSpec((tp, wide), lambda i: (i, 0)),
        out_shape=jax.ShapeDtypeStruct((rows, wide), F32),
        compiler_params=_params("parallel"),
        name="hyena_filter_hidden",
    )(*args)
    return packed.reshape(n2, hid)


def _kept_k1(n_a):
    kept = n_a // 2 + 1
    return kept, -(-kept // SUBLANES) * SUBLANES


def _dft_tables(n_a, a_in):
    n = n_a * DFT_B
    kept, padded = _kept_k1(n_a)
    b = np.arange(DFT_B)[:, None, None]
    k1 = np.arange(padded)[None, :, None]
    a = np.arange(a_in)[None, None, :]
    e = np.exp(-2j * np.pi * ((k1 * (DFT_B * a + b)) % n) / n) * (k1 < kept)
    g = np.concatenate([e.real, e.imag], axis=1)
    return jnp.asarray(g, F32)


def _idft_tables(n_a, a_out):
    n = n_a * DFT_B
    kept, padded = _kept_k1(n_a)
    b = np.arange(DFT_B)[:, None, None]
    a = np.arange(a_out)[None, :, None]
    k1 = np.arange(padded)[None, None, :]
    weight = np.where((k1 == 0) | (k1 == n_a // 2), 1.0, 2.0) * (k1 < kept)
    e = np.exp(2j * np.pi * ((k1 * (DFT_B * a + b)) % n) / n) * weight / n
    h = np.concatenate([e.real, -e.imag], axis=2)
    return jnp.asarray(h, F32)


def _dft128_embed():
    i = np.arange(DFT_B)
    f = np.exp(-2j * np.pi * ((i[:, None] * i[None, :]) % DFT_B) / DFT_B)
    fwd = np.block([[f.real, -f.imag], [f.imag, f.real]])
    inv = np.block([[f.real, f.imag], [-f.imag, f.real]])
    return jnp.asarray(fwd, F32), jnp.asarray(inv, F32)


_PITCH = DFT_B + SUBLANES
_K1_BLOCK = 24
_B_UNROLL = 16


def _stage_rows(k1):
    return pl.ds(pl.multiple_of(k1 * _PITCH, SUBLANES), DFT_B)


def _single(shape, index_map):
    return pl.BlockSpec(shape, index_map, pipeline_mode=pl.Buffered(1))


def _pack_pair(re, im):
    hi = lax.bitcast_convert_type(re.astype(BF16).astype(F32), jnp.uint32)
    lo = lax.bitcast_convert_type(im.astype(BF16).astype(F32), jnp.uint32)
    return hi | (lo >> 16)


def _unpack_pair(w):
    re = lax.bitcast_convert_type(w & jnp.uint32(0xFFFF0000), F32)
    im = lax.bitcast_convert_type(w << 16, F32)
    return re.astype(BF16), im.astype(BF16)


def _spectrum_kernel(h_ref, w3f_ref, w3b_ref, delta_ref, g_ref, wf_ref, o_ref, s0_ref, s1_ref, *, seq_len, n_a):
    s = pl.program_id(1)
    half = n_a // 2
    cw = o_ref.shape[3]

    @pl.when(s == 0)
    def _():
        a_i = lax.broadcasted_iota(jnp.int32, (n_a, cw), 0)

        def body(b, carry):
            base = pl.multiple_of(b * n_a, SUBLANES)
            hf = jnp.dot(h_ref[pl.ds(base, half), :].astype(BF16), w3f_ref[...].astype(BF16),
                         preferred_element_type=F32)
            hb = jnp.dot(h_ref[pl.ds(base + half, half), :].astype(BF16), w3b_ref[...].astype(BF16),
                         preferred_element_type=F32)
            hh = jnp.concatenate([hf, hb], axis=0)
            j = a_i * DFT_B + b
            pos_i = jnp.where(j < seq_len, j, 2 * seq_len - j)
            t = pos_i.astype(F32) / float(max(seq_len - 1, 1))
            k2 = jnp.where(pos_i != seq_len, hh * jnp.exp(-t * delta_ref[...]), 0.0)
            r = jnp.dot(g_ref[b], k2.astype(BF16), preferred_element_type=F32)
            nk = r.shape[0] // 2
            w = _pack_pair(r[:nk], r[nk:])
            s0_ref[pl.ds(b, nk, stride=_PITCH), :] = w[:, :LANES]
            s1_ref[pl.ds(b, nk, stride=_PITCH), :] = w[:, LANES:]
            return carry

        lax.fori_loop(0, DFT_B, body, 0, unroll=_B_UNROLL)

    @pl.when(s > 0)
    def _():
        k0 = (s - 1) * o_ref.shape[1]
        for i in range(o_ref.shape[1]):
            rows = _stage_rows(k0 + i)
            re, im = _unpack_pair(jnp.concatenate([s0_ref[rows, :], s1_ref[rows, :]], axis=1))
            z = jnp.dot(wf_ref[...], jnp.concatenate([re, im], axis=0), preferred_element_type=F32)
            o_ref[0, i] = z[:DFT_B]
            o_ref[1, i] = z[DFT_B:]


def _filter_spectrum(hperm, w3, seq_len, width, g_k, wf):
    n_a = 2 * seq_len // DFT_B
    nk = g_k.shape[1] // 2
    kb = min(nk, _K1_BLOCK)
    cw = 2 * LANES
    ncb = width // cw
    hid = FILTER_HIDDEN
    return pl.pallas_call(
        functools.partial(_spectrum_kernel, seq_len=seq_len, n_a=n_a),
        grid=(ncb, 1 + nk // kb),
        in_specs=[_single(hperm.shape, lambda ci, s: (0, 0)),
                  pl.BlockSpec((hid, cw), lambda ci, s: (0, ci)),
                  pl.BlockSpec((hid, cw), lambda ci, s: (0, ncb + ci)),
                  pl.BlockSpec((1, cw), lambda ci, s: (0, ci)),
                  _single(g_k.shape, lambda ci, s: (0, 0, 0)),
                  _single(wf.shape, lambda ci, s: (0, 0))],
        out_specs=pl.BlockSpec((2, kb, DFT_B, cw), lambda ci, s: (0, jnp.maximum(s - 1, 0), 0, ci)),
        out_shape=jax.ShapeDtypeStruct((2, nk, DFT_B, width), F32),
        scratch_shapes=[pltpu.VMEM((nk * _PITCH, LANES), jnp.uint32), pltpu.VMEM((nk * _PITCH, LANES), jnp.uint32)],
        compiler_params=_params("parallel", "arbitrary"),
        name="filter_spectrum",
    )(hperm, w3, w3, _filter_delta(width), g_k, wf)


def _hyena_epilogue(y, z, x0, bias, gain):
    yv = (y + z * bias) * x0
    ms = jnp.mean(yv * yv, axis=-1, keepdims=True)
    return yv * lax.rsqrt(ms + NORM_EPS) * gain


def _pack_groups(y, zx, bias, gain):
    z, x0 = _unpack_pair(zx)
    out = _hyena_epilogue_groups(y, z.astype(F32), x0.astype(F32), bias, gain)
    return _pack_pair(out[:, :LANES], out[:, LANES:])


def _hyena_epilogue_groups(y, z, x0, bias, gain):
    return jnp.concatenate([_hyena_epilogue(y[:, g * LANES:(g + 1) * LANES], z[:, g * LANES:(g + 1) * LANES],
                                            x0[:, g * LANES:(g + 1) * LANES], bias[:, g * LANES:(g + 1) * LANES],
                                            gain[:, g * LANES:(g + 1) * LANES])
                            for g in range(y.shape[1] // LANES)], axis=1)


def _long_conv_kernel(zx0_ref, zx1_ref, kf_ref, g_ref, h_ref, wf_ref, wi_ref, bias_ref, gain_ref, o_ref,
                      s0_ref, s1_ref, *, n_a):
    s = pl.program_id(1)
    half = n_a // 2
    nk = g_ref.shape[1] // 2
    kb = kf_ref.shape[1]
    nkb = nk // kb

    def load_zx(b):
        return jnp.concatenate([zx0_ref[pl.ds(b, half, stride=DFT_B), :],
                                zx1_ref[pl.ds(b, half, stride=DFT_B), :]], axis=1)

    def load_stage(rows):
        return _unpack_pair(jnp.concatenate([s0_ref[rows, :], s1_ref[rows, :]], axis=1))

    def store_stage(rows, re, im):
        w = _pack_pair(re, im)
        s0_ref[rows, :] = w[:, :LANES]
        s1_ref[rows, :] = w[:, LANES:]

    @pl.when(s == 0)
    def _():
        def body(b, carry):
            z, _ = _unpack_pair(load_zx(b))
            r = jnp.dot(g_ref[b], z, preferred_element_type=F32)
            store_stage(pl.ds(b, nk, stride=_PITCH), r[:nk], r[nk:])
            return carry

        lax.fori_loop(0, DFT_B, body, 0, unroll=_B_UNROLL)

    @pl.when((s > 0) & (s <= nkb))
    def _():
        k0 = (s - 1) * kb
        for i in range(kb):
            rows = _stage_rows(k0 + i)
            re, im = load_stage(rows)
            zz = jnp.dot(wf_ref[...], jnp.concatenate([re, im], axis=0), preferred_element_type=F32)
            zr, zi = zz[:DFT_B], zz[DFT_B:]
            kr, ki = kf_ref[0, i], kf_ref[1, i]
            y = jnp.concatenate([zr * kr - zi * ki, zr * ki + zi * kr], axis=0).astype(BF16)
            cc = jnp.dot(wi_ref[...], y, preferred_element_type=F32)
            store_stage(rows, cc[:DFT_B], cc[DFT_B:])

    @pl.when(s == nkb + 1)
    def _():
        def body(b, carry):
            re, im = load_stage(pl.ds(b, nk, stride=_PITCH))
            y = jnp.dot(h_ref[b], jnp.concatenate([re, im], axis=0), preferred_element_type=F32)
            o_ref[pl.ds(b, half, stride=DFT_B), :] = _pack_groups(y, load_zx(b), bias_ref[...], gain_ref[...])
            return carry

        lax.fori_loop(0, DFT_B, body, 0, unroll=_B_UNROLL)


def _long_conv(zx, kf, g_z, h_tab, wf, wi, bias, gain):
    l, c = zx.shape
    n_a = 2 * l // DFT_B
    nk = g_z.shape[1] // 2
    kb = min(nk, _K1_BLOCK)
    nkb = nk // kb
    cw = 2 * LANES
    return pl.pallas_call(
        functools.partial(_long_conv_kernel, n_a=n_a),
        grid=(c // cw, nkb + 2),
        in_specs=[_single((l, LANES), lambda ci, s: (0, 2 * ci)), _single((l, LANES), lambda ci, s: (0, 2 * ci + 1)),
                  pl.BlockSpec((2, kb, DFT_B, cw), lambda ci, s: (0, jnp.clip(s - 1, 0, nkb - 1), 0, ci)),
                  _single(g_z.shape, lambda ci, s: (0, 0, 0)), _single(h_tab.shape, lambda ci, s: (0, 0, 0)),
                  _single(wf.shape, lambda ci, s: (0, 0)), _single(wi.shape, lambda ci, s: (0, 0)),
                  pl.BlockSpec((1, cw), lambda ci, s: (0, ci)), pl.BlockSpec((1, cw), lambda ci, s: (0, ci))],
        out_specs=pl.BlockSpec((l, LANES), lambda ci, s: (0, ci)),
        out_shape=jax.ShapeDtypeStruct((l, c // 2), jnp.uint32),
        scratch_shapes=[pltpu.VMEM((nk * _PITCH, LANES), jnp.uint32), pltpu.VMEM((nk * _PITCH, LANES), jnp.uint32)],
        compiler_params=_params("parallel", "arbitrary"),
        name="long_conv",
    )(zx, zx, kf, g_z, h_tab, wf, wi, bias.reshape(1, c), gain.reshape(1, c))


def _hyena_long(zx, filt, bias, gain):
    l, c = zx.shape
    n_a = 2 * l // DFT_B
    wf, wi = _dft128_embed()
    wf, wi = wf.astype(BF16), wi.astype(BF16)
    g_z = _dft_tables(n_a, n_a // 2).astype(BF16)
    g_k = _dft_tables(n_a, n_a).astype(BF16)
    h_tab = _idft_tables(n_a, n_a // 2).astype(BF16)
    hperm = _filter_hidden_permuted(l, filt['hy_w1'], filt['hy_b1'], filt['hy_freq'], filt['hy_w2'], filt['hy_b2'])
    kf = _filter_spectrum(hperm, filt['hy_w3'], l, c, g_k, wf)
    return _long_conv(zx, kf, g_z, h_tab, wf, wi, bias, gain)


def _ctx_conv_kernel(zx_ref, k_ref, fz_ref, fk_ref, hi_ref, bias_ref, gain_ref, o_ref):
    n = k_ref.shape[0]
    zx = zx_ref[...]
    z, _ = _unpack_pair(zx)
    zs = jnp.dot(fz_ref[...], z, preferred_element_type=F32)
    ks = jnp.dot(fk_ref[...], k_ref[...].astype(BF16), preferred_element_type=F32)
    zr, zi, kr, ki = zs[:n], zs[n:], ks[:n], ks[n:]
    y = jnp.concatenate([zr * kr - zi * ki, zr * ki + zi * kr], axis=0).astype(BF16)
    yt = jnp.dot(hi_ref[...], y, preferred_element_type=F32)
    o_ref[...] = _pack_groups(yt, zx, bias_ref[...], gain_ref[...])


def _hyena_short(zx, k2, bias, gain):
    l, c = zx.shape
    n = 2 * l
    k = np.arange(n)[:, None]
    t = np.arange(n)[None, :]
    e = np.exp(-2j * np.pi * ((k * t) % n) / n)
    fk = jnp.asarray(np.concatenate([e.real, e.imag], axis=0), F32).astype(BF16)
    fz = fk[:, :l]
    ei = np.conj(e[:, :l]).T / n
    hi = jnp.asarray(np.concatenate([ei.real, -ei.imag], axis=1), F32).astype(BF16)
    cw = 2 * LANES
    col = lambda rows: pl.BlockSpec((rows, cw), lambda ci: (0, ci))
    full = lambda a: pl.BlockSpec(a.shape, lambda ci: (0, 0))
    return pl.pallas_call(
        _ctx_conv_kernel,
        grid=(c // cw,),
        in_specs=[col(l), col(n), full(fz), full(fk), full(hi), col(1), col(1)],
        out_specs=pl.BlockSpec((l, LANES), lambda ci: (0, ci)),
        out_shape=jax.ShapeDtypeStruct((l, c // 2), jnp.uint32),
        compiler_params=_params("parallel"),
        name="hyena_short",
    )(zx, k2, fz, fk, hi, bias.reshape(1, c), gain.reshape(1, c))


def _split3(x):
    hi = x.astype(BF16)
    r1 = x - hi.astype(F32)
    mid = r1.astype(BF16)
    lo = (r1 - mid.astype(F32)).astype(BF16)
    return hi, mid, lo


def _gla_kernel(*refs, reverse, combine):
    if combine:
        (k_ref, q_ref, vlo_ref, vhi_ref, r_ref, wg_ref, bg_ref, s0_ref, of_ref, gate_ref, gout_ref,
         o_ref, sfin_ref, st_ref, la_ref, qe_ref, upd_ref, dec_ref, ke_ref, kl_ref, q2_ref, sc_ref, ob_ref) = refs
    else:
        (k_ref, q_ref, vlo_ref, vhi_ref, r_ref, wg_ref, bg_ref, s0_ref,
         o_ref, sfin_ref, st_ref, la_ref, qe_ref, upd_ref, dec_ref, ke_ref, kl_ref, q2_ref, sc_ref) = refs
        ob_ref = o_ref
    v_half = (vlo_ref, vhi_ref)
    rows_blk = k_ref.shape[0]
    nc = rows_blk // GLA_CHUNK
    qk = k_ref.shape[1]

    @pl.when(pl.program_id(0) == 0)
    def _():
        st_ref[...] = s0_ref[...]

    ri = lax.broadcasted_iota(jnp.int32, (2 * GLA_CHUNK, GLA_CHUNK), 0) % GLA_CHUNK
    ci = lax.broadcasted_iota(jnp.int32, (2 * GLA_CHUNK, GLA_CHUNK), 1)
    r3 = lax.broadcasted_iota(jnp.int32, (GLA_CHUNK, 3 * GLA_CHUNK), 0)
    c3 = lax.broadcasted_iota(jnp.int32, (GLA_CHUNK, 3 * GLA_CHUNK), 1) % GLA_CHUNK
    if reverse:
        tri3 = (c3 >= r3).astype(BF16)
        keep = ci > ri
    else:
        tri3 = (c3 <= r3).astype(BF16)
        keep = ci <= ri
    lane = lax.broadcasted_iota(jnp.int32, (1, LANES), 1)
    first = (lane < GLA_DK).astype(F32)
    srow = lax.broadcasted_iota(jnp.int32, (2 * GLA_DV, LANES), 0)
    scol = lax.broadcasted_iota(jnp.int32, (2 * GLA_DV, LANES), 1)
    own = (srow < GLA_DV) == (scol < GLA_DK)
    nt = (((1,), (1,)), ((), ()))
    tn = (((0,), (0,)), ((), ()))
    npair = GLA_HEADS // 2

    rh, rm, _ = _split3(r_ref[...])
    wh, wm, _ = _split3(wg_ref[...])
    pre = jnp.dot(jnp.concatenate([rh, rh, rm], axis=1), jnp.concatenate([wh, wm, wh], axis=0),
                  preferred_element_type=F32) + bg_ref[...]
    la_ref[...] = jax.nn.log_sigmoid(pre) / GLA_GATE_TAU

    def chunk_rows(c):
        return pl.ds(pl.multiple_of(c * GLA_CHUNK, GLA_CHUNK), GLA_CHUNK)

    def cum_pass(c, carry):
        rows = chunk_rows(c)
        hi, mid, lo = _split3(la_ref[rows, :])
        la_ref[rows, :] = jnp.dot(tri3, jnp.concatenate([hi, mid, lo], axis=0), preferred_element_type=F32)
        return carry

    def decay_pass(c, carry):
        rows = chunk_rows(c)
        cum = la_ref[rows, :]
        tot = cum[0:1] if reverse else cum[GLA_CHUNK - 1:GLA_CHUNK]
        k = k_ref[rows, :]
        qe = q_ref[rows, :] * (GLA_DK ** -0.5) * jnp.exp(cum)
        ke_ref[rows, :] = (k * jnp.exp(-cum)).astype(BF16)
        kl_ref[rows, :] = (k * jnp.exp(tot - cum)).astype(BF16)
        dec_ref[pl.ds(c, 1), :] = jnp.exp(tot)
        qe_ref[rows, :] = qe.astype(BF16)
        for p in range(npair):
            pr = slice(p * LANES, (p + 1) * LANES)
            q2_ref[c * npair + p] = jnp.concatenate([qe[:, pr] * first, qe[:, pr] * (1.0 - first)],
                                                    axis=0).astype(BF16)
        return carry

    def score_pass(c, carry):
        rows = chunk_rows(c)
        for p in range(npair):
            pr = slice(p * LANES, (p + 1) * LANES)
            sc = lax.dot_general(q2_ref[c * npair + p], ke_ref[rows, pr], nt, preferred_element_type=F32)
            sc_ref[c * npair + p] = jnp.where(keep, sc, 0.0).astype(BF16)
        return carry

    def value_pass(c, carry):
        rows = chunk_rows(c)
        for p in range(npair):
            pr = slice(p * LANES, (p + 1) * LANES)
            pv = slice(2 * p * GLA_DV, 2 * (p + 1) * GLA_DV)
            v_p = v_half[p // 2][rows, pl.ds((p % 2) * 2 * GLA_DV, 2 * GLA_DV)].astype(BF16)
            o2 = jnp.dot(sc_ref[c * npair + p], v_p, preferred_element_type=F32)
            ob_ref[rows, pl.ds(2 * p * GLA_DV, GLA_DV)] = o2[:GLA_CHUNK, :GLA_DV]
            ob_ref[rows, pl.ds((2 * p + 1) * GLA_DV, GLA_DV)] = o2[GLA_CHUNK:, GLA_DV:]
            u2 = lax.dot_general(v_p, kl_ref[rows, pr], tn, preferred_element_type=F32)
            upd_ref[c * npair + p] = jnp.where(own, u2, 0.0)
        return carry

    lax.fori_loop(0, nc, cum_pass, 0, unroll=True)
    lax.fori_loop(0, nc, decay_pass, 0, unroll=2)
    lax.fori_loop(0, nc, score_pass, 0, unroll=True)
    lax.fori_loop(0, nc, value_pass, 0, unroll=True)

    def carried(idx, carry):
        c = (nc - 1 - idx) if reverse else idx
        rows = pl.ds(pl.multiple_of(c * GLA_CHUNK, GLA_CHUNK), GLA_CHUNK)
        dec = dec_ref[pl.ds(c, 1), :]
        for p in range(npair):
            pr = slice(p * LANES, (p + 1) * LANES)
            pv = slice(2 * p * GLA_DV, 2 * (p + 1) * GLA_DV)
            st = st_ref[p]
            ob_ref[rows, pv] += lax.dot_general(qe_ref[rows, pr], st.astype(BF16), nt, preferred_element_type=F32)
            st_ref[p] = st * dec[:, pr] + upd_ref[c * npair + p]
        return carry

    lax.fori_loop(0, nc, carried, 0, unroll=4)

    if combine:
        for h in range(GLA_HEADS):
            hv = slice(h * GLA_DV, (h + 1) * GLA_DV)
            o = of_ref[:, hv] + ob_ref[:, hv]
            ms = jnp.mean(o * o, axis=-1, keepdims=True)
            o = o * lax.rsqrt(ms + NORM_EPS) * gout_ref[...]
            o_ref[:, hv] = (o * _silu(gate_ref[:, hv])).astype(o_ref.dtype)

    @pl.when(pl.program_id(0) == pl.num_programs(0) - 1)
    def _():
        sfin_ref[...] = st_ref[...]


_GLA_STATE = (GLA_HEADS // 2, 2 * GLA_DV, LANES)

_COL_K, _COL_V, _COL_Q, _COL_G = 24, 28, 36, 40


def _gla_pass(p, p_r, wg_pad, bg, s0, reverse, of=None, gout=None):
    l = p.shape[0]
    rb = min(l, 512)
    nblk = l // rb
    qk = GLA_HEADS * GLA_DK
    vw = GLA_HEADS * GLA_DV
    combine = of is not None
    rmap = (lambda i: nblk - 1 - i) if reverse else (lambda i: i)

    def cols(width, off128):
        return pl.BlockSpec((rb, width), lambda i: (rmap(i), off128 * LANES // width))

    state = pl.BlockSpec(_GLA_STATE, lambda i: (0, 0, 0))
    in_specs = [cols(qk, _COL_K), cols(qk, _COL_Q), cols(vw // 2, _COL_V), cols(vw // 2, _COL_V + vw // 2 // LANES),
                pl.BlockSpec((rb, LANES), lambda i: (rmap(i), 0)),
                pl.BlockSpec((LANES, qk), lambda i: (0, 0)), pl.BlockSpec((1, qk), lambda i: (0, 0)), state]
    args = [p, p, p, p, p_r, wg_pad, bg.reshape(1, qk), s0]
    nc = rb // GLA_CHUNK
    npair = _GLA_STATE[0]
    scratch = [pltpu.VMEM(_GLA_STATE, F32),
               pltpu.VMEM((rb, qk), F32),
               pltpu.VMEM((rb, qk), BF16),
               pltpu.VMEM((nc * npair,) + _GLA_STATE[1:], F32),
               pltpu.VMEM((nc, qk), F32),
               pltpu.VMEM((rb, qk), BF16),
               pltpu.VMEM((rb, qk), BF16),
               pltpu.VMEM((nc * npair, 2 * GLA_CHUNK, LANES), BF16),
               pltpu.VMEM((nc * npair, 2 * GLA_CHUNK, GLA_CHUNK), BF16)]
    if combine:
        in_specs += [pl.BlockSpec((rb, vw), lambda i: (rmap(i), 0)), cols(vw, _COL_G),
                     pl.BlockSpec((1, GLA_DV), lambda i: (0, 0))]
        args += [of, p, gout.reshape(1, GLA_DV)]
        scratch.append(pltpu.VMEM((rb, vw), F32))
    out_dtype = BF16 if combine else F32
    return pl.pallas_call(
        functools.partial(_gla_kernel, reverse=reverse, combine=combine),
        grid=(nblk,),
        in_specs=in_specs,
        out_specs=[pl.BlockSpec((rb, vw), lambda i: (rmap(i), 0)), state],
        out_shape=[jax.ShapeDtypeStruct((l, vw), out_dtype),
                   jax.ShapeDtypeStruct(_GLA_STATE, F32)],
        scratch_shapes=scratch,
        compiler_params=_params("arbitrary"),
        name="gla_bwd" if reverse else "gla_fwd",
    )(*args)


def _out_proj_kernel(hy_ref, o_ref, w_ref, x_ref, ga_ref, g_ref, sh_ref, sc_ref, xo_ref, h_ref):
    q = hy_ref.shape[1]
    hy_a, hy_b = _unpack_pair(hy_ref[...])
    mix = (jnp.dot(hy_a, w_ref[:q, :], preferred_element_type=F32)
           + jnp.dot(hy_b, w_ref[q:2 * q, :], preferred_element_type=F32)
           + jnp.dot(o_ref[...], w_ref[2 * q:, :], preferred_element_type=F32))
    xn = x_ref[...] + ga_ref[...] * mix
    xo_ref[...] = xn
    h_ref[...] = _norm_mod(xn, g_ref[...], sh_ref[...], sc_ref[...]).astype(BF16)


def _out_proj(hy, o, w, x, ga, g, shift, scale, layer):
    m, d = x.shape
    tm = min(m, 256)
    vec = pl.BlockSpec((1, d), lambda i: (0, 0))
    row = lambda width: pl.BlockSpec((tm, width), lambda i: (i, 0))
    return pl.pallas_call(
        _out_proj_kernel,
        grid=(m // tm,),
        in_specs=[row(hy.shape[1]), row(o.shape[1]), pl.BlockSpec((None, w.shape[1], d), lambda i: (layer, 0, 0)),
                  row(d), vec, vec, vec, vec],
        out_specs=[row(d), row(d)],
        out_shape=[jax.ShapeDtypeStruct((m, d), F32), jax.ShapeDtypeStruct((m, d), BF16)],
        compiler_params=_params("parallel"),
        name="out_proj",
    )(hy, o, w, x, ga, g, shift, scale)


_HALO = GRID_W + SUBLANES


def _ffn_up_conv_kernel(x_ref, wa_ref, wu_ref, cw_ref, o_ref, a_ref, u_ref, wb_ref, *, vertical, cols, nm):
    t = pl.program_id(0)
    total = pl.num_programs(0) - 1
    i = lax.rem(t, nm)
    tt = x_ref.shape[0]
    cur = lax.rem(t, 2)
    prv = 1 - cur

    @pl.when(t == 0)
    def _():
        a_ref[...] = jnp.zeros_like(a_ref)
        u_ref[...] = jnp.zeros_like(u_ref)

    def conv_previous_tile():
        col = lax.broadcasted_iota(jnp.int32, o_ref.shape, 0) & (cols - 1)
        acc = None
        for dc in (-1, 0, 1):
            part = None
            for dr in ((-1, 0, 1) if vertical else (0,)):
                tap = cw_ref[3 * (dr + 1) + (dc + 1):3 * (dr + 1) + (dc + 1) + 1, :]
                term = tap * a_ref[prv, pl.ds(_HALO + GRID_W * dr + dc, tt), :]
                part = term if part is None else part + term
            if dc == -1:
                part = jnp.where(col == 0, 0.0, part)
            elif dc == 1:
                part = jnp.where(col == cols - 1, 0.0, part)
            acc = part if acc is None else acc + part
        o_ref[...] = (_silu(acc) * u_ref[prv].astype(F32)).astype(o_ref.dtype)

    @pl.when((i == 0) & (t < total))
    def _():
        wb_ref[0] = wa_ref[...].astype(BF16)
        wb_ref[1] = wu_ref[...].astype(BF16)

    @pl.when(t < total)
    def _():
        x = x_ref[...]
        a_new = jnp.dot(x, wb_ref[0], preferred_element_type=F32)
        u_ref[cur] = jnp.dot(x, wb_ref[1], preferred_element_type=F32).astype(BF16)
        upper = a_ref[prv, pl.ds(_HALO + tt - GRID_W, GRID_W), :]
        a_ref[cur, pl.ds(SUBLANES, GRID_W), :] = jnp.where(i == 0, 0.0, upper)
        a_ref[cur, pl.ds(_HALO, tt), :] = a_new
        a_ref[prv, pl.ds(_HALO + tt, GRID_W), :] = jnp.where(i == 0, 0.0, a_new[:GRID_W])
        conv_previous_tile()

    @pl.when(t == total)
    def _():
        a_ref[prv, pl.ds(_HALO + tt, GRID_W), :] = jnp.zeros((GRID_W, a_ref.shape[2]), F32)
        conv_previous_tile()


def _ffn_up_conv(h, w_up, conv_w, cols, layer):
    m, d = h.shape
    dff = conv_w.shape[-1]
    vertical = m > cols
    assert cols & (cols - 1) == 0 and (cols == GRID_W or not vertical)
    tt = min(m, 1024)
    assert tt % cols == 0
    cb = 512
    ncb = dff // cb
    nm = m // tt
    total = ncb * nm
    cur_t = lambda t: jnp.minimum(t, total - 1)
    prev_t = lambda t: jnp.maximum(t - 1, 0)
    return pl.pallas_call(
        functools.partial(_ffn_up_conv_kernel, vertical=vertical, cols=cols, nm=nm),
        grid=(total + 1,),
        in_specs=[pl.BlockSpec((tt, d), lambda t: (cur_t(t) % nm, 0)),
                  pl.BlockSpec((None, d, cb), lambda t: (layer, 0, cur_t(t) // nm)),
                  pl.BlockSpec((None, d, cb), lambda t: (layer, 0, ncb + cur_t(t) // nm)),
                  pl.BlockSpec((9, cb), lambda t: (0, prev_t(t) // nm))],
        out_specs=pl.BlockSpec((tt, cb), lambda t: (prev_t(t) % nm, prev_t(t) // nm)),
        out_shape=jax.ShapeDtypeStruct((m, dff), BF16),
        scratch_shapes=[pltpu.VMEM((2, tt + 2 * _HALO, cb), F32), pltpu.VMEM((2, tt, cb), BF16),
                        pltpu.VMEM((2, d, cb), BF16)],
        compiler_params=_params("arbitrary"),
        name="ffn_up_conv",
    )(h, w_up, w_up, conv_w.reshape(9, dff))


def _ffn_down_kernel(g_ref, w_ref, x_ref, gate_ref, ng_ref, nsh_ref, nsc_ref, o_ref, *h_ref, post):
    xn = x_ref[...] + gate_ref[...] * jnp.dot(g_ref[...], w_ref[...], preferred_element_type=F32)
    if post == 'final':
        ms = jnp.mean(xn * xn, axis=-1, keepdims=True)
        xn = xn * lax.rsqrt(ms + NORM_EPS) * ng_ref[...]
    o_ref[...] = xn
    if post == 'next':
        h_ref[0][...] = _norm_mod(xn, ng_ref[...], nsh_ref[...], nsc_ref[...]).astype(BF16)


def _ffn_down(g, w, x, gate, norm, post, layer):
    m, d = x.shape
    dff = g.shape[1]
    tm = min(m, 256)
    vec = pl.BlockSpec((1, d), lambda i: (0, 0))
    row = pl.BlockSpec((tm, d), lambda i: (i, 0))
    out_specs, out_shape = [row], [jax.ShapeDtypeStruct((m, d), F32)]
    if post == 'next':
        out_specs.append(row)
        out_shape.append(jax.ShapeDtypeStruct((m, d), BF16))
    return pl.pallas_call(
        functools.partial(_ffn_down_kernel, post=post),
        grid=(m // tm,),
        in_specs=[pl.BlockSpec((tm, dff), lambda i: (i, 0)),
                  _single((None, dff, d), lambda i: (layer, 0, 0)),
                  row, vec, vec, vec, vec],
        out_specs=out_specs,
        out_shape=out_shape,
        compiler_params=_params("parallel"),
        name="ffn_down",
    )(g, w, x, gate, *norm)


def _regroup_w_in(w_in_t, hy_proj):
    qk = GLA_HEADS * GLA_DK
    vw = GLA_HEADS * GLA_DV
    head = hy_proj + qk + vw
    r = w_in_t[:, head:head + 2 * GLA_RANK]
    tail = w_in_t[:, head + 2 * GLA_RANK:]
    pad = jnp.zeros((w_in_t.shape[0], LANES - 2 * GLA_RANK, w_in_t.shape[2]), w_in_t.dtype)
    return head, tail, jnp.concatenate([r, pad], axis=1)


def _pad_gate_w(wg, lane0):
    return jnp.zeros((LANES, wg.shape[1]), F32).at[lane0:lane0 + GLA_RANK].set(wg)


def _mixer(p, p_r, lp, s0_f, s0_b, want_output):
    l = p.shape[0]
    width = lp['hy_bias'].shape[0]
    o_f, s_f = _gla_pass(p, p_r, lp['wg_f_pad'], lp['gla_bg_f'], s0_f, reverse=False)
    o, s_b = _gla_pass(p, p_r, lp['wg_b_pad'], lp['gla_bg_b'], s0_b, reverse=True, of=o_f, gout=lp['gla_out_g'])
    if not want_output:
        return None, None, s_f, s_b
    zx = _hy_pre(p, lp['hy_conv'], width)
    if 2 * l // DFT_B >= 2 * SUBLANES:
        hy = _hyena_long(zx, lp, lp['hy_bias'], lp['hy_out_g'])
    else:
        k2 = _hyena_filter2(l, lp['hy_w1'], lp['hy_b1'], lp['hy_freq'], lp['hy_w2'], lp['hy_b2'], lp['hy_w3'], width)
        hy = _hyena_short(zx, k2, lp['hy_bias'], lp['hy_out_g'])
    return hy, o, s_f, s_b


def _order_w_out(w_out, hy_width):
    depth, _, d = w_out.shape
    hy_rows = w_out[:, :hy_width].reshape(depth, hy_width // (2 * LANES), 2, LANES, d)
    hy_rows = jnp.swapaxes(hy_rows, 1, 2).reshape(depth, hy_width, d)
    return jnp.concatenate([hy_rows, w_out[:, hy_width:]], axis=1).astype(BF16)


def _ffn(h2, x, gate, lp, cols, norm, post, layer):
    gact = _ffn_up_conv(h2, lp['ffn_w_up'], lp['ffn_conv'], cols, layer)
    return _ffn_down(gact, lp['ffn_w_down'], x, gate, norm, post, layer)


def kernel(x, c, ctx, c_ctx, w_mod, b_mod, g_attn, w_in, hy_conv, hy_w1, hy_b1, hy_freq, hy_w2, hy_b2, hy_w3,
           hy_bias, hy_out_g, gla_wg_f, gla_bg_f, gla_wg_b, gla_bg_b, gla_out_g, w_out, g_ffn, ffn_w_up,
           ffn_conv, ffn_w_down, g_final):
    depth = w_mod.shape[0]
    batch, seq, d = x.shape
    assert batch == 1 and c.shape[0] == 1 and ctx.shape[0] == 1
    hy_proj = hy_conv.shape[-1]
    xs = x[0]
    cs = ctx[0]
    cc = jnp.zeros((SUBLANES, d), F32).at[0].set(c[0]).at[1].set(c_ctx)
    zero_state = jnp.zeros(_GLA_STATE, F32)
    gfin = g_final.reshape(1, d)
    w_in = jnp.swapaxes(w_in, 1, 2)
    n_head_cols, w_tail, w_rank = _regroup_w_in(w_in, hy_proj)
    w_out_b = _order_w_out(w_out, hy_bias.shape[-1])
    w_down_b = ffn_w_down.astype(BF16)
    mods =[_adaln(cc, w_mod, b_mod, l) for l in range(depth)]

    def attn_norm(l, row):
        return (g_attn[l].reshape(1, d), mods[l][row:row + 1, 0:d], mods[l][row:row + 1, d:2 * d])

    hx = _norm_mod_call(xs, *attn_norm(0, 0))
    hc = _norm_mod_call(cs, *attn_norm(0, 1))
    for l in range(depth):
        last = l == depth - 1
        lp = {
            'hy_conv': hy_conv[l], 'hy_w1': hy_w1[l], 'hy_b1': hy_b1[l], 'hy_freq': hy_freq[l],
            'hy_w2': hy_w2[l], 'hy_b2': hy_b2[l], 'hy_w3': hy_w3[l], 'hy_bias': hy_bias[l],
            'hy_out_g': hy_out_g[l], 'gla_bg_f': gla_bg_f[l], 'gla_bg_b': gla_bg_b[l],
            'gla_out_g': gla_out_g[l],
            'wg_f_pad': _pad_gate_w(gla_wg_f[l], 0), 'wg_b_pad': _pad_gate_w(gla_wg_b[l], GLA_RANK),
            'ffn_w_up': ffn_w_up, 'ffn_conv': ffn_conv[l], 'ffn_w_down': w_down_b,
        }
        mod = mods[l]
        _, _, ga, sf, scf, gf = [mod[0:1, i * d:(i + 1) * d] for i in range(6)]
        _, _, cga, csf, cscf, cgf = [mod[1:2, i * d:(i + 1) * d] for i in range(6)]
        gf_row = g_ffn[l].reshape(1, d)

        pc = _in_proj(hc, w_in, w_tail, n_head_cols, l)
        hy_c, o_c, s_f, s_b = _mixer(pc, _gate_rank_proj(hc, w_rank, l), lp, zero_state, zero_state,
                                     want_output=not last)
        if not last:
            cs, h2c = _out_proj(hy_c, o_c, w_out_b, cs, cga, gf_row, csf, cscf, l)
            cs, hc = _ffn(h2c, cs, cgf, lp, cs.shape[0], attn_norm(l + 1, 1), 'next', l)

        px = _in_proj(hx, w_in, w_tail, n_head_cols, l)
        hy_x, o_x, _, _ = _mixer(px, _gate_rank_proj(hx, w_rank, l), lp, s_f, s_b, want_output=True)
        xs, h2 = _out_proj(hy_x, o_x, w_out_b, xs, ga, gf_row, sf, scf, l)
        if last:
            (xs,) = _ffn(h2, xs, gf, lp, GRID_W, (gfin, gfin, gfin), 'final', l)
        else:
            xs, hx = _ffn(h2, xs, gf, lp, GRID_W, attn_norm(l + 1, 0), 'next', l)
    return xs[None]
```
